```python
import jax
import jax.numpy as jnp
from jax import lax
import numpy as np

D_MODEL = 1024
BATCH = 2
SEQ = 8192
DEPTH = 2
DEC_BATCH = 128
DEC_SEQ = 8
PAST_LEN = 2048
PAGE_SIZE = 128

HEAD_DIM = 64
N_HEADS = D_MODEL // HEAD_DIM
H_A = (3 * N_HEADS) // 8
G_B = N_HEADS // 4
H_C = N_HEADS - H_A - G_B
W_A = H_A * HEAD_DIM
W_B = G_B * HEAD_DIM
W_C = H_C * HEAD_DIM
MIX_WIDTH = W_A + W_B + W_C
GDN_CONV = 4
GDN_CHUNK = 64
CHUNK = 128
FOX_QBLOCK = 128
D_FF = 2816
FFN_CONV = 3
N_IN = 4 * W_A + 2 * H_A + 2 * W_B + 3 * W_C + H_C
EPS = 1e-6
NEG_INF = -1e30

kernel_name = 'hymba_gdn_gmlp_fox_convffn_step'


def rmsnorm(x, g):
    x32 = x.astype(jnp.float32)
    y = x32 * lax.rsqrt(jnp.mean(x32 * x32, axis=-1, keepdims=True) + EPS)
    return y * g.astype(jnp.float32)


def layernorm(x, g, b):
    x32 = x.astype(jnp.float32)
    mu = jnp.mean(x32, axis=-1, keepdims=True)
    xc = x32 - mu
    var = jnp.mean(xc * xc, axis=-1, keepdims=True)
    return xc * lax.rsqrt(var + EPS) * g.astype(jnp.float32) + b.astype(jnp.float32)


def l2norm(x):
    return x * lax.rsqrt(jnp.sum(x * x, axis=-1, keepdims=True) + EPS)


def causal_dwconv(x, prev, w):
    width, L = w.shape[0], x.shape[1]
    xp = jnp.concatenate([prev.astype(x.dtype), x], axis=1)
    y = xp[:, 0:L] * w[0]
    for i in range(1, width):
        y = y + xp[:, i:i + L] * w[i]
    return y, xp[:, L:]


def gated_delta_rule(q, k, v, beta, g, S0):
    B, L, H, dk = q.shape
    dv = v.shape[-1]
    C = min(GDN_CHUNK, L)
    n = -(-L // C)
    pad = n * C - L

    def to_chunks(t):
        t = jnp.pad(t, [(0, 0), (0, pad)] + [(0, 0)] * (t.ndim - 2))
        return jnp.moveaxis(t.reshape((B, n, C) + t.shape[2:]), 3, 1)

    q, k, v, beta, g = (to_chunks(t) for t in (q * dk ** -0.5, k, v, beta, g))
    gc = jnp.cumsum(g, axis=-1)
    idx = jnp.arange(C)
    tri = idx[:, None] >= idx[None, :]
    strict = idx[:, None] > idx[None, :]
    diff = gc[..., :, None] - gc[..., None, :]
    decay = jnp.where(tri, jnp.exp(jnp.where(tri, diff, 0.0)), 0.0)
    kb = k * beta[..., None]
    lower = jnp.where(strict, jnp.einsum('bhncd,bhnsd->bhncs', kb, k) * decay, 0.0)
    rhs = jnp.concatenate([v * beta[..., None], kb * jnp.exp(gc)[..., None]], axis=-1)
    sol = lax.linalg.triangular_solve(lower + jnp.eye(C, dtype=lower.dtype), rhs, left_side=True, lower=True)
    u, w = sol[..., :dv], sol[..., dv:]
    attn = jnp.einsum('bhncd,bhnsd->bhncs', q, k) * decay
    q_in = q * jnp.exp(gc)[..., None]
    k_out = k * jnp.exp(gc[..., -1:] - gc)[..., None]
    g_last = jnp.exp(gc[..., -1])

    def step(S, xs):
        q_c, k_c, u_c, w_c, a_c, gl = xs
        v_new = u_c - jnp.einsum('bhck,bhkv->bhcv', w_c, S)
        o = jnp.einsum('bhck,bhkv->bhcv', q_c, S) + jnp.einsum('bhcs,bhsv->bhcv', a_c, v_new)
        S = S * gl[..., None, None] + jnp.einsum('bhck,bhcv->bhkv', k_c, v_new)
        return S, o

    xs = tuple(jnp.moveaxis(t, 2, 0) for t in (q_in, k_out, u, w, attn, g_last))
    S, o = lax.scan(step, S0, xs)
    o = jnp.moveaxis(o, 0, 2).reshape(B, H, n * C, dv)
    return jnp.swapaxes(o, 1, 2)[:, :L], S


def chunk_spatial_gate(u, v, w_s, b_s):
    B, L, _ = u.shape
    n = -(-L // CHUNK)
    pad = n * CHUNK - L
    vc = jnp.pad(v, ((0, 0), (0, pad), (0, 0))).reshape(B, n, CHUNK, G_B, HEAD_DIM)
    mix = jnp.einsum('gts,bnsgc->bntgc', jnp.tril(w_s.astype(jnp.float32)), vc)
    mix = mix + jnp.swapaxes(b_s.astype(jnp.float32), 0, 1)[:, :, None]
    return u * mix.reshape(B, n * CHUNK, W_B)[:, :L]


def forgetting_attention(q, k, v, dq, dk, qpos, kpos):
    B, Lq, H, d = q.shape
    QB = min(FOX_QBLOCK, Lq)
    nb = -(-Lq // QB)
    pad = nb * QB - Lq
    qb = jnp.moveaxis(jnp.pad(q, ((0, 0), (0, pad), (0, 0), (0, 0))).reshape(B, nb, QB, H, d), 1, 0)
    dqb = jnp.moveaxis(jnp.pad(dq, ((0, 0), (0, pad), (0, 0))).reshape(B, nb, QB, H), 1, 0)
    pb = jnp.pad(qpos, (0, pad), mode='edge').reshape(nb, QB)
    dk_t = jnp.swapaxes(dk, 1, 2)
    scale = d ** -0.5

    def one_block(args):
        qi, dqi, pi = args
        s = jnp.einsum('bqhd,bkhd->bhqk', qi, k) * scale + jnp.swapaxes(dqi, 1, 2)[..., None] - dk_t[:, :, None, :]
        s = jnp.where(kpos[None, :] <= pi[:, None], s, NEG_INF)
        p = jax.nn.softmax(s, axis=-1)
        return jnp.einsum('bhqk,bkhd->bqhd', p, v)

    o = lax.map(one_block, (qb, dqb, pb))
    return jnp.moveaxis(o, 0, 1).reshape(B, nb * QB, H, d)[:, :Lq]


def block(x, c, l, P, gdn_conv_prev, gdn_S0, fox_past, ffn_conv_prev):
    f32 = jnp.float32
    B, L, _ = x.shape
    mod = (jax.nn.silu(c.astype(f32)) @ P['w_ada'][l].astype(f32) + P['b_ada'][l].astype(f32)).reshape(B, 6, D_MODEL)
    shift1, scale1, gate1, shift2, scale2, gate2 = [mod[:, i, None, :] for i in range(6)]

    h = (rmsnorm(x, P['g_pre_mix'][l]) * (1.0 + scale1) + shift1).astype(x.dtype)
    z = h @ P['w_in'][l]
    offs = [3 * W_A, 3 * W_A + H_A, 3 * W_A + 2 * H_A, 4 * W_A + 2 * H_A,
            4 * W_A + 2 * H_A + 2 * W_B, 4 * W_A + 2 * H_A + 2 * W_B + 3 * W_C]
    qkv_a, beta_raw, a_raw, gate_a, uv_b, qkv_c, f_raw = jnp.split(z, offs, axis=-1)

    qkv_a, gdn_conv_new = causal_dwconv(qkv_a, gdn_conv_prev, P['gdn_conv_w'][l])
    qkv_a = jax.nn.silu(qkv_a.astype(f32))
    qa, ka, va = [t.reshape(B, L, H_A, HEAD_DIM) for t in jnp.split(qkv_a, 3, axis=-1)]
    beta = jax.nn.sigmoid(beta_raw.astype(f32))
    g = -jnp.exp(P['gdn_A_log'][l].astype(f32)) * jax.nn.softplus(a_raw.astype(f32) + P['gdn_dt_bias'][l].astype(f32))
    oa, S_new = gated_delta_rule(l2norm(qa), l2norm(ka), va, beta, g, gdn_S0.astype(f32))
    oa = rmsnorm(oa, P['gdn_norm_g'][l]) * jax.nn.silu(gate_a.astype(f32)).reshape(B, L, H_A, HEAD_DIM)
    oa = oa.reshape(B, L, W_A)

    ub, vb = jnp.split(jax.nn.gelu(uv_b.astype(f32)), 2, axis=-1)
    vb = layernorm(vb, P['chunk_ln_g'][l], P['chunk_ln_b'][l])
    ob = chunk_spatial_gate(ub, vb, P['chunk_w_s'][l], P['chunk_b_s'][l])

    qc, kc, vc = [t.astype(f32).reshape(B, L, H_C, HEAD_DIM) for t in jnp.split(qkv_c, 3, axis=-1)]
    logf = jax.nn.log_sigmoid(f_raw.astype(f32) + P['fox_f_bias'][l].astype(f32))
    if fox_past is None:
        k_all, v_all, lf_all = kc, vc, logf
        kpos = jnp.arange(L)
        qpos = kpos
    else:
        pk, pv, plf = fox_past
        past = pk.shape[1]
        k_all = jnp.concatenate([pk.astype(f32), kc], axis=1)
        v_all = jnp.concatenate([pv.astype(f32), vc], axis=1)
        lf_all = jnp.concatenate([plf.astype(f32), logf], axis=1)
        kpos = jnp.arange(past + L)
        qpos = past + jnp.arange(L)
    d_all = jnp.cumsum(lf_all, axis=1)
    oc = forgetting_attention(qc, k_all, v_all, d_all[:, -L:], d_all, qpos, kpos).reshape(B, L, W_C)

    mixed = jnp.concatenate([oa, ob, oc], axis=-1).astype(x.dtype)
    o = mixed @ P['w_out'][l]
    x = x + (gate1 * rmsnorm(o, P['g_post_mix'][l])).astype(x.dtype)

    h = (rmsnorm(x, P['g_pre_ffn'][l]) * (1.0 + scale2) + shift2).astype(x.dtype)
    up = h @ P['w_up'][l]
    up, ffn_conv_new = causal_dwconv(up, ffn_conv_prev, P['ffn_conv_w'][l])
    a, b = jnp.split(up.astype(f32), 2, axis=-1)
    y = (jax.nn.gelu(a) * b).astype(x.dtype) @ P['w_down'][l]
    x = x + (gate2 * rmsnorm(y, P['g_post_ffn'][l])).astype(x.dtype)
    return x, gdn_conv_new, S_new, kc, vc, logf, vb, ffn_conv_new


def setup_inputs(seed: int = 0) -> dict:
    key = jax.random.key(seed)
    ks = iter(jax.random.split(key, 48))
    f32 = jnp.float32
    n_pages = PAST_LEN // PAGE_SIZE
    n_used = DEC_BATCH * n_pages
    n_pool = n_used + n_used // 4

    def nrm(shape, s=1.0):
        return s * jax.random.normal(next(ks), shape, f32)

    def unif(shape, lo, hi):
        return jax.random.uniform(next(ks), shape, f32, lo, hi)

    x_prompt = nrm((BATCH, SEQ, D_MODEL))
    x_sample = nrm((DEC_BATCH, DEC_SEQ, D_MODEL))
    state_gdn_conv = nrm((DEPTH, DEC_BATCH, GDN_CONV - 1, 3 * W_A))
    state_gdn_S = nrm((DEPTH, DEC_BATCH, H_A, HEAD_DIM, HEAD_DIM), 0.1)
    cache_fox_k = nrm((DEPTH, n_pool, PAGE_SIZE, H_C, HEAD_DIM))
    cache_fox_v = nrm((DEPTH, n_pool, PAGE_SIZE, H_C, HEAD_DIM))
    cache_fox_logf = jax.nn.log_sigmoid(nrm((DEPTH, n_pool, PAGE_SIZE, H_C)) + 2.5)
    state_ffn_conv = nrm((DEPTH, DEC_BATCH, FFN_CONV - 1, 2 * D_FF))
    page_table = jax.random.permutation(next(ks), n_pool)[:n_used].reshape(DEC_BATCH, n_pages).astype(jnp.int32)
    c_prompt = nrm((BATCH, D_MODEL))
    c_sample = nrm((DEC_BATCH, D_MODEL))

    dt = jnp.exp(unif((DEPTH, H_A), float(np.log(1e-3)), float(np.log(1e-1))))
    return {
        'x_prompt': x_prompt,
        'x_sample': x_sample,
        'state_gdn_conv': state_gdn_conv,
        'state_gdn_S': state_gdn_S,
        'cache_fox_k': cache_fox_k,
        'cache_fox_v': cache_fox_v,
        'cache_fox_logf': cache_fox_logf,
        'state_ffn_conv': state_ffn_conv,
        'page_table': page_table,
        'c_prompt': c_prompt,
        'c_sample': c_sample,
        'w_ada': nrm((DEPTH, D_MODEL, 6 * D_MODEL), D_MODEL ** -0.5),
        'b_ada': nrm((DEPTH, 6 * D_MODEL), 0.02),
        'g_pre_mix': 1.0 + nrm((DEPTH, D_MODEL), 0.05),
        'g_post_mix': 1.0 + nrm((DEPTH, D_MODEL), 0.05),
        'g_pre_ffn': 1.0 + nrm((DEPTH, D_MODEL), 0.05),
        'g_post_ffn': 1.0 + nrm((DEPTH, D_MODEL), 0.05),
        'w_in': nrm((DEPTH, D_MODEL, N_IN), D_MODEL ** -0.5),
        'w_out': nrm((DEPTH, MIX_WIDTH, D_MODEL), MIX_WIDTH ** -0.5),
        'gdn_conv_w': nrm((DEPTH, GDN_CONV, 3 * W_A), GDN_CONV ** -0.5),
        'gdn_A_log': jnp.log(unif((DEPTH, H_A), 1.0, 16.0)),
        'gdn_dt_bias': dt + jnp.log(-jnp.expm1(-dt)),
        'gdn_norm_g': 1.0 + nrm((DEPTH, HEAD_DIM), 0.05),
        'chunk_ln_g': 1.0 + nrm((DEPTH, W_B), 0.05),
        'chunk_ln_b': nrm((DEPTH, W_B), 0.02),
        'chunk_w_s': nrm((DEPTH, G_B, CHUNK, CHUNK), CHUNK ** -0.5),
        'chunk_b_s': 1.0 + nrm((DEPTH, G_B, CHUNK), 0.1),
        'fox_f_bias': unif((DEPTH, H_C), 1.0, 4.0),
        'w_up': nrm((DEPTH, D_MODEL, 2 * D_FF), D_MODEL ** -0.5),
        'ffn_conv_w': nrm((DEPTH, FFN_CONV, 2 * D_FF), FFN_CONV ** -0.5),
        'w_down': nrm((DEPTH, D_FF, D_MODEL), D_FF ** -0.5),
    }


def reference(x_prompt, x_sample, state_gdn_conv, state_gdn_S, cache_fox_k, cache_fox_v, cache_fox_logf,
              state_ffn_conv, page_table, c_prompt, c_sample, w_ada, b_ada, g_pre_mix, g_post_mix, g_pre_ffn,
              g_post_ffn, w_in, w_out, gdn_conv_w, gdn_A_log, gdn_dt_bias, gdn_norm_g, chunk_ln_g, chunk_ln_b,
              chunk_w_s, chunk_b_s, fox_f_bias, w_up, ffn_conv_w, w_down):
    P = {'w_ada': w_ada, 'b_ada': b_ada, 'g_pre_mix': g_pre_mix, 'g_post_mix': g_post_mix,
         'g_pre_ffn': g_pre_ffn, 'g_post_ffn': g_post_ffn, 'w_in': w_in, 'w_out': w_out,
         'gdn_conv_w': gdn_conv_w, 'gdn_A_log': gdn_A_log, 'gdn_dt_bias': gdn_dt_bias, 'gdn_norm_g': gdn_norm_g,
         'chunk_ln_g': chunk_ln_g, 'chunk_ln_b': chunk_ln_b, 'chunk_w_s': chunk_w_s, 'chunk_b_s': chunk_b_s,
         'fox_f_bias': fox_f_bias, 'w_up': w_up, 'ffn_conv_w': ffn_conv_w, 'w_down': w_down}
    Bp, Lp, _ = x_prompt.shape
    Bs = x_sample.shape[0]
    n_pages = page_table.shape[1]
    past = n_pages * cache_fox_k.shape[2]

    p_conv, p_S, p_k, p_v, p_lf, p_ffn = [], [], [], [], [], []
    s_conv, s_S, s_k, s_v, s_lf, s_cv, s_ffn = [], [], [], [], [], [], []
    xp, xs = x_prompt, x_sample
    for l in range(DEPTH):
        xp, a_conv, a_S, kc, vc, lf, _, f_conv = block(
            xp, c_prompt, l, P,
            jnp.zeros((Bp, GDN_CONV - 1, 3 * W_A), xp.dtype),
            jnp.zeros((Bp, H_A, HEAD_DIM, HEAD_DIM), jnp.float32),
            None,
            jnp.zeros((Bp, FFN_CONV - 1, 2 * D_FF), xp.dtype))
        p_conv.append(a_conv)
        p_S.append(a_S)
        p_k.append(kc.reshape(Bp, Lp // PAGE_SIZE, PAGE_SIZE, H_C, HEAD_DIM))
        p_v.append(vc.reshape(Bp, Lp // PAGE_SIZE, PAGE_SIZE, H_C, HEAD_DIM))
        p_lf.append(lf.reshape(Bp, Lp // PAGE_SIZE, PAGE_SIZE, H_C))
        p_ffn.append(f_conv)

        pk = cache_fox_k[l][page_table].reshape(Bs, past, H_C, HEAD_DIM)
        pv = cache_fox_v[l][page_table].reshape(Bs, past, H_C, HEAD_DIM)
        plf = cache_fox_logf[l][page_table].reshape(Bs, past, H_C)
        xs, a_conv, a_S, kc, vc, lf, vrows, f_conv = block(
            xs, c_sample, l, P, state_gdn_conv[l], state_gdn_S[l], (pk, pv, plf), state_ffn_conv[l])
        s_conv.append(a_conv)
        s_S.append(a_S)
        s_k.append(kc)
        s_v.append(vc)
        s_lf.append(lf)
        s_cv.append(vrows)
        s_ffn.append(f_conv)

    return (xp, xs,
            jnp.stack(p_conv), jnp.stack(p_S), jnp.stack(p_k), jnp.stack(p_v), jnp.stack(p_lf), jnp.stack(p_ffn),
            jnp.stack(s_conv), jnp.stack(s_S), jnp.stack(s_k), jnp.stack(s_v), jnp.stack(s_lf), jnp.stack(s_cv),
            jnp.stack(s_ffn))
```

```python
import functools

import jax
import jax.numpy as jnp
from jax import lax
from jax.experimental import pallas as pl
from jax.experimental.pallas import tpu as pltpu

F32 = jnp.float32
BF16 = jnp.bfloat16

D_MODEL = 1024
HEAD_DIM = 64
H_A = 6
G_B = 4
H_C = 6
W_A = H_A * HEAD_DIM
W_B = G_B * HEAD_DIM
W_C = H_C * HEAD_DIM
GDN_CONV = 4
GDN_CHUNK = 64
CHUNK = 128
D_FF = 2816
FFN_CONV = 3
EPS = 1e-6
NEG_INF = -1e30
PAGE = 128

LANES = 128
SUBLANES = 8
VMEM_LIMIT = 56 * 1024 * 1024

QA_OFF = 0
GA_OFF = QA_OFF + 3 * W_A
UV_OFF = GA_OFF + W_A
QC_OFF = UV_OFF + 2 * W_B
SM_OFF = QC_OFF + 3 * W_C
N_FUSED = SM_OFF + LANES
SM_BETA = 0
SM_G = H_A
SM_LF = 2 * H_A

FF_CHUNK = D_FF // 2


def _silu(x):
    return x * jax.nn.sigmoid(x)


def _softplus(x):
    return jnp.maximum(x, 0.0) + jnp.log1p(jnp.exp(-jnp.abs(x)))


def _dot(a, b):
    return jnp.dot(a, b, preferred_element_type=F32)


def _dot_hi(a, b):
    return jnp.dot(a, b, preferred_element_type=F32, precision=lax.Precision.HIGHEST)


def _dot_nt(a, b):
    return lax.dot_general(a, b, (((1,), (1,)), ((), ())), preferred_element_type=F32)


def _dot_tn(a, b):
    return lax.dot_general(a, b, (((0,), (0,)), ((), ())), preferred_element_type=F32)


def _iota2(shape, dim):
    return lax.broadcasted_iota(jnp.int32, shape, dim)


def _params(sem):
    return pltpu.CompilerParams(dimension_semantics=sem, vmem_limit_bytes=VMEM_LIMIT)


def _const_spec(shape):
    nd = len(shape)
    return pl.BlockSpec(shape, lambda *_: (0,) * nd, pipeline_mode=pl.Buffered(1))


def _ada_kernel(c_ref, w_ref, b_ref, o_ref):
    c = _silu(c_ref[...]).astype(BF16)
    o_ref[...] = _dot(c, w_ref[...].astype(BF16)) + b_ref[...]


def _ada(c, w, b):
    m = c.shape[0]
    n = w.shape[1]
    tn = 1024
    return pl.pallas_call(
        _ada_kernel,
        grid=(n // tn,),
        in_specs=[pl.BlockSpec((m, D_MODEL), lambda j: (0, 0)),
                  pl.BlockSpec((D_MODEL, tn), lambda j: (0, j)),
                  pl.BlockSpec((1, tn), lambda j: (0, j))],
        out_specs=pl.BlockSpec((m, tn), lambda j: (0, j)),
        out_shape=jax.ShapeDtypeStruct((m, n), F32),
        compiler_params=_params(("arbitrary",)),
        name="ada_mod",
    )(c, w, b.reshape(1, n))


def _in_kernel(dec, bb, lt, *refs):
    (x_ref, shift_ref, scale_ref, gpre_ref, w_ref, cw_ref, cst_ref, smb_ref, alog_ref,
     lng_ref, lnb_ref, ws_ref, bsb_ref) = refs[:13]
    if dec:
        (qkva_ref, gate_ref, small_ref, ob_ref, kc_ref, vc_ref, tail_ref, vb_ref, q_ref,
         ext_ref) = refs[13:]
    else:
        (qkva_ref, gate_ref, small_ref, ob_ref, kc_ref, vc_ref, tail_ref, qbf_ref, kbf_ref,
         vbf_ref, dcum_ref, ext_ref, dcar_ref) = refs[13:]
    l = pl.program_id(1)
    m = bb * lt

    x = x_ref[...]
    ms = jnp.mean(x * x, axis=-1, keepdims=True)
    h = x * lax.rsqrt(ms + EPS) * gpre_ref[...]
    h = h * (1.0 + scale_ref[...]) + shift_ref[...]
    h2 = h.reshape(m, D_MODEL).astype(BF16)

    wa = 3 * W_A
    za = _dot(h2, w_ref[:, QA_OFF:QA_OFF + wa]).reshape(bb, lt, wa)

    @pl.when(l == 0)
    def _():
        ext_ref[:, SUBLANES - (GDN_CONV - 1):SUBLANES, :] = cst_ref[...]

    @pl.when(l > 0)
    def _():
        ext_ref[:, 0:SUBLANES, :] = ext_ref[:, lt:lt + SUBLANES, :]

    ext_ref[:, SUBLANES:, :] = za
    y = za * cw_ref[GDN_CONV - 1:GDN_CONV, :]
    for i in range(GDN_CONV - 1):
        o = SUBLANES - (GDN_CONV - 1) + i
        y = y + ext_ref[:, o:o + lt, :] * cw_ref[i:i + 1, :]
    qkva_ref[...] = _silu(y)
    tail_ref[...] = ext_ref[:, lt:lt + SUBLANES, :]

    gate_ref[...] = _silu(_dot(h2, w_ref[:, GA_OFF:GA_OFF + W_A])).reshape(bb, lt, W_A)

    zs = _dot(h2, w_ref[:, SM_OFF:SM_OFF + LANES]) + smb_ref[...]
    lane = _iota2((1, LANES), 1)
    beta = jax.nn.sigmoid(zs)
    gval = -jnp.exp(alog_ref[...]) * _softplus(zs)
    lf = -_softplus(-zs)
    small = jnp.where(lane < SM_G, beta,
                      jnp.where(lane < SM_LF, gval, jnp.where(lane < SM_LF + H_C, lf, 0.0)))
    small_ref[...] = small.reshape(bb, lt, LANES)

    if not dec:
        tri = (_iota2((lt, lt), 0) >= _iota2((lt, lt), 1)).astype(F32)

        @pl.when(l == 0)
        def _():
            dcar_ref[...] = jnp.zeros_like(dcar_ref)

        dc = _dot_hi(tri, small) + dcar_ref[0:1, :]
        dcum_ref[...] = dc.reshape(bb, lt, LANES)
        dcar_ref[0:1, :] = dc[lt - 1:lt, :]

    uv = jax.nn.gelu(_dot(h2, w_ref[:, UV_OFF:UV_OFF + 2 * W_B]))
    u = uv[:, :W_B]
    v = uv[:, W_B:]
    mu = jnp.mean(v, axis=-1, keepdims=True)
    vc0 = v - mu
    var = jnp.mean(vc0 * vc0, axis=-1, keepdims=True)
    vb = vc0 * lax.rsqrt(var + EPS) * lng_ref[...] + lnb_ref[...]
    if dec:
        vb3 = vb.reshape(bb, lt, W_B)
        vb_ref[...] = vb3
        trow = _iota2((lt, W_B), 0)
        mix = jnp.zeros((bb, lt, W_B), F32)
        for s in range(lt):
            coef = jnp.where(trow >= s, ws_ref[s], 0.0)
            mix = mix + coef[None] * vb3[:, s:s + 1, :]
        ob_ref[...] = u.reshape(bb, lt, W_B) * (mix + bsb_ref[...][None])
    else:
        r_t = _iota2((G_B * CHUNK, CHUNK), 0) % CHUNK
        r_s = _iota2((G_B * CHUNK, CHUNK), 1)
        wst = jnp.where(r_s <= r_t, ws_ref[...], 0.0).astype(BF16)
        grp = _iota2((1, W_B), 1) // HEAD_DIM
        for c in range(lt // CHUNK):
            rows = slice(c * CHUNK, (c + 1) * CHUNK)
            r = _dot(wst, vb[rows].astype(BF16))
            mix = jnp.zeros((CHUNK, W_B), F32)
            for g in range(G_B):
                mix = jnp.where(grp == g, r[g * CHUNK:(g + 1) * CHUNK], mix)
            ob_ref[0, rows, :] = u[rows] * (mix + bsb_ref[...])

    zc = _dot(h2, w_ref[:, QC_OFF:QC_OFF + 3 * W_C])
    qc = zc[:, 0:W_C]
    kc = zc[:, W_C:2 * W_C]
    vc = zc[:, 2 * W_C:3 * W_C]
    kc_ref[...] = kc.reshape(bb, lt, W_C)
    vc_ref[...] = vc.reshape(bb, lt, W_C)
    if dec:
        q_ref[...] = qc.reshape(bb, lt, W_C)
    else:
        qbf_ref[...] = (qc * (HEAD_DIM ** -0.5)).astype(BF16).reshape(bb, lt, W_C)
        kbf_ref[...] = kc.astype(BF16).reshape(bb, lt, W_C)
        vbf_ref[...] = vc.astype(BF16).reshape(bb, lt, W_C)


def _in_proj(dec, x, shift, scale, gpre, w_fused, conv_w, conv_state, smb, alog, lng, lnb, ws, bsb):
    b, L, _ = x.shape
    if dec:
        bb, lt = min(32, b), L
    else:
        bb, lt = 1, min(512, L)
    grid = (b // bb, L // lt)
    tok = lambda w: pl.BlockSpec((bb, lt, w), lambda i, j: (i, j, 0))
    per_b = lambda r, w: pl.BlockSpec((bb, r, w), lambda i, j: (i, 0, 0))
    in_specs = [tok(D_MODEL), per_b(1, D_MODEL), per_b(1, D_MODEL), _const_spec((1, D_MODEL)),
                _const_spec((D_MODEL, N_FUSED)), _const_spec((GDN_CONV, 3 * W_A)),
                per_b(GDN_CONV - 1, 3 * W_A), _const_spec((1, LANES)), _const_spec((1, LANES)),
                _const_spec((1, W_B)), _const_spec((1, W_B)), _const_spec(ws.shape), _const_spec(bsb.shape)]
    sds = lambda w, dt=F32: jax.ShapeDtypeStruct((b, L, w), dt)
    out_shape = [sds(3 * W_A), sds(W_A), sds(LANES), sds(W_B), sds(W_C), sds(W_C),
                 jax.ShapeDtypeStruct((b, SUBLANES, 3 * W_A), F32)]
    out_specs = [tok(3 * W_A), tok(W_A), tok(LANES), tok(W_B), tok(W_C), tok(W_C), per_b(SUBLANES, 3 * W_A)]
    scratch = [pltpu.VMEM((bb, lt + SUBLANES, 3 * W_A), F32)]
    if dec:
        out_shape += [sds(W_B), sds(W_C)]
        out_specs += [tok(W_B), tok(W_C)]
    else:
        out_shape += [sds(W_C, BF16), sds(W_C, BF16), sds(W_C, BF16), sds(LANES)]
        out_specs += [tok(W_C), tok(W_C), tok(W_C), tok(LANES)]
        scratch += [pltpu.VMEM((SUBLANES, LANES), F32)]
    return pl.pallas_call(
        functools.partial(_in_kernel, dec, bb, lt),
        grid=grid, in_specs=in_specs, out_specs=out_specs, out_shape=out_shape,
        scratch_shapes=scratch,
        compiler_params=_params(("arbitrary", "arbitrary")),
        name="in_proj_dec" if dec else "in_proj_seq",
    )(x, shift, scale, gpre, w_fused, conv_w, conv_state, smb, alog, lng, lnb, ws, bsb)


def _tri_inverse(lm, eye, blk):
    dm = jnp.where(blk, lm, 0.0)
    nm = lm - dm
    d2 = _dot_hi(dm, dm)
    d4 = _dot_hi(d2, d2)
    d8 = _dot_hi(d4, d4)
    td = _dot_hi(_dot_hi(eye - dm, eye + d2), _dot_hi(eye + d4, eye + d8))
    mm = _dot_hi(td, nm)
    m2 = _dot_hi(mm, mm)
    return _dot_hi(_dot_hi(eye - mm, eye + m2), td)


def _gdn_seq_kernel(lt, qkv_ref, small_ref, gate_ref, ng_ref, s0_ref, oa_ref, sout_ref, s_sc):
    l = pl.program_id(1)
    C = GDN_CHUNK

    @pl.when(l == 0)
    def _():
        s_sc[...] = s0_ref[0]

    row = _iota2((C, C), 0)
    col = _iota2((C, C), 1)
    tri = row >= col
    strict = row > col
    eye = (row == col).astype(F32)
    blk = (row // 16) == (col // 16)
    trif = tri.astype(F32)

    def body(c, carry):
        r0 = pl.multiple_of(c * C, C)
        qkv = qkv_ref[0, pl.ds(r0, C), :]
        sm = small_ref[0, pl.ds(r0, C), :]
        gate = gate_ref[0, pl.ds(r0, C), :]
        gc_all = _dot_hi(trif, sm)
        gc_t = gc_all.T
        outs = []
        for h in range(H_A):
            sl = slice(h * HEAD_DIM, (h + 1) * HEAD_DIM)
            q = qkv[:, sl]
            k = qkv[:, W_A + h * HEAD_DIM:W_A + (h + 1) * HEAD_DIM]
            v = qkv[:, 2 * W_A + h * HEAD_DIM:2 * W_A + (h + 1) * HEAD_DIM]
            q = q * lax.rsqrt(jnp.sum(q * q, axis=-1, keepdims=True) + EPS) * (HEAD_DIM ** -0.5)
            k = k * lax.rsqrt(jnp.sum(k * k, axis=-1, keepdims=True) + EPS)
            beta = sm[:, SM_BETA + h:SM_BETA + h + 1]
            gcol = gc_all[:, SM_G + h:SM_G + h + 1]
            grow = gc_t[SM_G + h:SM_G + h + 1, :]
            glast = gcol[C - 1:C, :]
            diff = gcol - grow
            decay = jnp.where(tri, jnp.exp(jnp.where(tri, diff, 0.0)), 0.0)
            eg = jnp.exp(gcol)
            kb = k * beta
            kbf = k.astype(BF16)
            lm = jnp.where(strict, _dot_nt(kb.astype(BF16), kbf) * decay, 0.0)
            tinv = _tri_inverse(lm, eye, blk)
            u = _dot_hi(tinv, v * beta)
            w = _dot_hi(tinv, kb * eg)
            attn = _dot_nt(q.astype(BF16), kbf) * decay
            s_old = s_sc[h]
            s_bf = s_old.astype(BF16)
            v_new = u - _dot(w.astype(BF16), s_bf)
            vn_bf = v_new.astype(BF16)
            o = _dot((q * eg).astype(BF16), s_bf) + _dot(attn.astype(BF16), vn_bf)
            k_out = (k * jnp.exp(glast - gcol)).astype(BF16)
            s_sc[h] = s_old * jnp.exp(glast) + _dot_tn(k_out, vn_bf)
            on = o * lax.rsqrt(jnp.mean(o * o, axis=-1, keepdims=True) + EPS)
            outs.append(on * ng_ref[:, sl] * gate[:, sl])
        oa_ref[0, pl.ds(r0, C), :] = jnp.concatenate(outs, axis=-1)
        return carry

    lax.fori_loop(0, lt // C, body, 0)
    sout_ref[0] = s_sc[...]


def _gdn_seq(qkva, small, gate, ng, s0):
    b, L, _ = qkva.shape
    lt = min(512, L)
    tok = lambda w: pl.BlockSpec((1, lt, w), lambda i, j: (i, j, 0))
    st = pl.BlockSpec((1, H_A, HEAD_DIM, HEAD_DIM), lambda i, j: (i, 0, 0, 0))
    return pl.pallas_call(
        functools.partial(_gdn_seq_kernel, lt),
        grid=(b, L // lt),
        in_specs=[tok(3 * W_A), tok(LANES), tok(W_A), _const_spec((1, W_A)), st],
        out_specs=[tok(W_A), st],
        out_shape=[jax.ShapeDtypeStruct((b, L, W_A), F32),
                   jax.ShapeDtypeStruct((b, H_A, HEAD_DIM, HEAD_DIM), F32)],
        scratch_shapes=[pltpu.VMEM((H_A, HEAD_DIM, HEAD_DIM), F32)],
        compiler_params=_params(("arbitrary", "arbitrary")),
        name="gdn_seq",
    )(qkva, small, gate, ng, s0)


def _gdn_dec_kernel(bb, L, qkv_ref, small_ref, gate_ref, ng_ref, s0_ref, oa_ref, sout_ref):
    qkv = qkv_ref[...]
    sm = small_ref[...]
    gate = gate_ref[...]
    qk_t = jnp.swapaxes(qkv[:, :, 0:2 * W_A], 1, 2)
    outs = []
    for h in range(H_A):
        sl = slice(h * HEAD_DIM, (h + 1) * HEAD_DIM)
        q_t = qk_t[:, h * HEAD_DIM:(h + 1) * HEAD_DIM, :]
        k_t = qk_t[:, W_A + h * HEAD_DIM:W_A + (h + 1) * HEAD_DIM, :]
        q_t = q_t * lax.rsqrt(jnp.sum(q_t * q_t, axis=1, keepdims=True) + EPS) * (HEAD_DIM ** -0.5)
        k_t = k_t * lax.rsqrt(jnp.sum(k_t * k_t, axis=1, keepdims=True) + EPS)
        v = qkv[:, :, 2 * W_A + h * HEAD_DIM:2 * W_A + (h + 1) * HEAD_DIM]
        s = s0_ref[:, h]
        rows = []
        for t in range(L):
            kc = k_t[:, :, t:t + 1]
            qc = q_t[:, :, t:t + 1]
            vt = v[:, t:t + 1, :]
            beta = sm[:, t:t + 1, SM_BETA + h:SM_BETA + h + 1]
            a = jnp.exp(sm[:, t:t + 1, SM_G + h:SM_G + h + 1])
            ks = jnp.sum(s * kc, axis=1, keepdims=True)
            s = a * s + kc * (beta * (vt - a * ks))
            rows.append(jnp.sum(s * qc, axis=1, keepdims=True))
        sout_ref[:, h] = s
        o = jnp.concatenate(rows, axis=1)
        on = o * lax.rsqrt(jnp.mean(o * o, axis=-1, keepdims=True) + EPS)
        outs.append(on * ng_ref[:, sl][None] * gate[:, :, sl])
    oa_ref[...] = jnp.concatenate(outs, axis=-1)


def _gdn_dec(qkva, small, gate, ng, s0):
    b, L, _ = qkva.shape
    bb = min(8, b)
    tok = lambda w: pl.BlockSpec((bb, L, w), lambda i: (i, 0, 0))
    st = pl.BlockSpec((bb, H_A, HEAD_DIM, HEAD_DIM), lambda i: (i, 0, 0, 0))
    return pl.pallas_call(
        functools.partial(_gdn_dec_kernel, bb, L),
        grid=(b // bb,),
        in_specs=[tok(3 * W_A), tok(LANES), tok(W_A), _const_spec((1, W_A)), st],
        out_specs=[tok(W_A), st],
        out_shape=[jax.ShapeDtypeStruct((b, L, W_A), F32),
                   jax.ShapeDtypeStruct((b, H_A, HEAD_DIM, HEAD_DIM), F32)],
        compiler_params=_params(("arbitrary",)),
        name="gdn_dec",
    )(qkva, small, gate, ng, s0)


def _fox_seq_kernel(t, q_ref, k_ref, v_ref, dq_ref, dkt_ref, o_ref, m_sc, l_sc, acc_sc):
    qi = pl.program_id(1)
    npair = H_C // 2
    lane = _iota2((1, LANES), 1)
    lo = lane < HEAD_DIM
    m_sc[...] = jnp.full_like(m_sc, NEG_INF)
    l_sc[...] = jnp.zeros_like(l_sc)
    acc_sc[...] = jnp.zeros_like(acc_sc)
    q = q_ref[0]
    dq = dq_ref[0]
    qms = []
    for h in range(H_C):
        qp = q[:, (h // 2) * LANES:(h // 2 + 1) * LANES]
        keep = lo if h % 2 == 0 else jnp.logical_not(lo)
        qms.append(jnp.where(keep, qp, jnp.zeros_like(qp)))
    causal = _iota2((t, t), 1) <= _iota2((t, t), 0)

    def step(ki, masked):
        k0 = pl.multiple_of(ki * t, t)
        for j in range(npair):
            kp = k_ref[0, pl.ds(k0, t), j * LANES:(j + 1) * LANES]
            vp = v_ref[0, pl.ds(k0, t), j * LANES:(j + 1) * LANES]
            pvs, alphas = [], []
            for hh in range(2):
                h = 2 * j + hh
                s = _dot_nt(qms[h], kp)
                s = s + dq[:, SM_LF + h:SM_LF + h + 1] - dkt_ref[0, h:h + 1, pl.ds(k0, t)]
                if masked:
                    s = jnp.where(causal, s, NEG_INF)
                m_old = m_sc[h]
                m_new = jnp.maximum(m_old, jnp.max(s, axis=-1, keepdims=True))
                alpha = jnp.exp(m_old - m_new)
                p = jnp.exp(s - m_new)
                l_sc[h] = alpha * l_sc[h] + jnp.sum(p, axis=-1, keepdims=True)
                m_sc[h] = m_new
                pvs.append(_dot(p.astype(BF16), vp))
                alphas.append(alpha)
            acc_sc[j] = jnp.where(lo, alphas[0], alphas[1]) * acc_sc[j] + jnp.where(lo, pvs[0], pvs[1])

    def body(ki, carry):
        step(ki, False)
        return carry

    lax.fori_loop(0, qi, body, 0)
    step(qi, True)
    for j in range(npair):
        o_ref[0, :, j * LANES:(j + 1) * LANES] = acc_sc[j] / jnp.where(lo, l_sc[2 * j], l_sc[2 * j + 1])


def _fox_seq(qbf, kbf, vbf, dcum, dkt):
    b, L, _ = qbf.shape
    t = min(512, L)
    full = lambda w: pl.BlockSpec((1, L, w), lambda i, j: (i, 0, 0))
    return pl.pallas_call(
        functools.partial(_fox_seq_kernel, t),
        grid=(b, L // t),
        in_specs=[pl.BlockSpec((1, t, W_C), lambda i, j: (i, j, 0)), full(W_C), full(W_C),
                  pl.BlockSpec((1, t, LANES), lambda i, j: (i, j, 0)),
                  pl.BlockSpec((1, SUBLANES, L), lambda i, j: (i, 0, 0))],
        out_specs=pl.BlockSpec((1, t, W_C), lambda i, j: (i, j, 0)),
        out_shape=jax.ShapeDtypeStruct((b, L, W_C), F32),
        scratch_shapes=[pltpu.VMEM((H_C, t, 1), F32), pltpu.VMEM((H_C, t, 1), F32),
                        pltpu.VMEM((H_C // 2, t, LANES), F32)],
        compiler_params=_params(("arbitrary", "arbitrary")),
        name="fox_seq",
    )(qbf, kbf, vbf, dcum, dkt)


def _fox_dec_kernel(L, n_pages, pt_ref, q_ref, kn_ref, vn_ref, sm_ref, lf_ref, *refs):
    k_refs = refs[0:n_pages]
    v_refs = refs[n_pages:2 * n_pages]
    o_ref = refs[2 * n_pages]
    x_sc = refs[2 * n_pages + 1]
    i = pl.program_id(0)
    R = H_C * L
    q = q_ref[0] * (HEAD_DIM ** -0.5)
    rowh = _iota2((R, W_C), 0) // L
    colh = _iota2((R, W_C), 1) // HEAD_DIM
    qbd = jnp.where(rowh == colh, jnp.concatenate([q] * H_C, axis=0), 0.0).astype(BF16)

    for p in range(n_pages):
        pg = pt_ref[i * n_pages + p]
        for h in range(H_C):
            r = h * n_pages + p
            x_sc[r:r + 1, :] = lf_ref[h, pl.ds(pg, 1), :]
    x = x_sc[...]
    n = H_C * n_pages
    later = (_iota2((PAGE, PAGE), 0) > _iota2((PAGE, PAGE), 1)).astype(F32)
    within = _dot_hi(x, later)
    tot = _dot_hi(x, jnp.ones((PAGE, PAGE), F32))
    ri = _iota2((n, n), 0)
    ci = _iota2((n, n), 1)
    later_pages = ((ci // n_pages == ri // n_pages) & (ci % n_pages > ri % n_pages)).astype(F32)
    rsum = within + _dot_hi(later_pages, tot)

    sm = sm_ref[0]
    tri = (_iota2((L, L), 0) >= _iota2((L, L), 1)).astype(F32)
    cq = _dot_hi(tri, sm)
    cq_t = cq.T
    cq_col = jnp.concatenate([cq[:, SM_LF + h:SM_LF + h + 1] for h in range(H_C)], axis=0)
    cq_row = jnp.concatenate([jnp.broadcast_to(cq_t[SM_LF + h:SM_LF + h + 1, :], (L, L))
                              for h in range(H_C)], axis=0)

    s_pages = []
    for p in range(n_pages):
        sp = _dot(qbd, k_refs[p][...].astype(BF16))
        bias = jnp.concatenate(
            [jnp.broadcast_to(rsum[h * n_pages + p:h * n_pages + p + 1, :], (L, PAGE)) for h in range(H_C)],
            axis=0)
        s_pages.append(sp + bias + cq_col)
    s_new = _dot_nt(qbd, kn_ref[0].astype(BF16)) + cq_col - cq_row
    qpos = _iota2((R, L), 0) % L
    s_new = jnp.where(_iota2((R, L), 1) <= qpos, s_new, NEG_INF)

    mx = jnp.max(s_new, axis=-1, keepdims=True)
    for sp in s_pages:
        mx = jnp.maximum(mx, jnp.max(sp, axis=-1, keepdims=True))
    p_new = jnp.exp(s_new - mx)
    den = jnp.sum(p_new, axis=-1, keepdims=True)
    acc = _dot(p_new.astype(BF16), vn_ref[0].astype(BF16))
    for p in range(n_pages):
        pp = jnp.exp(s_pages[p] - mx)
        den = den + jnp.sum(pp, axis=-1, keepdims=True)
        acc = acc + _dot_nt(pp.astype(BF16), v_refs[p][...].astype(BF16))
    acc = jnp.where(rowh == colh, acc / den, 0.0)
    out = acc[0:L]
    for h in range(1, H_C):
        out = out + acc[h * L:(h + 1) * L]
    o_ref[0] = out


def _fox_dec(layer, page_table, q, kn, vn, small, cache_kt, cache_vt, cache_lft):
    b, L, _ = q.shape
    n_pages = page_table.shape[1]
    n_pool = cache_kt.shape[1]
    tok = lambda w: pl.BlockSpec((1, L, w), lambda i, pt: (i, 0, 0))

    def page_spec(p):
        return pl.BlockSpec((None, None, W_C, PAGE), lambda i, pt: (layer, pt[i * n_pages + p], 0, 0))

    in_specs = [tok(W_C), tok(W_C), tok(W_C), tok(LANES),
                pl.BlockSpec((None, H_C, n_pool, PAGE), lambda i, pt: (layer, 0, 0, 0),
                             pipeline_mode=pl.Buffered(1))]
    in_specs += [page_spec(p) for p in range(n_pages)]
    in_specs += [page_spec(p) for p in range(n_pages)]
    grid_spec = pltpu.PrefetchScalarGridSpec(
        num_scalar_prefetch=1, grid=(b,), in_specs=in_specs, out_specs=tok(W_C),
        scratch_shapes=[pltpu.VMEM((H_C * n_pages, PAGE), F32)])
    return pl.pallas_call(
        functools.partial(_fox_dec_kernel, L, n_pages),
        grid_spec=grid_spec,
        out_shape=jax.ShapeDtypeStruct((b, L, W_C), F32),
        compiler_params=_params(("arbitrary",)),
        name="fox_dec",
    )(page_table.reshape(-1), q, kn, vn, small, cache_lft,
      *([cache_kt] * n_pages), *([cache_vt] * n_pages))


def _post_kernel(bb, lt, x_ref, oa_ref, ob_ref, oc_ref, gate1_ref, shift2_ref, scale2_ref, gate2_ref,
                 gpm_ref, gpf_ref, gqf_ref, wo_ref, wu_ref, cw_ref, cst_ref, wd_ref,
                 y_ref, tail_ref, ext_ref, car_ref):
    l = pl.program_id(1)
    m = bb * lt
    keep = FFN_CONV - 1

    def rms(v, g_ref):
        return v * lax.rsqrt(jnp.mean(v * v, axis=-1, keepdims=True) + EPS) * g_ref[...]

    oa = oa_ref[...].reshape(m, W_A).astype(BF16)
    ob = ob_ref[...].reshape(m, W_B).astype(BF16)
    oc = oc_ref[...].reshape(m, W_C).astype(BF16)
    o = (_dot(oa, wo_ref[0:W_A, :]) + _dot(ob, wo_ref[W_A:W_A + W_B, :])
         + _dot(oc, wo_ref[W_A + W_B:W_A + W_B + W_C, :]))
    x1 = x_ref[...] + gate1_ref[...] * rms(o, gpm_ref).reshape(bb, lt, D_MODEL)

    h = rms(x1, gpf_ref) * (1.0 + scale2_ref[...]) + shift2_ref[...]
    h2 = h.reshape(m, D_MODEL).astype(BF16)

    @pl.when(l == 0)
    def _():
        car_ref[...] = jnp.zeros_like(car_ref)
        car_ref[:, SUBLANES - keep:SUBLANES, :] = cst_ref[...]

    def conv_cols(c0, w):
        up = _dot(h2, wu_ref[:, c0:c0 + w]).reshape(bb, lt, w)
        ext_ref[:, 0:SUBLANES, :] = car_ref[:, :, c0:c0 + w]
        ext_ref[:, SUBLANES:, :] = up
        car_ref[:, :, c0:c0 + w] = ext_ref[:, lt:lt + SUBLANES, :]
        y = up * cw_ref[FFN_CONV - 1:FFN_CONV, c0:c0 + w]
        for i in range(keep):
            off = SUBLANES - keep + i
            y = y + ext_ref[:, off:off + lt, :] * cw_ref[i:i + 1, c0:c0 + w]
        return y

    y = jnp.zeros((m, D_MODEL), F32)
    for j in range(D_FF // FF_CHUNK):
        a = conv_cols(j * FF_CHUNK, FF_CHUNK)
        b = conv_cols(D_FF + j * FF_CHUNK, FF_CHUNK)
        g = (jax.nn.gelu(a) * b).reshape(m, FF_CHUNK).astype(BF16)
        y = y + _dot(g, wd_ref[j * FF_CHUNK:(j + 1) * FF_CHUNK, :])
    tail_ref[...] = car_ref[...]
    y_ref[...] = x1 + gate2_ref[...] * rms(y, gqf_ref).reshape(bb, lt, D_MODEL)


def _post(dec, x, oa, ob, oc, gate1, shift2, scale2, gate2, gpm, gpf, gqf, wo, wu, cw, cst, wd):
    b, L, _ = x.shape
    if dec:
        bb, lt = min(16, b), L
    else:
        bb, lt = 1, min(512, L)
    tok = lambda w: pl.BlockSpec((bb, lt, w), lambda i, j: (i, j, 0))
    per_b = lambda r, w: pl.BlockSpec((bb, r, w), lambda i, j: (i, 0, 0))
    vec = _const_spec((1, D_MODEL))
    in_specs = [tok(D_MODEL), tok(W_A), tok(W_B), tok(W_C), per_b(1, D_MODEL), per_b(1, D_MODEL),
                per_b(1, D_MODEL), per_b(1, D_MODEL), vec, vec, vec,
                _const_spec((W_A + W_B + W_C, D_MODEL)), _const_spec((D_MODEL, 2 * D_FF)),
                _const_spec((FFN_CONV, 2 * D_FF)), per_b(FFN_CONV - 1, 2 * D_FF),
                _const_spec((D_FF, D_MODEL))]
    return pl.pallas_call(
        functools.partial(_post_kernel, bb, lt),
        grid=(b // bb, L // lt),
        in_specs=in_specs,
        out_specs=[tok(D_MODEL), per_b(SUBLANES, 2 * D_FF)],
        out_shape=[jax.ShapeDtypeStruct((b, L, D_MODEL), F32),
                   jax.ShapeDtypeStruct((b, SUBLANES, 2 * D_FF), F32)],
        scratch_shapes=[pltpu.VMEM((bb, lt + SUBLANES, FF_CHUNK), F32),
                        pltpu.VMEM((bb, SUBLANES, 2 * D_FF), F32)],
        compiler_params=_params(("arbitrary", "arbitrary")),
        name="post_dec" if dec else "post_seq",
    )(x, oa, ob, oc, gate1, shift2, scale2, gate2, gpm, gpf, gqf, wo, wu, cw, cst, wd)


def _layer_params(l, w_in, gdn_A_log, gdn_dt_bias, fox_f_bias, chunk_w_s, chunk_b_s, dec_len):
    wl = w_in[l]
    a0 = 3 * W_A
    o_beta, o_a, o_gate = a0, a0 + H_A, a0 + 2 * H_A
    o_uv = o_gate + W_A
    o_qc = o_uv + 2 * W_B
    o_f = o_qc + 3 * W_C
    w_small = jnp.concatenate(
        [wl[:, o_beta:o_beta + H_A], wl[:, o_a:o_a + H_A], wl[:, o_f:o_f + H_C],
         jnp.zeros((D_MODEL, LANES - 2 * H_A - H_C), wl.dtype)], axis=1)
    w_fused = jnp.concatenate(
        [wl[:, 0:a0], wl[:, o_gate:o_gate + W_A], wl[:, o_uv:o_uv + 2 * W_B], wl[:, o_qc:o_qc + 3 * W_C],
         w_small], axis=1).astype(BF16)
    z = lambda n: jnp.zeros((n,), F32)
    smb = jnp.concatenate([z(H_A), gdn_dt_bias[l], fox_f_bias[l], z(LANES - 2 * H_A - H_C)]).reshape(1, LANES)
    alog = jnp.concatenate([z(H_A), gdn_A_log[l], z(LANES - 2 * H_A)]).reshape(1, LANES)
    ws = chunk_w_s[l]
    bs = chunk_b_s[l]
    ws_seq = ws.reshape(G_B * CHUNK, CHUNK)
    bsb_seq = jnp.repeat(bs.T, HEAD_DIM, axis=1)
    ws_dec = jnp.repeat(jnp.transpose(ws[:, :dec_len, :dec_len], (2, 1, 0)), HEAD_DIM, axis=2)
    bsb_dec = bsb_seq[:dec_len]
    return w_fused, smb, alog, ws_seq, bsb_seq, ws_dec, bsb_dec


def kernel(x_prompt, x_sample, state_gdn_conv, state_gdn_S, cache_fox_k, cache_fox_v, cache_fox_logf,
           state_ffn_conv, page_table, c_prompt, c_sample, w_ada, b_ada, g_pre_mix, g_post_mix, g_pre_ffn,
           g_post_ffn, w_in, w_out, gdn_conv_w, gdn_A_log, gdn_dt_bias, gdn_norm_g, chunk_ln_g, chunk_ln_b,
           chunk_w_s, chunk_b_s, fox_f_bias, w_up, ffn_conv_w, w_down):
    depth = w_in.shape[0]
    bp, lp, _ = x_prompt.shape
    bs, ls, _ = x_sample.shape
    n_pool = cache_fox_k.shape[1]

    cache_kt = jnp.transpose(cache_fox_k, (0, 1, 3, 4, 2)).reshape(depth, n_pool, W_C, PAGE)
    cache_vt = jnp.transpose(cache_fox_v, (0, 1, 3, 4, 2)).reshape(depth, n_pool, W_C, PAGE)
    cache_lft = jnp.transpose(cache_fox_logf, (0, 3, 1, 2))

    c_all = jnp.concatenate([c_prompt, c_sample], axis=0)
    pad = (-c_all.shape[0]) % SUBLANES
    c_all = jnp.pad(c_all, ((0, pad), (0, 0)))

    vec = lambda a: a.reshape(1, -1)
    outs = {k: [] for k in ("p_conv", "p_S", "p_k", "p_v", "p_lf", "p_ffn",
                            "s_conv", "s_S", "s_k", "s_v", "s_lf", "s_cv", "s_ffn")}
    xp, xs = x_prompt, x_sample
    for l in range(depth):
        mod = _ada(c_all, w_ada[l], b_ada[l])
        mod_p = mod[:bp].reshape(bp, 6, 1, D_MODEL)
        mod_s = mod[bp:bp + bs].reshape(bs, 6, 1, D_MODEL)
        w_fused, smb, alog, ws_seq, bsb_seq, ws_dec, bsb_dec = _layer_params(
            l, w_in, gdn_A_log, gdn_dt_bias, fox_f_bias, chunk_w_s, chunk_b_s, ls)
        ng = jnp.tile(gdn_norm_g[l], H_A).reshape(1, W_A)
        wo = w_out[l].astype(BF16)
        wu = w_up[l].astype(BF16)
        wd = w_down[l].astype(BF16)
        common_in = (vec(g_pre_mix[l]), w_fused, gdn_conv_w[l])
        common_b = (smb, alog, vec(chunk_ln_g[l]), vec(chunk_ln_b[l]))
        post_w = (vec(g_post_mix[l]), vec(g_pre_ffn[l]), vec(g_post_ffn[l]), wo, wu, ffn_conv_w[l])

        (qkva, gate, small, ob, kc, vc, tail, qbf, kbf, vbf, dcum) = _in_proj(
            False, xp, mod_p[:, 0], mod_p[:, 1], *common_in,
            jnp.zeros((bp, GDN_CONV - 1, 3 * W_A), F32), *common_b, ws_seq, bsb_seq)
        oa, s_new = _gdn_seq(qkva, small, gate, ng, jnp.zeros((bp, H_A, HEAD_DIM, HEAD_DIM), F32))
        dkt = jnp.swapaxes(dcum[:, :, SM_LF:SM_LF + SUBLANES], 1, 2)
        oc = _fox_seq(qbf, kbf, vbf, dcum, dkt)
        xp, ftail = _post(False, xp, oa, ob, oc, mod_p[:, 2], mod_p[:, 3], mod_p[:, 4], mod_p[:, 5],
                          *post_w, jnp.zeros((bp, FFN_CONV - 1, 2 * D_FF), F32), wd)
        outs["p_conv"].append(tail[:, SUBLANES - (GDN_CONV - 1):])
        outs["p_S"].append(s_new)
        outs["p_k"].append(kc.reshape(bp, lp // PAGE, PAGE, H_C, HEAD_DIM))
        outs["p_v"].append(vc.reshape(bp, lp // PAGE, PAGE, H_C, HEAD_DIM))
        outs["p_lf"].append(small[:, :, SM_LF:SM_LF + H_C].reshape(bp, lp // PAGE, PAGE, H_C))
        outs["p_ffn"].append(ftail[:, SUBLANES - (FFN_CONV - 1):])

        (qkva, gate, small, ob, kc, vc, tail, vb, qd) = _in_proj(
            True, xs, mod_s[:, 0], mod_s[:, 1], *common_in, state_gdn_conv[l], *common_b, ws_dec, bsb_dec)
        oa, s_new = _gdn_dec(qkva, small, gate, ng, state_gdn_S[l])
        oc = _fox_dec(l, page_table, qd, kc, vc, small, cache_kt, cache_vt, cache_lft)
        xs, ftail = _post(True, xs, oa, ob, oc, mod_s[:, 2], mod_s[:, 3], mod_s[:, 4], mod_s[:, 5],
                          *post_w, state_ffn_conv[l], wd)
        outs["s_conv"].append(tail[:, SUBLANES - (GDN_CONV - 1):])
        outs["s_S"].append(s_new)
        outs["s_k"].append(kc.reshape(bs, ls, H_C, HEAD_DIM))
        outs["s_v"].append(vc.reshape(bs, ls, H_C, HEAD_DIM))
        outs["s_lf"].append(small[:, :, SM_LF:SM_LF + H_C])
        outs["s_cv"].append(vb)
        outs["s_ffn"].append(ftail[:, SUBLANES - (FFN_CONV - 1):])

    st = lambda k: jnp.stack(outs[k])
    return (xp, xs, st("p_conv"), st("p_S"), st("p_k"), st("p_v"), st("p_lf"), st("p_ffn"),
            st("s_conv"), st("s_S"), st("s_k"), st("s_v"), st("s_lf"), st("s_cv"), st("s_ffn"))
```

```python
import functools

import jax
import jax.numpy as jnp
from jax import lax
from jax.experimental import pallas as pl
from jax.experimental.pallas import tpu as pltpu

F32 = jnp.float32
BF16 = jnp.bfloat16

D_MODEL = 1024
HEAD_DIM = 64
H_A = 6
G_B = 4
H_C = 6
W_A = H_A * HEAD_DIM
W_B = G_B * HEAD_DIM
W_C = H_C * HEAD_DIM
GDN_CONV = 4
GDN_CHUNK = 64
CHUNK = 128
D_FF = 2816
FFN_CONV = 3
EPS = 1e-6
NEG_INF = -1e30
PAGE = 128
LOG2E = 1.4426950408889634

LANES = 128
SUBLANES = 8
VMEM_LIMIT = 56 * 1024 * 1024

QA_OFF = 0
GA_OFF = QA_OFF + 3 * W_A
UV_OFF = GA_OFF + W_A
QC_OFF = UV_OFF + 2 * W_B
SM_OFF = QC_OFF + 3 * W_C
N_FUSED = SM_OFF + LANES
SM_BETA = 0
SM_G = H_A
SM_LF = 2 * H_A

FF_CHUNK = D_FF // 2


def _silu(x):
    return x * jax.nn.sigmoid(x)


def _softplus(x):
    return jnp.maximum(x, 0.0) + jnp.log1p(jnp.exp(-jnp.abs(x)))


def _dot(a, b):
    return jnp.dot(a, b, preferred_element_type=F32)


def _dot_hi(a, b):
    return jnp.dot(a, b, preferred_element_type=F32, precision=lax.Precision.HIGHEST)


def _dot_nt(a, b):
    return lax.dot_general(a, b, (((1,), (1,)), ((), ())), preferred_element_type=F32)


def _dot_tn(a, b):
    return lax.dot_general(a, b, (((0,), (0,)), ((), ())), preferred_element_type=F32)


def _iota2(shape, dim):
    return lax.broadcasted_iota(jnp.int32, shape, dim)


def _params(sem):
    return pltpu.CompilerParams(dimension_semantics=sem, vmem_limit_bytes=VMEM_LIMIT)


def _const_spec(shape):
    nd = len(shape)
    return pl.BlockSpec(shape, lambda *_: (0,) * nd, pipeline_mode=pl.Buffered(1))


def _ada_kernel(c_ref, w_ref, b_ref, o_ref):
    c = _silu(c_ref[...]).astype(BF16)
    o_ref[...] = _dot(c, w_ref[...].astype(BF16)) + b_ref[...]


def _ada(c, w, b):
    m = c.shape[0]
    n = w.shape[1]
    tn = 1024
    return pl.pallas_call(
        _ada_kernel,
        grid=(n // tn,),
        in_specs=[pl.BlockSpec((m, D_MODEL), lambda j: (0, 0)),
                  pl.BlockSpec((D_MODEL, tn), lambda j: (0, j)),
                  pl.BlockSpec((1, tn), lambda j: (0, j))],
        out_specs=pl.BlockSpec((m, tn), lambda j: (0, j)),
        out_shape=jax.ShapeDtypeStruct((m, n), F32),
        compiler_params=_params(("arbitrary",)),
        name="ada_mod",
    )(c, w, b.reshape(1, n))


def _in_kernel(dec, bb, lt, *refs):
    (x_ref, shift_ref, scale_ref, gpre_ref, w_ref, cw_ref, cst_ref, smb_ref, alog_ref,
     lng_ref, lnb_ref, ws_ref, bsb_ref) = refs[:13]
    if dec:
        (qkva_ref, gate_ref, small_ref, ob_ref, kc_ref, vc_ref, tail_ref, vb_ref, q_ref,
         ext_ref) = refs[13:]
    else:
        (qkva_ref, gate_ref, small_ref, ob_ref, kc_ref, vc_ref, tail_ref, qbf_ref, kbf_ref,
         vbf_ref, dcum_ref, ext_ref, dcar_ref) = refs[13:]
    l = pl.program_id(1)
    m = bb * lt

    x = x_ref[...]
    ms = jnp.mean(x * x, axis=-1, keepdims=True)
    h = x * lax.rsqrt(ms + EPS) * gpre_ref[...]
    h = h * (1.0 + scale_ref[...]) + shift_ref[...]
    h2 = h.reshape(m, D_MODEL).astype(BF16)

    wa = 3 * W_A
    za = _dot(h2, w_ref[:, QA_OFF:QA_OFF + wa]).reshape(bb, lt, wa)

    @pl.when(l == 0)
    def _():
        ext_ref[:, SUBLANES - (GDN_CONV - 1):SUBLANES, :] = cst_ref[...]

    @pl.when(l > 0)
    def _():
        ext_ref[:, 0:SUBLANES, :] = ext_ref[:, lt:lt + SUBLANES, :]

    ext_ref[:, SUBLANES:, :] = za
    y = za * cw_ref[GDN_CONV - 1:GDN_CONV, :]
    for i in range(GDN_CONV - 1):
        o = SUBLANES - (GDN_CONV - 1) + i
        y = y + ext_ref[:, o:o + lt, :] * cw_ref[i:i + 1, :]
    qkva_ref[...] = _silu(y)
    tail_ref[...] = ext_ref[:, lt:lt + SUBLANES, :]

    gate_ref[...] = _silu(_dot(h2, w_ref[:, GA_OFF:GA_OFF + W_A])).reshape(bb, lt, W_A)

    zs = _dot(h2, w_ref[:, SM_OFF:SM_OFF + LANES]) + smb_ref[...]
    lane = _iota2((1, LANES), 1)
    beta = jax.nn.sigmoid(zs)
    gval = -jnp.exp(alog_ref[...]) * _softplus(zs)
    lf = -_softplus(-zs)
    small = jnp.where(lane < SM_G, beta,
                      jnp.where(lane < SM_LF, gval, jnp.where(lane < SM_LF + H_C, lf, 0.0)))
    small_ref[...] = small.reshape(bb, lt, LANES)

    if not dec:
        tri = (_iota2((lt, lt), 0) >= _iota2((lt, lt), 1)).astype(F32)

        @pl.when(l == 0)
        def _():
            dcar_ref[...] = jnp.zeros_like(dcar_ref)

        dc = _dot_hi(tri, small) + dcar_ref[0:1, :]
        dcum_ref[...] = dc.reshape(bb, lt, LANES)
        dcar_ref[0:1, :] = dc[lt - 1:lt, :]

    uv = jax.nn.gelu(_dot(h2, w_ref[:, UV_OFF:UV_OFF + 2 * W_B]))
    u = uv[:, :W_B]
    v = uv[:, W_B:]
    mu = jnp.mean(v, axis=-1, keepdims=True)
    vc0 = v - mu
    var = jnp.mean(vc0 * vc0, axis=-1, keepdims=True)
    vb = vc0 * lax.rsqrt(var + EPS) * lng_ref[...] + lnb_ref[...]
    if dec:
        vb3 = vb.reshape(bb, lt, W_B)
        vb_ref[...] = vb3
        trow = _iota2((lt, W_B), 0)
        mix = jnp.zeros((bb, lt, W_B), F32)
        for s in range(lt):
            coef = jnp.where(trow >= s, ws_ref[s], 0.0)
            mix = mix + coef[None] * vb3[:, s:s + 1, :]
        ob_ref[...] = u.reshape(bb, lt, W_B) * (mix + bsb_ref[...][None])
    else:
        r_t = _iota2((G_B * CHUNK, CHUNK), 0) % CHUNK
        r_s = _iota2((G_B * CHUNK, CHUNK), 1)
        wst = jnp.where(r_s <= r_t, ws_ref[...], 0.0).astype(BF16)
        grp = _iota2((1, W_B), 1) // HEAD_DIM
        for c in range(lt // CHUNK):
            rows = slice(c * CHUNK, (c + 1) * CHUNK)
            r = _dot(wst, vb[rows].astype(BF16))
            mix = jnp.zeros((CHUNK, W_B), F32)
            for g in range(G_B):
                mix = jnp.where(grp == g, r[g * CHUNK:(g + 1) * CHUNK], mix)
            ob_ref[0, rows, :] = u[rows] * (mix + bsb_ref[...])

    zc = _dot(h2, w_ref[:, QC_OFF:QC_OFF + 3 * W_C])
    qc = zc[:, 0:W_C]
    kc = zc[:, W_C:2 * W_C]
    vc = zc[:, 2 * W_C:3 * W_C]
    kc_ref[...] = kc.reshape(bb, lt, W_C)
    vc_ref[...] = vc.reshape(bb, lt, W_C)
    if dec:
        q_ref[...] = qc.reshape(bb, lt, W_C)
    else:
        qbf_ref[...] = (qc * (LOG2E * HEAD_DIM ** -0.5)).astype(BF16).reshape(bb, lt, W_C)
        kbf_ref[...] = kc.astype(BF16).reshape(bb, lt, W_C)
        vbf_ref[...] = vc.astype(BF16).reshape(bb, lt, W_C)


def _in_proj(dec, x, shift, scale, gpre, w_fused, conv_w, conv_state, smb, alog, lng, lnb, ws, bsb):
    b, L, _ = x.shape
    if dec:
        bb, lt = min(32, b), L
    else:
        bb, lt = 1, min(512, L)
    grid = (b // bb, L // lt)
    tok = lambda w: pl.BlockSpec((bb, lt, w), lambda i, j: (i, j, 0))
    per_b = lambda r, w: pl.BlockSpec((bb, r, w), lambda i, j: (i, 0, 0))
    in_specs = [tok(D_MODEL), per_b(1, D_MODEL), per_b(1, D_MODEL), _const_spec((1, D_MODEL)),
                _const_spec((D_MODEL, N_FUSED)), _const_spec((GDN_CONV, 3 * W_A)),
                per_b(GDN_CONV - 1, 3 * W_A), _const_spec((1, LANES)), _const_spec((1, LANES)),
                _const_spec((1, W_B)), _const_spec((1, W_B)), _const_spec(ws.shape), _const_spec(bsb.shape)]
    sds = lambda w, dt=F32: jax.ShapeDtypeStruct((b, L, w), dt)
    out_shape = [sds(3 * W_A), sds(W_A), sds(LANES), sds(W_B), sds(W_C), sds(W_C),
                 jax.ShapeDtypeStruct((b, SUBLANES, 3 * W_A), F32)]
    out_specs = [tok(3 * W_A), tok(W_A), tok(LANES), tok(W_B), tok(W_C), tok(W_C), per_b(SUBLANES, 3 * W_A)]
    scratch = [pltpu.VMEM((bb, lt + SUBLANES, 3 * W_A), F32)]
    if dec:
        out_shape += [sds(W_B), sds(W_C)]
        out_specs += [tok(W_B), tok(W_C)]
    else:
        out_shape += [sds(W_C, BF16), sds(W_C, BF16), sds(W_C, BF16), sds(LANES)]
        out_specs += [tok(W_C), tok(W_C), tok(W_C), tok(LANES)]
        scratch += [pltpu.VMEM((SUBLANES, LANES), F32)]
    return pl.pallas_call(
        functools.partial(_in_kernel, dec, bb, lt),
        grid=grid, in_specs=in_specs, out_specs=out_specs, out_shape=out_shape,
        scratch_shapes=scratch,
        compiler_params=_params(("arbitrary", "arbitrary")),
        name="in_proj_dec" if dec else "in_proj_seq",
    )(x, shift, scale, gpre, w_fused, conv_w, conv_state, smb, alog, lng, lnb, ws, bsb)


GDN_GROUP = 256
N_PAIR = H_A // 2


def _tri_inverse_m1(lm, blk):
    bf = lambda x: x.astype(BF16)
    dm = jnp.where(blk, lm, 0.0)
    nm = lm - dm
    dmb = bf(dm)
    d2 = _dot(dmb, dmb)
    d2b = bf(d2)
    d4 = _dot(d2b, d2b)
    d4b = bf(d4)
    d8 = _dot(d4b, d4b)
    a1 = d2 - dm - _dot(dmb, d2b)
    a2 = d4 + d8 + _dot(d4b, bf(d8))
    et = a1 + a2 + _dot(bf(a1), bf(a2))
    etb = bf(et)
    mm = nm + _dot(etb, bf(nm))
    mmb = bf(mm)
    m2 = _dot(mmb, mmb)
    a3 = m2 - mm - _dot(mmb, bf(m2))
    return a3 + et + _dot(bf(a3), etb)


def _pair_cols(lo, a, c0, c1):
    return jnp.where(lo, a[:, c0:c0 + 1], a[:, c1:c1 + 1])


def _pair_rsqrt_norm(lo, x, scale):
    x2 = x * x
    s_lo = jnp.sum(jnp.where(lo, x2, 0.0), axis=-1, keepdims=True)
    s_hi = jnp.sum(jnp.where(lo, 0.0, x2), axis=-1, keepdims=True)
    return jnp.where(lo, lax.rsqrt(s_lo * scale + EPS), lax.rsqrt(s_hi * scale + EPS))


def _gdn_pre_kernel(T, qkv_ref, small_ref, u_ref, w_ref, attn_ref, qin_ref, kout_ref, egl_ref):
    C = GDN_CHUNK
    row = _iota2((T, T), 0)
    col = _iota2((T, T), 1)
    same = (row // C) == (col // C)
    bd_tri = same & (row >= col)
    bd_strict = same & (row > col)
    blk = (row // 16) == (col // 16)
    lane = _iota2((1, LANES), 1)
    lo = lane < HEAD_DIM
    frow = _iota2((T, LANES), 0) % C
    fcol = _iota2((T, LANES), 1)
    folds = ((frow == fcol).astype(BF16), (frow + C == fcol).astype(BF16))

    sm = small_ref[0]
    gc_all = _dot_hi(bd_tri.astype(F32), sm)
    gl_all = _dot_hi((col == (row // C) * C + (C - 1)).astype(F32), gc_all)
    gc_t = gc_all.T
    for j in range(N_PAIR):
        sl = slice(j * LANES, (j + 1) * LANES)
        qp = qkv_ref[0, :, j * LANES:(j + 1) * LANES]
        kp = qkv_ref[0, :, W_A + j * LANES:W_A + (j + 1) * LANES]
        vp = qkv_ref[0, :, 2 * W_A + j * LANES:2 * W_A + (j + 1) * LANES]
        qn = qp * _pair_rsqrt_norm(lo, qp, 1.0) * (HEAD_DIM ** -0.5)
        kn = kp * _pair_rsqrt_norm(lo, kp, 1.0)
        h0, h1 = 2 * j, 2 * j + 1
        beta_p = _pair_cols(lo, sm, SM_BETA + h0, SM_BETA + h1)
        gc_p = _pair_cols(lo, gc_all, SM_G + h0, SM_G + h1)
        gl_p = _pair_cols(lo, gl_all, SM_G + h0, SM_G + h1)
        eg_p = jnp.exp(gc_p)
        kb = kn * beta_p
        kn_bf = kn.astype(BF16)
        rhs_u = vp * beta_p
        rhs_w = kb * eg_p
        rhs_b = jnp.concatenate([rhs_u, rhs_w], axis=1).astype(BF16)
        us, ws = [], []
        attn_c = jnp.zeros((T, LANES), F32)
        for hh in range(2):
            h = 2 * j + hh
            keep = lo if hh == 0 else jnp.logical_not(lo)
            gcol = gc_all[:, SM_G + h:SM_G + h + 1]
            grow = gc_t[SM_G + h:SM_G + h + 1, :]
            decay = jnp.where(bd_tri, jnp.exp(jnp.where(bd_tri, gcol - grow, 0.0)), 0.0)
            g = _dot_nt(jnp.where(keep, kb, 0.0).astype(BF16), kn_bf)
            tm1 = _tri_inverse_m1(jnp.where(bd_strict, g * decay, 0.0), blk).astype(BF16)
            uw = _dot(tm1, rhs_b)
            us.append(rhs_u + uw[:, :LANES])
            ws.append(rhs_w + uw[:, LANES:])
            qk = _dot_nt(jnp.where(keep, qn, 0.0).astype(BF16), kn_bf)
            attn_c = attn_c + _dot((qk * decay).astype(BF16), folds[hh])
        u_ref[0, :, sl] = jnp.where(lo, us[0], us[1])
        w_ref[0, :, sl] = jnp.where(lo, ws[0], ws[1]).astype(BF16)
        attn_ref[0, :, sl] = attn_c.astype(BF16)
        qin_ref[0, :, sl] = (qn * eg_p).astype(BF16)
        kout_ref[0, :, sl] = (kn * jnp.exp(gl_p - gc_p)).astype(BF16)
        egl_ref[0, :, sl] = jnp.exp(gl_p)


def _gdn_pre(qkva, small):
    b, L, _ = qkva.shape
    T = min(GDN_GROUP, L)
    tok = lambda w: pl.BlockSpec((1, T, w), lambda i, j: (i, j, 0))
    sds = lambda dt: jax.ShapeDtypeStruct((b, L, W_A), dt)
    return pl.pallas_call(
        functools.partial(_gdn_pre_kernel, T),
        grid=(b, L // T),
        in_specs=[tok(3 * W_A), tok(LANES)],
        out_specs=[tok(W_A)] * 6,
        out_shape=[sds(F32), sds(BF16), sds(BF16), sds(BF16), sds(BF16), sds(F32)],
        compiler_params=_params(("arbitrary", "arbitrary")),
        name="gdn_pre",
    )(qkva, small)


def _gdn_scan_kernel(nb, lt, u_ref, w_ref, attn_ref, qin_ref, kout_ref, egl_ref, gate_ref, ng_ref, s0_ref,
                     oa_ref, sout_ref, s_sc):
    l = pl.program_id(0)
    C = GDN_CHUNK

    @pl.when(l == 0)
    def _():
        s_sc[...] = s0_ref[...]

    lane = _iota2((1, LANES), 1)
    lo = lane < HEAD_DIM
    bd = (_iota2((LANES, LANES), 0) // HEAD_DIM) == (_iota2((LANES, LANES), 1) // HEAD_DIM)

    def body(c, carry):
        r0 = pl.multiple_of(c * C, C)
        rows = pl.ds(r0, C)
        for b in range(nb):
            for j in range(N_PAIR):
                sl = slice(j * LANES, (j + 1) * LANES)
                s_old = s_sc[b, j]
                s_bf = s_old.astype(BF16)
                vn = u_ref[b, rows, sl] - _dot(w_ref[b, rows, sl], s_bf)
                vn_bf = vn.astype(BF16)
                vv = jnp.concatenate([vn_bf, vn_bf], axis=0)
                at = attn_ref[b, rows, sl]
                zero = jnp.zeros_like(at)
                o = _dot(qin_ref[b, rows, sl], s_bf) + jnp.where(
                    lo, _dot(jnp.where(lo, at, zero), vv), _dot(jnp.where(lo, zero, at), vv))
                kv = _dot_tn(kout_ref[b, rows, sl], vn_bf)
                s_sc[b, j] = s_old * egl_ref[b, pl.ds(r0, 1), sl] + jnp.where(bd, kv, 0.0)
                on = o * _pair_rsqrt_norm(lo, o, 1.0 / HEAD_DIM)
                oa_ref[b, rows, sl] = on * ng_ref[:, sl] * gate_ref[b, rows, sl]
        return carry

    lax.fori_loop(0, lt // C, body, 0)
    sout_ref[...] = s_sc[...]


def _gdn_scan(u, w, attn, qin, kout, egl, gate, ng, s0_bd):
    b, L, _ = u.shape
    lt = min(512, L)
    tok = pl.BlockSpec((b, lt, W_A), lambda i: (0, i, 0))
    st = pl.BlockSpec((b, N_PAIR, LANES, LANES), lambda i: (0, 0, 0, 0))
    return pl.pallas_call(
        functools.partial(_gdn_scan_kernel, b, lt),
        grid=(L // lt,),
        in_specs=[tok] * 7 + [_const_spec((1, W_A)), st],
        out_specs=[tok, st],
        out_shape=[jax.ShapeDtypeStruct((b, L, W_A), F32),
                   jax.ShapeDtypeStruct((b, N_PAIR, LANES, LANES), F32)],
        scratch_shapes=[pltpu.VMEM((b, N_PAIR, LANES, LANES), F32)],
        compiler_params=_params(("arbitrary",)),
        name="gdn_scan",
    )(u, w, attn, qin, kout, egl, gate, ng, s0_bd)


def _to_pair_blockdiag(s):
    b = s.shape[0]
    s = s.reshape(b, N_PAIR, 2, HEAD_DIM, HEAD_DIM)
    z = jnp.zeros_like(s[:, :, 0])
    top = jnp.concatenate([s[:, :, 0], z], axis=-1)
    bot = jnp.concatenate([z, s[:, :, 1]], axis=-1)
    return jnp.concatenate([top, bot], axis=-2)


def _from_pair_blockdiag(sbd):
    b = sbd.shape[0]
    s = jnp.stack([sbd[:, :, :HEAD_DIM, :HEAD_DIM], sbd[:, :, HEAD_DIM:, HEAD_DIM:]], axis=2)
    return s.reshape(b, H_A, HEAD_DIM, HEAD_DIM)


def _gdn_seq(qkva, small, gate, ng, s0):
    u, w, attn, qin, kout, egl = _gdn_pre(qkva, small)
    oa, sbd = _gdn_scan(u, w, attn, qin, kout, egl, gate, ng, _to_pair_blockdiag(s0))
    return oa, _from_pair_blockdiag(sbd)


def _gdn_dec_kernel(bb, L, qkv_ref, small_ref, gate_ref, ng_ref, s0_ref, oa_ref, sout_ref):
    qkv = qkv_ref[...]
    sm = small_ref[...]
    gate = gate_ref[...]
    qk_t = jnp.swapaxes(qkv[:, :, 0:2 * W_A], 1, 2)
    outs = []
    for h in range(H_A):
        sl = slice(h * HEAD_DIM, (h + 1) * HEAD_DIM)
        q_t = qk_t[:, h * HEAD_DIM:(h + 1) * HEAD_DIM, :]
        k_t = qk_t[:, W_A + h * HEAD_DIM:W_A + (h + 1) * HEAD_DIM, :]
        q_t = q_t * lax.rsqrt(jnp.sum(q_t * q_t, axis=1, keepdims=True) + EPS) * (HEAD_DIM ** -0.5)
        k_t = k_t * lax.rsqrt(jnp.sum(k_t * k_t, axis=1, keepdims=True) + EPS)
        v = qkv[:, :, 2 * W_A + h * HEAD_DIM:2 * W_A + (h + 1) * HEAD_DIM]
        s = s0_ref[:, h]
        rows = []
        for t in range(L):
            kc = k_t[:, :, t:t + 1]
            qc = q_t[:, :, t:t + 1]
            vt = v[:, t:t + 1, :]
            beta = sm[:, t:t + 1, SM_BETA + h:SM_BETA + h + 1]
            a = jnp.exp(sm[:, t:t + 1, SM_G + h:SM_G + h + 1])
            ks = jnp.sum(s * kc, axis=1, keepdims=True)
            s = a * s + kc * (beta * (vt - a * ks))
            rows.append(jnp.sum(s * qc, axis=1, keepdims=True))
        sout_ref[:, h] = s
        o = jnp.concatenate(rows, axis=1)
        on = o * lax.rsqrt(jnp.mean(o * o, axis=-1, keepdims=True) + EPS)
        outs.append(on * ng_ref[:, sl][None] * gate[:, :, sl])
    oa_ref[...] = jnp.concatenate(outs, axis=-1)


def _gdn_dec(qkva, small, gate, ng, s0):
    b, L, _ = qkva.shape
    bb = min(8, b)
    tok = lambda w: pl.BlockSpec((bb, L, w), lambda i: (i, 0, 0))
    st = pl.BlockSpec((bb, H_A, HEAD_DIM, HEAD_DIM), lambda i: (i, 0, 0, 0))
    return pl.pallas_call(
        functools.partial(_gdn_dec_kernel, bb, L),
        grid=(b // bb,),
        in_specs=[tok(3 * W_A), tok(LANES), tok(W_A), _const_spec((1, W_A)), st],
        out_specs=[tok(W_A), st],
        out_shape=[jax.ShapeDtypeStruct((b, L, W_A), F32),
                   jax.ShapeDtypeStruct((b, H_A, HEAD_DIM, HEAD_DIM), F32)],
        compiler_params=_params(("arbitrary",)),
        name="gdn_dec",
    )(qkva, small, gate, ng, s0)


def _fox_seq_kernel(t, q_ref, k_ref, v_ref, dq_ref, dkt_ref, o_ref, m_sc, l_sc, acc_sc, dqb_sc):
    qi = pl.program_id(1)
    npair = H_C // 2
    lane = _iota2((1, LANES), 1)
    lo = lane < HEAD_DIM
    m_sc[...] = jnp.full_like(m_sc, NEG_INF)
    l_sc[...] = jnp.zeros_like(l_sc)
    acc_sc[...] = jnp.zeros_like(acc_sc)
    q = q_ref[0]
    dq = dq_ref[0] * LOG2E
    qms = []
    for h in range(H_C):
        qp = q[:, (h // 2) * LANES:(h // 2 + 1) * LANES]
        keep = lo if h % 2 == 0 else jnp.logical_not(lo)
        qms.append(jnp.where(keep, qp, jnp.zeros_like(qp)))
        dqb_sc[h] = jnp.broadcast_to(dq[:, SM_LF + h:SM_LF + h + 1], (t, LANES))
    causal = _iota2((t, t), 1) <= _iota2((t, t), 0)
    rep = t // LANES

    def step(ki, masked):
        k0 = pl.multiple_of(ki * t, t)
        for j in range(npair):
            kp = k_ref[0, pl.ds(k0, t), j * LANES:(j + 1) * LANES]
            vp = v_ref[0, pl.ds(k0, t), j * LANES:(j + 1) * LANES]
            pvs, alphas = [], []
            for hh in range(2):
                h = 2 * j + hh
                s = _dot_nt(qms[h], kp) - dkt_ref[0, h:h + 1, pl.ds(k0, t)] * LOG2E
                if masked:
                    s = jnp.where(causal, s, NEG_INF)
                dqb = dqb_sc[h]
                m_old = m_sc[h]
                m_new = jnp.maximum(m_old, jnp.max(s, axis=-1, keepdims=True) + dqb)
                alpha = jnp.exp2(m_old - m_new)
                p = jnp.exp2(s - pltpu.repeat(m_new - dqb, rep, axis=1))
                l_sc[h] = alpha * l_sc[h] + jnp.sum(p, axis=-1, keepdims=True)
                m_sc[h] = m_new
                pvs.append(_dot(p.astype(BF16), vp))
                alphas.append(alpha)
            acc_sc[j] = jnp.where(lo, alphas[0], alphas[1]) * acc_sc[j] + jnp.where(lo, pvs[0], pvs[1])

    def body(ki, carry):
        step(ki, False)
        return carry

    lax.fori_loop(0, qi, body, 0)
    step(qi, True)
    for j in range(npair):
        o_ref[0, :, j * LANES:(j + 1) * LANES] = acc_sc[j] / jnp.where(lo, l_sc[2 * j], l_sc[2 * j + 1])


def _fox_seq(qbf, kbf, vbf, dcum, dkt):
    b, L, _ = qbf.shape
    t = min(512, L)
    full = lambda w: pl.BlockSpec((1, L, w), lambda i, j: (i, 0, 0))
    return pl.pallas_call(
        functools.partial(_fox_seq_kernel, t),
        grid=(b, L // t),
        in_specs=[pl.BlockSpec((1, t, W_C), lambda i, j: (i, j, 0)), full(W_C), full(W_C),
                  pl.BlockSpec((1, t, LANES), lambda i, j: (i, j, 0)),
                  pl.BlockSpec((1, SUBLANES, L), lambda i, j: (i, 0, 0))],
        out_specs=pl.BlockSpec((1, t, W_C), lambda i, j: (i, j, 0)),
        out_shape=jax.ShapeDtypeStruct((b, L, W_C), F32),
        scratch_shapes=[pltpu.VMEM((H_C, t, LANES), F32), pltpu.VMEM((H_C, t, LANES), F32),
                        pltpu.VMEM((H_C // 2, t, LANES), F32), pltpu.VMEM((H_C, t, LANES), F32)],
        compiler_params=_params(("arbitrary", "arbitrary")),
        name="fox_seq",
    )(qbf, kbf, vbf, dcum, dkt)


def _fox_dec_kernel(L, n_pages, pt_ref, q_ref, kn_ref, vn_ref, sm_ref, lf_ref, *refs):
    k_refs = refs[0:n_pages]
    v_refs = refs[n_pages:2 * n_pages]
    o_ref = refs[2 * n_pages]
    x_sc = refs[2 * n_pages + 1]
    i = pl.program_id(0)
    R = H_C * L
    q = q_ref[0] * (HEAD_DIM ** -0.5)
    rowh = _iota2((R, W_C), 0) // L
    colh = _iota2((R, W_C), 1) // HEAD_DIM
    qbd = jnp.where(rowh == colh, jnp.concatenate([q] * H_C, axis=0), 0.0).astype(BF16)

    for p in range(n_pages):
        pg = pt_ref[i * n_pages + p]
        for h in range(H_C):
            r = h * n_pages + p
            x_sc[r:r + 1, :] = lf_ref[h, pl.ds(pg, 1), :]
    x = x_sc[...]
    n = H_C * n_pages
    later = (_iota2((PAGE, PAGE), 0) > _iota2((PAGE, PAGE), 1)).astype(F32)
    within = _dot_hi(x, later)
    tot = _dot_hi(x, jnp.ones((PAGE, PAGE), F32))
    ri = _iota2((n, n), 0)
    ci = _iota2((n, n), 1)
    later_pages = ((ci // n_pages == ri // n_pages) & (ci % n_pages > ri % n_pages)).astype(F32)
    rsum = within + _dot_hi(later_pages, tot)

    sm = sm_ref[0]
    tri = (_iota2((L, L), 0) >= _iota2((L, L), 1)).astype(F32)
    cq = _dot_hi(tri, sm)
    cq_t = cq.T
    cq_col = jnp.concatenate([cq[:, SM_LF + h:SM_LF + h + 1] for h in range(H_C)], axis=0)
    cq_row = jnp.concatenate([jnp.broadcast_to(cq_t[SM_LF + h:SM_LF + h + 1, :], (L, L))
                              for h in range(H_C)], axis=0)

    s_pages = []
    for p in range(n_pages):
        sp = _dot(qbd, k_refs[p][...].astype(BF16))
        bias = jnp.concatenate(
            [jnp.broadcast_to(rsum[h * n_pages + p:h * n_pages + p + 1, :], (L, PAGE)) for h in range(H_C)],
            axis=0)
        s_pages.append(sp + bias + cq_col)
    s_new = _dot_nt(qbd, kn_ref[0].astype(BF16)) + cq_col - cq_row
    qpos = _iota2((R, L), 0) % L
    s_new = jnp.where(_iota2((R, L), 1) <= qpos, s_new, NEG_INF)

    mx = jnp.max(s_new, axis=-1, keepdims=True)
    for sp in s_pages:
        mx = jnp.maximum(mx, jnp.max(sp, axis=-1, keepdims=True))
    p_new = jnp.exp(s_new - mx)
    den = jnp.sum(p_new, axis=-1, keepdims=True)
    acc = _dot(p_new.astype(BF16), vn_ref[0].astype(BF16))
    for p in range(n_pages):
        pp = jnp.exp(s_pages[p] - mx)
        den = den + jnp.sum(pp, axis=-1, keepdims=True)
        acc = acc + _dot_nt(pp.astype(BF16), v_refs[p][...].astype(BF16))
    acc = jnp.where(rowh == colh, acc / den, 0.0)
    out = acc[0:L]
    for h in range(1, H_C):
        out = out + acc[h * L:(h + 1) * L]
    o_ref[0] = out


def _fox_dec(layer, page_table, q, kn, vn, small, cache_kt, cache_vt, cache_lft):
    b, L, _ = q.shape
    n_pages = page_table.shape[1]
    n_pool = cache_kt.shape[1]
    tok = lambda w: pl.BlockSpec((1, L, w), lambda i, pt: (i, 0, 0))

    def page_spec(p):
        return pl.BlockSpec((None, None, W_C, PAGE), lambda i, pt: (layer, pt[i * n_pages + p], 0, 0))

    in_specs = [tok(W_C), tok(W_C), tok(W_C), tok(LANES),
                pl.BlockSpec((None, H_C, n_pool, PAGE), lambda i, pt: (layer, 0, 0, 0),
                             pipeline_mode=pl.Buffered(1))]
    in_specs += [page_spec(p) for p in range(n_pages)]
    in_specs += [page_spec(p) for p in range(n_pages)]
    grid_spec = pltpu.PrefetchScalarGridSpec(
        num_scalar_prefetch=1, grid=(b,), in_specs=in_specs, out_specs=tok(W_C),
        scratch_shapes=[pltpu.VMEM((H_C * n_pages, PAGE), F32)])
    return pl.pallas_call(
        functools.partial(_fox_dec_kernel, L, n_pages),
        grid_spec=grid_spec,
        out_shape=jax.ShapeDtypeStruct((b, L, W_C), F32),
        compiler_params=_params(("arbitrary",)),
        name="fox_dec",
    )(page_table.reshape(-1), q, kn, vn, small, cache_lft,
      *([cache_kt] * n_pages), *([cache_vt] * n_pages))


def _post_kernel(bb, lt, x_ref, oa_ref, ob_ref, oc_ref, gate1_ref, shift2_ref, scale2_ref, gate2_ref,
                 gpm_ref, gpf_ref, gqf_ref, wo_ref, wu_ref, cw_ref, cst_ref, wd_ref,
                 y_ref, tail_ref, ext_ref, car_ref):
    l = pl.program_id(1)
    m = bb * lt
    keep = FFN_CONV - 1

    def rms(v, g_ref):
        return v * lax.rsqrt(jnp.mean(v * v, axis=-1, keepdims=True) + EPS) * g_ref[...]

    oa = oa_ref[...].reshape(m, W_A).astype(BF16)
    ob = ob_ref[...].reshape(m, W_B).astype(BF16)
    oc = oc_ref[...].reshape(m, W_C).astype(BF16)
    o = (_dot(oa, wo_ref[0:W_A, :]) + _dot(ob, wo_ref[W_A:W_A + W_B, :])
         + _dot(oc, wo_ref[W_A + W_B:W_A + W_B + W_C, :]))
    x1 = x_ref[...] + gate1_ref[...] * rms(o, gpm_ref).reshape(bb, lt, D_MODEL)

    h = rms(x1, gpf_ref) * (1.0 + scale2_ref[...]) + shift2_ref[...]
    h2 = h.reshape(m, D_MODEL).astype(BF16)

    @pl.when(l == 0)
    def _():
        car_ref[...] = jnp.zeros_like(car_ref)
        car_ref[:, SUBLANES - keep:SUBLANES, :] = cst_ref[...]

    def conv_cols(c0, w):
        up = _dot(h2, wu_ref[:, c0:c0 + w]).reshape(bb, lt, w)
        ext_ref[:, 0:SUBLANES, :] = car_ref[:, :, c0:c0 + w]
        ext_ref[:, SUBLANES:, :] = up
        car_ref[:, :, c0:c0 + w] = ext_ref[:, lt:lt + SUBLANES, :]
        y = up * cw_ref[FFN_CONV - 1:FFN_CONV, c0:c0 + w]
        for i in range(keep):
            off = SUBLANES - keep + i
            y = y + ext_ref[:, off:off + lt, :] * cw_ref[i:i + 1, c0:c0 + w]
        return y

    y = jnp.zeros((m, D_MODEL), F32)
    for j in range(D_FF // FF_CHUNK):
        a = conv_cols(j * FF_CHUNK, FF_CHUNK)
        b = conv_cols(D_FF + j * FF_CHUNK, FF_CHUNK)
        g = (jax.nn.gelu(a) * b).reshape(m, FF_CHUNK).astype(BF16)
        y = y + _dot(g, wd_ref[j * FF_CHUNK:(j + 1) * FF_CHUNK, :])
    tail_ref[...] = car_ref[...]
    y_ref[...] = x1 + gate2_ref[...] * rms(y, gqf_ref).reshape(bb, lt, D_MODEL)


def _post(dec, x, oa, ob, oc, gate1, shift2, scale2, gate2, gpm, gpf, gqf, wo, wu, cw, cst, wd):
    b, L, _ = x.shape
    if dec:
        bb, lt = min(16, b), L
    else:
        bb, lt = 1, min(512, L)
    tok = lambda w: pl.BlockSpec((bb, lt, w), lambda i, j: (i, j, 0))
    per_b = lambda r, w: pl.BlockSpec((bb, r, w), lambda i, j: (i, 0, 0))
    vec = _const_spec((1, D_MODEL))
    in_specs = [tok(D_MODEL), tok(W_A), tok(W_B), tok(W_C), per_b(1, D_MODEL), per_b(1, D_MODEL),
                per_b(1, D_MODEL), per_b(1, D_MODEL), vec, vec, vec,
                _const_spec((W_A + W_B + W_C, D_MODEL)), _const_spec((D_MODEL, 2 * D_FF)),
                _const_spec((FFN_CONV, 2 * D_FF)), per_b(FFN_CONV - 1, 2 * D_FF),
                _const_spec((D_FF, D_MODEL))]
    return pl.pallas_call(
        functools.partial(_post_kernel, bb, lt),
        grid=(b // bb, L // lt),
        in_specs=in_specs,
        out_specs=[tok(D_MODEL), per_b(SUBLANES, 2 * D_FF)],
        out_shape=[jax.ShapeDtypeStruct((b, L, D_MODEL), F32),
                   jax.ShapeDtypeStruct((b, SUBLANES, 2 * D_FF), F32)],
        scratch_shapes=[pltpu.VMEM((bb, lt + SUBLANES, FF_CHUNK), F32),
                        pltpu.VMEM((bb, SUBLANES, 2 * D_FF), F32)],
        compiler_params=_params(("arbitrary", "arbitrary")),
        name="post_dec" if dec else "post_seq",
    )(x, oa, ob, oc, gate1, shift2, scale2, gate2, gpm, gpf, gqf, wo, wu, cw, cst, wd)


def _layer_params(l, w_in, gdn_A_log, gdn_dt_bias, fox_f_bias, chunk_w_s, chunk_b_s, dec_len):
    wl = w_in[l]
    a0 = 3 * W_A
    o_beta, o_a, o_gate = a0, a0 + H_A, a0 + 2 * H_A
    o_uv = o_gate + W_A
    o_qc = o_uv + 2 * W_B
    o_f = o_qc + 3 * W_C
    w_small = jnp.concatenate(
        [wl[:, o_beta:o_beta + H_A], wl[:, o_a:o_a + H_A], wl[:, o_f:o_f + H_C],
         jnp.zeros((D_MODEL, LANES - 2 * H_A - H_C), wl.dtype)], axis=1)
    w_fused = jnp.concatenate(
        [wl[:, 0:a0], wl[:, o_gate:o_gate + W_A], wl[:, o_uv:o_uv + 2 * W_B], wl[:, o_qc:o_qc + 3 * W_C],
         w_small], axis=1).astype(BF16)
    z = lambda n: jnp.zeros((n,), F32)
    smb = jnp.concatenate([z(H_A), gdn_dt_bias[l], fox_f_bias[l], z(LANES - 2 * H_A - H_C)]).reshape(1, LANES)
    alog = jnp.concatenate([z(H_A), gdn_A_log[l], z(LANES - 2 * H_A)]).reshape(1, LANES)
    ws = chunk_w_s[l]
    bs = chunk_b_s[l]
    ws_seq = ws.reshape(G_B * CHUNK, CHUNK)
    bsb_seq = jnp.repeat(bs.T, HEAD_DIM, axis=1)
    ws_dec = jnp.repeat(jnp.transpose(ws[:, :dec_len, :dec_len], (2, 1, 0)), HEAD_DIM, axis=2)
    bsb_dec = bsb_seq[:dec_len]
    return w_fused, smb, alog, ws_seq, bsb_seq, ws_dec, bsb_dec


def kernel(x_prompt, x_sample, state_gdn_conv, state_gdn_S, cache_fox_k, cache_fox_v, cache_fox_logf,
           state_ffn_conv, page_table, c_prompt, c_sample, w_ada, b_ada, g_pre_mix, g_post_mix, g_pre_ffn,
           g_post_ffn, w_in, w_out, gdn_conv_w, gdn_A_log, gdn_dt_bias, gdn_norm_g, chunk_ln_g, chunk_ln_b,
           chunk_w_s, chunk_b_s, fox_f_bias, w_up, ffn_conv_w, w_down):
    depth = w_in.shape[0]
    bp, lp, _ = x_prompt.shape
    bs, ls, _ = x_sample.shape
    n_pool = cache_fox_k.shape[1]

    cache_kt = jnp.transpose(cache_fox_k, (0, 1, 3, 4, 2)).reshape(depth, n_pool, W_C, PAGE)
    cache_vt = jnp.transpose(cache_fox_v, (0, 1, 3, 4, 2)).reshape(depth, n_pool, W_C, PAGE)
    cache_lft = jnp.transpose(cache_fox_logf, (0, 3, 1, 2))

    c_all = jnp.concatenate([c_prompt, c_sample], axis=0)
    pad = (-c_all.shape[0]) % SUBLANES
    c_all = jnp.pad(c_all, ((0, pad), (0, 0)))

    vec = lambda a: a.reshape(1, -1)
    outs = {k: [] for k in ("p_conv", "p_S", "p_k", "p_v", "p_lf", "p_ffn",
                            "s_conv", "s_S", "s_k", "s_v", "s_lf", "s_cv", "s_ffn")}
    xp, xs = x_prompt, x_sample
    for l in range(depth):
        mod = _ada(c_all, w_ada[l], b_ada[l])
        mod_p = mod[:bp].reshape(bp, 6, 1, D_MODEL)
        mod_s = mod[bp:bp + bs].reshape(bs, 6, 1, D_MODEL)
        w_fused, smb, alog, ws_seq, bsb_seq, ws_dec, bsb_dec = _layer_params(
            l, w_in, gdn_A_log, gdn_dt_bias, fox_f_bias, chunk_w_s, chunk_b_s, ls)
        ng = jnp.tile(gdn_norm_g[l], H_A).reshape(1, W_A)
        wo = w_out[l].astype(BF16)
        wu = w_up[l].astype(BF16)
        wd = w_down[l].astype(BF16)
        common_in = (vec(g_pre_mix[l]), w_fused, gdn_conv_w[l])
        common_b = (smb, alog, vec(chunk_ln_g[l]), vec(chunk_ln_b[l]))
        post_w = (vec(g_post_mix[l]), vec(g_pre_ffn[l]), vec(g_post_ffn[l]), wo, wu, ffn_conv_w[l])

        (qkva, gate, small, ob, kc, vc, tail, qbf, kbf, vbf, dcum) = _in_proj(
            False, xp, mod_p[:, 0], mod_p[:, 1], *common_in,
            jnp.zeros((bp, GDN_CONV - 1, 3 * W_A), F32), *common_b, ws_seq, bsb_seq)
        oa, s_new = _gdn_seq(qkva, small, gate, ng, jnp.zeros((bp, H_A, HEAD_DIM, HEAD_DIM), F32))
        dkt = jnp.swapaxes(dcum[:, :, SM_LF:SM_LF + SUBLANES], 1, 2)
        oc = _fox_seq(qbf, kbf, vbf, dcum, dkt)
        xp, ftail = _post(False, xp, oa, ob, oc, mod_p[:, 2], mod_p[:, 3], mod_p[:, 4], mod_p[:, 5],
                          *post_w, jnp.zeros((bp, FFN_CONV - 1, 2 * D_FF), F32), wd)
        outs["p_conv"].append(tail[:, SUBLANES - (GDN_CONV - 1):])
        outs["p_S"].append(s_new)
        outs["p_k"].append(kc.reshape(bp, lp // PAGE, PAGE, H_C, HEAD_DIM))
        outs["p_v"].append(vc.reshape(bp, lp // PAGE, PAGE, H_C, HEAD_DIM))
        outs["p_lf"].append(small[:, :, SM_LF:SM_LF + H_C].reshape(bp, lp // PAGE, PAGE, H_C))
        outs["p_ffn"].append(ftail[:, SUBLANES - (FFN_CONV - 1):])

        (qkva, gate, small, ob, kc, vc, tail, vb, qd) = _in_proj(
            True, xs, mod_s[:, 0], mod_s[:, 1], *common_in, state_gdn_conv[l], *common_b, ws_dec, bsb_dec)
        oa, s_new = _gdn_dec(qkva, small, gate, ng, state_gdn_S[l])
        oc = _fox_dec(l, page_table, qd, kc, vc, small, cache_kt, cache_vt, cache_lft)
        xs, ftail = _post(True, xs, oa, ob, oc, mod_s[:, 2], mod_s[:, 3], mod_s[:, 4], mod_s[:, 5],
                          *post_w, state_ffn_conv[l], wd)
        outs["s_conv"].append(tail[:, SUBLANES - (GDN_CONV - 1):])
        outs["s_S"].append(s_new)
        outs["s_k"].append(kc.reshape(bs, ls, H_C, HEAD_DIM))
        outs["s_v"].append(vc.reshape(bs, ls, H_C, HEAD_DIM))
        outs["s_lf"].append(small[:, :, SM_LF:SM_LF + H_C])
        outs["s_cv"].append(vb)
        outs["s_ffn"].append(ftail[:, SUBLANES - (FFN_CONV - 1):])

    st = lambda k: jnp.stack(outs[k])
    return (xp, xs, st("p_conv"), st("p_S"), st("p_k"), st("p_v"), st("p_lf"), st("p_ffn"),
            st("s_conv"), st("s_S"), st("s_k"), st("s_v"), st("s_lf"), st("s_cv"), st("s_ffn"))
```

```python
import functools

import jax
import jax.numpy as jnp
from jax import lax
from jax.experimental import pallas as pl
from jax.experimental.pallas import tpu as pltpu

F32 = jnp.float32
BF16 = jnp.bfloat16

D_MODEL = 1024
HEAD_DIM = 64
H_A = 6
G_B = 4
H_C = 6
W_A = H_A * HEAD_DIM
W_B = G_B * HEAD_DIM
W_C = H_C * HEAD_DIM
GDN_CONV = 4
GDN_CHUNK = 64
CHUNK = 128
D_FF = 2816
FFN_CONV = 3
EPS = 1e-6
NEG_INF = -1e30
PAGE = 128
LOG2E = 1.4426950408889634

LANES = 128
SUBLANES = 8
VMEM_LIMIT = 56 * 1024 * 1024

QA_OFF = 0
GA_OFF = QA_OFF + 3 * W_A
UV_OFF = GA_OFF + W_A
QC_OFF = UV_OFF + 2 * W_B
SM_OFF = QC_OFF + 3 * W_C
N_FUSED = SM_OFF + LANES
SM_BETA = 0
SM_G = H_A
SM_LF = 2 * H_A

FF_CHUNK = D_FF // 2


def _silu(x):
    return x * jax.nn.sigmoid(x)


def _softplus(x):
    return jnp.maximum(x, 0.0) + jnp.log1p(jnp.exp(-jnp.abs(x)))


def _dot(a, b):
    return jnp.dot(a, b, preferred_element_type=F32)


def _dot_hi(a, b):
    return jnp.dot(a, b, preferred_element_type=F32, precision=lax.Precision.HIGHEST)


def _dot_nt(a, b):
    return lax.dot_general(a, b, (((1,), (1,)), ((), ())), preferred_element_type=F32)


def _dot_tn(a, b):
    return lax.dot_general(a, b, (((0,), (0,)), ((), ())), preferred_element_type=F32)


def _iota2(shape, dim):
    return lax.broadcasted_iota(jnp.int32, shape, dim)


def _params(sem):
    return pltpu.CompilerParams(dimension_semantics=sem, vmem_limit_bytes=VMEM_LIMIT)


def _const_spec(shape):
    nd = len(shape)
    return pl.BlockSpec(shape, lambda *_: (0,) * nd, pipeline_mode=pl.Buffered(1))


def _ada_kernel(c_ref, w_ref, b_ref, o_ref):
    c = _silu(c_ref[...]).astype(BF16)
    o_ref[...] = _dot(c, w_ref[...].astype(BF16)) + b_ref[...]


def _ada(c, w, b):
    m = c.shape[0]
    n = w.shape[1]
    tn = 1024
    return pl.pallas_call(
        _ada_kernel,
        grid=(n // tn,),
        in_specs=[pl.BlockSpec((m, D_MODEL), lambda j: (0, 0)),
                  pl.BlockSpec((D_MODEL, tn), lambda j: (0, j)),
                  pl.BlockSpec((1, tn), lambda j: (0, j))],
        out_specs=pl.BlockSpec((m, tn), lambda j: (0, j)),
        out_shape=jax.ShapeDtypeStruct((m, n), F32),
        compiler_params=_params(("arbitrary",)),
        name="ada_mod",
    )(c, w, b.reshape(1, n))


def _in_kernel(dec, bb, lt, *refs):
    (x_ref, shift_ref, scale_ref, gpre_ref, w_ref, cw_ref, cst_ref, smb_ref, alog_ref,
     lng_ref, lnb_ref, ws_ref, bsb_ref) = refs[:13]
    if dec:
        (qkva_ref, gate_ref, small_ref, ob_ref, kc_ref, vc_ref, tail_ref, vb_ref, q_ref,
         ext_ref) = refs[13:]
    else:
        (qkva_ref, gate_ref, small_ref, ob_ref, kc_ref, vc_ref, tail_ref, qbf_ref, kbf_ref,
         vbf_ref, dcum_ref, ext_ref, dcar_ref) = refs[13:]
    l = pl.program_id(1)
    m = bb * lt

    x = x_ref[...]
    ms = jnp.mean(x * x, axis=-1, keepdims=True)
    h = x * lax.rsqrt(ms + EPS) * gpre_ref[...]
    h = h * (1.0 + scale_ref[...]) + shift_ref[...]
    h2 = h.reshape(m, D_MODEL).astype(BF16)

    wa = 3 * W_A
    za = _dot(h2, w_ref[:, QA_OFF:QA_OFF + wa]).reshape(bb, lt, wa)

    @pl.when(l == 0)
    def _():
        ext_ref[:, SUBLANES - (GDN_CONV - 1):SUBLANES, :] = cst_ref[...]

    @pl.when(l > 0)
    def _():
        ext_ref[:, 0:SUBLANES, :] = ext_ref[:, lt:lt + SUBLANES, :]

    ext_ref[:, SUBLANES:, :] = za
    y = za * cw_ref[GDN_CONV - 1:GDN_CONV, :]
    for i in range(GDN_CONV - 1):
        o = SUBLANES - (GDN_CONV - 1) + i
        y = y + ext_ref[:, o:o + lt, :] * cw_ref[i:i + 1, :]
    qkva_ref[...] = _silu(y)
    tail_ref[...] = ext_ref[:, lt:lt + SUBLANES, :]

    gate_ref[...] = _silu(_dot(h2, w_ref[:, GA_OFF:GA_OFF + W_A])).reshape(bb, lt, W_A)

    zs = _dot(h2, w_ref[:, SM_OFF:SM_OFF + LANES]) + smb_ref[...]
    lane = _iota2((1, LANES), 1)
    beta = jax.nn.sigmoid(zs)
    gval = -jnp.exp(alog_ref[...]) * _softplus(zs)
    lf = -_softplus(-zs)
    small = jnp.where(lane < SM_G, beta,
                      jnp.where(lane < SM_LF, gval, jnp.where(lane < SM_LF + H_C, lf, 0.0)))
    small_ref[...] = small.reshape(bb, lt, LANES)

    if not dec:
        tri = (_iota2((lt, lt), 0) >= _iota2((lt, lt), 1)).astype(F32)

        @pl.when(l == 0)
        def _():
            dcar_ref[...] = jnp.zeros_like(dcar_ref)

        dc = _dot_hi(tri, small) + dcar_ref[0:1, :]
        dcum_ref[...] = dc.reshape(bb, lt, LANES)
        dcar_ref[0:1, :] = dc[lt - 1:lt, :]

    uv = jax.nn.gelu(_dot(h2, w_ref[:, UV_OFF:UV_OFF + 2 * W_B]))
    u = uv[:, :W_B]
    v = uv[:, W_B:]
    mu = jnp.mean(v, axis=-1, keepdims=True)
    vc0 = v - mu
    var = jnp.mean(vc0 * vc0, axis=-1, keepdims=True)
    vb = vc0 * lax.rsqrt(var + EPS) * lng_ref[...] + lnb_ref[...]
    if dec:
        vb3 = vb.reshape(bb, lt, W_B)
        vb_ref[...] = vb3
        trow = _iota2((lt, W_B), 0)
        mix = jnp.zeros((bb, lt, W_B), F32)
        for s in range(lt):
            coef = jnp.where(trow >= s, ws_ref[s], 0.0)
            mix = mix + coef[None] * vb3[:, s:s + 1, :]
        ob_ref[...] = u.reshape(bb, lt, W_B) * (mix + bsb_ref[...][None])
    else:
        r_t = _iota2((G_B * CHUNK, CHUNK), 0) % CHUNK
        r_s = _iota2((G_B * CHUNK, CHUNK), 1)
        wst = jnp.where(r_s <= r_t, ws_ref[...], 0.0).astype(BF16)
        grp = _iota2((1, W_B), 1) // HEAD_DIM
        for c in range(lt // CHUNK):
            rows = slice(c * CHUNK, (c + 1) * CHUNK)
            r = _dot(wst, vb[rows].astype(BF16))
            mix = jnp.zeros((CHUNK, W_B), F32)
            for g in range(G_B):
                mix = jnp.where(grp == g, r[g * CHUNK:(g + 1) * CHUNK], mix)
            ob_ref[0, rows, :] = u[rows] * (mix + bsb_ref[...])

    zc = _dot(h2, w_ref[:, QC_OFF:QC_OFF + 3 * W_C])
    qc = zc[:, 0:W_C]
    kc = zc[:, W_C:2 * W_C]
    vc = zc[:, 2 * W_C:3 * W_C]
    kc_ref[...] = kc.reshape(bb, lt, W_C)
    vc_ref[...] = vc.reshape(bb, lt, W_C)
    if dec:
        q_ref[...] = qc.reshape(bb, lt, W_C)
    else:
        qbf_ref[...] = (qc * (LOG2E * HEAD_DIM ** -0.5)).astype(BF16).reshape(bb, lt, W_C)
        kbf_ref[...] = kc.astype(BF16).reshape(bb, lt, W_C)
        vbf_ref[...] = vc.astype(BF16).reshape(bb, lt, W_C)


def _in_proj(dec, x, shift, scale, gpre, w_fused, conv_w, conv_state, smb, alog, lng, lnb, ws, bsb):
    b, L, _ = x.shape
    if dec:
        bb, lt = min(32, b), L
    else:
        bb, lt = 1, min(512, L)
    grid = (b // bb, L // lt)
    tok = lambda w: pl.BlockSpec((bb, lt, w), lambda i, j: (i, j, 0))
    per_b = lambda r, w: pl.BlockSpec((bb, r, w), lambda i, j: (i, 0, 0))
    in_specs = [tok(D_MODEL), per_b(1, D_MODEL), per_b(1, D_MODEL), _const_spec((1, D_MODEL)),
                _const_spec((D_MODEL, N_FUSED)), _const_spec((GDN_CONV, 3 * W_A)),
                per_b(GDN_CONV - 1, 3 * W_A), _const_spec((1, LANES)), _const_spec((1, LANES)),
                _const_spec((1, W_B)), _const_spec((1, W_B)), _const_spec(ws.shape), _const_spec(bsb.shape)]
    sds = lambda w, dt=F32: jax.ShapeDtypeStruct((b, L, w), dt)
    out_shape = [sds(3 * W_A), sds(W_A), sds(LANES), sds(W_B), sds(W_C), sds(W_C),
                 jax.ShapeDtypeStruct((b, SUBLANES, 3 * W_A), F32)]
    out_specs = [tok(3 * W_A), tok(W_A), tok(LANES), tok(W_B), tok(W_C), tok(W_C), per_b(SUBLANES, 3 * W_A)]
    scratch = [pltpu.VMEM((bb, lt + SUBLANES, 3 * W_A), F32)]
    if dec:
        out_shape += [sds(W_B), sds(W_C)]
        out_specs += [tok(W_B), tok(W_C)]
    else:
        out_shape += [sds(W_C, BF16), sds(W_C, BF16), sds(W_C, BF16), sds(LANES)]
        out_specs += [tok(W_C), tok(W_C), tok(W_C), tok(LANES)]
        scratch += [pltpu.VMEM((SUBLANES, LANES), F32)]
    return pl.pallas_call(
        functools.partial(_in_kernel, dec, bb, lt),
        grid=grid, in_specs=in_specs, out_specs=out_specs, out_shape=out_shape,
        scratch_shapes=scratch,
        compiler_params=_params(("arbitrary", "arbitrary")),
        name="in_proj_dec" if dec else "in_proj_seq",
    )(x, shift, scale, gpre, w_fused, conv_w, conv_state, smb, alog, lng, lnb, ws, bsb)


GDN_GROUP = 256
N_PAIR = H_A // 2


def _tri_inverse_m1(lms, blk):
    bf = lambda xs: [x.astype(BF16) for x in xs]
    mm_ = lambda xs, ys: [_dot(x, y) for x, y in zip(xs, ys)]
    dm = [jnp.where(blk, lm, 0.0) for lm in lms]
    nm = [lm - d for lm, d in zip(lms, dm)]
    dmb = bf(dm)
    d2 = mm_(dmb, dmb)
    d2b = bf(d2)
    d4 = mm_(d2b, d2b)
    d4b = bf(d4)
    d8 = mm_(d4b, d4b)
    a1 = [x2 - x1 - c for x2, x1, c in zip(d2, dm, mm_(dmb, d2b))]
    a2 = [x4 + x8 + c for x4, x8, c in zip(d4, d8, mm_(d4b, bf(d8)))]
    et = [x + y + c for x, y, c in zip(a1, a2, mm_(bf(a1), bf(a2)))]
    etb = bf(et)
    mm = [n + c for n, c in zip(nm, mm_(etb, bf(nm)))]
    mmb = bf(mm)
    m2 = mm_(mmb, mmb)
    a3 = [x2 - x1 - c for x2, x1, c in zip(m2, mm, mm_(mmb, bf(m2)))]
    return [x + e + c for x, e, c in zip(a3, et, mm_(bf(a3), etb))]


def _pair_cols(lo, a, c0, c1):
    return jnp.where(lo, a[:, c0:c0 + 1], a[:, c1:c1 + 1])


def _pair_rsqrt_norm(lo, x, scale):
    x2 = x * x
    s_lo = jnp.sum(jnp.where(lo, x2, 0.0), axis=-1, keepdims=True)
    s_hi = jnp.sum(jnp.where(lo, 0.0, x2), axis=-1, keepdims=True)
    return jnp.where(lo, lax.rsqrt(s_lo * scale + EPS), lax.rsqrt(s_hi * scale + EPS))


def _gdn_pre_kernel(T, qkv_ref, small_ref, u_ref, w_ref, attn_ref, qin_ref, kout_ref, egl_ref):
    C = GDN_CHUNK
    row = _iota2((T, T), 0)
    col = _iota2((T, T), 1)
    same = (row // C) == (col // C)
    bd_tri = same & (row >= col)
    bd_strict = same & (row > col)
    blk = (row // 16) == (col // 16)
    lane = _iota2((1, LANES), 1)
    lo = lane < HEAD_DIM
    frow = _iota2((T, LANES), 0) % C
    fcol = _iota2((T, LANES), 1)
    folds = ((frow == fcol).astype(BF16), (frow + C == fcol).astype(BF16))

    sm = small_ref[0]
    gc_all = _dot_hi(bd_tri.astype(F32), sm)
    gl_all = _dot_hi((col == (row // C) * C + (C - 1)).astype(F32), gc_all)
    gc_t = gc_all.T
    rhs, rhs_b, lms, attn_cs = [], [], [], []
    for j in range(N_PAIR):
        sl = slice(j * LANES, (j + 1) * LANES)
        qp = qkv_ref[0, :, j * LANES:(j + 1) * LANES]
        kp = qkv_ref[0, :, W_A + j * LANES:W_A + (j + 1) * LANES]
        vp = qkv_ref[0, :, 2 * W_A + j * LANES:2 * W_A + (j + 1) * LANES]
        qn = qp * _pair_rsqrt_norm(lo, qp, 1.0) * (HEAD_DIM ** -0.5)
        kn = kp * _pair_rsqrt_norm(lo, kp, 1.0)
        h0, h1 = 2 * j, 2 * j + 1
        beta_p = _pair_cols(lo, sm, SM_BETA + h0, SM_BETA + h1)
        gc_p = _pair_cols(lo, gc_all, SM_G + h0, SM_G + h1)
        gl_p = _pair_cols(lo, gl_all, SM_G + h0, SM_G + h1)
        eg_p = jnp.exp(gc_p)
        kb = kn * beta_p
        kn_bf = kn.astype(BF16)
        rhs_u = vp * beta_p
        rhs_w = kb * eg_p
        rhs.append((rhs_u, rhs_w))
        rhs_b.append(jnp.concatenate([rhs_u, rhs_w], axis=1).astype(BF16))
        qin_ref[0, :, sl] = (qn * eg_p).astype(BF16)
        kout_ref[0, :, sl] = (kn * jnp.exp(gl_p - gc_p)).astype(BF16)
        egl_ref[0, :, sl] = jnp.exp(gl_p)
        attn_c = jnp.zeros((T, LANES), F32)
        for hh in range(2):
            h = 2 * j + hh
            keep = lo if hh == 0 else jnp.logical_not(lo)
            gcol = gc_all[:, SM_G + h:SM_G + h + 1]
            grow = gc_t[SM_G + h:SM_G + h + 1, :]
            decay = jnp.where(bd_tri, jnp.exp(jnp.where(bd_tri, gcol - grow, 0.0)), 0.0)
            g = _dot_nt(jnp.where(keep, kb, 0.0).astype(BF16), kn_bf)
            lms.append(jnp.where(bd_strict, g * decay, 0.0))
            qk = _dot_nt(jnp.where(keep, qn, 0.0).astype(BF16), kn_bf)
            attn_c = attn_c + _dot((qk * decay).astype(BF16), folds[hh])
        attn_ref[0, :, sl] = attn_c.astype(BF16)

    tm1 = _tri_inverse_m1(lms, blk)
    for j in range(N_PAIR):
        sl = slice(j * LANES, (j + 1) * LANES)
        rhs_u, rhs_w = rhs[j]
        uw0 = _dot(tm1[2 * j].astype(BF16), rhs_b[j])
        uw1 = _dot(tm1[2 * j + 1].astype(BF16), rhs_b[j])
        u_ref[0, :, sl] = rhs_u + jnp.where(lo, uw0[:, :LANES], uw1[:, :LANES])
        w_ref[0, :, sl] = (rhs_w + jnp.where(lo, uw0[:, LANES:], uw1[:, LANES:])).astype(BF16)


def _gdn_pre(qkva, small):
    b, L, _ = qkva.shape
    T = min(GDN_GROUP, L)
    tok = lambda w: pl.BlockSpec((1, T, w), lambda i, j: (i, j, 0))
    sds = lambda dt: jax.ShapeDtypeStruct((b, L, W_A), dt)
    return pl.pallas_call(
        functools.partial(_gdn_pre_kernel, T),
        grid=(b, L // T),
        in_specs=[tok(3 * W_A), tok(LANES)],
        out_specs=[tok(W_A)] * 6,
        out_shape=[sds(F32), sds(BF16), sds(BF16), sds(BF16), sds(BF16), sds(F32)],
        compiler_params=_params(("arbitrary", "arbitrary")),
        name="gdn_pre",
    )(qkva, small)


def _gdn_scan_kernel(nb, lt, u_ref, w_ref, attn_ref, qin_ref, kout_ref, egl_ref, gate_ref, ng_ref, s0_ref,
                     oa_ref, sout_ref, s_sc):
    l = pl.program_id(0)
    C = GDN_CHUNK

    @pl.when(l == 0)
    def _():
        s_sc[...] = s0_ref[...]

    lane = _iota2((1, LANES), 1)
    lo = lane < HEAD_DIM
    bd = (_iota2((LANES, LANES), 0) // HEAD_DIM) == (_iota2((LANES, LANES), 1) // HEAD_DIM)

    def body(c, carry):
        r0 = pl.multiple_of(c * C, C)
        rows = pl.ds(r0, C)
        for b in range(nb):
            for j in range(N_PAIR):
                sl = slice(j * LANES, (j + 1) * LANES)
                s_old = s_sc[b, j]
                s_bf = s_old.astype(BF16)
                vn = u_ref[b, rows, sl] - _dot(w_ref[b, rows, sl], s_bf)
                vn_bf = vn.astype(BF16)
                vv = jnp.concatenate([vn_bf, vn_bf], axis=0)
                at = attn_ref[b, rows, sl]
                zero = jnp.zeros_like(at)
                o = _dot(qin_ref[b, rows, sl], s_bf) + jnp.where(
                    lo, _dot(jnp.where(lo, at, zero), vv), _dot(jnp.where(lo, zero, at), vv))
                kv = _dot_tn(kout_ref[b, rows, sl], vn_bf)
                s_sc[b, j] = s_old * egl_ref[b, pl.ds(r0, 1), sl] + jnp.where(bd, kv, 0.0)
                on = o * _pair_rsqrt_norm(lo, o, 1.0 / HEAD_DIM)
                oa_ref[b, rows, sl] = on * ng_ref[:, sl] * gate_ref[b, rows, sl]
        return carry

    lax.fori_loop(0, lt // C, body, 0)
    sout_ref[...] = s_sc[...]


def _gdn_scan(u, w, attn, qin, kout, egl, gate, ng, s0_bd):
    b, L, _ = u.shape
    lt = min(512, L)
    tok = pl.BlockSpec((b, lt, W_A), lambda i: (0, i, 0))
    st = pl.BlockSpec((b, N_PAIR, LANES, LANES), lambda i: (0, 0, 0, 0))
    return pl.pallas_call(
        functools.partial(_gdn_scan_kernel, b, lt),
        grid=(L // lt,),
        in_specs=[tok] * 7 + [_const_spec((1, W_A)), st],
        out_specs=[tok, st],
        out_shape=[jax.ShapeDtypeStruct((b, L, W_A), F32),
                   jax.ShapeDtypeStruct((b, N_PAIR, LANES, LANES), F32)],
        scratch_shapes=[pltpu.VMEM((b, N_PAIR, LANES, LANES), F32)],
        compiler_params=_params(("arbitrary",)),
        name="gdn_scan",
    )(u, w, attn, qin, kout, egl, gate, ng, s0_bd)


def _to_pair_blockdiag(s):
    b = s.shape[0]
    s = s.reshape(b, N_PAIR, 2, HEAD_DIM, HEAD_DIM)
    z = jnp.zeros_like(s[:, :, 0])
    top = jnp.concatenate([s[:, :, 0], z], axis=-1)
    bot = jnp.concatenate([z, s[:, :, 1]], axis=-1)
    return jnp.concatenate([top, bot], axis=-2)


def _from_pair_blockdiag(sbd):
    b = sbd.shape[0]
    s = jnp.stack([sbd[:, :, :HEAD_DIM, :HEAD_DIM], sbd[:, :, HEAD_DIM:, HEAD_DIM:]], axis=2)
    return s.reshape(b, H_A, HEAD_DIM, HEAD_DIM)


def _gdn_seq(qkva, small, gate, ng, s0):
    u, w, attn, qin, kout, egl = _gdn_pre(qkva, small)
    oa, sbd = _gdn_scan(u, w, attn, qin, kout, egl, gate, ng, _to_pair_blockdiag(s0))
    return oa, _from_pair_blockdiag(sbd)


def _gdn_dec_kernel(L, q_ref, k_ref, v_ref, smt_ref, gate_ref, ngb_ref, s0_ref, oa_ref, sout_ref, kq_sc):
    h = pl.program_id(0)
    nb = q_ref.shape[-1]
    for t in range(L):
        q_t = q_ref[t]
        k_t = k_ref[t]
        q_t = q_t * (lax.rsqrt(jnp.sum(q_t * q_t, axis=0, keepdims=True) + EPS) * (HEAD_DIM ** -0.5))
        k_t = k_t * lax.rsqrt(jnp.sum(k_t * k_t, axis=0, keepdims=True) + EPS)
        kq_sc[0] = k_t
        kq_sc[1] = q_t
        beta = smt_ref[t, pl.ds(SM_BETA + h, 1), :]
        a = jnp.exp(smt_ref[t, pl.ds(SM_G + h, 1), :])
        src = s0_ref if t == 0 else sout_ref

        def k_dot_s(kk, acc):
            return acc + src[0, kk] * kq_sc[0, pl.ds(kk, 1), :]

        ks = lax.fori_loop(0, HEAD_DIM, k_dot_s, jnp.zeros((HEAD_DIM, nb), F32), unroll=8)
        delta = beta * (v_ref[t] - a * ks)

        def update(kk, acc):
            s_new = a * src[0, kk] + kq_sc[0, pl.ds(kk, 1), :] * delta
            sout_ref[0, kk] = s_new
            return acc + s_new * kq_sc[1, pl.ds(kk, 1), :]

        o = lax.fori_loop(0, HEAD_DIM, update, jnp.zeros((HEAD_DIM, nb), F32), unroll=8)
        on = o * lax.rsqrt(jnp.mean(o * o, axis=0, keepdims=True) + EPS)
        oa_ref[t] = on * ngb_ref[...] * gate_ref[t]


def _gdn_dec(qkv_t, small_t, gate_t, ngb, s0_t):
    L, _, nb = qkv_t.shape
    nh = W_A // HEAD_DIM
    head = lambda off: pl.BlockSpec((L, HEAD_DIM, nb), lambda h: (0, off + h, 0))
    st = pl.BlockSpec((1, HEAD_DIM, HEAD_DIM, nb), lambda h: (h, 0, 0, 0))
    return pl.pallas_call(
        functools.partial(_gdn_dec_kernel, L),
        grid=(H_A,),
        in_specs=[head(0), head(nh), head(2 * nh),
                  pl.BlockSpec((L, LANES, nb), lambda h: (0, 0, 0), pipeline_mode=pl.Buffered(1)),
                  head(0), pl.BlockSpec((HEAD_DIM, nb), lambda h: (h, 0)), st],
        out_specs=[head(0), st],
        out_shape=[jax.ShapeDtypeStruct((L, W_A, nb), F32),
                   jax.ShapeDtypeStruct((H_A, HEAD_DIM, HEAD_DIM, nb), F32)],
        scratch_shapes=[pltpu.VMEM((2, HEAD_DIM, nb), F32)],
        compiler_params=_params(("arbitrary",)),
        name="gdn_dec",
    )(qkv_t, qkv_t, qkv_t, small_t, gate_t, ngb, s0_t)


FOX_STRIP = 64


def _fox_seq_kernel(t, q_ref, k_ref, v_ref, dq_ref, dkt_ref, o_ref, m_sc, l_sc, acc_sc, dqb_sc,
                    s_sc, p_sc, al_sc):
    qi = pl.program_id(1)
    npair = H_C // 2
    lane = _iota2((1, LANES), 1)
    lo = lane < HEAD_DIM
    m_sc[...] = jnp.full_like(m_sc, NEG_INF)
    l_sc[...] = jnp.zeros_like(l_sc)
    acc_sc[...] = jnp.zeros_like(acc_sc)
    q = q_ref[0]
    dq = dq_ref[0] * LOG2E
    qms = []
    for h in range(H_C):
        qp = q[:, (h // 2) * LANES:(h // 2 + 1) * LANES]
        keep = lo if h % 2 == 0 else jnp.logical_not(lo)
        qms.append(jnp.where(keep, qp, jnp.zeros_like(qp)))
        dqb_sc[h] = jnp.broadcast_to(dq[:, SM_LF + h:SM_LF + h + 1], (t, LANES))
    def step(ki, masked):
        k0 = pl.multiple_of(ki * t, t)
        for j in range(npair):
            kp = k_ref[0, pl.ds(k0, t), j * LANES:(j + 1) * LANES]
            vp = v_ref[0, pl.ds(k0, t), j * LANES:(j + 1) * LANES]
            pvs = []
            for hh in range(2):
                h = 2 * j + hh
                s_sc[hh] = _dot_nt(qms[h], kp)
                dk2 = dkt_ref[0, h:h + 1, pl.ds(k0, t)] * LOG2E
                for r in range(t // FOX_STRIP):
                    rows = slice(r * FOX_STRIP, (r + 1) * FOX_STRIP)
                    nc = min(t, -(-((r + 1) * FOX_STRIP) // LANES) * LANES) if masked else t
                    s = s_sc[hh, rows, 0:nc] - dk2[:, 0:nc]
                    if masked:
                        keep = _iota2((FOX_STRIP, nc), 1) <= _iota2((FOX_STRIP, nc), 0) + r * FOX_STRIP
                        s = jnp.where(keep, s, NEG_INF)
                    dqb = dqb_sc[h, rows, :]
                    m_old = m_sc[h, rows, :]
                    m_new = jnp.maximum(m_old, jnp.max(s, axis=-1, keepdims=True) + dqb)
                    alpha = jnp.exp2(m_old - m_new)
                    shift = m_new - dqb
                    if nc > LANES:
                        shift = pltpu.repeat(shift, nc // LANES, axis=1)
                    p = jnp.exp2(s - shift)
                    l_sc[h, rows, :] = alpha * l_sc[h, rows, :] + jnp.sum(p, axis=-1, keepdims=True)
                    m_sc[h, rows, :] = m_new
                    al_sc[hh, rows, :] = alpha
                    p_sc[hh, rows, 0:nc] = p.astype(BF16)
                    if nc < t:
                        p_sc[hh, rows, nc:t] = jnp.zeros((FOX_STRIP, t - nc), BF16)
                pvs.append(_dot(p_sc[hh], vp))
            acc_sc[j] = jnp.where(lo, al_sc[0], al_sc[1]) * acc_sc[j] + jnp.where(lo, pvs[0], pvs[1])

    def body(ki, carry):
        step(ki, False)
        return carry

    lax.fori_loop(0, qi, body, 0)
    step(qi, True)
    for j in range(npair):
        o_ref[0, :, j * LANES:(j + 1) * LANES] = acc_sc[j] / jnp.where(lo, l_sc[2 * j], l_sc[2 * j + 1])


def _fox_seq(qbf, kbf, vbf, dcum, dkt):
    b, L, _ = qbf.shape
    t = min(512, L)
    full = lambda w: pl.BlockSpec((1, L, w), lambda i, j: (i, 0, 0))
    return pl.pallas_call(
        functools.partial(_fox_seq_kernel, t),
        grid=(b, L // t),
        in_specs=[pl.BlockSpec((1, t, W_C), lambda i, j: (i, j, 0)), full(W_C), full(W_C),
                  pl.BlockSpec((1, t, LANES), lambda i, j: (i, j, 0)),
                  pl.BlockSpec((1, SUBLANES, L), lambda i, j: (i, 0, 0))],
        out_specs=pl.BlockSpec((1, t, W_C), lambda i, j: (i, j, 0)),
        out_shape=jax.ShapeDtypeStruct((b, L, W_C), F32),
        scratch_shapes=[pltpu.VMEM((H_C, t, LANES), F32), pltpu.VMEM((H_C, t, LANES), F32),
                        pltpu.VMEM((H_C // 2, t, LANES), F32), pltpu.VMEM((H_C, t, LANES), F32),
                        pltpu.VMEM((2, t, t), F32), pltpu.VMEM((2, t, t), BF16),
                        pltpu.VMEM((2, t, LANES), F32)],
        compiler_params=_params(("arbitrary", "arbitrary")),
        name="fox_seq",
    )(qbf, kbf, vbf, dcum, dkt)


def _fox_dec_kernel(L, n_pages, pt_ref, q_ref, kn_ref, vn_ref, sm_ref, lf_ref, *refs):
    k_refs = refs[0:n_pages]
    v_refs = refs[n_pages:2 * n_pages]
    o_ref = refs[2 * n_pages]
    x_sc = refs[2 * n_pages + 1]
    i = pl.program_id(0)
    R = H_C * L
    q = q_ref[0] * (HEAD_DIM ** -0.5)
    rowh = _iota2((R, W_C), 0) // L
    colh = _iota2((R, W_C), 1) // HEAD_DIM
    qbd = jnp.where(rowh == colh, jnp.concatenate([q] * H_C, axis=0), 0.0).astype(BF16)

    for p in range(n_pages):
        pg = pt_ref[i * n_pages + p]
        for h in range(H_C):
            r = h * n_pages + p
            x_sc[r:r + 1, :] = lf_ref[h, pl.ds(pg, 1), :]
    x = x_sc[...]
    n = H_C * n_pages
    later = (_iota2((PAGE, PAGE), 0) > _iota2((PAGE, PAGE), 1)).astype(F32)
    within = _dot_hi(x, later)
    tot = _dot_hi(x, jnp.ones((PAGE, PAGE), F32))
    ri = _iota2((n, n), 0)
    ci = _iota2((n, n), 1)
    later_pages = ((ci // n_pages == ri // n_pages) & (ci % n_pages > ri % n_pages)).astype(F32)
    rsum = within + _dot_hi(later_pages, tot)

    sm = sm_ref[0]
    tri = (_iota2((L, L), 0) >= _iota2((L, L), 1)).astype(F32)
    cq = _dot_hi(tri, sm)
    cq_t = cq.T
    cq_col = jnp.concatenate([cq[:, SM_LF + h:SM_LF + h + 1] for h in range(H_C)], axis=0)
    cq_row = jnp.concatenate([jnp.broadcast_to(cq_t[SM_LF + h:SM_LF + h + 1, :], (L, L))
                              for h in range(H_C)], axis=0)

    s_pages = []
    for p in range(n_pages):
        sp = _dot(qbd, k_refs[p][...].astype(BF16))
        bias = jnp.concatenate(
            [jnp.broadcast_to(rsum[h * n_pages + p:h * n_pages + p + 1, :], (L, PAGE)) for h in range(H_C)],
            axis=0)
        s_pages.append(sp + bias + cq_col)
    s_new = _dot_nt(qbd, kn_ref[0].astype(BF16)) + cq_col - cq_row
    qpos = _iota2((R, L), 0) % L
    s_new = jnp.where(_iota2((R, L), 1) <= qpos, s_new, NEG_INF)

    mx = jnp.max(s_new, axis=-1, keepdims=True)
    for sp in s_pages:
        mx = jnp.maximum(mx, jnp.max(sp, axis=-1, keepdims=True))
    p_new = jnp.exp(s_new - mx)
    den = jnp.sum(p_new, axis=-1, keepdims=True)
    acc = _dot(p_new.astype(BF16), vn_ref[0].astype(BF16))
    for p in range(n_pages):
        pp = jnp.exp(s_pages[p] - mx)
        den = den + jnp.sum(pp, axis=-1, keepdims=True)
        acc = acc + _dot_nt(pp.astype(BF16), v_refs[p][...].astype(BF16))
    acc = jnp.where(rowh == colh, acc / den, 0.0)
    out = acc[0:L]
    for h in range(1, H_C):
        out = out + acc[h * L:(h + 1) * L]
    o_ref[0] = out


def _fox_dec(layer, page_table, q, kn, vn, small, cache_kt, cache_vt, cache_lft):
    b, L, _ = q.shape
    n_pages = page_table.shape[1]
    n_pool = cache_kt.shape[1]
    tok = lambda w: pl.BlockSpec((1, L, w), lambda i, pt: (i, 0, 0))

    def page_spec(p):
        return pl.BlockSpec((None, None, W_C, PAGE), lambda i, pt: (layer, pt[i * n_pages + p], 0, 0))

    in_specs = [tok(W_C), tok(W_C), tok(W_C), tok(LANES),
                pl.BlockSpec((None, H_C, n_pool, PAGE), lambda i, pt: (layer, 0, 0, 0),
                             pipeline_mode=pl.Buffered(1))]
    in_specs += [page_spec(p) for p in range(n_pages)]
    in_specs += [page_spec(p) for p in range(n_pages)]
    grid_spec = pltpu.PrefetchScalarGridSpec(
        num_scalar_prefetch=1, grid=(b,), in_specs=in_specs, out_specs=tok(W_C),
        scratch_shapes=[pltpu.VMEM((H_C * n_pages, PAGE), F32)])
    return pl.pallas_call(
        functools.partial(_fox_dec_kernel, L, n_pages),
        grid_spec=grid_spec,
        out_shape=jax.ShapeDtypeStruct((b, L, W_C), F32),
        compiler_params=_params(("arbitrary",)),
        name="fox_dec",
    )(page_table.reshape(-1), q, kn, vn, small, cache_lft,
      *([cache_kt] * n_pages), *([cache_vt] * n_pages))


def _post_kernel(bb, lt, x_ref, oa_ref, ob_ref, oc_ref, gate1_ref, shift2_ref, scale2_ref, gate2_ref,
                 gpm_ref, gpf_ref, gqf_ref, wo_ref, wu_ref, cw_ref, cst_ref, wd_ref,
                 y_ref, tail_ref, ext_ref, car_ref):
    l = pl.program_id(1)
    m = bb * lt
    keep = FFN_CONV - 1

    def rms(v, g_ref):
        return v * lax.rsqrt(jnp.mean(v * v, axis=-1, keepdims=True) + EPS) * g_ref[...]

    oa = oa_ref[...].reshape(m, W_A).astype(BF16)
    ob = ob_ref[...].reshape(m, W_B).astype(BF16)
    oc = oc_ref[...].reshape(m, W_C).astype(BF16)
    o = (_dot(oa, wo_ref[0:W_A, :]) + _dot(ob, wo_ref[W_A:W_A + W_B, :])
         + _dot(oc, wo_ref[W_A + W_B:W_A + W_B + W_C, :]))
    x1 = x_ref[...] + gate1_ref[...] * rms(o, gpm_ref).reshape(bb, lt, D_MODEL)

    h = rms(x1, gpf_ref) * (1.0 + scale2_ref[...]) + shift2_ref[...]
    h2 = h.reshape(m, D_MODEL).astype(BF16)

    @pl.when(l == 0)
    def _():
        car_ref[...] = jnp.zeros_like(car_ref)
        car_ref[:, SUBLANES - keep:SUBLANES, :] = cst_ref[...]

    def conv_cols(c0, w):
        up = _dot(h2, wu_ref[:, c0:c0 + w]).reshape(bb, lt, w)
        ext_ref[:, 0:SUBLANES, :] = car_ref[:, :, c0:c0 + w]
        ext_ref[:, SUBLANES:, :] = up
        car_ref[:, :, c0:c0 + w] = ext_ref[:, lt:lt + SUBLANES, :]
        y = up * cw_ref[FFN_CONV - 1:FFN_CONV, c0:c0 + w]
        for i in range(keep):
            off = SUBLANES - keep + i
            y = y + ext_ref[:, off:off + lt, :] * cw_ref[i:i + 1, c0:c0 + w]
        return y

    y = jnp.zeros((m, D_MODEL), F32)
    for j in range(D_FF // FF_CHUNK):
        a = conv_cols(j * FF_CHUNK, FF_CHUNK)
        b = conv_cols(D_FF + j * FF_CHUNK, FF_CHUNK)
        g = (jax.nn.gelu(a) * b).reshape(m, FF_CHUNK).astype(BF16)
        y = y + _dot(g, wd_ref[j * FF_CHUNK:(j + 1) * FF_CHUNK, :])
    tail_ref[...] = car_ref[...]
    y_ref[...] = x1 + gate2_ref[...] * rms(y, gqf_ref).reshape(bb, lt, D_MODEL)


def _post(dec, x, oa, ob, oc, gate1, shift2, scale2, gate2, gpm, gpf, gqf, wo, wu, cw, cst, wd):
    b, L, _ = x.shape
    if dec:
        bb, lt = min(16, b), L
    else:
        bb, lt = 1, min(512, L)
    tok = lambda w: pl.BlockSpec((bb, lt, w), lambda i, j: (i, j, 0))
    per_b = lambda r, w: pl.BlockSpec((bb, r, w), lambda i, j: (i, 0, 0))
    vec = _const_spec((1, D_MODEL))
    in_specs = [tok(D_MODEL), tok(W_A), tok(W_B), tok(W_C), per_b(1, D_MODEL), per_b(1, D_MODEL),
                per_b(1, D_MODEL), per_b(1, D_MODEL), vec, vec, vec,
                _const_spec((W_A + W_B + W_C, D_MODEL)), _const_spec((D_MODEL, 2 * D_FF)),
                _const_spec((FFN_CONV, 2 * D_FF)), per_b(FFN_CONV - 1, 2 * D_FF),
                _const_spec((D_FF, D_MODEL))]
    return pl.pallas_call(
        functools.partial(_post_kernel, bb, lt),
        grid=(b // bb, L // lt),
        in_specs=in_specs,
        out_specs=[tok(D_MODEL), per_b(SUBLANES, 2 * D_FF)],
        out_shape=[jax.ShapeDtypeStruct((b, L, D_MODEL), F32),
                   jax.ShapeDtypeStruct((b, SUBLANES, 2 * D_FF), F32)],
        scratch_shapes=[pltpu.VMEM((bb, lt + SUBLANES, FF_CHUNK), F32),
                        pltpu.VMEM((bb, SUBLANES, 2 * D_FF), F32)],
        compiler_params=_params(("arbitrary", "arbitrary")),
        name="post_dec" if dec else "post_seq",
    )(x, oa, ob, oc, gate1, shift2, scale2, gate2, gpm, gpf, gqf, wo, wu, cw, cst, wd)


def _layer_params(l, w_in, gdn_A_log, gdn_dt_bias, fox_f_bias, chunk_w_s, chunk_b_s, dec_len):
    wl = w_in[l]
    a0 = 3 * W_A
    o_beta, o_a, o_gate = a0, a0 + H_A, a0 + 2 * H_A
    o_uv = o_gate + W_A
    o_qc = o_uv + 2 * W_B
    o_f = o_qc + 3 * W_C
    w_small = jnp.concatenate(
        [wl[:, o_beta:o_beta + H_A], wl[:, o_a:o_a + H_A], wl[:, o_f:o_f + H_C],
         jnp.zeros((D_MODEL, LANES - 2 * H_A - H_C), wl.dtype)], axis=1)
    w_fused = jnp.concatenate(
        [wl[:, 0:a0], wl[:, o_gate:o_gate + W_A], wl[:, o_uv:o_uv + 2 * W_B], wl[:, o_qc:o_qc + 3 * W_C],
         w_small], axis=1).astype(BF16)
    z = lambda n: jnp.zeros((n,), F32)
    smb = jnp.concatenate([z(H_A), gdn_dt_bias[l], fox_f_bias[l], z(LANES - 2 * H_A - H_C)]).reshape(1, LANES)
    alog = jnp.concatenate([z(H_A), gdn_A_log[l], z(LANES - 2 * H_A)]).reshape(1, LANES)
    ws = chunk_w_s[l]
    bs = chunk_b_s[l]
    ws_seq = ws.reshape(G_B * CHUNK, CHUNK)
    bsb_seq = jnp.repeat(bs.T, HEAD_DIM, axis=1)
    ws_dec = jnp.repeat(jnp.transpose(ws[:, :dec_len, :dec_len], (2, 1, 0)), HEAD_DIM, axis=2)
    bsb_dec = bsb_seq[:dec_len]
    return w_fused, smb, alog, ws_seq, bsb_seq, ws_dec, bsb_dec


def kernel(x_prompt, x_sample, state_gdn_conv, state_gdn_S, cache_fox_k, cache_fox_v, cache_fox_logf,
           state_ffn_conv, page_table, c_prompt, c_sample, w_ada, b_ada, g_pre_mix, g_post_mix, g_pre_ffn,
           g_post_ffn, w_in, w_out, gdn_conv_w, gdn_A_log, gdn_dt_bias, gdn_norm_g, chunk_ln_g, chunk_ln_b,
           chunk_w_s, chunk_b_s, fox_f_bias, w_up, ffn_conv_w, w_down):
    depth = w_in.shape[0]
    bp, lp, _ = x_prompt.shape
    bs, ls, _ = x_sample.shape
    n_pool = cache_fox_k.shape[1]

    cache_kt = jnp.transpose(cache_fox_k, (0, 1, 3, 4, 2)).reshape(depth, n_pool, W_C, PAGE)
    cache_vt = jnp.transpose(cache_fox_v, (0, 1, 3, 4, 2)).reshape(depth, n_pool, W_C, PAGE)
    cache_lft = jnp.transpose(cache_fox_logf, (0, 3, 1, 2))

    c_all = jnp.concatenate([c_prompt, c_sample], axis=0)
    pad = (-c_all.shape[0]) % SUBLANES
    c_all = jnp.pad(c_all, ((0, pad), (0, 0)))

    vec = lambda a: a.reshape(1, -1)
    outs = {k: [] for k in ("p_conv", "p_S", "p_k", "p_v", "p_lf", "p_ffn",
                            "s_conv", "s_S", "s_k", "s_v", "s_lf", "s_cv", "s_ffn")}
    xp, xs = x_prompt, x_sample
    for l in range(depth):
        mod = _ada(c_all, w_ada[l], b_ada[l])
        mod_p = mod[:bp].reshape(bp, 6, 1, D_MODEL)
        mod_s = mod[bp:bp + bs].reshape(bs, 6, 1, D_MODEL)
        w_fused, smb, alog, ws_seq, bsb_seq, ws_dec, bsb_dec = _layer_params(
            l, w_in, gdn_A_log, gdn_dt_bias, fox_f_bias, chunk_w_s, chunk_b_s, ls)
        ng = jnp.tile(gdn_norm_g[l], H_A).reshape(1, W_A)
        wo = w_out[l].astype(BF16)
        wu = w_up[l].astype(BF16)
        wd = w_down[l].astype(BF16)
        common_in = (vec(g_pre_mix[l]), w_fused, gdn_conv_w[l])
        common_b = (smb, alog, vec(chunk_ln_g[l]), vec(chunk_ln_b[l]))
        post_w = (vec(g_post_mix[l]), vec(g_pre_ffn[l]), vec(g_post_ffn[l]), wo, wu, ffn_conv_w[l])

        (qkva, gate, small, ob, kc, vc, tail, qbf, kbf, vbf, dcum) = _in_proj(
            False, xp, mod_p[:, 0], mod_p[:, 1], *common_in,
            jnp.zeros((bp, GDN_CONV - 1, 3 * W_A), F32), *common_b, ws_seq, bsb_seq)
        oa, s_new = _gdn_seq(qkva, small, gate, ng, jnp.zeros((bp, H_A, HEAD_DIM, HEAD_DIM), F32))
        dkt = jnp.swapaxes(dcum[:, :, SM_LF:SM_LF + SUBLANES], 1, 2)
        oc = _fox_seq(qbf, kbf, vbf, dcum, dkt)
        xp, ftail = _post(False, xp, oa, ob, oc, mod_p[:, 2], mod_p[:, 3], mod_p[:, 4], mod_p[:, 5],
                          *post_w, jnp.zeros((bp, FFN_CONV - 1, 2 * D_FF), F32), wd)
        outs["p_conv"].append(tail[:, SUBLANES - (GDN_CONV - 1):])
        outs["p_S"].append(s_new)
        outs["p_k"].append(kc.reshape(bp, lp // PAGE, PAGE, H_C, HEAD_DIM))
        outs["p_v"].append(vc.reshape(bp, lp // PAGE, PAGE, H_C, HEAD_DIM))
        outs["p_lf"].append(small[:, :, SM_LF:SM_LF + H_C].reshape(bp, lp // PAGE, PAGE, H_C))
        outs["p_ffn"].append(ftail[:, SUBLANES - (FFN_CONV - 1):])

        (qkva, gate, small, ob, kc, vc, tail, vb, qd) = _in_proj(
            True, xs, mod_s[:, 0], mod_s[:, 1], *common_in, state_gdn_conv[l], *common_b, ws_dec, bsb_dec)
        to_lanes = lambda a: jnp.transpose(a, (1, 2, 0))
        oa_t, s_new_t = _gdn_dec(to_lanes(qkva), to_lanes(small), to_lanes(gate),
                                 jnp.broadcast_to(ng.reshape(W_A, 1), (W_A, bs)),
                                 jnp.transpose(state_gdn_S[l], (1, 2, 3, 0)))
        oa = jnp.transpose(oa_t, (2, 0, 1))
        s_new = jnp.transpose(s_new_t, (3, 0, 1, 2))
        oc = _fox_dec(l, page_table, qd, kc, vc, small, cache_kt, cache_vt, cache_lft)
        xs, ftail = _post(True, xs, oa, ob, oc, mod_s[:, 2], mod_s[:, 3], mod_s[:, 4], mod_s[:, 5],
                          *post_w, state_ffn_conv[l], wd)
        outs["s_conv"].append(tail[:, SUBLANES - (GDN_CONV - 1):])
        outs["s_S"].append(s_new)
        outs["s_k"].append(kc.reshape(bs, ls, H_C, HEAD_DIM))
        outs["s_v"].append(vc.reshape(bs, ls, H_C, HEAD_DIM))
        outs["s_lf"].append(small[:, :, SM_LF:SM_LF + H_C])
        outs["s_cv"].append(vb)
        outs["s_ffn"].append(ftail[:, SUBLANES - (FFN_CONV - 1):])

    st = lambda k: jnp.stack(outs[k])
    return (xp, xs, st("p_conv"), st("p_S"), st("p_k"), st("p_v"), st("p_lf"), st("p_ffn"),
            st("s_conv"), st("s_S"), st("s_k"), st("s_v"), st("s_lf"), st("s_cv"), st("s_ffn"))
```

```python
import functools

import jax
import jax.numpy as jnp
from jax import lax
from jax.experimental import pallas as pl
from jax.experimental.pallas import tpu as pltpu

F32 = jnp.float32
BF16 = jnp.bfloat16

D_MODEL = 1024
HEAD_DIM = 64
H_A = 6
G_B = 4
H_C = 6
W_A = H_A * HEAD_DIM
W_B = G_B * HEAD_DIM
W_C = H_C * HEAD_DIM
GDN_CONV = 4
GDN_CHUNK = 64
CHUNK = 128
D_FF = 2816
FFN_CONV = 3
EPS = 1e-6
NEG_INF = -1e30
PAGE = 128
LOG2E = 1.4426950408889634

LANES = 128
SUBLANES = 8
VMEM_LIMIT = 56 * 1024 * 1024

QA_OFF = 0
GA_OFF = QA_OFF + 3 * W_A
UV_OFF = GA_OFF + W_A
QC_OFF = UV_OFF + 2 * W_B
SM_OFF = QC_OFF + 3 * W_C
N_FUSED = SM_OFF + LANES
SM_BETA = 0
SM_G = H_A
SM_LF = 2 * H_A

FF_CHUNK = D_FF // 2


def _silu(x):
    return x * jax.nn.sigmoid(x)


def _softplus(x):
    return jnp.maximum(x, 0.0) + jnp.log1p(jnp.exp(-jnp.abs(x)))


def _dot(a, b):
    return jnp.dot(a, b, preferred_element_type=F32)


def _dot_hi(a, b):
    return jnp.dot(a, b, preferred_element_type=F32, precision=lax.Precision.HIGHEST)


def _dot_nt(a, b):
    return lax.dot_general(a, b, (((1,), (1,)), ((), ())), preferred_element_type=F32)


def _dot_tn(a, b):
    return lax.dot_general(a, b, (((0,), (0,)), ((), ())), preferred_element_type=F32)


def _iota2(shape, dim):
    return lax.broadcasted_iota(jnp.int32, shape, dim)


def _params(sem):
    return pltpu.CompilerParams(dimension_semantics=sem, vmem_limit_bytes=VMEM_LIMIT)


def _const_spec(shape):
    nd = len(shape)
    return pl.BlockSpec(shape, lambda *_: (0,) * nd, pipeline_mode=pl.Buffered(1))


def _ada_kernel(c_ref, w_ref, b_ref, o_ref):
    c = _silu(c_ref[...]).astype(BF16)
    o_ref[...] = _dot(c, w_ref[...].astype(BF16)) + b_ref[...]


def _ada(c, w, b):
    m = c.shape[0]
    n = w.shape[1]
    tn = 1024
    return pl.pallas_call(
        _ada_kernel,
        grid=(n // tn,),
        in_specs=[pl.BlockSpec((m, D_MODEL), lambda j: (0, 0)),
                  pl.BlockSpec((D_MODEL, tn), lambda j: (0, j)),
                  pl.BlockSpec((1, tn), lambda j: (0, j))],
        out_specs=pl.BlockSpec((m, tn), lambda j: (0, j)),
        out_shape=jax.ShapeDtypeStruct((m, n), F32),
        compiler_params=_params(("arbitrary",)),
        name="ada_mod",
    )(c, w, b.reshape(1, n))


def _w_in_prep_kernel(depth, a_ref, o_ref):
    l = pl.program_id(0)
    j = pl.program_id(1)
    nkt = D_MODEL // LANES
    rpn = nkt * depth
    n_regular = SM_OFF // LANES
    n_first = GA_OFF // LANES
    n_in = SM_OFF + 2 * H_A + H_C

    @pl.when(j < n_regular)
    def _():
        n0 = j * LANES + jnp.where(j >= n_first, 2 * H_A, 0)
        for kt in range(nkt):
            x = a_ref[pl.ds(n0 * rpn + kt * depth + l, LANES, stride=rpn), :]
            o_ref[0, kt * LANES:(kt + 1) * LANES, :] = x.T.astype(BF16)

    @pl.when(j == n_regular)
    def _():
        r1, r2 = 2 * SUBLANES, SUBLANES
        n2 = n_in - r2
        r = _iota2((r1 + r2, LANES), 0)
        m = _iota2((r1 + r2, LANES), 1)
        pick = ((m < 2 * H_A) & (r == m)) | (
            (m >= 2 * H_A) & (m < 2 * H_A + H_C) & (r == m - 2 * H_A + r1 + (r2 - H_C)))
        sel = pick.astype(BF16)
        for kt in range(nkt):
            x1 = a_ref[pl.ds(GA_OFF * rpn + kt * depth + l, r1, stride=rpn), :]
            x2 = a_ref[pl.ds(n2 * rpn + kt * depth + l, r2, stride=rpn), :]
            xc = jnp.concatenate([x1, x2], axis=0).astype(BF16)
            o_ref[0, kt * LANES:(kt + 1) * LANES, :] = _dot_tn(xc, sel).astype(BF16)


def _w_in_prep(w_in):
    depth, d, n_in = w_in.shape
    nkt = d // LANES
    view = w_in.reshape(depth, nkt, LANES, n_in).transpose(3, 1, 0, 2).reshape(n_in * nkt * depth, LANES)
    return pl.pallas_call(
        functools.partial(_w_in_prep_kernel, depth),
        grid=(depth, N_FUSED // LANES),
        in_specs=[pl.BlockSpec(view.shape, lambda l, j: (0, 0), pipeline_mode=pl.Buffered(1))],
        out_specs=pl.BlockSpec((1, d, LANES), lambda l, j: (l, 0, j)),
        out_shape=jax.ShapeDtypeStruct((depth, d, N_FUSED), BF16),
        compiler_params=_params(("arbitrary", "arbitrary")),
        name="w_in_prep",
    )(view)


def _in_kernel(dec, bb, lt, *refs):
    (x_ref, shift_ref, scale_ref, gpre_ref, w_ref, cw_ref, cst_ref, smb_ref, alog_ref,
     lng_ref, lnb_ref, ws_ref, bsb_ref) = refs[:13]
    if dec:
        (qkva_ref, gate_ref, small_ref, ob_ref, tail_ref, kc_ref, vc_ref, vb_ref, q_ref,
         ext_ref) = refs[13:]
    else:
        (_, _, qkva_ref, gate_ref, small_ref, ob_ref, tail_ref, pk_ref, pv_ref, qbf_ref, kbf_ref,
         vbf_ref, dcum_ref, smt_ref, dct_ref, ext_ref, dcar_ref) = refs[13:]
    l = pl.program_id(1)
    m = bb * lt

    x = x_ref[...]
    ms = jnp.mean(x * x, axis=-1, keepdims=True)
    h = x * lax.rsqrt(ms + EPS) * gpre_ref[...]
    h = h * (1.0 + scale_ref[...]) + shift_ref[...]
    h2 = h.reshape(m, D_MODEL).astype(BF16)

    wa = 3 * W_A
    za = _dot(h2, w_ref[:, QA_OFF:QA_OFF + wa]).reshape(bb, lt, wa)

    @pl.when(l == 0)
    def _():
        ext_ref[:, SUBLANES - (GDN_CONV - 1):SUBLANES, :] = cst_ref[...]

    @pl.when(l > 0)
    def _():
        ext_ref[:, 0:SUBLANES, :] = ext_ref[:, lt:lt + SUBLANES, :]

    ext_ref[:, SUBLANES:, :] = za
    y = za * cw_ref[GDN_CONV - 1:GDN_CONV, :]
    for i in range(GDN_CONV - 1):
        o = SUBLANES - (GDN_CONV - 1) + i
        y = y + ext_ref[:, o:o + lt, :] * cw_ref[i:i + 1, :]
    qkva_ref[...] = _silu(y)
    tail_ref[...] = ext_ref[:, lt:lt + SUBLANES, :]

    gate_ref[...] = _silu(_dot(h2, w_ref[:, GA_OFF:GA_OFF + W_A])).reshape(bb, lt, W_A)

    zs = _dot(h2, w_ref[:, SM_OFF:SM_OFF + LANES]) + smb_ref[...]
    lane = _iota2((1, LANES), 1)
    beta = jax.nn.sigmoid(zs)
    gval = -jnp.exp(alog_ref[...]) * _softplus(zs)
    lf = -_softplus(-zs)
    small = jnp.where(lane < SM_G, beta,
                      jnp.where(lane < SM_LF, gval, jnp.where(lane < SM_LF + H_C, lf, 0.0)))
    small_ref[...] = small.reshape(bb, lt, LANES)

    if not dec:
        tri = (_iota2((lt, lt), 0) >= _iota2((lt, lt), 1)).astype(F32)

        @pl.when(l == 0)
        def _():
            dcar_ref[...] = jnp.zeros_like(dcar_ref)

        dc = _dot_hi(tri, small) + dcar_ref[0:1, :]
        dcum_ref[...] = dc.reshape(bb, lt, LANES)
        dcar_ref[0:1, :] = dc[lt - 1:lt, :]
        smt_ref[0] = small.T
        dct_ref[0] = dc.T

    uv = jax.nn.gelu(_dot(h2, w_ref[:, UV_OFF:UV_OFF + 2 * W_B]))
    u = uv[:, :W_B]
    v = uv[:, W_B:]
    mu = jnp.mean(v, axis=-1, keepdims=True)
    vc0 = v - mu
    var = jnp.mean(vc0 * vc0, axis=-1, keepdims=True)
    vb = vc0 * lax.rsqrt(var + EPS) * lng_ref[...] + lnb_ref[...]
    if dec:
        vb3 = vb.reshape(bb, lt, W_B)
        vb_ref[...] = vb3
        trow = _iota2((lt, W_B), 0)
        mix = jnp.zeros((bb, lt, W_B), F32)
        for s in range(lt):
            coef = jnp.where(trow >= s, ws_ref[s], 0.0)
            mix = mix + coef[None] * vb3[:, s:s + 1, :]
        ob_ref[...] = u.reshape(bb, lt, W_B) * (mix + bsb_ref[...][None])
    else:
        r_t = _iota2((G_B * CHUNK, CHUNK), 0) % CHUNK
        r_s = _iota2((G_B * CHUNK, CHUNK), 1)
        wst = jnp.where(r_s <= r_t, ws_ref[...], 0.0).astype(BF16)
        grp = _iota2((1, W_B), 1) // HEAD_DIM
        for c in range(lt // CHUNK):
            rows = slice(c * CHUNK, (c + 1) * CHUNK)
            r = _dot(wst, vb[rows].astype(BF16))
            mix = jnp.zeros((CHUNK, W_B), F32)
            for g in range(G_B):
                mix = jnp.where(grp == g, r[g * CHUNK:(g + 1) * CHUNK], mix)
            ob_ref[0, rows, :] = u[rows] * (mix + bsb_ref[...])

    zc = _dot(h2, w_ref[:, QC_OFF:QC_OFF + 3 * W_C])
    qc = zc[:, 0:W_C]
    kc = zc[:, W_C:2 * W_C]
    vc = zc[:, 2 * W_C:3 * W_C]
    if dec:
        kc_ref[...] = kc.reshape(bb, lt, W_C)
        vc_ref[...] = vc.reshape(bb, lt, W_C)
        q_ref[...] = qc.reshape(bb, lt, W_C)
    else:
        for c in range(lt // PAGE):
            rows = slice(c * PAGE, (c + 1) * PAGE)
            pk_ref[0, c] = kc[rows].T
            pv_ref[0, c] = vc[rows].T
        qbf_ref[...] = (qc * (LOG2E * HEAD_DIM ** -0.5)).astype(BF16).reshape(bb, lt, W_C)
        kbf_ref[...] = kc.astype(BF16).reshape(bb, lt, W_C)
        vbf_ref[...] = vc.astype(BF16).reshape(bb, lt, W_C)


def _in_proj(dec, layer, x, shift, scale, gpre, w_fused, conv_w, conv_state, smb, alog, lng, lnb, ws, bsb,
             pages=()):
    b, L, _ = x.shape
    if dec:
        bb, lt = min(32, b), L
    else:
        bb, lt = 1, min(512, L)
    grid = (b // bb, L // lt)
    tok = lambda w: pl.BlockSpec((bb, lt, w), lambda i, j: (i, j, 0))
    per_b = lambda r, w: pl.BlockSpec((bb, r, w), lambda i, j: (i, 0, 0))
    in_specs = [tok(D_MODEL), per_b(1, D_MODEL), per_b(1, D_MODEL), _const_spec((1, D_MODEL)),
                pl.BlockSpec((None, D_MODEL, N_FUSED), lambda i, j: (layer, 0, 0), pipeline_mode=pl.Buffered(1)),
                _const_spec((GDN_CONV, 3 * W_A)),
                per_b(GDN_CONV - 1, 3 * W_A), _const_spec((1, LANES)), _const_spec((1, LANES)),
                _const_spec((1, W_B)), _const_spec((1, W_B)), _const_spec(ws.shape), _const_spec(bsb.shape)]
    sds = lambda w, dt=F32: jax.ShapeDtypeStruct((b, L, w), dt)
    out_shape = [sds(3 * W_A), sds(W_A), sds(LANES), sds(W_B), jax.ShapeDtypeStruct((b, SUBLANES, 3 * W_A), F32)]
    out_specs = [tok(3 * W_A), tok(W_A), tok(LANES), tok(W_B), per_b(SUBLANES, 3 * W_A)]
    scratch = [pltpu.VMEM((bb, lt + SUBLANES, 3 * W_A), F32)]
    aliases = {}
    if dec:
        out_shape += [sds(W_C), sds(W_C), sds(W_B), sds(W_C)]
        out_specs += [tok(W_C), tok(W_C), tok(W_B), tok(W_C)]
    else:
        n_in = len(in_specs)
        in_specs += [pl.BlockSpec(memory_space=pl.ANY)] * 2
        aliases = {n_in: len(out_shape), n_in + 1: len(out_shape) + 1}
        page_spec = pl.BlockSpec((None, 1, lt // PAGE, W_C, PAGE), lambda i, j: (layer, i, j, 0, 0))
        head_major = pl.BlockSpec((1, LANES, lt), lambda i, j: (i, 0, j))
        out_shape += [jax.ShapeDtypeStruct(p.shape, p.dtype) for p in pages]
        out_specs += [page_spec, page_spec]
        out_shape += [sds(W_C, BF16), sds(W_C, BF16), sds(W_C, BF16), sds(LANES),
                      jax.ShapeDtypeStruct((b, LANES, L), F32), jax.ShapeDtypeStruct((b, LANES, L), F32)]
        out_specs += [tok(W_C), tok(W_C), tok(W_C), tok(LANES), head_major, head_major]
        scratch += [pltpu.VMEM((SUBLANES, LANES), F32)]
    return pl.pallas_call(
        functools.partial(_in_kernel, dec, bb, lt),
        grid=grid, in_specs=in_specs, out_specs=out_specs, out_shape=out_shape,
        scratch_shapes=scratch, input_output_aliases=aliases,
        compiler_params=_params(("arbitrary", "arbitrary")),
        name="in_proj_dec" if dec else "in_proj_seq",
    )(x, shift, scale, gpre, w_fused, conv_w, conv_state, smb, alog, lng, lnb, ws, bsb, *pages)


GDN_GROUP = 256
N_PAIR = H_A // 2


def _tri_inverse_m1(lms, blk):
    bf = lambda xs: [x.astype(BF16) for x in xs]
    mm_ = lambda xs, ys: [_dot(x, y) for x, y in zip(xs, ys)]
    dm = [jnp.where(blk, lm, 0.0) for lm in lms]
    nm = [lm - d for lm, d in zip(lms, dm)]
    dmb = bf(dm)
    d2 = mm_(dmb, dmb)
    d2b = bf(d2)
    d4 = mm_(d2b, d2b)
    d4b = bf(d4)
    d8 = mm_(d4b, d4b)
    a1 = [x2 - x1 - c for x2, x1, c in zip(d2, dm, mm_(dmb, d2b))]
    a2 = [x4 + x8 + c for x4, x8, c in zip(d4, d8, mm_(d4b, bf(d8)))]
    et = [x + y + c for x, y, c in zip(a1, a2, mm_(bf(a1), bf(a2)))]
    etb = bf(et)
    mm = [n + c for n, c in zip(nm, mm_(etb, bf(nm)))]
    mmb = bf(mm)
    m2 = mm_(mmb, mmb)
    a3 = [x2 - x1 - c for x2, x1, c in zip(m2, mm, mm_(mmb, bf(m2)))]
    return [x + e + c for x, e, c in zip(a3, et, mm_(bf(a3), etb))]


def _pair_cols(lo, a, c0, c1):
    return jnp.where(lo, a[:, c0:c0 + 1], a[:, c1:c1 + 1])


def _pair_rsqrt_norm(lo, x, scale):
    x2 = x * x
    s_lo = jnp.sum(jnp.where(lo, x2, 0.0), axis=-1, keepdims=True)
    s_hi = jnp.sum(jnp.where(lo, 0.0, x2), axis=-1, keepdims=True)
    return jnp.where(lo, lax.rsqrt(s_lo * scale + EPS), lax.rsqrt(s_hi * scale + EPS))


def _gdn_pre_kernel(T, qkv_ref, small_ref, u_ref, w_ref, attn_ref, qin_ref, kout_ref, egl_ref):
    C = GDN_CHUNK
    row = _iota2((T, T), 0)
    col = _iota2((T, T), 1)
    same = (row // C) == (col // C)
    bd_tri = same & (row >= col)
    bd_strict = same & (row > col)
    blk = (row // 16) == (col // 16)
    lane = _iota2((1, LANES), 1)
    lo = lane < HEAD_DIM
    frow = _iota2((T, LANES), 0) % C
    fcol = _iota2((T, LANES), 1)
    folds = ((frow == fcol).astype(BF16), (frow + C == fcol).astype(BF16))

    sm = small_ref[0]
    gc_all = _dot_hi(bd_tri.astype(F32), sm)
    gl_all = _dot_hi((col == (row // C) * C + (C - 1)).astype(F32), gc_all)
    gc_t = gc_all.T
    rhs, rhs_b, lms, attn_cs = [], [], [], []
    for j in range(N_PAIR):
        sl = slice(j * LANES, (j + 1) * LANES)
        qp = qkv_ref[0, :, j * LANES:(j + 1) * LANES]
        kp = qkv_ref[0, :, W_A + j * LANES:W_A + (j + 1) * LANES]
        vp = qkv_ref[0, :, 2 * W_A + j * LANES:2 * W_A + (j + 1) * LANES]
        qn = qp * _pair_rsqrt_norm(lo, qp, 1.0) * (HEAD_DIM ** -0.5)
        kn = kp * _pair_rsqrt_norm(lo, kp, 1.0)
        h0, h1 = 2 * j, 2 * j + 1
        beta_p = _pair_cols(lo, sm, SM_BETA + h0, SM_BETA + h1)
        gc_p = _pair_cols(lo, gc_all, SM_G + h0, SM_G + h1)
        gl_p = _pair_cols(lo, gl_all, SM_G + h0, SM_G + h1)
        eg_p = jnp.exp(gc_p)
        kb = kn * beta_p
        kn_bf = kn.astype(BF16)
        rhs_u = vp * beta_p
        rhs_w = kb * eg_p
        rhs.append((rhs_u, rhs_w))
        rhs_b.append(jnp.concatenate([rhs_u, rhs_w], axis=1).astype(BF16))
        qin_ref[0, :, sl] = (qn * eg_p).astype(BF16)
        kout_ref[0, :, sl] = (kn * jnp.exp(gl_p - gc_p)).astype(BF16)
        egl_ref[0, :, sl] = jnp.exp(gl_p)
        attn_c = jnp.zeros((T, LANES), F32)
        for hh in range(2):
            h = 2 * j + hh
            keep = lo if hh == 0 else jnp.logical_not(lo)
            gcol = gc_all[:, SM_G + h:SM_G + h + 1]
            grow = gc_t[SM_G + h:SM_G + h + 1, :]
            decay = jnp.where(bd_tri, jnp.exp(jnp.where(bd_tri, gcol - grow, 0.0)), 0.0)
            g = _dot_nt(jnp.where(keep, kb, 0.0).astype(BF16), kn_bf)
            lms.append(jnp.where(bd_strict, g * decay, 0.0))
            qk = _dot_nt(jnp.where(keep, qn, 0.0).astype(BF16), kn_bf)
            attn_c = attn_c + _dot((qk * decay).astype(BF16), folds[hh])
        attn_ref[0, :, sl] = attn_c.astype(BF16)

    tm1 = _tri_inverse_m1(lms, blk)
    for j in range(N_PAIR):
        sl = slice(j * LANES, (j + 1) * LANES)
        rhs_u, rhs_w = rhs[j]
        uw0 = _dot(tm1[2 * j].astype(BF16), rhs_b[j])
        uw1 = _dot(tm1[2 * j + 1].astype(BF16), rhs_b[j])
        u_ref[0, :, sl] = rhs_u + jnp.where(lo, uw0[:, :LANES], uw1[:, :LANES])
        w_ref[0, :, sl] = (rhs_w + jnp.where(lo, uw0[:, LANES:], uw1[:, LANES:])).astype(BF16)


def _gdn_pre(qkva, small):
    b, L, _ = qkva.shape
    T = min(GDN_GROUP, L)
    tok = lambda w: pl.BlockSpec((1, T, w), lambda i, j: (i, j, 0))
    sds = lambda dt: jax.ShapeDtypeStruct((b, L, W_A), dt)
    return pl.pallas_call(
        functools.partial(_gdn_pre_kernel, T),
        grid=(b, L // T),
        in_specs=[tok(3 * W_A), tok(LANES)],
        out_specs=[tok(W_A)] * 6,
        out_shape=[sds(F32), sds(BF16), sds(BF16), sds(BF16), sds(BF16), sds(F32)],
        compiler_params=_params(("arbitrary", "arbitrary")),
        name="gdn_pre",
    )(qkva, small)


def _gdn_scan_kernel(nb, lt, u_ref, w_ref, attn_ref, qin_ref, kout_ref, egl_ref, gate_ref, ng_ref, s0_ref,
                     oa_ref, sout_ref, s_sc):
    l = pl.program_id(0)
    C = GDN_CHUNK

    @pl.when(l == 0)
    def _():
        s_sc[...] = s0_ref[...]

    lane = _iota2((1, LANES), 1)
    lo = lane < HEAD_DIM
    bd = (_iota2((LANES, LANES), 0) // HEAD_DIM) == (_iota2((LANES, LANES), 1) // HEAD_DIM)

    def body(c, carry):
        r0 = pl.multiple_of(c * C, C)
        rows = pl.ds(r0, C)
        for b in range(nb):
            for j in range(N_PAIR):
                sl = slice(j * LANES, (j + 1) * LANES)
                s_old = s_sc[b, j]
                s_bf = s_old.astype(BF16)
                vn = u_ref[b, rows, sl] - _dot(w_ref[b, rows, sl], s_bf)
                vn_bf = vn.astype(BF16)
                vv = jnp.concatenate([vn_bf, vn_bf], axis=0)
                at = attn_ref[b, rows, sl]
                zero = jnp.zeros_like(at)
                o = _dot(qin_ref[b, rows, sl], s_bf) + jnp.where(
                    lo, _dot(jnp.where(lo, at, zero), vv), _dot(jnp.where(lo, zero, at), vv))
                kv = _dot_tn(kout_ref[b, rows, sl], vn_bf)
                s_sc[b, j] = s_old * egl_ref[b, pl.ds(r0, 1), sl] + jnp.where(bd, kv, 0.0)
                on = o * _pair_rsqrt_norm(lo, o, 1.0 / HEAD_DIM)
                oa_ref[b, rows, sl] = on * ng_ref[:, sl] * gate_ref[b, rows, sl]
        return carry

    lax.fori_loop(0, lt // C, body, 0)
    sout_ref[...] = s_sc[...]


def _gdn_scan(u, w, attn, qin, kout, egl, gate, ng, s0_bd):
    b, L, _ = u.shape
    lt = min(512, L)
    tok = pl.BlockSpec((b, lt, W_A), lambda i: (0, i, 0))
    st = pl.BlockSpec((b, N_PAIR, LANES, LANES), lambda i: (0, 0, 0, 0))
    return pl.pallas_call(
        functools.partial(_gdn_scan_kernel, b, lt),
        grid=(L // lt,),
        in_specs=[tok] * 7 + [_const_spec((1, W_A)), st],
        out_specs=[tok, st],
        out_shape=[jax.ShapeDtypeStruct((b, L, W_A), F32),
                   jax.ShapeDtypeStruct((b, N_PAIR, LANES, LANES), F32)],
        scratch_shapes=[pltpu.VMEM((b, N_PAIR, LANES, LANES), F32)],
        compiler_params=_params(("arbitrary",)),
        name="gdn_scan",
    )(u, w, attn, qin, kout, egl, gate, ng, s0_bd)


def _to_pair_blockdiag(s):
    b = s.shape[0]
    s = s.reshape(b, N_PAIR, 2, HEAD_DIM, HEAD_DIM)
    z = jnp.zeros_like(s[:, :, 0])
    top = jnp.concatenate([s[:, :, 0], z], axis=-1)
    bot = jnp.concatenate([z, s[:, :, 1]], axis=-1)
    return jnp.concatenate([top, bot], axis=-2)


def _from_pair_blockdiag(sbd):
    b = sbd.shape[0]
    s = jnp.stack([sbd[:, :, :HEAD_DIM, :HEAD_DIM], sbd[:, :, HEAD_DIM:, HEAD_DIM:]], axis=2)
    return s.reshape(b, H_A, HEAD_DIM, HEAD_DIM)


def _gdn_seq(qkva, small, gate, ng, s0):
    u, w, attn, qin, kout, egl = _gdn_pre(qkva, small)
    oa, sbd = _gdn_scan(u, w, attn, qin, kout, egl, gate, ng, _to_pair_blockdiag(s0))
    return oa, _from_pair_blockdiag(sbd)


def _gdn_dec_kernel(L, q_ref, k_ref, v_ref, smt_ref, gate_ref, ngb_ref, s0_ref, oa_ref, sout_ref, kq_sc):
    h = pl.program_id(0)
    nb = q_ref.shape[-1]
    for t in range(L):
        q_t = q_ref[t]
        k_t = k_ref[t]
        q_t = q_t * (lax.rsqrt(jnp.sum(q_t * q_t, axis=0, keepdims=True) + EPS) * (HEAD_DIM ** -0.5))
        k_t = k_t * lax.rsqrt(jnp.sum(k_t * k_t, axis=0, keepdims=True) + EPS)
        kq_sc[0] = k_t
        kq_sc[1] = q_t
        beta = smt_ref[t, pl.ds(SM_BETA + h, 1), :]
        a = jnp.exp(smt_ref[t, pl.ds(SM_G + h, 1), :])
        src = s0_ref if t == 0 else sout_ref

        def k_dot_s(kk, acc):
            return acc + src[0, kk] * kq_sc[0, pl.ds(kk, 1), :]

        ks = lax.fori_loop(0, HEAD_DIM, k_dot_s, jnp.zeros((HEAD_DIM, nb), F32), unroll=8)
        delta = beta * (v_ref[t] - a * ks)

        def update(kk, acc):
            s_new = a * src[0, kk] + kq_sc[0, pl.ds(kk, 1), :] * delta
            sout_ref[0, kk] = s_new
            return acc + s_new * kq_sc[1, pl.ds(kk, 1), :]

        o = lax.fori_loop(0, HEAD_DIM, update, jnp.zeros((HEAD_DIM, nb), F32), unroll=8)
        on = o * lax.rsqrt(jnp.mean(o * o, axis=0, keepdims=True) + EPS)
        oa_ref[t] = on * ngb_ref[...] * gate_ref[t]


def _gdn_dec(qkv_t, small_t, gate_t, ngb, s0_t):
    L, _, nb = qkv_t.shape
    nh = W_A // HEAD_DIM
    head = lambda off: pl.BlockSpec((L, HEAD_DIM, nb), lambda h: (0, off + h, 0))
    st = pl.BlockSpec((1, HEAD_DIM, HEAD_DIM, nb), lambda h: (h, 0, 0, 0))
    return pl.pallas_call(
        functools.partial(_gdn_dec_kernel, L),
        grid=(H_A,),
        in_specs=[head(0), head(nh), head(2 * nh),
                  pl.BlockSpec((L, LANES, nb), lambda h: (0, 0, 0), pipeline_mode=pl.Buffered(1)),
                  head(0), pl.BlockSpec((HEAD_DIM, nb), lambda h: (h, 0)), st],
        out_specs=[head(0), st],
        out_shape=[jax.ShapeDtypeStruct((L, W_A, nb), F32),
                   jax.ShapeDtypeStruct((H_A, HEAD_DIM, HEAD_DIM, nb), F32)],
        scratch_shapes=[pltpu.VMEM((2, HEAD_DIM, nb), F32)],
        compiler_params=_params(("arbitrary",)),
        name="gdn_dec",
    )(qkv_t, qkv_t, qkv_t, small_t, gate_t, ngb, s0_t)


FOX_STRIP = 64


def _fox_seq_kernel(t, q_ref, k_ref, v_ref, dq_ref, dkt_ref, o_ref, m_sc, l_sc, acc_sc, dqb_sc,
                    s_sc, p_sc, al_sc):
    qi = pl.program_id(1)
    npair = H_C // 2
    lane = _iota2((1, LANES), 1)
    lo = lane < HEAD_DIM
    m_sc[...] = jnp.full_like(m_sc, NEG_INF)
    l_sc[...] = jnp.zeros_like(l_sc)
    acc_sc[...] = jnp.zeros_like(acc_sc)
    q = q_ref[0]
    dq = dq_ref[0] * LOG2E
    qms = []
    for h in range(H_C):
        qp = q[:, (h // 2) * LANES:(h // 2 + 1) * LANES]
        keep = lo if h % 2 == 0 else jnp.logical_not(lo)
        qms.append(jnp.where(keep, qp, jnp.zeros_like(qp)))
        dqb_sc[h] = jnp.broadcast_to(dq[:, SM_LF + h:SM_LF + h + 1], (t, LANES))
    def step(ki, masked):
        k0 = pl.multiple_of(ki * t, t)
        for j in range(npair):
            kp = k_ref[0, pl.ds(k0, t), j * LANES:(j + 1) * LANES]
            vp = v_ref[0, pl.ds(k0, t), j * LANES:(j + 1) * LANES]
            pvs = []
            for hh in range(2):
                h = 2 * j + hh
                s_sc[hh] = _dot_nt(qms[h], kp)
                dk2 = dkt_ref[0, SM_LF + h:SM_LF + h + 1, pl.ds(k0, t)] * LOG2E
                for r in range(t // FOX_STRIP):
                    rows = slice(r * FOX_STRIP, (r + 1) * FOX_STRIP)
                    nc = min(t, -(-((r + 1) * FOX_STRIP) // LANES) * LANES) if masked else t
                    s = s_sc[hh, rows, 0:nc] - dk2[:, 0:nc]
                    if masked:
                        keep = _iota2((FOX_STRIP, nc), 1) <= _iota2((FOX_STRIP, nc), 0) + r * FOX_STRIP
                        s = jnp.where(keep, s, NEG_INF)
                    dqb = dqb_sc[h, rows, :]
                    m_old = m_sc[h, rows, :]
                    m_new = jnp.maximum(m_old, jnp.max(s, axis=-1, keepdims=True) + dqb)
                    alpha = jnp.exp2(m_old - m_new)
                    shift = m_new - dqb
                    if nc > LANES:
                        shift = jnp.concatenate([shift] * (nc // LANES), axis=1)
                    p = jnp.exp2(s - shift)
                    l_sc[h, rows, :] = alpha * l_sc[h, rows, :] + jnp.sum(p, axis=-1, keepdims=True)
                    m_sc[h, rows, :] = m_new
                    al_sc[hh, rows, :] = alpha
                    p_sc[hh, rows, 0:nc] = p.astype(BF16)
                    if nc < t:
                        p_sc[hh, rows, nc:t] = jnp.zeros((FOX_STRIP, t - nc), BF16)
                pvs.append(_dot(p_sc[hh], vp))
            acc_sc[j] = jnp.where(lo, al_sc[0], al_sc[1]) * acc_sc[j] + jnp.where(lo, pvs[0], pvs[1])

    def body(ki, carry):
        step(ki, False)
        return carry

    lax.fori_loop(0, qi, body, 0)
    step(qi, True)
    for j in range(npair):
        o_ref[0, :, j * LANES:(j + 1) * LANES] = acc_sc[j] / jnp.where(lo, l_sc[2 * j], l_sc[2 * j + 1])


def _fox_seq(qbf, kbf, vbf, dcum, dcum_t):
    b, L, _ = qbf.shape
    t = min(512, L)
    full = lambda w: pl.BlockSpec((1, L, w), lambda i, j: (i, 0, 0))
    return pl.pallas_call(
        functools.partial(_fox_seq_kernel, t),
        grid=(b, L // t),
        in_specs=[pl.BlockSpec((1, t, W_C), lambda i, j: (i, j, 0)), full(W_C), full(W_C),
                  pl.BlockSpec((1, t, LANES), lambda i, j: (i, j, 0)),
                  pl.BlockSpec((1, LANES, L), lambda i, j: (i, 0, 0))],
        out_specs=pl.BlockSpec((1, t, W_C), lambda i, j: (i, j, 0)),
        out_shape=jax.ShapeDtypeStruct((b, L, W_C), F32),
        scratch_shapes=[pltpu.VMEM((H_C, t, LANES), F32), pltpu.VMEM((H_C, t, LANES), F32),
                        pltpu.VMEM((H_C // 2, t, LANES), F32), pltpu.VMEM((H_C, t, LANES), F32),
                        pltpu.VMEM((2, t, t), F32), pltpu.VMEM((2, t, t), BF16),
                        pltpu.VMEM((2, t, LANES), F32)],
        compiler_params=_params(("arbitrary", "arbitrary")),
        name="fox_seq",
    )(qbf, kbf, vbf, dcum, dcum_t)


def _fox_dec_kernel(L, n_pages, pt_ref, q_ref, kn_ref, vn_ref, sm_ref, lf_ref, *refs):
    k_refs = refs[0:n_pages]
    v_refs = refs[n_pages:2 * n_pages]
    o_ref = refs[2 * n_pages]
    x_sc = refs[2 * n_pages + 1]
    i = pl.program_id(0)
    R = H_C * L
    q = q_ref[0] * (HEAD_DIM ** -0.5)
    rowh = _iota2((R, W_C), 0) // L
    colh = _iota2((R, W_C), 1) // HEAD_DIM
    qbd = jnp.where(rowh == colh, jnp.concatenate([q] * H_C, axis=0), 0.0).astype(BF16)

    for p in range(n_pages):
        pg = pt_ref[i * n_pages + p]
        for h in range(H_C):
            r = h * n_pages + p
            x_sc[r:r + 1, :] = lf_ref[h, pl.ds(pg, 1), :]
    x = x_sc[...]
    n = H_C * n_pages
    later = (_iota2((PAGE, PAGE), 0) > _iota2((PAGE, PAGE), 1)).astype(F32)
    within = _dot_hi(x, later)
    tot = _dot_hi(x, jnp.ones((PAGE, PAGE), F32))
    ri = _iota2((n, n), 0)
    ci = _iota2((n, n), 1)
    later_pages = ((ci // n_pages == ri // n_pages) & (ci % n_pages > ri % n_pages)).astype(F32)
    rsum = within + _dot_hi(later_pages, tot)

    sm = sm_ref[0]
    tri = (_iota2((L, L), 0) >= _iota2((L, L), 1)).astype(F32)
    cq = _dot_hi(tri, sm)
    cq_t = cq.T
    cq_col = jnp.concatenate([cq[:, SM_LF + h:SM_LF + h + 1] for h in range(H_C)], axis=0)
    cq_row = jnp.concatenate([jnp.broadcast_to(cq_t[SM_LF + h:SM_LF + h + 1, :], (L, L))
                              for h in range(H_C)], axis=0)

    s_pages = []
    for p in range(n_pages):
        sp = _dot(qbd, k_refs[p][...].astype(BF16))
        bias = jnp.concatenate(
            [jnp.broadcast_to(rsum[h * n_pages + p:h * n_pages + p + 1, :], (L, PAGE)) for h in range(H_C)],
            axis=0)
        s_pages.append(sp + bias + cq_col)
    s_new = _dot_nt(qbd, kn_ref[0].astype(BF16)) + cq_col - cq_row
    qpos = _iota2((R, L), 0) % L
    s_new = jnp.where(_iota2((R, L), 1) <= qpos, s_new, NEG_INF)

    mx = jnp.max(s_new, axis=-1, keepdims=True)
    for sp in s_pages:
        mx = jnp.maximum(mx, jnp.max(sp, axis=-1, keepdims=True))
    p_new = jnp.exp(s_new - mx)
    den = jnp.sum(p_new, axis=-1, keepdims=True)
    acc = _dot(p_new.astype(BF16), vn_ref[0].astype(BF16))
    for p in range(n_pages):
        pp = jnp.exp(s_pages[p] - mx)
        den = den + jnp.sum(pp, axis=-1, keepdims=True)
        acc = acc + _dot_nt(pp.astype(BF16), v_refs[p][...].astype(BF16))
    acc = jnp.where(rowh == colh, acc / den, 0.0)
    out = acc[0:L]
    for h in range(1, H_C):
        out = out + acc[h * L:(h + 1) * L]
    o_ref[0] = out


def _fox_dec(layer, page_table, q, kn, vn, small, cache_kt, cache_vt, cache_lft):
    b, L, _ = q.shape
    n_pages = page_table.shape[1]
    n_pool = cache_kt.shape[1]
    tok = lambda w: pl.BlockSpec((1, L, w), lambda i, pt: (i, 0, 0))

    def page_spec(p):
        return pl.BlockSpec((None, None, W_C, PAGE), lambda i, pt: (layer, pt[i * n_pages + p], 0, 0))

    in_specs = [tok(W_C), tok(W_C), tok(W_C), tok(LANES),
                pl.BlockSpec((None, H_C, n_pool, PAGE), lambda i, pt: (layer, 0, 0, 0),
                             pipeline_mode=pl.Buffered(1))]
    in_specs += [page_spec(p) for p in range(n_pages)]
    in_specs += [page_spec(p) for p in range(n_pages)]
    grid_spec = pltpu.PrefetchScalarGridSpec(
        num_scalar_prefetch=1, grid=(b,), in_specs=in_specs, out_specs=tok(W_C),
        scratch_shapes=[pltpu.VMEM((H_C * n_pages, PAGE), F32)])
    return pl.pallas_call(
        functools.partial(_fox_dec_kernel, L, n_pages),
        grid_spec=grid_spec,
        out_shape=jax.ShapeDtypeStruct((b, L, W_C), F32),
        compiler_params=_params(("arbitrary",)),
        name="fox_dec",
    )(page_table.reshape(-1), q, kn, vn, small, cache_lft,
      *([cache_kt] * n_pages), *([cache_vt] * n_pages))


def _post_kernel(bb, lt, x_ref, oa_ref, ob_ref, oc_ref, gate1_ref, shift2_ref, scale2_ref, gate2_ref,
                 gpm_ref, gpf_ref, gqf_ref, wo_ref, wu_ref, cw_ref, cst_ref, wd_ref,
                 y_ref, tail_ref, ext_ref, car_ref):
    l = pl.program_id(1)
    m = bb * lt
    keep = FFN_CONV - 1

    def rms(v, g_ref):
        return v * lax.rsqrt(jnp.mean(v * v, axis=-1, keepdims=True) + EPS) * g_ref[...]

    oa = oa_ref[...].reshape(m, W_A).astype(BF16)
    ob = ob_ref[...].reshape(m, W_B).astype(BF16)
    oc = oc_ref[...].reshape(m, W_C).astype(BF16)
    o = (_dot(oa, wo_ref[0:W_A, :]) + _dot(ob, wo_ref[W_A:W_A + W_B, :])
         + _dot(oc, wo_ref[W_A + W_B:W_A + W_B + W_C, :]))
    x1 = x_ref[...] + gate1_ref[...] * rms(o, gpm_ref).reshape(bb, lt, D_MODEL)

    h = rms(x1, gpf_ref) * (1.0 + scale2_ref[...]) + shift2_ref[...]
    h2 = h.reshape(m, D_MODEL).astype(BF16)

    @pl.when(l == 0)
    def _():
        car_ref[...] = jnp.zeros_like(car_ref)
        car_ref[:, SUBLANES - keep:SUBLANES, :] = cst_ref[...]

    def conv_cols(c0, w):
        up = _dot(h2, wu_ref[:, c0:c0 + w]).reshape(bb, lt, w)
        ext_ref[:, 0:SUBLANES, :] = car_ref[:, :, c0:c0 + w]
        ext_ref[:, SUBLANES:, :] = up
        car_ref[:, :, c0:c0 + w] = ext_ref[:, lt:lt + SUBLANES, :]
        y = up * cw_ref[FFN_CONV - 1:FFN_CONV, c0:c0 + w]
        for i in range(keep):
            off = SUBLANES - keep + i
            y = y + ext_ref[:, off:off + lt, :] * cw_ref[i:i + 1, c0:c0 + w]
        return y

    y = jnp.zeros((m, D_MODEL), F32)
    for j in range(D_FF // FF_CHUNK):
        a = conv_cols(j * FF_CHUNK, FF_CHUNK)
        b = conv_cols(D_FF + j * FF_CHUNK, FF_CHUNK)
        g = (jax.nn.gelu(a) * b).reshape(m, FF_CHUNK).astype(BF16)
        y = y + _dot(g, wd_ref[j * FF_CHUNK:(j + 1) * FF_CHUNK, :])
    tail_ref[...] = car_ref[...]
    y_ref[...] = x1 + gate2_ref[...] * rms(y, gqf_ref).reshape(bb, lt, D_MODEL)


def _post(dec, x, oa, ob, oc, gate1, shift2, scale2, gate2, gpm, gpf, gqf, wo, wu, cw, cst, wd):
    b, L, _ = x.shape
    if dec:
        bb, lt = min(16, b), L
    else:
        bb, lt = 1, min(512, L)
    tok = lambda w: pl.BlockSpec((bb, lt, w), lambda i, j: (i, j, 0))
    per_b = lambda r, w: pl.BlockSpec((bb, r, w), lambda i, j: (i, 0, 0))
    vec = _const_spec((1, D_MODEL))
    in_specs = [tok(D_MODEL), tok(W_A), tok(W_B), tok(W_C), per_b(1, D_MODEL), per_b(1, D_MODEL),
                per_b(1, D_MODEL), per_b(1, D_MODEL), vec, vec, vec,
                _const_spec((W_A + W_B + W_C, D_MODEL)), _const_spec((D_MODEL, 2 * D_FF)),
                _const_spec((FFN_CONV, 2 * D_FF)), per_b(FFN_CONV - 1, 2 * D_FF),
                _const_spec((D_FF, D_MODEL))]
    return pl.pallas_call(
        functools.partial(_post_kernel, bb, lt),
        grid=(b // bb, L // lt),
        in_specs=in_specs,
        out_specs=[tok(D_MODEL), per_b(SUBLANES, 2 * D_FF)],
        out_shape=[jax.ShapeDtypeStruct((b, L, D_MODEL), F32),
                   jax.ShapeDtypeStruct((b, SUBLANES, 2 * D_FF), F32)],
        scratch_shapes=[pltpu.VMEM((bb, lt + SUBLANES, FF_CHUNK), F32),
                        pltpu.VMEM((bb, SUBLANES, 2 * D_FF), F32)],
        compiler_params=_params(("arbitrary", "arbitrary")),
        name="post_dec" if dec else "post_seq",
    )(x, oa, ob, oc, gate1, shift2, scale2, gate2, gpm, gpf, gqf, wo, wu, cw, cst, wd)


def _layer_params(l, gdn_A_log, gdn_dt_bias, fox_f_bias, chunk_w_s, chunk_b_s, dec_len):
    z = lambda n: jnp.zeros((n,), F32)
    smb = jnp.concatenate([z(H_A), gdn_dt_bias[l], fox_f_bias[l], z(LANES - 2 * H_A - H_C)]).reshape(1, LANES)
    alog = jnp.concatenate([z(H_A), gdn_A_log[l], z(LANES - 2 * H_A)]).reshape(1, LANES)
    ws = chunk_w_s[l]
    bs = chunk_b_s[l]
    ws_seq = ws.reshape(G_B * CHUNK, CHUNK)
    bsb_seq = jnp.repeat(bs.T, HEAD_DIM, axis=1)
    ws_dec = jnp.repeat(jnp.transpose(ws[:, :dec_len, :dec_len], (2, 1, 0)), HEAD_DIM, axis=2)
    bsb_dec = bsb_seq[:dec_len]
    return smb, alog, ws_seq, bsb_seq, ws_dec, bsb_dec


def kernel(x_prompt, x_sample, state_gdn_conv, state_gdn_S, cache_fox_k, cache_fox_v, cache_fox_logf,
           state_ffn_conv, page_table, c_prompt, c_sample, w_ada, b_ada, g_pre_mix, g_post_mix, g_pre_ffn,
           g_post_ffn, w_in, w_out, gdn_conv_w, gdn_A_log, gdn_dt_bias, gdn_norm_g, chunk_ln_g, chunk_ln_b,
           chunk_w_s, chunk_b_s, fox_f_bias, w_up, ffn_conv_w, w_down):
    depth = w_in.shape[0]
    bp, lp, _ = x_prompt.shape
    bs, ls, _ = x_sample.shape
    n_pool = cache_fox_k.shape[1]

    cache_kt = jnp.transpose(cache_fox_k, (0, 1, 3, 4, 2)).reshape(depth, n_pool, W_C, PAGE)
    cache_vt = jnp.transpose(cache_fox_v, (0, 1, 3, 4, 2)).reshape(depth, n_pool, W_C, PAGE)
    cache_lft = jnp.transpose(cache_fox_logf, (0, 3, 1, 2))

    c_all = jnp.concatenate([c_prompt, c_sample], axis=0)
    pad = (-c_all.shape[0]) % SUBLANES
    c_all = jnp.pad(c_all, ((0, pad), (0, 0)))

    w_fused = _w_in_prep(w_in)
    pages = tuple(jnp.zeros((depth, bp, lp // PAGE, W_C, PAGE), F32) for _ in range(2))

    vec = lambda a: a.reshape(1, -1)
    outs = {k: [] for k in ("p_conv", "p_S", "p_lf", "p_ffn",
                            "s_conv", "s_S", "s_k", "s_v", "s_lf", "s_cv", "s_ffn")}
    xp, xs = x_prompt, x_sample
    for l in range(depth):
        mod = _ada(c_all, w_ada[l], b_ada[l])
        mod_p = mod[:bp].reshape(bp, 6, 1, D_MODEL)
        mod_s = mod[bp:bp + bs].reshape(bs, 6, 1, D_MODEL)
        smb, alog, ws_seq, bsb_seq, ws_dec, bsb_dec = _layer_params(
            l, gdn_A_log, gdn_dt_bias, fox_f_bias, chunk_w_s, chunk_b_s, ls)
        ng = jnp.tile(gdn_norm_g[l], H_A).reshape(1, W_A)
        wo = w_out[l].astype(BF16)
        wu = w_up[l].astype(BF16)
        wd = w_down[l].astype(BF16)
        common_in = (vec(g_pre_mix[l]), w_fused, gdn_conv_w[l])
        common_b = (smb, alog, vec(chunk_ln_g[l]), vec(chunk_ln_b[l]))
        post_w = (vec(g_post_mix[l]), vec(g_pre_ffn[l]), vec(g_post_ffn[l]), wo, wu, ffn_conv_w[l])

        (qkva, gate, small, ob, tail, pk, pv, qbf, kbf, vbf, dcum, small_t, dcum_t) = _in_proj(
            False, l, xp, mod_p[:, 0], mod_p[:, 1], *common_in,
            jnp.zeros((bp, GDN_CONV - 1, 3 * W_A), F32), *common_b, ws_seq, bsb_seq, pages=pages)
        pages = (pk, pv)
        oa, s_new = _gdn_seq(qkva, small, gate, ng, jnp.zeros((bp, H_A, HEAD_DIM, HEAD_DIM), F32))
        oc = _fox_seq(qbf, kbf, vbf, dcum, dcum_t)
        xp, ftail = _post(False, xp, oa, ob, oc, mod_p[:, 2], mod_p[:, 3], mod_p[:, 4], mod_p[:, 5],
                          *post_w, jnp.zeros((bp, FFN_CONV - 1, 2 * D_FF), F32), wd)
        outs["p_conv"].append(tail[:, SUBLANES - (GDN_CONV - 1):])
        outs["p_S"].append(s_new)
        lf_hm = small_t[:, SM_LF:SM_LF + H_C, :].reshape(bp, H_C, lp // PAGE, PAGE)
        outs["p_lf"].append(jnp.transpose(lf_hm, (0, 2, 3, 1)))
        outs["p_ffn"].append(ftail[:, SUBLANES - (FFN_CONV - 1):])

        (qkva, gate, small, ob, tail, kc, vc, vb, qd) = _in_proj(
            True, l, xs, mod_s[:, 0], mod_s[:, 1], *common_in, state_gdn_conv[l], *common_b, ws_dec, bsb_dec)
        to_lanes = lambda a: jnp.transpose(a, (1, 2, 0))
        oa_t, s_new_t = _gdn_dec(to_lanes(qkva), to_lanes(small), to_lanes(gate),
                                 jnp.broadcast_to(ng.reshape(W_A, 1), (W_A, bs)),
                                 jnp.transpose(state_gdn_S[l], (1, 2, 3, 0)))
        oa = jnp.transpose(oa_t, (2, 0, 1))
        s_new = jnp.transpose(s_new_t, (3, 0, 1, 2))
        oc = _fox_dec(l, page_table, qd, kc, vc, small, cache_kt, cache_vt, cache_lft)
        xs, ftail = _post(True, xs, oa, ob, oc, mod_s[:, 2], mod_s[:, 3], mod_s[:, 4], mod_s[:, 5],
                          *post_w, state_ffn_conv[l], wd)
        outs["s_conv"].append(tail[:, SUBLANES - (GDN_CONV - 1):])
        outs["s_S"].append(s_new)
        outs["s_k"].append(kc.reshape(bs, ls, H_C, HEAD_DIM))
        outs["s_v"].append(vc.reshape(bs, ls, H_C, HEAD_DIM))
        outs["s_lf"].append(small[:, :, SM_LF:SM_LF + H_C])
        outs["s_cv"].append(vb)
        outs["s_ffn"].append(ftail[:, SUBLANES - (FFN_CONV - 1):])

    st = lambda k: jnp.stack(outs[k])
    from_pages = lambda p: jnp.transpose(
        p.reshape(depth, bp, lp // PAGE, H_C, HEAD_DIM, PAGE), (0, 1, 2, 5, 3, 4))
    return (xp, xs, st("p_conv"), st("p_S"), from_pages(pages[0]), from_pages(pages[1]), st("p_lf"), st("p_ffn"),
            st("s_conv"), st("s_S"), st("s_k"), st("s_v"), st("s_lf"), st("s_cv"), st("s_ffn"))
```

```python
import functools

import jax
import jax.numpy as jnp
from jax import lax
from jax.experimental import pallas as pl
from jax.experimental.pallas import tpu as pltpu

F32 = jnp.float32
BF16 = jnp.bfloat16

D_MODEL = 1024
HEAD_DIM = 64
H_A = 6
G_B = 4
H_C = 6
W_A = H_A * HEAD_DIM
W_B = G_B * HEAD_DIM
W_C = H_C * HEAD_DIM
GDN_CONV = 4
GDN_CHUNK = 64
CHUNK = 128
D_FF = 2816
FFN_CONV = 3
EPS = 1e-6
NEG_INF = -1e30
PAGE = 128
LOG2E = 1.4426950408889634

LANES = 128
SUBLANES = 8
VMEM_LIMIT = 56 * 1024 * 1024

QA_OFF = 0
GA_OFF = QA_OFF + 3 * W_A
UV_OFF = GA_OFF + W_A
QC_OFF = UV_OFF + 2 * W_B
SM_OFF = QC_OFF + 3 * W_C
N_FUSED = SM_OFF + LANES
SM_BETA = 0
SM_G = H_A
SM_LF = 2 * H_A

FF_CHUNK = D_FF // 2


def _silu(x):
    return x * jax.nn.sigmoid(x)


def _softplus(x):
    return jnp.maximum(x, 0.0) + jnp.log1p(jnp.exp(-jnp.abs(x)))


def _dot(a, b):
    return jnp.dot(a, b, preferred_element_type=F32)


def _dot_hi(a, b):
    return jnp.dot(a, b, preferred_element_type=F32, precision=lax.Precision.HIGHEST)


def _dot_nt(a, b):
    return lax.dot_general(a, b, (((1,), (1,)), ((), ())), preferred_element_type=F32)


def _dot_tn(a, b):
    return lax.dot_general(a, b, (((0,), (0,)), ((), ())), preferred_element_type=F32)


def _iota2(shape, dim):
    return lax.broadcasted_iota(jnp.int32, shape, dim)


def _params(sem):
    return pltpu.CompilerParams(dimension_semantics=sem, vmem_limit_bytes=VMEM_LIMIT)


def _const_spec(shape):
    nd = len(shape)
    return pl.BlockSpec(shape, lambda *_: (0,) * nd, pipeline_mode=pl.Buffered(1))


def _ada_kernel(c_ref, w_ref, b_ref, o_ref):
    c = _silu(c_ref[...]).astype(BF16)
    o_ref[...] = _dot(c, w_ref[...].astype(BF16)) + b_ref[...]


def _ada(c, w, b):
    m = c.shape[0]
    depth, _, n = w.shape
    tn = 1024
    return pl.pallas_call(
        _ada_kernel,
        grid=(depth, n // tn),
        in_specs=[pl.BlockSpec((m, D_MODEL), lambda l, j: (0, 0)),
                  pl.BlockSpec((None, D_MODEL, tn), lambda l, j: (l, 0, j)),
                  pl.BlockSpec((None, 1, tn), lambda l, j: (l, 0, j))],
        out_specs=pl.BlockSpec((None, m, tn), lambda l, j: (l, 0, j)),
        out_shape=jax.ShapeDtypeStruct((depth, m, n), F32),
        compiler_params=_params(("arbitrary", "arbitrary")),
        name="ada_mod",
    )(c, w, b.reshape(depth, 1, n))


def _w_in_prep_kernel(depth, a_ref, o_ref):
    l = pl.program_id(0)
    j = pl.program_id(1)
    nkt = D_MODEL // LANES
    rpn = nkt * depth
    n_regular = SM_OFF // LANES
    n_first = GA_OFF // LANES
    n_in = SM_OFF + 2 * H_A + H_C

    @pl.when(j < n_regular)
    def _():
        n0 = j * LANES + jnp.where(j >= n_first, 2 * H_A, 0)
        for kt in range(nkt):
            x = a_ref[pl.ds(n0 * rpn + kt * depth + l, LANES, stride=rpn), :]
            o_ref[0, kt * LANES:(kt + 1) * LANES, :] = x.T.astype(BF16)

    @pl.when(j == n_regular)
    def _():
        r1, r2 = 2 * SUBLANES, SUBLANES
        n2 = n_in - r2
        r = _iota2((r1 + r2, LANES), 0)
        m = _iota2((r1 + r2, LANES), 1)
        pick = ((m < 2 * H_A) & (r == m)) | (
            (m >= 2 * H_A) & (m < 2 * H_A + H_C) & (r == m - 2 * H_A + r1 + (r2 - H_C)))
        sel = pick.astype(BF16)
        for kt in range(nkt):
            x1 = a_ref[pl.ds(GA_OFF * rpn + kt * depth + l, r1, stride=rpn), :]
            x2 = a_ref[pl.ds(n2 * rpn + kt * depth + l, r2, stride=rpn), :]
            xc = jnp.concatenate([x1, x2], axis=0).astype(BF16)
            o_ref[0, kt * LANES:(kt + 1) * LANES, :] = _dot_tn(xc, sel).astype(BF16)


def _w_in_prep(w_in):
    depth, d, n_in = w_in.shape
    nkt = d // LANES
    view = w_in.reshape(depth, nkt, LANES, n_in).transpose(3, 1, 0, 2).reshape(n_in * nkt * depth, LANES)
    return pl.pallas_call(
        functools.partial(_w_in_prep_kernel, depth),
        grid=(depth, N_FUSED // LANES),
        in_specs=[pl.BlockSpec(view.shape, lambda l, j: (0, 0), pipeline_mode=pl.Buffered(1))],
        out_specs=pl.BlockSpec((1, d, LANES), lambda l, j: (l, 0, j)),
        out_shape=jax.ShapeDtypeStruct((depth, d, N_FUSED), BF16),
        compiler_params=_params(("arbitrary", "arbitrary")),
        name="w_in_prep",
    )(view)


def _in_kernel(dec, bb, lt, *refs):
    (x_ref, shift_ref, scale_ref, gpre_ref, w_ref, cw_ref, cst_ref, smb_ref, alog_ref,
     lng_ref, lnb_ref, ws_ref, bsb_ref) = refs[:13]
    if dec:
        (qkva_ref, gate_ref, small_ref, ob_ref, tail_ref, kc_ref, vc_ref, vb_ref, q_ref,
         ext_ref) = refs[13:]
    else:
        (_, _, qkva_ref, gate_ref, small_ref, ob_ref, tail_ref, pk_ref, pv_ref, qbf_ref, kbf_ref,
         vbf_ref, dcum_ref, smt_ref, dct_ref, ext_ref, dcar_ref) = refs[13:]
    l = pl.program_id(1)
    m = bb * lt

    x = x_ref[...]
    ms = jnp.mean(x * x, axis=-1, keepdims=True)
    h = x * lax.rsqrt(ms + EPS) * gpre_ref[...]
    h = h * (1.0 + scale_ref[...]) + shift_ref[...]
    h2 = h.reshape(m, D_MODEL).astype(BF16)

    wa = 3 * W_A
    za = _dot(h2, w_ref[:, QA_OFF:QA_OFF + wa]).reshape(bb, lt, wa)

    @pl.when(l == 0)
    def _():
        ext_ref[:, SUBLANES - (GDN_CONV - 1):SUBLANES, :] = cst_ref[...]

    @pl.when(l > 0)
    def _():
        ext_ref[:, 0:SUBLANES, :] = ext_ref[:, lt:lt + SUBLANES, :]

    ext_ref[:, SUBLANES:, :] = za
    y = za * cw_ref[GDN_CONV - 1:GDN_CONV, :]
    for i in range(GDN_CONV - 1):
        o = SUBLANES - (GDN_CONV - 1) + i
        y = y + ext_ref[:, o:o + lt, :] * cw_ref[i:i + 1, :]
    qkva_ref[...] = _silu(y)
    tail_ref[...] = ext_ref[:, lt:lt + SUBLANES, :]

    gate_ref[...] = _silu(_dot(h2, w_ref[:, GA_OFF:GA_OFF + W_A])).reshape(bb, lt, W_A)

    zs = _dot(h2, w_ref[:, SM_OFF:SM_OFF + LANES]) + smb_ref[...]
    lane = _iota2((1, LANES), 1)
    beta = jax.nn.sigmoid(zs)
    gval = -jnp.exp(alog_ref[...]) * _softplus(zs)
    lf = -_softplus(-zs)
    small = jnp.where(lane < SM_G, beta,
                      jnp.where(lane < SM_LF, gval, jnp.where(lane < SM_LF + H_C, lf, 0.0)))
    small_ref[...] = small.reshape(bb, lt, LANES)

    if not dec:
        tri = (_iota2((CHUNK, CHUNK), 0) >= _iota2((CHUNK, CHUNK), 1)).astype(F32)

        @pl.when(l == 0)
        def _():
            dcar_ref[...] = jnp.zeros_like(dcar_ref)

        run = dcar_ref[0:1, :]
        parts = []
        for c in range(lt // CHUNK):
            parts.append(_dot_hi(tri, small[c * CHUNK:(c + 1) * CHUNK]) + run)
            run = parts[-1][CHUNK - 1:CHUNK, :]
        dc = jnp.concatenate(parts, axis=0)
        dcum_ref[...] = dc.reshape(bb, lt, LANES)
        dcar_ref[0:1, :] = dc[lt - 1:lt, :]
        smt_ref[0] = small.T
        dct_ref[0] = dc.T

    uv = jax.nn.gelu(_dot(h2, w_ref[:, UV_OFF:UV_OFF + 2 * W_B]))
    u = uv[:, :W_B]
    v = uv[:, W_B:]
    mu = jnp.mean(v, axis=-1, keepdims=True)
    vc0 = v - mu
    var = jnp.mean(vc0 * vc0, axis=-1, keepdims=True)
    vb = vc0 * lax.rsqrt(var + EPS) * lng_ref[...] + lnb_ref[...]
    if dec:
        vb3 = vb.reshape(bb, lt, W_B)
        vb_ref[...] = vb3
        trow = _iota2((lt, W_B), 0)
        mix = jnp.zeros((bb, lt, W_B), F32)
        for s in range(lt):
            coef = jnp.where(trow >= s, ws_ref[s], 0.0)
            mix = mix + coef[None] * vb3[:, s:s + 1, :]
        ob_ref[...] = u.reshape(bb, lt, W_B) * (mix + bsb_ref[...][None])
    else:
        r_t = _iota2((G_B * CHUNK, CHUNK), 0) % CHUNK
        r_s = _iota2((G_B * CHUNK, CHUNK), 1)
        wst = jnp.where(r_s <= r_t, ws_ref[...], 0.0).astype(BF16)
        grp = _iota2((1, W_B), 1) // HEAD_DIM
        for c in range(lt // CHUNK):
            rows = slice(c * CHUNK, (c + 1) * CHUNK)
            r = _dot(wst, vb[rows].astype(BF16))
            mix = jnp.zeros((CHUNK, W_B), F32)
            for g in range(G_B):
                mix = jnp.where(grp == g, r[g * CHUNK:(g + 1) * CHUNK], mix)
            ob_ref[0, rows, :] = u[rows] * (mix + bsb_ref[...])

    zc = _dot(h2, w_ref[:, QC_OFF:QC_OFF + 3 * W_C])
    qc = zc[:, 0:W_C]
    kc = zc[:, W_C:2 * W_C]
    vc = zc[:, 2 * W_C:3 * W_C]
    if dec:
        kc_ref[...] = kc.reshape(bb, lt, W_C)
        vc_ref[...] = vc.reshape(bb, lt, W_C)
        q_ref[...] = qc.reshape(bb, lt, W_C)
    else:
        for c in range(lt // PAGE):
            rows = slice(c * PAGE, (c + 1) * PAGE)
            pk_ref[0, c] = kc[rows].T
            pv_ref[0, c] = vc[rows].T
        qbf_ref[...] = (qc * (LOG2E * HEAD_DIM ** -0.5)).astype(BF16).reshape(bb, lt, W_C)
        kbf_ref[...] = kc.astype(BF16).reshape(bb, lt, W_C)
        vbf_ref[...] = vc.astype(BF16).reshape(bb, lt, W_C)


def _in_proj(dec, layer, x, shift, scale, gpre, w_fused, conv_w, conv_state, smb, alog, lng, lnb, ws, bsb,
             pages=()):
    b, L, _ = x.shape
    if dec:
        bb, lt = min(32, b), L
    else:
        bb, lt = 1, min(512, L)
    grid = (b // bb, L // lt)
    tok = lambda w: pl.BlockSpec((bb, lt, w), lambda i, j: (i, j, 0))
    per_b = lambda r, w: pl.BlockSpec((bb, r, w), lambda i, j: (i, 0, 0))
    in_specs = [tok(D_MODEL), per_b(1, D_MODEL), per_b(1, D_MODEL), _const_spec((1, D_MODEL)),
                pl.BlockSpec((None, D_MODEL, N_FUSED), lambda i, j: (layer, 0, 0), pipeline_mode=pl.Buffered(1)),
                _const_spec((GDN_CONV, 3 * W_A)),
                per_b(GDN_CONV - 1, 3 * W_A), _const_spec((1, LANES)), _const_spec((1, LANES)),
                _const_spec((1, W_B)), _const_spec((1, W_B)), _const_spec(ws.shape), _const_spec(bsb.shape)]
    sds = lambda w, dt=F32: jax.ShapeDtypeStruct((b, L, w), dt)
    out_shape = [sds(3 * W_A), sds(W_A), sds(LANES), sds(W_B), jax.ShapeDtypeStruct((b, SUBLANES, 3 * W_A), F32)]
    out_specs = [tok(3 * W_A), tok(W_A), tok(LANES), tok(W_B), per_b(SUBLANES, 3 * W_A)]
    scratch = [pltpu.VMEM((bb, lt + SUBLANES, 3 * W_A), F32)]
    aliases = {}
    if dec:
        out_shape += [sds(W_C), sds(W_C), sds(W_B), sds(W_C)]
        out_specs += [tok(W_C), tok(W_C), tok(W_B), tok(W_C)]
    else:
        n_in = len(in_specs)
        in_specs += [pl.BlockSpec(memory_space=pl.ANY)] * 2
        aliases = {n_in: len(out_shape), n_in + 1: len(out_shape) + 1}
        page_spec = pl.BlockSpec((None, 1, lt // PAGE, W_C, PAGE), lambda i, j: (layer, i, j, 0, 0))
        head_major = pl.BlockSpec((1, LANES, lt), lambda i, j: (i, 0, j))
        out_shape += [jax.ShapeDtypeStruct(p.shape, p.dtype) for p in pages]
        out_specs += [page_spec, page_spec]
        out_shape += [sds(W_C, BF16), sds(W_C, BF16), sds(W_C, BF16), sds(LANES),
                      jax.ShapeDtypeStruct((b, LANES, L), F32), jax.ShapeDtypeStruct((b, LANES, L), F32)]
        out_specs += [tok(W_C), tok(W_C), tok(W_C), tok(LANES), head_major, head_major]
        scratch += [pltpu.VMEM((SUBLANES, LANES), F32)]
    return pl.pallas_call(
        functools.partial(_in_kernel, dec, bb, lt),
        grid=grid, in_specs=in_specs, out_specs=out_specs, out_shape=out_shape,
        scratch_shapes=scratch, input_output_aliases=aliases,
        compiler_params=_params(("arbitrary", "arbitrary")),
        name="in_proj_dec" if dec else "in_proj_seq",
    )(x, shift, scale, gpre, w_fused, conv_w, conv_state, smb, alog, lng, lnb, ws, bsb, *pages)


GDN_GROUP = 256
N_PAIR = H_A // 2


def _tri_inverse_m1(lms, blk):
    bf = lambda xs: [x.astype(BF16) for x in xs]
    mm_ = lambda xs, ys: [_dot(x, y) for x, y in zip(xs, ys)]
    dm = [jnp.where(blk, lm, 0.0) for lm in lms]
    nm = [lm - d for lm, d in zip(lms, dm)]
    dmb = bf(dm)
    d2 = mm_(dmb, dmb)
    d2b = bf(d2)
    d4 = mm_(d2b, d2b)
    d4b = bf(d4)
    d8 = mm_(d4b, d4b)
    a1 = [x2 - x1 - c for x2, x1, c in zip(d2, dm, mm_(dmb, d2b))]
    a2 = [x4 + x8 + c for x4, x8, c in zip(d4, d8, mm_(d4b, bf(d8)))]
    et = [x + y + c for x, y, c in zip(a1, a2, mm_(bf(a1), bf(a2)))]
    etb = bf(et)
    mm = [n + c for n, c in zip(nm, mm_(etb, bf(nm)))]
    mmb = bf(mm)
    m2 = mm_(mmb, mmb)
    a3 = [x2 - x1 - c for x2, x1, c in zip(m2, mm, mm_(mmb, bf(m2)))]
    return [x + e + c for x, e, c in zip(a3, et, mm_(bf(a3), etb))]


def _pair_cols(lo, a, c0, c1):
    return jnp.where(lo, a[:, c0:c0 + 1], a[:, c1:c1 + 1])


def _pair_rsqrt_norm(lo, x, scale):
    x2 = x * x
    s_lo = jnp.sum(jnp.where(lo, x2, 0.0), axis=-1, keepdims=True)
    s_hi = jnp.sum(jnp.where(lo, 0.0, x2), axis=-1, keepdims=True)
    return jnp.where(lo, lax.rsqrt(s_lo * scale + EPS), lax.rsqrt(s_hi * scale + EPS))


def _gdn_pre_kernel(T, qkv_ref, small_ref, qe_ref, o0_ref, egl_ref, m_ref, bm_ref):
    C = GDN_CHUNK
    row = _iota2((T, T), 0)
    col = _iota2((T, T), 1)
    same = (row // C) == (col // C)
    bd_tri = same & (row >= col)
    bd_strict = same & (row > col)
    blk = (row // 16) == (col // 16)
    lane = _iota2((1, LANES), 1)
    lo = lane < HEAD_DIM
    lo2 = (_iota2((1, 2 * LANES), 1) % LANES) < HEAD_DIM
    bd2 = (_iota2((LANES, 2 * LANES), 0) // HEAD_DIM) == ((_iota2((LANES, 2 * LANES), 1) % LANES) // HEAD_DIM)

    sm = small_ref[0]
    gc_all = _dot_hi(bd_tri.astype(F32), sm)
    gl_all = _dot_hi((col == (row // C) * C + (C - 1)).astype(F32), gc_all)
    gc_t = gc_all.T
    rhs, rhs_b, lms, attns, qins, kouts = [], [], [], [], [], []
    for j in range(N_PAIR):
        sl = slice(j * LANES, (j + 1) * LANES)
        qp = qkv_ref[0, :, j * LANES:(j + 1) * LANES]
        kp = qkv_ref[0, :, W_A + j * LANES:W_A + (j + 1) * LANES]
        vp = qkv_ref[0, :, 2 * W_A + j * LANES:2 * W_A + (j + 1) * LANES]
        qn = qp * _pair_rsqrt_norm(lo, qp, 1.0) * (HEAD_DIM ** -0.5)
        kn = kp * _pair_rsqrt_norm(lo, kp, 1.0)
        h0, h1 = 2 * j, 2 * j + 1
        beta_p = _pair_cols(lo, sm, SM_BETA + h0, SM_BETA + h1)
        gc_p = _pair_cols(lo, gc_all, SM_G + h0, SM_G + h1)
        gl_p = _pair_cols(lo, gl_all, SM_G + h0, SM_G + h1)
        eg_p = jnp.exp(gc_p)
        kb = kn * beta_p
        kn_bf = kn.astype(BF16)
        rhs.append(jnp.concatenate([kb * eg_p, vp * beta_p], axis=1))
        rhs_b.append(rhs[j].astype(BF16))
        qins.append(qn * eg_p)
        kouts.append((kn * jnp.exp(gl_p - gc_p)).astype(BF16))
        egl_ref[0, :, sl] = jnp.exp(gl_p)
        for hh in range(2):
            h = 2 * j + hh
            keep = lo if hh == 0 else jnp.logical_not(lo)
            gcol = gc_all[:, SM_G + h:SM_G + h + 1]
            grow = gc_t[SM_G + h:SM_G + h + 1, :]
            decay = jnp.where(bd_tri, jnp.exp(jnp.where(bd_tri, gcol - grow, 0.0)), 0.0)
            g = _dot_nt(jnp.where(keep, kb, 0.0).astype(BF16), kn_bf)
            lms.append(jnp.where(bd_strict, g * decay, 0.0))
            qk = _dot_nt(jnp.where(keep, qn, 0.0).astype(BF16), kn_bf)
            attns.append((qk * decay).astype(BF16))

    tm1 = _tri_inverse_m1(lms, blk)
    for j in range(N_PAIR):
        sl = slice(j * LANES, (j + 1) * LANES)
        wu = rhs[j] + jnp.where(lo2, _dot(tm1[2 * j].astype(BF16), rhs_b[j]),
                                _dot(tm1[2 * j + 1].astype(BF16), rhs_b[j]))
        wu_b = wu.astype(BF16)
        aw = jnp.where(lo2, _dot(attns[2 * j], wu_b), _dot(attns[2 * j + 1], wu_b))
        qe_ref[0, :, sl] = (qins[j] - aw[:, :LANES]).astype(BF16)
        o0_ref[0, :, sl] = aw[:, LANES:]
        for c in range(T // C):
            rows = slice(c * C, (c + 1) * C)
            mb = jnp.where(bd2, _dot_tn(kouts[j][rows], wu_b[rows]), 0.0)
            m_ref[0, c, j] = mb[:, :LANES].astype(BF16)
            bm_ref[0, c, j] = mb[:, LANES:]


def _gdn_pre(qkva, small):
    b, L, _ = qkva.shape
    T = min(GDN_GROUP, L)
    nc = T // GDN_CHUNK
    tok = lambda w: pl.BlockSpec((1, T, w), lambda i, j: (i, j, 0))
    per_chunk = pl.BlockSpec((1, nc, N_PAIR, LANES, LANES), lambda i, j: (i, j, 0, 0, 0))
    sds = lambda dt: jax.ShapeDtypeStruct((b, L, W_A), dt)
    chunk_sds = lambda dt: jax.ShapeDtypeStruct((b, L // GDN_CHUNK, N_PAIR, LANES, LANES), dt)
    return pl.pallas_call(
        functools.partial(_gdn_pre_kernel, T),
        grid=(b, L // T),
        in_specs=[tok(3 * W_A), tok(LANES)],
        out_specs=[tok(W_A)] * 3 + [per_chunk] * 2,
        out_shape=[sds(BF16), sds(F32), sds(F32), chunk_sds(BF16), chunk_sds(F32)],
        compiler_params=_params(("arbitrary", "arbitrary")),
        name="gdn_pre",
    )(qkva, small)


def _gdn_scan_kernel(nb, lt, qe_ref, o0_ref, egl_ref, m_ref, bm_ref, gate_ref, ng_ref, s0_ref,
                     oa_ref, sout_ref, s_sc):
    l = pl.program_id(0)
    C = GDN_CHUNK

    @pl.when(l == 0)
    def _():
        s_sc[...] = s0_ref[...]

    lane = _iota2((1, LANES), 1)
    lo = lane < HEAD_DIM

    def body(c, carry):
        r0 = pl.multiple_of(c * C, C)
        rows = pl.ds(r0, C)
        for b in range(nb):
            for j in range(N_PAIR):
                sl = slice(j * LANES, (j + 1) * LANES)
                s_old = s_sc[b, j]
                lhs = jnp.concatenate([m_ref[b, c, j], qe_ref[b, rows, sl]], axis=0)
                r = _dot(lhs, s_old.astype(BF16))
                s_sc[b, j] = s_old * egl_ref[b, pl.ds(r0, 1), sl] + (bm_ref[b, c, j] - r[:LANES])
                o = r[LANES:] + o0_ref[b, rows, sl]
                on = o * _pair_rsqrt_norm(lo, o, 1.0 / HEAD_DIM)
                oa_ref[b, rows, sl] = on * ng_ref[:, sl] * gate_ref[b, rows, sl]
        return carry

    lax.fori_loop(0, lt // C, body, 0)
    sout_ref[...] = s_sc[...]


def _gdn_scan(qe, o0, egl, m, bm, gate, ng, s0_bd):
    b, L, _ = qe.shape
    lt = min(512, L)
    tok = pl.BlockSpec((b, lt, W_A), lambda i: (0, i, 0))
    per_chunk = pl.BlockSpec((b, lt // GDN_CHUNK, N_PAIR, LANES, LANES), lambda i: (0, i, 0, 0, 0))
    st = pl.BlockSpec((b, N_PAIR, LANES, LANES), lambda i: (0, 0, 0, 0))
    return pl.pallas_call(
        functools.partial(_gdn_scan_kernel, b, lt),
        grid=(L // lt,),
        in_specs=[tok, tok, tok, per_chunk, per_chunk, tok, _const_spec((1, W_A)), st],
        out_specs=[tok, st],
        out_shape=[jax.ShapeDtypeStruct((b, L, W_A), F32),
                   jax.ShapeDtypeStruct((b, N_PAIR, LANES, LANES), F32)],
        scratch_shapes=[pltpu.VMEM((b, N_PAIR, LANES, LANES), F32)],
        compiler_params=_params(("arbitrary",)),
        name="gdn_scan",
    )(qe, o0, egl, m, bm, gate, ng, s0_bd)


def _to_pair_blockdiag(s):
    b = s.shape[0]
    s = s.reshape(b, N_PAIR, 2, HEAD_DIM, HEAD_DIM)
    z = jnp.zeros_like(s[:, :, 0])
    top = jnp.concatenate([s[:, :, 0], z], axis=-1)
    bot = jnp.concatenate([z, s[:, :, 1]], axis=-1)
    return jnp.concatenate([top, bot], axis=-2)


def _from_pair_blockdiag(sbd):
    b = sbd.shape[0]
    s = jnp.stack([sbd[:, :, :HEAD_DIM, :HEAD_DIM], sbd[:, :, HEAD_DIM:, HEAD_DIM:]], axis=2)
    return s.reshape(b, H_A, HEAD_DIM, HEAD_DIM)


def _gdn_seq(qkva, small, gate, ng, s0):
    qe, o0, egl, m, bm = _gdn_pre(qkva, small)
    oa, sbd = _gdn_scan(qe, o0, egl, m, bm, gate, ng, _to_pair_blockdiag(s0))
    return oa, _from_pair_blockdiag(sbd)


def _gdn_dec_kernel(L, q_ref, k_ref, v_ref, smt_ref, gate_ref, ngb_ref, s0_ref, oa_ref, sout_ref, kq_sc):
    h = pl.program_id(0)
    nb = q_ref.shape[-1]
    for t in range(L):
        q_t = q_ref[t]
        k_t = k_ref[t]
        q_t = q_t * (lax.rsqrt(jnp.sum(q_t * q_t, axis=0, keepdims=True) + EPS) * (HEAD_DIM ** -0.5))
        k_t = k_t * lax.rsqrt(jnp.sum(k_t * k_t, axis=0, keepdims=True) + EPS)
        kq_sc[0] = k_t
        kq_sc[1] = q_t
        beta = smt_ref[t, pl.ds(SM_BETA + h, 1), :]
        a = jnp.exp(smt_ref[t, pl.ds(SM_G + h, 1), :])
        src = s0_ref if t == 0 else sout_ref

        def k_dot_s(kk, acc):
            return acc + src[0, kk] * kq_sc[0, pl.ds(kk, 1), :]

        ks = lax.fori_loop(0, HEAD_DIM, k_dot_s, jnp.zeros((HEAD_DIM, nb), F32), unroll=8)
        delta = beta * (v_ref[t] - a * ks)

        def update(kk, acc):
            s_new = a * src[0, kk] + kq_sc[0, pl.ds(kk, 1), :] * delta
            sout_ref[0, kk] = s_new
            return acc + s_new * kq_sc[1, pl.ds(kk, 1), :]

        o = lax.fori_loop(0, HEAD_DIM, update, jnp.zeros((HEAD_DIM, nb), F32), unroll=8)
        on = o * lax.rsqrt(jnp.mean(o * o, axis=0, keepdims=True) + EPS)
        oa_ref[t] = on * ngb_ref[...] * gate_ref[t]


def _gdn_dec(qkv_t, small_t, gate_t, ngb, s0_t):
    L, _, nb = qkv_t.shape
    nh = W_A // HEAD_DIM
    head = lambda off: pl.BlockSpec((L, HEAD_DIM, nb), lambda h: (0, off + h, 0))
    st = pl.BlockSpec((1, HEAD_DIM, HEAD_DIM, nb), lambda h: (h, 0, 0, 0))
    return pl.pallas_call(
        functools.partial(_gdn_dec_kernel, L),
        grid=(H_A,),
        in_specs=[head(0), head(nh), head(2 * nh),
                  pl.BlockSpec((L, LANES, nb), lambda h: (0, 0, 0), pipeline_mode=pl.Buffered(1)),
                  head(0), pl.BlockSpec((HEAD_DIM, nb), lambda h: (h, 0)), st],
        out_specs=[head(0), st],
        out_shape=[jax.ShapeDtypeStruct((L, W_A, nb), F32),
                   jax.ShapeDtypeStruct((H_A, HEAD_DIM, HEAD_DIM, nb), F32)],
        scratch_shapes=[pltpu.VMEM((2, HEAD_DIM, nb), F32)],
        compiler_params=_params(("arbitrary",)),
        name="gdn_dec",
    )(qkv_t, qkv_t, qkv_t, small_t, gate_t, ngb, s0_t)


FOX_STRIP = 64


def _fox_seq_kernel(t, q_ref, k_ref, v_ref, dq_ref, dkt_ref, o_ref, m_sc, l_sc, acc_sc, dqb_sc,
                    s_sc, p_sc, al_sc):
    qi = pl.program_id(1)
    npair = H_C // 2
    lane = _iota2((1, LANES), 1)
    lo = lane < HEAD_DIM
    m_sc[...] = jnp.full_like(m_sc, NEG_INF)
    l_sc[...] = jnp.zeros_like(l_sc)
    acc_sc[...] = jnp.zeros_like(acc_sc)
    q = q_ref[0]
    dq = dq_ref[0] * LOG2E
    qms = []
    for h in range(H_C):
        qp = q[:, (h // 2) * LANES:(h // 2 + 1) * LANES]
        keep = lo if h % 2 == 0 else jnp.logical_not(lo)
        qms.append(jnp.where(keep, qp, jnp.zeros_like(qp)))
        dqb_sc[h] = jnp.broadcast_to(dq[:, SM_LF + h:SM_LF + h + 1], (t, LANES))
    def step(ki, masked):
        k0 = pl.multiple_of(ki * t, t)
        for j in range(npair):
            kp = k_ref[0, pl.ds(k0, t), j * LANES:(j + 1) * LANES]
            vp = v_ref[0, pl.ds(k0, t), j * LANES:(j + 1) * LANES]
            pvs = []
            for hh in range(2):
                h = 2 * j + hh
                s_sc[hh] = _dot_nt(qms[h], kp)
                dk2 = dkt_ref[0, SM_LF + h:SM_LF + h + 1, pl.ds(k0, t)] * LOG2E
                for r in range(t // FOX_STRIP):
                    rows = slice(r * FOX_STRIP, (r + 1) * FOX_STRIP)
                    nc = min(t, -(-((r + 1) * FOX_STRIP) // LANES) * LANES) if masked else t
                    s = s_sc[hh, rows, 0:nc] - dk2[:, 0:nc]
                    if masked:
                        keep = _iota2((FOX_STRIP, nc), 1) <= _iota2((FOX_STRIP, nc), 0) + r * FOX_STRIP
                        s = jnp.where(keep, s, NEG_INF)
                    dqb = dqb_sc[h, rows, :]
                    m_old = m_sc[h, rows, :]
                    m_new = jnp.maximum(m_old, jnp.max(s, axis=-1, keepdims=True) + dqb)
                    alpha = jnp.exp2(m_old - m_new)
                    shift = m_new - dqb
                    if nc > LANES:
                        shift = jnp.concatenate([shift] * (nc // LANES), axis=1)
                    p = jnp.exp2(s - shift)
                    l_sc[h, rows, :] = alpha * l_sc[h, rows, :] + jnp.sum(p, axis=-1, keepdims=True)
                    m_sc[h, rows, :] = m_new
                    al_sc[hh, rows, :] = alpha
                    p_sc[hh, rows, 0:nc] = p.astype(BF16)
                    if nc < t:
                        p_sc[hh, rows, nc:t] = jnp.zeros((FOX_STRIP, t - nc), BF16)
                pvs.append(_dot(p_sc[hh], vp))
            acc_sc[j] = jnp.where(lo, al_sc[0], al_sc[1]) * acc_sc[j] + jnp.where(lo, pvs[0], pvs[1])

    def body(ki, carry):
        step(ki, False)
        return carry

    lax.fori_loop(0, qi, body, 0)
    step(qi, True)
    for j in range(npair):
        o_ref[0, :, j * LANES:(j + 1) * LANES] = acc_sc[j] / jnp.where(lo, l_sc[2 * j], l_sc[2 * j + 1])


def _fox_seq(qbf, kbf, vbf, dcum, dcum_t):
    b, L, _ = qbf.shape
    t = min(512, L)
    full = lambda w: pl.BlockSpec((1, L, w), lambda i, j: (i, 0, 0))
    return pl.pallas_call(
        functools.partial(_fox_seq_kernel, t),
        grid=(b, L // t),
        in_specs=[pl.BlockSpec((1, t, W_C), lambda i, j: (i, j, 0)), full(W_C), full(W_C),
                  pl.BlockSpec((1, t, LANES), lambda i, j: (i, j, 0)),
                  pl.BlockSpec((1, LANES, L), lambda i, j: (i, 0, 0))],
        out_specs=pl.BlockSpec((1, t, W_C), lambda i, j: (i, j, 0)),
        out_shape=jax.ShapeDtypeStruct((b, L, W_C), F32),
        scratch_shapes=[pltpu.VMEM((H_C, t, LANES), F32), pltpu.VMEM((H_C, t, LANES), F32),
                        pltpu.VMEM((H_C // 2, t, LANES), F32), pltpu.VMEM((H_C, t, LANES), F32),
                        pltpu.VMEM((2, t, t), F32), pltpu.VMEM((2, t, t), BF16),
                        pltpu.VMEM((2, t, LANES), F32)],
        compiler_params=_params(("arbitrary", "arbitrary")),
        name="fox_seq",
    )(qbf, kbf, vbf, dcum, dcum_t)


def _fox_dec_kernel(L, n_pages, pt_ref, q_ref, kn_ref, vn_ref, sm_ref, lf_ref, *refs):
    k_refs = refs[0:n_pages]
    v_refs = refs[n_pages:2 * n_pages]
    o_ref = refs[2 * n_pages]
    x_sc = refs[2 * n_pages + 1]
    i = pl.program_id(0)
    R = H_C * L
    q = q_ref[0] * (HEAD_DIM ** -0.5)
    rowh = _iota2((R, W_C), 0) // L
    colh = _iota2((R, W_C), 1) // HEAD_DIM
    qbd = jnp.where(rowh == colh, jnp.concatenate([q] * H_C, axis=0), 0.0).astype(BF16)

    for p in range(n_pages):
        pg = pt_ref[i * n_pages + p]
        for h in range(H_C):
            r = h * n_pages + p
            x_sc[r:r + 1, :] = lf_ref[h, pl.ds(pg, 1), :]
    x = x_sc[...]
    n = H_C * n_pages
    later = (_iota2((PAGE, PAGE), 0) > _iota2((PAGE, PAGE), 1)).astype(F32)
    within = _dot_hi(x, later)
    tot = _dot_hi(x, jnp.ones((PAGE, PAGE), F32))
    ri = _iota2((n, n), 0)
    ci = _iota2((n, n), 1)
    later_pages = ((ci // n_pages == ri // n_pages) & (ci % n_pages > ri % n_pages)).astype(F32)
    rsum = within + _dot_hi(later_pages, tot)

    sm = sm_ref[0]
    tri = (_iota2((L, L), 0) >= _iota2((L, L), 1)).astype(F32)
    cq = _dot_hi(tri, sm)
    cq_t = cq.T
    cq_col = jnp.concatenate([cq[:, SM_LF + h:SM_LF + h + 1] for h in range(H_C)], axis=0)
    cq_row = jnp.concatenate([jnp.broadcast_to(cq_t[SM_LF + h:SM_LF + h + 1, :], (L, L))
                              for h in range(H_C)], axis=0)

    s_pages = []
    for p in range(n_pages):
        sp = _dot(qbd, k_refs[p][...].astype(BF16))
        bias = jnp.concatenate(
            [jnp.broadcast_to(rsum[h * n_pages + p:h * n_pages + p + 1, :], (L, PAGE)) for h in range(H_C)],
            axis=0)
        s_pages.append(sp + bias + cq_col)
    s_new = _dot_nt(qbd, kn_ref[0].astype(BF16)) + cq_col - cq_row
    qpos = _iota2((R, L), 0) % L
    s_new = jnp.where(_iota2((R, L), 1) <= qpos, s_new, NEG_INF)

    mx = jnp.max(s_new, axis=-1, keepdims=True)
    for sp in s_pages:
        mx = jnp.maximum(mx, jnp.max(sp, axis=-1, keepdims=True))
    p_new = jnp.exp(s_new - mx)
    den = jnp.sum(p_new, axis=-1, keepdims=True)
    acc = _dot(p_new.astype(BF16), vn_ref[0].astype(BF16))
    for p in range(n_pages):
        pp = jnp.exp(s_pages[p] - mx)
        den = den + jnp.sum(pp, axis=-1, keepdims=True)
        acc = acc + _dot_nt(pp.astype(BF16), v_refs[p][...].astype(BF16))
    acc = jnp.where(rowh == colh, acc / den, 0.0)
    out = acc[0:L]
    for h in range(1, H_C):
        out = out + acc[h * L:(h + 1) * L]
    o_ref[0] = out


def _fox_dec(layer, page_table, q, kn, vn, small, cache_kt, cache_vt, cache_lft):
    b, L, _ = q.shape
    n_pages = page_table.shape[1]
    n_pool = cache_kt.shape[1]
    tok = lambda w: pl.BlockSpec((1, L, w), lambda i, pt: (i, 0, 0))

    def page_spec(p):
        return pl.BlockSpec((None, None, W_C, PAGE), lambda i, pt: (layer, pt[i * n_pages + p], 0, 0))

    in_specs = [tok(W_C), tok(W_C), tok(W_C), tok(LANES),
                pl.BlockSpec((None, H_C, n_pool, PAGE), lambda i, pt: (layer, 0, 0, 0),
                             pipeline_mode=pl.Buffered(1))]
    in_specs += [page_spec(p) for p in range(n_pages)]
    in_specs += [page_spec(p) for p in range(n_pages)]
    grid_spec = pltpu.PrefetchScalarGridSpec(
        num_scalar_prefetch=1, grid=(b,), in_specs=in_specs, out_specs=tok(W_C),
        scratch_shapes=[pltpu.VMEM((H_C * n_pages, PAGE), F32)])
    return pl.pallas_call(
        functools.partial(_fox_dec_kernel, L, n_pages),
        grid_spec=grid_spec,
        out_shape=jax.ShapeDtypeStruct((b, L, W_C), F32),
        compiler_params=_params(("arbitrary",)),
        name="fox_dec",
    )(page_table.reshape(-1), q, kn, vn, small, cache_lft,
      *([cache_kt] * n_pages), *([cache_vt] * n_pages))


def _post_kernel(bb, lt, x_ref, oa_ref, ob_ref, oc_ref, gate1_ref, shift2_ref, scale2_ref, gate2_ref,
                 gpm_ref, gpf_ref, gqf_ref, wo_ref, wu_ref, cw_ref, cst_ref, wd_ref,
                 y_ref, tail_ref, ext_ref, car_ref):
    l = pl.program_id(1)
    m = bb * lt
    keep = FFN_CONV - 1

    def rms(v, g_ref):
        return v * lax.rsqrt(jnp.mean(v * v, axis=-1, keepdims=True) + EPS) * g_ref[...]

    oa = oa_ref[...].reshape(m, W_A).astype(BF16)
    ob = ob_ref[...].reshape(m, W_B).astype(BF16)
    oc = oc_ref[...].reshape(m, W_C).astype(BF16)
    o = (_dot(oa, wo_ref[0:W_A, :]) + _dot(ob, wo_ref[W_A:W_A + W_B, :])
         + _dot(oc, wo_ref[W_A + W_B:W_A + W_B + W_C, :]))
    x1 = x_ref[...] + gate1_ref[...] * rms(o, gpm_ref).reshape(bb, lt, D_MODEL)

    h = rms(x1, gpf_ref) * (1.0 + scale2_ref[...]) + shift2_ref[...]
    h2 = h.reshape(m, D_MODEL).astype(BF16)

    @pl.when(l == 0)
    def _():
        car_ref[...] = jnp.zeros_like(car_ref)
        car_ref[:, SUBLANES - keep:SUBLANES, :] = cst_ref[...]

    def conv_cols(c0, w):
        up = _dot(h2, wu_ref[:, c0:c0 + w]).reshape(bb, lt, w)
        ext_ref[:, 0:SUBLANES, :] = car_ref[:, :, c0:c0 + w]
        ext_ref[:, SUBLANES:, :] = up
        car_ref[:, :, c0:c0 + w] = ext_ref[:, lt:lt + SUBLANES, :]
        y = up * cw_ref[FFN_CONV - 1:FFN_CONV, c0:c0 + w]
        for i in range(keep):
            off = SUBLANES - keep + i
            y = y + ext_ref[:, off:off + lt, :] * cw_ref[i:i + 1, c0:c0 + w]
        return y

    y = jnp.zeros((m, D_MODEL), F32)
    for j in range(D_FF // FF_CHUNK):
        a = conv_cols(j * FF_CHUNK, FF_CHUNK)
        b = conv_cols(D_FF + j * FF_CHUNK, FF_CHUNK)
        g = (jax.nn.gelu(a) * b).reshape(m, FF_CHUNK).astype(BF16)
        y = y + _dot(g, wd_ref[j * FF_CHUNK:(j + 1) * FF_CHUNK, :])
    tail_ref[...] = car_ref[...]
    y_ref[...] = x1 + gate2_ref[...] * rms(y, gqf_ref).reshape(bb, lt, D_MODEL)


def _post(dec, layer, x, oa, ob, oc, gate1, shift2, scale2, gate2, gpm, gpf, gqf, wo, wu, cw, cst, wd):
    b, L, _ = x.shape
    if dec:
        bb, lt = min(16, b), L
    else:
        bb, lt = 1, min(512, L)
    tok = lambda w: pl.BlockSpec((bb, lt, w), lambda i, j: (i, j, 0))
    per_b = lambda r, w: pl.BlockSpec((bb, r, w), lambda i, j: (i, 0, 0))
    vec = _const_spec((1, D_MODEL))
    weight = lambda r, c: pl.BlockSpec((None, r, c), lambda i, j: (layer, 0, 0), pipeline_mode=pl.Buffered(1))
    in_specs = [tok(D_MODEL), tok(W_A), tok(W_B), tok(W_C), per_b(1, D_MODEL), per_b(1, D_MODEL),
                per_b(1, D_MODEL), per_b(1, D_MODEL), vec, vec, vec,
                weight(W_A + W_B + W_C, D_MODEL), weight(D_MODEL, 2 * D_FF),
                _const_spec((FFN_CONV, 2 * D_FF)), per_b(FFN_CONV - 1, 2 * D_FF),
                weight(D_FF, D_MODEL)]
    return pl.pallas_call(
        functools.partial(_post_kernel, bb, lt),
        grid=(b // bb, L // lt),
        in_specs=in_specs,
        out_specs=[tok(D_MODEL), per_b(SUBLANES, 2 * D_FF)],
        out_shape=[jax.ShapeDtypeStruct((b, L, D_MODEL), F32),
                   jax.ShapeDtypeStruct((b, SUBLANES, 2 * D_FF), F32)],
        scratch_shapes=[pltpu.VMEM((bb, lt + SUBLANES, FF_CHUNK), F32),
                        pltpu.VMEM((bb, SUBLANES, 2 * D_FF), F32)],
        compiler_params=_params(("arbitrary", "arbitrary")),
        name="post_dec" if dec else "post_seq",
    )(x, oa, ob, oc, gate1, shift2, scale2, gate2, gpm, gpf, gqf, wo, wu, cw, cst, wd)


def _layer_params(l, gdn_A_log, gdn_dt_bias, fox_f_bias, chunk_w_s, chunk_b_s, dec_len):
    z = lambda n: jnp.zeros((n,), F32)
    smb = jnp.concatenate([z(H_A), gdn_dt_bias[l], fox_f_bias[l], z(LANES - 2 * H_A - H_C)]).reshape(1, LANES)
    alog = jnp.concatenate([z(H_A), gdn_A_log[l], z(LANES - 2 * H_A)]).reshape(1, LANES)
    ws = chunk_w_s[l]
    bs = chunk_b_s[l]
    ws_seq = ws.reshape(G_B * CHUNK, CHUNK)
    bsb_seq = jnp.repeat(bs.T, HEAD_DIM, axis=1)
    ws_dec = jnp.repeat(jnp.transpose(ws[:, :dec_len, :dec_len], (2, 1, 0)), HEAD_DIM, axis=2)
    bsb_dec = bsb_seq[:dec_len]
    return smb, alog, ws_seq, bsb_seq, ws_dec, bsb_dec


def kernel(x_prompt, x_sample, state_gdn_conv, state_gdn_S, cache_fox_k, cache_fox_v, cache_fox_logf,
           state_ffn_conv, page_table, c_prompt, c_sample, w_ada, b_ada, g_pre_mix, g_post_mix, g_pre_ffn,
           g_post_ffn, w_in, w_out, gdn_conv_w, gdn_A_log, gdn_dt_bias, gdn_norm_g, chunk_ln_g, chunk_ln_b,
           chunk_w_s, chunk_b_s, fox_f_bias, w_up, ffn_conv_w, w_down):
    depth = w_in.shape[0]
    bp, lp, _ = x_prompt.shape
    bs, ls, _ = x_sample.shape
    n_pool = cache_fox_k.shape[1]

    cache_kt = jnp.transpose(cache_fox_k, (0, 1, 3, 4, 2)).reshape(depth, n_pool, W_C, PAGE)
    cache_vt = jnp.transpose(cache_fox_v, (0, 1, 3, 4, 2)).reshape(depth, n_pool, W_C, PAGE)
    cache_lft = jnp.transpose(cache_fox_logf, (0, 3, 1, 2))

    c_all = jnp.concatenate([c_prompt, c_sample], axis=0)
    pad = (-c_all.shape[0]) % SUBLANES
    c_all = jnp.pad(c_all, ((0, pad), (0, 0)))

    w_fused = _w_in_prep(w_in)
    pages = tuple(jnp.zeros((depth, bp, lp // PAGE, W_C, PAGE), F32) for _ in range(2))

    vec = lambda a: a.reshape(1, -1)
    outs = {k: [] for k in ("p_conv", "p_S", "p_lf", "p_ffn",
                            "s_conv", "s_S", "s_k", "s_v", "s_lf", "s_cv", "s_ffn")}
    xp, xs = x_prompt, x_sample
    mod_all = _ada(c_all, w_ada, b_ada)
    wo = w_out.astype(BF16)
    wu = w_up.astype(BF16)
    wd = w_down.astype(BF16)
    for l in range(depth):
        mod_p = mod_all[l, :bp].reshape(bp, 6, 1, D_MODEL)
        mod_s = mod_all[l, bp:bp + bs].reshape(bs, 6, 1, D_MODEL)
        smb, alog, ws_seq, bsb_seq, ws_dec, bsb_dec = _layer_params(
            l, gdn_A_log, gdn_dt_bias, fox_f_bias, chunk_w_s, chunk_b_s, ls)
        ng = jnp.tile(gdn_norm_g[l], H_A).reshape(1, W_A)
        common_in = (vec(g_pre_mix[l]), w_fused, gdn_conv_w[l])
        common_b = (smb, alog, vec(chunk_ln_g[l]), vec(chunk_ln_b[l]))
        post_w = (vec(g_post_mix[l]), vec(g_pre_ffn[l]), vec(g_post_ffn[l]), wo, wu, ffn_conv_w[l])

        (qkva, gate, small, ob, tail, pk, pv, qbf, kbf, vbf, dcum, small_t, dcum_t) = _in_proj(
            False, l, xp, mod_p[:, 0], mod_p[:, 1], *common_in,
            jnp.zeros((bp, GDN_CONV - 1, 3 * W_A), F32), *common_b, ws_seq, bsb_seq, pages=pages)
        pages = (pk, pv)
        oa, s_new = _gdn_seq(qkva, small, gate, ng, jnp.zeros((bp, H_A, HEAD_DIM, HEAD_DIM), F32))
        oc = _fox_seq(qbf, kbf, vbf, dcum, dcum_t)
        xp, ftail = _post(False, l, xp, oa, ob, oc, mod_p[:, 2], mod_p[:, 3], mod_p[:, 4], mod_p[:, 5],
                          *post_w, jnp.zeros((bp, FFN_CONV - 1, 2 * D_FF), F32), wd)
        outs["p_conv"].append(tail[:, SUBLANES - (GDN_CONV - 1):])
        outs["p_S"].append(s_new)
        lf_hm = small_t[:, SM_LF:SM_LF + H_C, :].reshape(bp, H_C, lp // PAGE, PAGE)
        outs["p_lf"].append(jnp.transpose(lf_hm, (0, 2, 3, 1)))
        outs["p_ffn"].append(ftail[:, SUBLANES - (FFN_CONV - 1):])

        (qkva, gate, small, ob, tail, kc, vc, vb, qd) = _in_proj(
            True, l, xs, mod_s[:, 0], mod_s[:, 1], *common_in, state_gdn_conv[l], *common_b, ws_dec, bsb_dec)
        to_lanes = lambda a: jnp.transpose(a, (1, 2, 0))
        oa_t, s_new_t = _gdn_dec(to_lanes(qkva), to_lanes(small), to_lanes(gate),
                                 jnp.broadcast_to(ng.reshape(W_A, 1), (W_A, bs)),
                                 jnp.transpose(state_gdn_S[l], (1, 2, 3, 0)))
        oa = jnp.transpose(oa_t, (2, 0, 1))
        s_new = jnp.transpose(s_new_t, (3, 0, 1, 2))
        oc = _fox_dec(l, page_table, qd, kc, vc, small, cache_kt, cache_vt, cache_lft)
        xs, ftail = _post(True, l, xs, oa, ob, oc, mod_s[:, 2], mod_s[:, 3], mod_s[:, 4], mod_s[:, 5],
                          *post_w, state_ffn_conv[l], wd)
        outs["s_conv"].append(tail[:, SUBLANES - (GDN_CONV - 1):])
        outs["s_S"].append(s_new)
        outs["s_k"].append(kc.reshape(bs, ls, H_C, HEAD_DIM))
        outs["s_v"].append(vc.reshape(bs, ls, H_C, HEAD_DIM))
        outs["s_lf"].append(small[:, :, SM_LF:SM_LF + H_C])
        outs["s_cv"].append(vb)
        outs["s_ffn"].append(ftail[:, SUBLANES - (FFN_CONV - 1):])

    st = lambda k: jnp.stack(outs[k])
    from_pages = lambda p: jnp.transpose(
        p.reshape(depth, bp, lp // PAGE, H_C, HEAD_DIM, PAGE), (0, 1, 2, 5, 3, 4))
    return (xp, xs, st("p_conv"), st("p_S"), from_pages(pages[0]), from_pages(pages[1]), st("p_lf"), st("p_ffn"),
            st("s_conv"), st("s_S"), st("s_k"), st("s_v"), st("s_lf"), st("s_cv"), st("s_ffn"))
```

```python
import functools

import jax
import jax.numpy as jnp
from jax import lax
from jax.experimental import pallas as pl
from jax.experimental.pallas import tpu as pltpu

F32 = jnp.float32
BF16 = jnp.bfloat16

D_MODEL = 1024
HEAD_DIM = 64
H_A = 6
G_B = 4
H_C = 6
W_A = H_A * HEAD_DIM
W_B = G_B * HEAD_DIM
W_C = H_C * HEAD_DIM
GDN_CONV = 4
GDN_CHUNK = 64
CHUNK = 128
D_FF = 2816
FFN_CONV = 3
EPS = 1e-6
NEG_INF = -1e30
PAGE = 128
LOG2E = 1.4426950408889634

LANES = 128
SUBLANES = 8
VMEM_LIMIT = 56 * 1024 * 1024

QA_OFF = 0
GA_OFF = QA_OFF + 3 * W_A
UV_OFF = GA_OFF + W_A
QC_OFF = UV_OFF + 2 * W_B
SM_OFF = QC_OFF + 3 * W_C
N_FUSED = SM_OFF + LANES
SM_BETA = 0
SM_G = H_A
SM_LF = 2 * H_A

FF_CHUNK = D_FF // 2


def _silu(x):
    return x * jax.nn.sigmoid(x)


def _softplus(x):
    return jnp.maximum(x, 0.0) + jnp.log1p(jnp.exp(-jnp.abs(x)))


def _dot(a, b):
    return jnp.dot(a, b, preferred_element_type=F32)


def _dot_hi(a, b):
    return jnp.dot(a, b, preferred_element_type=F32, precision=lax.Precision.HIGHEST)


def _dot_nt(a, b):
    return lax.dot_general(a, b, (((1,), (1,)), ((), ())), preferred_element_type=F32)


def _dot_tn(a, b):
    return lax.dot_general(a, b, (((0,), (0,)), ((), ())), preferred_element_type=F32)


def _iota2(shape, dim):
    return lax.broadcasted_iota(jnp.int32, shape, dim)


def _params(sem):
    return pltpu.CompilerParams(dimension_semantics=sem, vmem_limit_bytes=VMEM_LIMIT)


def _const_spec(shape):
    nd = len(shape)
    return pl.BlockSpec(shape, lambda *_: (0,) * nd, pipeline_mode=pl.Buffered(1))


def _ada_kernel(c_ref, w_ref, b_ref, o_ref):
    c = _silu(c_ref[...]).astype(BF16)
    o_ref[...] = _dot(c, w_ref[...].astype(BF16)) + b_ref[...]


def _ada(c, w, b):
    m = c.shape[0]
    depth, _, n = w.shape
    tn = 1024
    return pl.pallas_call(
        _ada_kernel,
        grid=(depth, n // tn),
        in_specs=[pl.BlockSpec((m, D_MODEL), lambda l, j: (0, 0)),
                  pl.BlockSpec((None, D_MODEL, tn), lambda l, j: (l, 0, j)),
                  pl.BlockSpec((None, 1, tn), lambda l, j: (l, 0, j))],
        out_specs=pl.BlockSpec((None, m, tn), lambda l, j: (l, 0, j)),
        out_shape=jax.ShapeDtypeStruct((depth, m, n), F32),
        compiler_params=_params(("arbitrary", "arbitrary")),
        name="ada_mod",
    )(c, w, b.reshape(depth, 1, n))


def _w_in_prep_kernel(depth, a_ref, o_ref):
    l = pl.program_id(0)
    j = pl.program_id(1)
    nkt = D_MODEL // LANES
    rpn = nkt * depth
    n_regular = SM_OFF // LANES
    n_first = GA_OFF // LANES
    n_in = SM_OFF + 2 * H_A + H_C

    @pl.when(j < n_regular)
    def _():
        n0 = j * LANES + jnp.where(j >= n_first, 2 * H_A, 0)
        for kt in range(nkt):
            x = a_ref[pl.ds(n0 * rpn + kt * depth + l, LANES, stride=rpn), :]
            o_ref[0, kt * LANES:(kt + 1) * LANES, :] = x.T.astype(BF16)

    @pl.when(j == n_regular)
    def _():
        r1, r2 = 2 * SUBLANES, SUBLANES
        n2 = n_in - r2
        r = _iota2((r1 + r2, LANES), 0)
        m = _iota2((r1 + r2, LANES), 1)
        pick = ((m < 2 * H_A) & (r == m)) | (
            (m >= 2 * H_A) & (m < 2 * H_A + H_C) & (r == m - 2 * H_A + r1 + (r2 - H_C)))
        sel = pick.astype(BF16)
        for kt in range(nkt):
            x1 = a_ref[pl.ds(GA_OFF * rpn + kt * depth + l, r1, stride=rpn), :]
            x2 = a_ref[pl.ds(n2 * rpn + kt * depth + l, r2, stride=rpn), :]
            xc = jnp.concatenate([x1, x2], axis=0).astype(BF16)
            o_ref[0, kt * LANES:(kt + 1) * LANES, :] = _dot_tn(xc, sel).astype(BF16)


def _w_in_prep(w_in):
    depth, d, n_in = w_in.shape
    nkt = d // LANES
    view = w_in.reshape(depth, nkt, LANES, n_in).transpose(3, 1, 0, 2).reshape(n_in * nkt * depth, LANES)
    return pl.pallas_call(
        functools.partial(_w_in_prep_kernel, depth),
        grid=(depth, N_FUSED // LANES),
        in_specs=[pl.BlockSpec(view.shape, lambda l, j: (0, 0), pipeline_mode=pl.Buffered(1))],
        out_specs=pl.BlockSpec((1, d, LANES), lambda l, j: (l, 0, j)),
        out_shape=jax.ShapeDtypeStruct((depth, d, N_FUSED), BF16),
        compiler_params=_params(("arbitrary", "arbitrary")),
        name="w_in_prep",
    )(view)


def _in_kernel(dec, bb, lt, *refs):
    (x_ref, shift_ref, scale_ref, gpre_ref, w_ref, cw_ref, cst_ref, smb_ref, alog_ref,
     lng_ref, lnb_ref, ws_ref, bsb_ref) = refs[:13]
    if dec:
        (qkva_ref, gate_ref, small_ref, ob_ref, tail_ref, kc_ref, vc_ref, vb_ref, q_ref,
         ext_ref) = refs[13:]
    else:
        (_, _, qkva_ref, gate_ref, small_ref, ob_ref, tail_ref, pk_ref, pv_ref, qbf_ref, kbf_ref,
         vbf_ref, dcum_ref, smt_ref, ext_ref, dcar_ref) = refs[13:]
    l = pl.program_id(1)
    m = bb * lt

    x = x_ref[...]
    ms = jnp.mean(x * x, axis=-1, keepdims=True)
    h = x * lax.rsqrt(ms + EPS) * gpre_ref[...]
    h = h * (1.0 + scale_ref[...]) + shift_ref[...]
    h2 = h.reshape(m, D_MODEL).astype(BF16)

    wa = 3 * W_A
    za = _dot(h2, w_ref[:, QA_OFF:QA_OFF + wa]).reshape(bb, lt, wa)

    @pl.when(l == 0)
    def _():
        ext_ref[:, SUBLANES - (GDN_CONV - 1):SUBLANES, :] = cst_ref[...]

    @pl.when(l > 0)
    def _():
        ext_ref[:, 0:SUBLANES, :] = ext_ref[:, lt:lt + SUBLANES, :]

    ext_ref[:, SUBLANES:, :] = za
    y = za * cw_ref[GDN_CONV - 1:GDN_CONV, :]
    for i in range(GDN_CONV - 1):
        o = SUBLANES - (GDN_CONV - 1) + i
        y = y + ext_ref[:, o:o + lt, :] * cw_ref[i:i + 1, :]
    qkva_ref[...] = _silu(y)
    tail_ref[...] = ext_ref[:, lt:lt + SUBLANES, :]

    gate_ref[...] = _silu(_dot(h2, w_ref[:, GA_OFF:GA_OFF + W_A])).reshape(bb, lt, W_A)

    zs = _dot(h2, w_ref[:, SM_OFF:SM_OFF + LANES]) + smb_ref[...]
    lane = _iota2((1, LANES), 1)
    beta = jax.nn.sigmoid(zs)
    gval = -jnp.exp(alog_ref[...]) * _softplus(zs)
    lf = -_softplus(-zs)
    small = jnp.where(lane < SM_G, beta,
                      jnp.where(lane < SM_LF, gval, jnp.where(lane < SM_LF + H_C, lf, 0.0)))
    small_ref[...] = small.reshape(bb, lt, LANES)

    if not dec:
        tri = (_iota2((CHUNK, CHUNK), 0) >= _iota2((CHUNK, CHUNK), 1)).astype(F32)

        @pl.when(l == 0)
        def _():
            dcar_ref[...] = jnp.zeros_like(dcar_ref)

        run = dcar_ref[0:1, :]
        parts = []
        for c in range(lt // CHUNK):
            parts.append(_dot_hi(tri, small[c * CHUNK:(c + 1) * CHUNK]) + run)
            run = parts[-1][CHUNK - 1:CHUNK, :]
        dc = jnp.concatenate(parts, axis=0)
        dcum_ref[...] = dc.reshape(bb, lt, LANES)
        dcar_ref[0:1, :] = dc[lt - 1:lt, :]
        smt_ref[0] = small.T

    uv = jax.nn.gelu(_dot(h2, w_ref[:, UV_OFF:UV_OFF + 2 * W_B]))
    u = uv[:, :W_B]
    v = uv[:, W_B:]
    mu = jnp.mean(v, axis=-1, keepdims=True)
    vc0 = v - mu
    var = jnp.mean(vc0 * vc0, axis=-1, keepdims=True)
    vb = vc0 * lax.rsqrt(var + EPS) * lng_ref[...] + lnb_ref[...]
    if dec:
        vb3 = vb.reshape(bb, lt, W_B)
        vb_ref[...] = vb3
        trow = _iota2((lt, W_B), 0)
        mix = jnp.zeros((bb, lt, W_B), F32)
        for s in range(lt):
            coef = jnp.where(trow >= s, ws_ref[s], 0.0)
            mix = mix + coef[None] * vb3[:, s:s + 1, :]
        ob_ref[...] = u.reshape(bb, lt, W_B) * (mix + bsb_ref[...][None])
    else:
        r_t = _iota2((G_B * CHUNK, CHUNK), 0) % CHUNK
        r_s = _iota2((G_B * CHUNK, CHUNK), 1)
        wst = jnp.where(r_s <= r_t, ws_ref[...], 0.0).astype(BF16)
        grp = _iota2((1, W_B), 1) // HEAD_DIM
        for c in range(lt // CHUNK):
            rows = slice(c * CHUNK, (c + 1) * CHUNK)
            r = _dot(wst, vb[rows].astype(BF16))
            mix = jnp.zeros((CHUNK, W_B), F32)
            for g in range(G_B):
                mix = jnp.where(grp == g, r[g * CHUNK:(g + 1) * CHUNK], mix)
            ob_ref[0, rows, :] = u[rows] * (mix + bsb_ref[...])

    zc = _dot(h2, w_ref[:, QC_OFF:QC_OFF + 3 * W_C])
    qc = zc[:, 0:W_C]
    kc = zc[:, W_C:2 * W_C]
    vc = zc[:, 2 * W_C:3 * W_C]
    if dec:
        kc_ref[...] = kc.reshape(bb, lt, W_C)
        vc_ref[...] = vc.reshape(bb, lt, W_C)
        q_ref[...] = qc.reshape(bb, lt, W_C)
    else:
        for c in range(lt // PAGE):
            rows = slice(c * PAGE, (c + 1) * PAGE)
            pk_ref[0, c] = kc[rows].T
            pv_ref[0, c] = vc[rows].T
        hd = HEAD_DIM
        d2 = dc * LOG2E
        for h in range(H_C):
            pair = slice((h // 2) * LANES, (h // 2 + 1) * LANES)
            at0 = (lambda a: a) if h % 2 == 0 else (lambda a: pltpu.roll(a, hd, axis=1))
            dh = d2[:, SM_LF + h:SM_LF + h + 1]
            t1 = dh.astype(BF16).astype(F32)
            t2 = (dh - t1).astype(BF16).astype(F32)
            t3 = dh - t1 - t2
            qa = jnp.where(lane < hd, at0(qc[:, pair]) * (LOG2E * hd ** -0.5), jnp.where(lane < hd + 3, 1.0, 0.0))
            ka = jnp.where(lane < hd, at0(kc[:, pair]),
                           jnp.where(lane == hd, -t1, jnp.where(lane == hd + 1, -t2,
                                                                jnp.where(lane == hd + 2, -t3, 0.0))))
            va = jnp.where(lane < hd, at0(vc[:, pair]), jnp.where(lane == hd, 1.0, 0.0))
            grp = slice(h * LANES, (h + 1) * LANES)
            qbf_ref[0, :, grp] = qa.astype(BF16)
            kbf_ref[0, :, grp] = ka.astype(BF16)
            vbf_ref[0, :, grp] = va.astype(BF16)


def _in_proj(dec, layer, x, shift, scale, gpre, w_fused, conv_w, conv_state, smb, alog, lng, lnb, ws, bsb,
             pages=()):
    b, L, _ = x.shape
    if dec:
        bb, lt = min(32, b), L
    else:
        bb, lt = 1, min(512, L)
    grid = (b // bb, L // lt)
    tok = lambda w: pl.BlockSpec((bb, lt, w), lambda i, j: (i, j, 0))
    per_b = lambda r, w: pl.BlockSpec((bb, r, w), lambda i, j: (i, 0, 0))
    in_specs = [tok(D_MODEL), per_b(1, D_MODEL), per_b(1, D_MODEL), _const_spec((1, D_MODEL)),
                pl.BlockSpec((None, D_MODEL, N_FUSED), lambda i, j: (layer, 0, 0), pipeline_mode=pl.Buffered(1)),
                _const_spec((GDN_CONV, 3 * W_A)),
                per_b(GDN_CONV - 1, 3 * W_A), _const_spec((1, LANES)), _const_spec((1, LANES)),
                _const_spec((1, W_B)), _const_spec((1, W_B)), _const_spec(ws.shape), _const_spec(bsb.shape)]
    sds = lambda w, dt=F32: jax.ShapeDtypeStruct((b, L, w), dt)
    out_shape = [sds(3 * W_A), sds(W_A), sds(LANES), sds(W_B), jax.ShapeDtypeStruct((b, SUBLANES, 3 * W_A), F32)]
    out_specs = [tok(3 * W_A), tok(W_A), tok(LANES), tok(W_B), per_b(SUBLANES, 3 * W_A)]
    scratch = [pltpu.VMEM((bb, lt + SUBLANES, 3 * W_A), F32)]
    aliases = {}
    if dec:
        out_shape += [sds(W_C), sds(W_C), sds(W_B), sds(W_C)]
        out_specs += [tok(W_C), tok(W_C), tok(W_B), tok(W_C)]
    else:
        n_in = len(in_specs)
        in_specs += [pl.BlockSpec(memory_space=pl.ANY)] * 2
        aliases = {n_in: len(out_shape), n_in + 1: len(out_shape) + 1}
        page_spec = pl.BlockSpec((None, 1, lt // PAGE, W_C, PAGE), lambda i, j: (layer, i, j, 0, 0))
        head_major = pl.BlockSpec((1, LANES, lt), lambda i, j: (i, 0, j))
        out_shape += [jax.ShapeDtypeStruct(p.shape, p.dtype) for p in pages]
        out_specs += [page_spec, page_spec]
        wf = H_C * LANES
        out_shape += [sds(wf, BF16), sds(wf, BF16), sds(wf, BF16), sds(LANES),
                      jax.ShapeDtypeStruct((b, LANES, L), F32)]
        out_specs += [tok(wf), tok(wf), tok(wf), tok(LANES), head_major]
        scratch += [pltpu.VMEM((SUBLANES, LANES), F32)]
    return pl.pallas_call(
        functools.partial(_in_kernel, dec, bb, lt),
        grid=grid, in_specs=in_specs, out_specs=out_specs, out_shape=out_shape,
        scratch_shapes=scratch, input_output_aliases=aliases,
        compiler_params=_params(("arbitrary", "arbitrary")),
        name="in_proj_dec" if dec else "in_proj_seq",
    )(x, shift, scale, gpre, w_fused, conv_w, conv_state, smb, alog, lng, lnb, ws, bsb, *pages)


GDN_GROUP = 256
N_PAIR = H_A // 2


def _tri_inverse_m1(lms, blk):
    bf = lambda xs: [x.astype(BF16) for x in xs]
    mm_ = lambda xs, ys: [_dot(x, y) for x, y in zip(xs, ys)]
    dm = [jnp.where(blk, lm, 0.0) for lm in lms]
    nm = [lm - d for lm, d in zip(lms, dm)]
    dmb = bf(dm)
    d2 = mm_(dmb, dmb)
    d2b = bf(d2)
    d4 = mm_(d2b, d2b)
    d4b = bf(d4)
    d8 = mm_(d4b, d4b)
    a1 = [x2 - x1 - c for x2, x1, c in zip(d2, dm, mm_(dmb, d2b))]
    a2 = [x4 + x8 + c for x4, x8, c in zip(d4, d8, mm_(d4b, bf(d8)))]
    et = [x + y + c for x, y, c in zip(a1, a2, mm_(bf(a1), bf(a2)))]
    etb = bf(et)
    mm = [n + c for n, c in zip(nm, mm_(etb, bf(nm)))]
    mmb = bf(mm)
    m2 = mm_(mmb, mmb)
    a3 = [x2 - x1 - c for x2, x1, c in zip(m2, mm, mm_(mmb, bf(m2)))]
    return [x + e + c for x, e, c in zip(a3, et, mm_(bf(a3), etb))]


def _pair_cols(lo, a, c0, c1):
    return jnp.where(lo, a[:, c0:c0 + 1], a[:, c1:c1 + 1])


def _pair_rsqrt_norm(lo, x, scale):
    x2 = x * x
    s_lo = jnp.sum(jnp.where(lo, x2, 0.0), axis=-1, keepdims=True)
    s_hi = jnp.sum(jnp.where(lo, 0.0, x2), axis=-1, keepdims=True)
    return jnp.where(lo, lax.rsqrt(s_lo * scale + EPS), lax.rsqrt(s_hi * scale + EPS))


def _gdn_pre_kernel(T, qkv_ref, small_ref, qe_ref, o0_ref, egl_ref, m_ref, bm_ref):
    C = GDN_CHUNK
    row = _iota2((T, T), 0)
    col = _iota2((T, T), 1)
    same = (row // C) == (col // C)
    bd_tri = same & (row >= col)
    bd_strict = same & (row > col)
    blk = (row // 16) == (col // 16)
    lane = _iota2((1, LANES), 1)
    lo = lane < HEAD_DIM
    lo2 = (_iota2((1, 2 * LANES), 1) % LANES) < HEAD_DIM
    bd2 = (_iota2((LANES, 2 * LANES), 0) // HEAD_DIM) == ((_iota2((LANES, 2 * LANES), 1) % LANES) // HEAD_DIM)

    sm = small_ref[0]
    gc_all = _dot_hi(bd_tri.astype(F32), sm)
    gl_all = _dot_hi((col == (row // C) * C + (C - 1)).astype(F32), gc_all)
    gc_t = gc_all.T
    rhs, rhs_b, lms, attns, qins, kouts = [], [], [], [], [], []
    for j in range(N_PAIR):
        sl = slice(j * LANES, (j + 1) * LANES)
        qp = qkv_ref[0, :, j * LANES:(j + 1) * LANES]
        kp = qkv_ref[0, :, W_A + j * LANES:W_A + (j + 1) * LANES]
        vp = qkv_ref[0, :, 2 * W_A + j * LANES:2 * W_A + (j + 1) * LANES]
        qn = qp * _pair_rsqrt_norm(lo, qp, 1.0) * (HEAD_DIM ** -0.5)
        kn = kp * _pair_rsqrt_norm(lo, kp, 1.0)
        h0, h1 = 2 * j, 2 * j + 1
        beta_p = _pair_cols(lo, sm, SM_BETA + h0, SM_BETA + h1)
        gc_p = _pair_cols(lo, gc_all, SM_G + h0, SM_G + h1)
        gl_p = _pair_cols(lo, gl_all, SM_G + h0, SM_G + h1)
        eg_p = jnp.exp(gc_p)
        kb = kn * beta_p
        kn_bf = kn.astype(BF16)
        rhs.append(jnp.concatenate([kb * eg_p, vp * beta_p], axis=1))
        rhs_b.append(rhs[j].astype(BF16))
        qins.append(qn * eg_p)
        kouts.append((kn * jnp.exp(gl_p - gc_p)).astype(BF16))
        egl_ref[0, :, sl] = jnp.exp(gl_p)
        for hh in range(2):
            h = 2 * j + hh
            keep = lo if hh == 0 else jnp.logical_not(lo)
            gcol = gc_all[:, SM_G + h:SM_G + h + 1]
            grow = gc_t[SM_G + h:SM_G + h + 1, :]
            decay = jnp.where(bd_tri, jnp.exp(jnp.where(bd_tri, gcol - grow, 0.0)), 0.0)
            g = _dot_nt(jnp.where(keep, kb, 0.0).astype(BF16), kn_bf)
            lms.append(jnp.where(bd_strict, g * decay, 0.0))
            qk = _dot_nt(jnp.where(keep, qn, 0.0).astype(BF16), kn_bf)
            attns.append((qk * decay).astype(BF16))

    tm1 = _tri_inverse_m1(lms, blk)
    for j in range(N_PAIR):
        sl = slice(j * LANES, (j + 1) * LANES)
        wu = rhs[j] + jnp.where(lo2, _dot(tm1[2 * j].astype(BF16), rhs_b[j]),
                                _dot(tm1[2 * j + 1].astype(BF16), rhs_b[j]))
        wu_b = wu.astype(BF16)
        aw = jnp.where(lo2, _dot(attns[2 * j], wu_b), _dot(attns[2 * j + 1], wu_b))
        qe_ref[0, :, sl] = (qins[j] - aw[:, :LANES]).astype(BF16)
        o0_ref[0, :, sl] = aw[:, LANES:]
        for c in range(T // C):
            rows = slice(c * C, (c + 1) * C)
            mb = jnp.where(bd2, _dot_tn(kouts[j][rows], wu_b[rows]), 0.0)
            m_ref[0, c, j] = mb[:, :LANES].astype(BF16)
            bm_ref[0, c, j] = mb[:, LANES:]


def _gdn_pre(qkva, small):
    b, L, _ = qkva.shape
    T = min(GDN_GROUP, L)
    nc = T // GDN_CHUNK
    tok = lambda w: pl.BlockSpec((1, T, w), lambda i, j: (i, j, 0))
    per_chunk = pl.BlockSpec((1, nc, N_PAIR, LANES, LANES), lambda i, j: (i, j, 0, 0, 0))
    sds = lambda dt: jax.ShapeDtypeStruct((b, L, W_A), dt)
    chunk_sds = lambda dt: jax.ShapeDtypeStruct((b, L // GDN_CHUNK, N_PAIR, LANES, LANES), dt)
    return pl.pallas_call(
        functools.partial(_gdn_pre_kernel, T),
        grid=(b, L // T),
        in_specs=[tok(3 * W_A), tok(LANES)],
        out_specs=[tok(W_A)] * 3 + [per_chunk] * 2,
        out_shape=[sds(BF16), sds(F32), sds(F32), chunk_sds(BF16), chunk_sds(F32)],
        compiler_params=_params(("arbitrary", "arbitrary")),
        name="gdn_pre",
    )(qkva, small)


def _gdn_scan_kernel(nb, lt, qe_ref, o0_ref, egl_ref, m_ref, bm_ref, gate_ref, ng_ref, s0_ref,
                     oa_ref, sout_ref, s_sc):
    l = pl.program_id(0)
    C = GDN_CHUNK

    @pl.when(l == 0)
    def _():
        s_sc[...] = s0_ref[...]

    lane = _iota2((1, LANES), 1)
    lo = lane < HEAD_DIM

    def body(c, carry):
        r0 = pl.multiple_of(c * C, C)
        rows = pl.ds(r0, C)
        for b in range(nb):
            for j in range(N_PAIR):
                sl = slice(j * LANES, (j + 1) * LANES)
                s_old = s_sc[b, j]
                lhs = jnp.concatenate([m_ref[b, c, j], qe_ref[b, rows, sl]], axis=0)
                r = _dot(lhs, s_old.astype(BF16))
                s_sc[b, j] = s_old * egl_ref[b, pl.ds(r0, 1), sl] + (bm_ref[b, c, j] - r[:LANES])
                o = r[LANES:] + o0_ref[b, rows, sl]
                on = o * _pair_rsqrt_norm(lo, o, 1.0 / HEAD_DIM)
                oa_ref[b, rows, sl] = on * ng_ref[:, sl] * gate_ref[b, rows, sl]
        return carry

    lax.fori_loop(0, lt // C, body, 0)
    sout_ref[...] = s_sc[...]


def _gdn_scan(qe, o0, egl, m, bm, gate, ng, s0_bd):
    b, L, _ = qe.shape
    lt = min(512, L)
    tok = pl.BlockSpec((b, lt, W_A), lambda i: (0, i, 0))
    per_chunk = pl.BlockSpec((b, lt // GDN_CHUNK, N_PAIR, LANES, LANES), lambda i: (0, i, 0, 0, 0))
    st = pl.BlockSpec((b, N_PAIR, LANES, LANES), lambda i: (0, 0, 0, 0))
    return pl.pallas_call(
        functools.partial(_gdn_scan_kernel, b, lt),
        grid=(L // lt,),
        in_specs=[tok, tok, tok, per_chunk, per_chunk, tok, _const_spec((1, W_A)), st],
        out_specs=[tok, st],
        out_shape=[jax.ShapeDtypeStruct((b, L, W_A), F32),
                   jax.ShapeDtypeStruct((b, N_PAIR, LANES, LANES), F32)],
        scratch_shapes=[pltpu.VMEM((b, N_PAIR, LANES, LANES), F32)],
        compiler_params=_params(("arbitrary",)),
        name="gdn_scan",
    )(qe, o0, egl, m, bm, gate, ng, s0_bd)


def _to_pair_blockdiag(s):
    b = s.shape[0]
    s = s.reshape(b, N_PAIR, 2, HEAD_DIM, HEAD_DIM)
    z = jnp.zeros_like(s[:, :, 0])
    top = jnp.concatenate([s[:, :, 0], z], axis=-1)
    bot = jnp.concatenate([z, s[:, :, 1]], axis=-1)
    return jnp.concatenate([top, bot], axis=-2)


def _from_pair_blockdiag(sbd):
    b = sbd.shape[0]
    s = jnp.stack([sbd[:, :, :HEAD_DIM, :HEAD_DIM], sbd[:, :, HEAD_DIM:, HEAD_DIM:]], axis=2)
    return s.reshape(b, H_A, HEAD_DIM, HEAD_DIM)


def _gdn_seq(qkva, small, gate, ng, s0):
    qe, o0, egl, m, bm = _gdn_pre(qkva, small)
    oa, sbd = _gdn_scan(qe, o0, egl, m, bm, gate, ng, _to_pair_blockdiag(s0))
    return oa, _from_pair_blockdiag(sbd)


def _gdn_dec_kernel(L, q_ref, k_ref, v_ref, smt_ref, gate_ref, ngb_ref, s0_ref, oa_ref, sout_ref, kq_sc):
    h = pl.program_id(0)
    nb = q_ref.shape[-1]
    for t in range(L):
        q_t = q_ref[t]
        k_t = k_ref[t]
        q_t = q_t * (lax.rsqrt(jnp.sum(q_t * q_t, axis=0, keepdims=True) + EPS) * (HEAD_DIM ** -0.5))
        k_t = k_t * lax.rsqrt(jnp.sum(k_t * k_t, axis=0, keepdims=True) + EPS)
        kq_sc[0] = k_t
        kq_sc[1] = q_t
        beta = smt_ref[t, pl.ds(SM_BETA + h, 1), :]
        a = jnp.exp(smt_ref[t, pl.ds(SM_G + h, 1), :])
        src = s0_ref if t == 0 else sout_ref

        def k_dot_s(kk, acc):
            return acc + src[0, kk] * kq_sc[0, pl.ds(kk, 1), :]

        ks = lax.fori_loop(0, HEAD_DIM, k_dot_s, jnp.zeros((HEAD_DIM, nb), F32), unroll=8)
        delta = beta * (v_ref[t] - a * ks)

        def update(kk, acc):
            s_new = a * src[0, kk] + kq_sc[0, pl.ds(kk, 1), :] * delta
            sout_ref[0, kk] = s_new
            return acc + s_new * kq_sc[1, pl.ds(kk, 1), :]

        o = lax.fori_loop(0, HEAD_DIM, update, jnp.zeros((HEAD_DIM, nb), F32), unroll=8)
        on = o * lax.rsqrt(jnp.mean(o * o, axis=0, keepdims=True) + EPS)
        oa_ref[t] = on * ngb_ref[...] * gate_ref[t]


def _gdn_dec(qkv_t, small_t, gate_t, ngb, s0_t):
    L, _, nb = qkv_t.shape
    nh = W_A // HEAD_DIM
    head = lambda off: pl.BlockSpec((L, HEAD_DIM, nb), lambda h: (0, off + h, 0))
    st = pl.BlockSpec((1, HEAD_DIM, HEAD_DIM, nb), lambda h: (h, 0, 0, 0))
    return pl.pallas_call(
        functools.partial(_gdn_dec_kernel, L),
        grid=(H_A,),
        in_specs=[head(0), head(nh), head(2 * nh),
                  pl.BlockSpec((L, LANES, nb), lambda h: (0, 0, 0), pipeline_mode=pl.Buffered(1)),
                  head(0), pl.BlockSpec((HEAD_DIM, nb), lambda h: (h, 0)), st],
        out_specs=[head(0), st],
        out_shape=[jax.ShapeDtypeStruct((L, W_A, nb), F32),
                   jax.ShapeDtypeStruct((H_A, HEAD_DIM, HEAD_DIM, nb), F32)],
        scratch_shapes=[pltpu.VMEM((2, HEAD_DIM, nb), F32)],
        compiler_params=_params(("arbitrary",)),
        name="gdn_dec",
    )(qkv_t, qkv_t, qkv_t, small_t, gate_t, ngb, s0_t)


FOX_STRIP = 64
FOX_TQ = 512
FOX_TK = 512


def _fox_seq_kernel(tq, tk, q_ref, k_ref, v_ref, dq_ref, o_ref, m_sc, acc_sc, dqb_sc, s_sc, p_sc, al_sc):
    qi = pl.program_id(1)
    ndiag = tq // tk
    m_sc[...] = jnp.full_like(m_sc, NEG_INF)
    acc_sc[...] = jnp.zeros_like(acc_sc)
    dq = dq_ref[0] * LOG2E
    for h in range(H_C):
        dqb_sc[h] = jnp.broadcast_to(dq[:, SM_LF + h:SM_LF + h + 1], (tq, LANES))

    def step(ki, diag):
        k0 = pl.multiple_of(ki * tk, tk)
        r_lo = 0 if diag is None else diag * tk
        live = slice(r_lo, tq)

        def scores(h):
            grp = slice(h * LANES, (h + 1) * LANES)
            s_sc[h % 2, live, :] = _dot_nt(q_ref[0, live, grp], k_ref[0, pl.ds(k0, tk), grp])

        def accumulate(h):
            grp = slice(h * LANES, (h + 1) * LANES)
            acc_sc[h, live, :] = (al_sc[h % 2, live, :] * acc_sc[h, live, :]
                                  + _dot(p_sc[h % 2, live, :], v_ref[0, pl.ds(k0, tk), grp]))

        scores(0)
        for h in range(H_C):
            buf = h % 2
            if h + 1 < H_C:
                scores(h + 1)
            for r in range(r_lo // FOX_STRIP, tq // FOX_STRIP):
                rows = slice(r * FOX_STRIP, (r + 1) * FOX_STRIP)
                nc = tk if diag is None else min(tk, -(-((r + 1) * FOX_STRIP - r_lo) // LANES) * LANES)
                s = s_sc[buf, rows, 0:nc]
                if diag is not None:
                    keep = _iota2((FOX_STRIP, nc), 1) <= _iota2((FOX_STRIP, nc), 0) + (r * FOX_STRIP - r_lo)
                    s = jnp.where(keep, s, NEG_INF)
                dqb = dqb_sc[h, rows, :]
                m_old = m_sc[h, rows, :]
                m_new = jnp.maximum(m_old, jnp.max(s, axis=-1, keepdims=True) + dqb)
                shift = m_new - dqb
                if nc > LANES:
                    shift = jnp.concatenate([shift] * (nc // LANES), axis=1)
                m_sc[h, rows, :] = m_new
                al_sc[buf, rows, :] = jnp.exp2(m_old - m_new)
                p_sc[buf, rows, 0:nc] = jnp.exp2(s - shift).astype(BF16)
                if nc < tk:
                    p_sc[buf, rows, nc:tk] = jnp.zeros((FOX_STRIP, tk - nc), BF16)
            if h > 0:
                accumulate(h - 1)
        accumulate(H_C - 1)

    def body(ki, carry):
        step(ki, None)
        return carry

    lax.fori_loop(0, ndiag * qi, body, 0)
    for d in range(ndiag):
        step(ndiag * qi + d, d)
    for j in range(H_C // 2):
        halves = []
        for h in (2 * j, 2 * j + 1):
            acc = acc_sc[h]
            halves.append(acc[:, :HEAD_DIM] / acc[:, HEAD_DIM:HEAD_DIM + 1])
        o_ref[0, :, j * LANES:(j + 1) * LANES] = jnp.concatenate(halves, axis=1)


def _fox_seq(q_aug, k_aug, v_aug, dcum):
    b, L, wf = q_aug.shape
    tk = min(FOX_TK, L)
    tq = min(FOX_TQ, L)
    full = pl.BlockSpec((1, L, wf), lambda i, j: (i, 0, 0), pipeline_mode=pl.Buffered(1))
    return pl.pallas_call(
        functools.partial(_fox_seq_kernel, tq, tk),
        grid=(b, L // tq),
        in_specs=[pl.BlockSpec((1, tq, wf), lambda i, j: (i, j, 0)), full, full,
                  pl.BlockSpec((1, tq, LANES), lambda i, j: (i, j, 0))],
        out_specs=pl.BlockSpec((1, tq, W_C), lambda i, j: (i, j, 0)),
        out_shape=jax.ShapeDtypeStruct((b, L, W_C), F32),
        scratch_shapes=[pltpu.VMEM((H_C, tq, LANES), F32), pltpu.VMEM((H_C, tq, LANES), F32),
                        pltpu.VMEM((H_C, tq, LANES), F32),
                        pltpu.VMEM((2, tq, tk), F32), pltpu.VMEM((2, tq, tk), BF16),
                        pltpu.VMEM((2, tq, LANES), F32)],
        compiler_params=_params(("arbitrary", "arbitrary")),
        name="fox_seq",
    )(q_aug, k_aug, v_aug, dcum)


def _fox_dec_kernel(L, n_pages, pt_ref, q_ref, kn_ref, vn_ref, sm_ref, lf_ref, *refs):
    k_refs = refs[0:n_pages]
    v_refs = refs[n_pages:2 * n_pages]
    o_ref = refs[2 * n_pages]
    x_sc = refs[2 * n_pages + 1]
    i = pl.program_id(0)
    R = H_C * L
    q = q_ref[0] * (HEAD_DIM ** -0.5)
    rowh = _iota2((R, W_C), 0) // L
    colh = _iota2((R, W_C), 1) // HEAD_DIM
    qbd = jnp.where(rowh == colh, jnp.concatenate([q] * H_C, axis=0), 0.0).astype(BF16)

    for p in range(n_pages):
        pg = pt_ref[i * n_pages + p]
        for h in range(H_C):
            r = h * n_pages + p
            x_sc[r:r + 1, :] = lf_ref[h, pl.ds(pg, 1), :]
    x = x_sc[...]
    n = H_C * n_pages
    later = (_iota2((PAGE, PAGE), 0) > _iota2((PAGE, PAGE), 1)).astype(F32)
    within = _dot_hi(x, later)
    tot = _dot_hi(x, jnp.ones((PAGE, PAGE), F32))
    ri = _iota2((n, n), 0)
    ci = _iota2((n, n), 1)
    later_pages = ((ci // n_pages == ri // n_pages) & (ci % n_pages > ri % n_pages)).astype(F32)
    rsum = within + _dot_hi(later_pages, tot)

    sm = sm_ref[0]
    tri = (_iota2((L, L), 0) >= _iota2((L, L), 1)).astype(F32)
    cq = _dot_hi(tri, sm)
    cq_t = cq.T
    cq_col = jnp.concatenate([cq[:, SM_LF + h:SM_LF + h + 1] for h in range(H_C)], axis=0)
    cq_row = jnp.concatenate([jnp.broadcast_to(cq_t[SM_LF + h:SM_LF + h + 1, :], (L, L))
                              for h in range(H_C)], axis=0)

    s_pages = []
    for p in range(n_pages):
        sp = _dot(qbd, k_refs[p][...].astype(BF16))
        bias = jnp.concatenate(
            [jnp.broadcast_to(rsum[h * n_pages + p:h * n_pages + p + 1, :], (L, PAGE)) for h in range(H_C)],
            axis=0)
        s_pages.append(sp + bias + cq_col)
    s_new = _dot_nt(qbd, kn_ref[0].astype(BF16)) + cq_col - cq_row
    qpos = _iota2((R, L), 0) % L
    s_new = jnp.where(_iota2((R, L), 1) <= qpos, s_new, NEG_INF)

    mx = jnp.max(s_new, axis=-1, keepdims=True)
    for sp in s_pages:
        mx = jnp.maximum(mx, jnp.max(sp, axis=-1, keepdims=True))
    p_new = jnp.exp(s_new - mx)
    den = jnp.sum(p_new, axis=-1, keepdims=True)
    acc = _dot(p_new.astype(BF16), vn_ref[0].astype(BF16))
    for p in range(n_pages):
        pp = jnp.exp(s_pages[p] - mx)
        den = den + jnp.sum(pp, axis=-1, keepdims=True)
        acc = acc + _dot_nt(pp.astype(BF16), v_refs[p][...].astype(BF16))
    acc = jnp.where(rowh == colh, acc / den, 0.0)
    out = acc[0:L]
    for h in range(1, H_C):
        out = out + acc[h * L:(h + 1) * L]
    o_ref[0] = out


def _fox_dec(layer, page_table, q, kn, vn, small, cache_kt, cache_vt, cache_lft):
    b, L, _ = q.shape
    n_pages = page_table.shape[1]
    n_pool = cache_kt.shape[1]
    tok = lambda w: pl.BlockSpec((1, L, w), lambda i, pt: (i, 0, 0))

    def page_spec(p):
        return pl.BlockSpec((None, None, W_C, PAGE), lambda i, pt: (layer, pt[i * n_pages + p], 0, 0))

    in_specs = [tok(W_C), tok(W_C), tok(W_C), tok(LANES),
                pl.BlockSpec((None, H_C, n_pool, PAGE), lambda i, pt: (layer, 0, 0, 0),
                             pipeline_mode=pl.Buffered(1))]
    in_specs += [page_spec(p) for p in range(n_pages)]
    in_specs += [page_spec(p) for p in range(n_pages)]
    grid_spec = pltpu.PrefetchScalarGridSpec(
        num_scalar_prefetch=1, grid=(b,), in_specs=in_specs, out_specs=tok(W_C),
        scratch_shapes=[pltpu.VMEM((H_C * n_pages, PAGE), F32)])
    return pl.pallas_call(
        functools.partial(_fox_dec_kernel, L, n_pages),
        grid_spec=grid_spec,
        out_shape=jax.ShapeDtypeStruct((b, L, W_C), F32),
        compiler_params=_params(("arbitrary",)),
        name="fox_dec",
    )(page_table.reshape(-1), q, kn, vn, small, cache_lft,
      *([cache_kt] * n_pages), *([cache_vt] * n_pages))


def _post_kernel(bb, lt, x_ref, oa_ref, ob_ref, oc_ref, gate1_ref, shift2_ref, scale2_ref, gate2_ref,
                 gpm_ref, gpf_ref, gqf_ref, wo_ref, wu_ref, cw_ref, cst_ref, wd_ref,
                 y_ref, tail_ref, ext_ref, car_ref):
    l = pl.program_id(1)
    m = bb * lt
    keep = FFN_CONV - 1

    def rms(v, g_ref):
        return v * lax.rsqrt(jnp.mean(v * v, axis=-1, keepdims=True) + EPS) * g_ref[...]

    oa = oa_ref[...].reshape(m, W_A).astype(BF16)
    ob = ob_ref[...].reshape(m, W_B).astype(BF16)
    oc = oc_ref[...].reshape(m, W_C).astype(BF16)
    o = (_dot(oa, wo_ref[0:W_A, :]) + _dot(ob, wo_ref[W_A:W_A + W_B, :])
         + _dot(oc, wo_ref[W_A + W_B:W_A + W_B + W_C, :]))
    x1 = x_ref[...] + gate1_ref[...] * rms(o, gpm_ref).reshape(bb, lt, D_MODEL)

    h = rms(x1, gpf_ref) * (1.0 + scale2_ref[...]) + shift2_ref[...]
    h2 = h.reshape(m, D_MODEL).astype(BF16)

    @pl.when(l == 0)
    def _():
        car_ref[...] = jnp.zeros_like(car_ref)
        car_ref[:, SUBLANES - keep:SUBLANES, :] = cst_ref[...]

    w = FF_CHUNK
    nchunk = D_FF // FF_CHUNK

    def project(slot, c0):
        up = _dot(h2, wu_ref[:, c0:c0 + w]).reshape(bb, lt, w)
        ext_ref[slot, :, 0:SUBLANES, :] = car_ref[:, :, c0:c0 + w]
        ext_ref[slot, :, SUBLANES:, :] = up
        car_ref[:, :, c0:c0 + w] = ext_ref[slot, :, lt:lt + SUBLANES, :]

    def conv(slot, c0):
        y = None
        for i in range(FFN_CONV):
            off = SUBLANES - keep + i
            term = ext_ref[slot, :, off:off + lt, :] * cw_ref[i:i + 1, c0:c0 + w]
            y = term if y is None else y + term
        return y

    project(0, 0)
    project(1, D_FF)
    y = jnp.zeros((m, D_MODEL), F32)
    for j in range(nchunk):
        if j + 1 < nchunk:
            project(2 * (j + 1), (j + 1) * w)
            project(2 * (j + 1) + 1, D_FF + (j + 1) * w)
        a = conv(2 * j, j * w)
        b = conv(2 * j + 1, D_FF + j * w)
        g = (jax.nn.gelu(a) * b).reshape(m, w).astype(BF16)
        y = y + _dot(g, wd_ref[j * w:(j + 1) * w, :])
    tail_ref[...] = car_ref[...]
    y_ref[...] = x1 + gate2_ref[...] * rms(y, gqf_ref).reshape(bb, lt, D_MODEL)


def _post(dec, layer, x, oa, ob, oc, gate1, shift2, scale2, gate2, gpm, gpf, gqf, wo, wu, cw, cst, wd):
    b, L, _ = x.shape
    if dec:
        bb, lt = min(16, b), L
    else:
        bb, lt = 1, min(512, L)
    tok = lambda w: pl.BlockSpec((bb, lt, w), lambda i, j: (i, j, 0))
    per_b = lambda r, w: pl.BlockSpec((bb, r, w), lambda i, j: (i, 0, 0))
    vec = _const_spec((1, D_MODEL))
    weight = lambda r, c: pl.BlockSpec((None, r, c), lambda i, j: (layer, 0, 0), pipeline_mode=pl.Buffered(1))
    in_specs = [tok(D_MODEL), tok(W_A), tok(W_B), tok(W_C), per_b(1, D_MODEL), per_b(1, D_MODEL),
                per_b(1, D_MODEL), per_b(1, D_MODEL), vec, vec, vec,
                weight(W_A + W_B + W_C, D_MODEL), weight(D_MODEL, 2 * D_FF),
                _const_spec((FFN_CONV, 2 * D_FF)), per_b(FFN_CONV - 1, 2 * D_FF),
                weight(D_FF, D_MODEL)]
    return pl.pallas_call(
        functools.partial(_post_kernel, bb, lt),
        grid=(b // bb, L // lt),
        in_specs=in_specs,
        out_specs=[tok(D_MODEL), per_b(SUBLANES, 2 * D_FF)],
        out_shape=[jax.ShapeDtypeStruct((b, L, D_MODEL), F32),
                   jax.ShapeDtypeStruct((b, SUBLANES, 2 * D_FF), F32)],
        scratch_shapes=[pltpu.VMEM((2 * D_FF // FF_CHUNK, bb, lt + SUBLANES, FF_CHUNK), F32),
                        pltpu.VMEM((bb, SUBLANES, 2 * D_FF), F32)],
        compiler_params=_params(("arbitrary", "arbitrary")),
        name="post_dec" if dec else "post_seq",
    )(x, oa, ob, oc, gate1, shift2, scale2, gate2, gpm, gpf, gqf, wo, wu, cw, cst, wd)


def _layer_params(l, gdn_A_log, gdn_dt_bias, fox_f_bias, chunk_w_s, chunk_b_s, dec_len):
    z = lambda n: jnp.zeros((n,), F32)
    smb = jnp.concatenate([z(H_A), gdn_dt_bias[l], fox_f_bias[l], z(LANES - 2 * H_A - H_C)]).reshape(1, LANES)
    alog = jnp.concatenate([z(H_A), gdn_A_log[l], z(LANES - 2 * H_A)]).reshape(1, LANES)
    ws = chunk_w_s[l]
    bs = chunk_b_s[l]
    ws_seq = ws.reshape(G_B * CHUNK, CHUNK)
    bsb_seq = jnp.repeat(bs.T, HEAD_DIM, axis=1)
    ws_dec = jnp.repeat(jnp.transpose(ws[:, :dec_len, :dec_len], (2, 1, 0)), HEAD_DIM, axis=2)
    bsb_dec = bsb_seq[:dec_len]
    return smb, alog, ws_seq, bsb_seq, ws_dec, bsb_dec


def kernel(x_prompt, x_sample, state_gdn_conv, state_gdn_S, cache_fox_k, cache_fox_v, cache_fox_logf,
           state_ffn_conv, page_table, c_prompt, c_sample, w_ada, b_ada, g_pre_mix, g_post_mix, g_pre_ffn,
           g_post_ffn, w_in, w_out, gdn_conv_w, gdn_A_log, gdn_dt_bias, gdn_norm_g, chunk_ln_g, chunk_ln_b,
           chunk_w_s, chunk_b_s, fox_f_bias, w_up, ffn_conv_w, w_down):
    depth = w_in.shape[0]
    bp, lp, _ = x_prompt.shape
    bs, ls, _ = x_sample.shape
    n_pool = cache_fox_k.shape[1]

    cache_kt = jnp.transpose(cache_fox_k, (0, 1, 3, 4, 2)).reshape(depth, n_pool, W_C, PAGE)
    cache_vt = jnp.transpose(cache_fox_v, (0, 1, 3, 4, 2)).reshape(depth, n_pool, W_C, PAGE)
    cache_lft = jnp.transpose(cache_fox_logf, (0, 3, 1, 2))

    c_all = jnp.concatenate([c_prompt, c_sample], axis=0)
    pad = (-c_all.shape[0]) % SUBLANES
    c_all = jnp.pad(c_all, ((0, pad), (0, 0)))

    w_fused = _w_in_prep(w_in)
    pages = tuple(jnp.zeros((depth, bp, lp // PAGE, W_C, PAGE), F32) for _ in range(2))

    vec = lambda a: a.reshape(1, -1)
    outs = {k: [] for k in ("p_conv", "p_S", "p_lf", "p_ffn",
                            "s_conv", "s_S", "s_k", "s_v", "s_lf", "s_cv", "s_ffn")}
    xp, xs = x_prompt, x_sample
    mod_all = _ada(c_all, w_ada, b_ada)
    wo = w_out.astype(BF16)
    wu = w_up.astype(BF16)
    wd = w_down.astype(BF16)
    for l in range(depth):
        mod_p = mod_all[l, :bp].reshape(bp, 6, 1, D_MODEL)
        mod_s = mod_all[l, bp:bp + bs].reshape(bs, 6, 1, D_MODEL)
        smb, alog, ws_seq, bsb_seq, ws_dec, bsb_dec = _layer_params(
            l, gdn_A_log, gdn_dt_bias, fox_f_bias, chunk_w_s, chunk_b_s, ls)
        ng = jnp.tile(gdn_norm_g[l], H_A).reshape(1, W_A)
        common_in = (vec(g_pre_mix[l]), w_fused, gdn_conv_w[l])
        common_b = (smb, alog, vec(chunk_ln_g[l]), vec(chunk_ln_b[l]))
        post_w = (vec(g_post_mix[l]), vec(g_pre_ffn[l]), vec(g_post_ffn[l]), wo, wu, ffn_conv_w[l])

        (qkva, gate, small, ob, tail, pk, pv, qbf, kbf, vbf, dcum, small_t) = _in_proj(
            False, l, xp, mod_p[:, 0], mod_p[:, 1], *common_in,
            jnp.zeros((bp, GDN_CONV - 1, 3 * W_A), F32), *common_b, ws_seq, bsb_seq, pages=pages)
        pages = (pk, pv)
        oa, s_new = _gdn_seq(qkva, small, gate, ng, jnp.zeros((bp, H_A, HEAD_DIM, HEAD_DIM), F32))
        oc = _fox_seq(qbf, kbf, vbf, dcum)
        xp, ftail = _post(False, l, xp, oa, ob, oc, mod_p[:, 2], mod_p[:, 3], mod_p[:, 4], mod_p[:, 5],
                          *post_w, jnp.zeros((bp, FFN_CONV - 1, 2 * D_FF), F32), wd)
        outs["p_conv"].append(tail[:, SUBLANES - (GDN_CONV - 1):])
        outs["p_S"].append(s_new)
        lf_hm = small_t[:, SM_LF:SM_LF + H_C, :].reshape(bp, H_C, lp // PAGE, PAGE)
        outs["p_lf"].append(jnp.transpose(lf_hm, (0, 2, 3, 1)))
        outs["p_ffn"].append(ftail[:, SUBLANES - (FFN_CONV - 1):])

        (qkva, gate, small, ob, tail, kc, vc, vb, qd) = _in_proj(
            True, l, xs, mod_s[:, 0], mod_s[:, 1], *common_in, state_gdn_conv[l], *common_b, ws_dec, bsb_dec)
        to_lanes = lambda a: jnp.transpose(a, (1, 2, 0))
        oa_t, s_new_t = _gdn_dec(to_lanes(qkva), to_lanes(small), to_lanes(gate),
                                 jnp.broadcast_to(ng.reshape(W_A, 1), (W_A, bs)),
                                 jnp.transpose(state_gdn_S[l], (1, 2, 3, 0)))
        oa = jnp.transpose(oa_t, (2, 0, 1))
        s_new = jnp.transpose(s_new_t, (3, 0, 1, 2))
        oc = _fox_dec(l, page_table, qd, kc, vc, small, cache_kt, cache_vt, cache_lft)
        xs, ftail = _post(True, l, xs, oa, ob, oc, mod_s[:, 2], mod_s[:, 3], mod_s[:, 4], mod_s[:, 5],
                          *post_w, state_ffn_conv[l], wd)
        outs["s_conv"].append(tail[:, SUBLANES - (GDN_CONV - 1):])
        outs["s_S"].append(s_new)
        outs["s_k"].append(kc.reshape(bs, ls, H_C, HEAD_DIM))
        outs["s_v"].append(vc.reshape(bs, ls, H_C, HEAD_DIM))
        outs["s_lf"].append(small[:, :, SM_LF:SM_LF + H_C])
        outs["s_cv"].append(vb)
        outs["s_ffn"].append(ftail[:, SUBLANES - (FFN_CONV - 1):])

    st = lambda k: jnp.stack(outs[k])
    from_pages = lambda p: jnp.transpose(
        p.reshape(depth, bp, lp // PAGE, H_C, HEAD_DIM, PAGE), (0, 1, 2, 5, 3, 4))
    return (xp, xs, st("p_conv"), st("p_S"), from_pages(pages[0]), from_pages(pages[1]), st("p_lf"), st("p_ffn"),
            st("s_conv"), st("s_S"), st("s_k"), st("s_v"), st("s_lf"), st("s_cv"), st("s_ffn"))
```

```python
import functools

import jax
import jax.numpy as jnp
from jax import lax
from jax.experimental import pallas as pl
from jax.experimental.pallas import tpu as pltpu

F32 = jnp.float32
BF16 = jnp.bfloat16

D_MODEL = 1024
HEAD_DIM = 64
H_A = 6
G_B = 4
H_C = 6
W_A = H_A * HEAD_DIM
W_B = G_B * HEAD_DIM
W_C = H_C * HEAD_DIM
GDN_CONV = 4
GDN_CHUNK = 64
CHUNK = 128
D_FF = 2816
FFN_CONV = 3
EPS = 1e-6
NEG_INF = -1e30
PAGE = 128
LOG2E = 1.4426950408889634

LANES = 128
SUBLANES = 8
VMEM_LIMIT = 56 * 1024 * 1024

QA_OFF = 0
GA_OFF = QA_OFF + 3 * W_A
UV_OFF = GA_OFF + W_A
QC_OFF = UV_OFF + 2 * W_B
SM_OFF = QC_OFF + 3 * W_C
N_FUSED = SM_OFF + LANES
SM_BETA = 0
SM_G = H_A
SM_LF = 2 * H_A

FF_CHUNK = D_FF // 2


def _silu(x):
    return x * jax.nn.sigmoid(x)


def _softplus(x):
    return jnp.maximum(x, 0.0) + jnp.log1p(jnp.exp(-jnp.abs(x)))


def _dot(a, b):
    return jnp.dot(a, b, preferred_element_type=F32)


def _split3(x):
    t1 = x.astype(BF16)
    r1 = x - t1.astype(F32)
    t2 = r1.astype(BF16)
    t3 = (r1 - t2.astype(F32)).astype(BF16)
    return t1, t2, t3


def _sel_dot(sel, x):
    s = sel.astype(BF16)
    t1, t2, t3 = _split3(x)
    return _dot(s, t1) + (_dot(s, t2) + _dot(s, t3))


def _dot_sel(x, sel):
    s = sel.astype(BF16)
    t1, t2, t3 = _split3(x)
    return _dot(t1, s) + (_dot(t2, s) + _dot(t3, s))


def _dot_nt(a, b):
    return lax.dot_general(a, b, (((1,), (1,)), ((), ())), preferred_element_type=F32)


def _dot_tn(a, b):
    return lax.dot_general(a, b, (((0,), (0,)), ((), ())), preferred_element_type=F32)


def _iota2(shape, dim):
    return lax.broadcasted_iota(jnp.int32, shape, dim)


def _params(sem):
    return pltpu.CompilerParams(dimension_semantics=sem, vmem_limit_bytes=VMEM_LIMIT)


def _const_spec(shape):
    nd = len(shape)
    return pl.BlockSpec(shape, lambda *_: (0,) * nd, pipeline_mode=pl.Buffered(1))


def _ada_kernel(c_ref, w_ref, b_ref, o_ref):
    c = _silu(c_ref[...]).astype(BF16)
    o_ref[...] = _dot(c, w_ref[...].astype(BF16)) + b_ref[...]


def _ada(c, w, b):
    m = c.shape[0]
    depth, _, n = w.shape
    tn = 1024
    return pl.pallas_call(
        _ada_kernel,
        grid=(depth, n // tn),
        in_specs=[pl.BlockSpec((m, D_MODEL), lambda l, j: (0, 0)),
                  pl.BlockSpec((None, D_MODEL, tn), lambda l, j: (l, 0, j)),
                  pl.BlockSpec((None, 1, tn), lambda l, j: (l, 0, j))],
        out_specs=pl.BlockSpec((None, m, tn), lambda l, j: (l, 0, j)),
        out_shape=jax.ShapeDtypeStruct((depth, m, n), F32),
        compiler_params=_params(("arbitrary", "arbitrary")),
        name="ada_mod",
    )(c, w, b.reshape(depth, 1, n))


def _w_in_prep_kernel(depth, a_ref, o_ref):
    l = pl.program_id(0)
    j = pl.program_id(1)
    nkt = D_MODEL // LANES
    rpn = nkt * depth
    n_regular = SM_OFF // LANES
    n_first = GA_OFF // LANES
    n_in = SM_OFF + 2 * H_A + H_C

    @pl.when(j < n_regular)
    def _():
        n0 = j * LANES + jnp.where(j >= n_first, 2 * H_A, 0)
        for kt in range(nkt):
            x = a_ref[pl.ds(n0 * rpn + kt * depth + l, LANES, stride=rpn), :]
            o_ref[0, kt * LANES:(kt + 1) * LANES, :] = x.T.astype(BF16)

    @pl.when(j == n_regular)
    def _():
        r1, r2 = 2 * SUBLANES, SUBLANES
        n2 = n_in - r2
        r = _iota2((r1 + r2, LANES), 0)
        m = _iota2((r1 + r2, LANES), 1)
        pick = ((m < 2 * H_A) & (r == m)) | (
            (m >= 2 * H_A) & (m < 2 * H_A + H_C) & (r == m - 2 * H_A + r1 + (r2 - H_C)))
        sel = pick.astype(BF16)
        for kt in range(nkt):
            x1 = a_ref[pl.ds(GA_OFF * rpn + kt * depth + l, r1, stride=rpn), :]
            x2 = a_ref[pl.ds(n2 * rpn + kt * depth + l, r2, stride=rpn), :]
            xc = jnp.concatenate([x1, x2], axis=0).astype(BF16)
            o_ref[0, kt * LANES:(kt + 1) * LANES, :] = _dot_tn(xc, sel).astype(BF16)


def _w_in_prep(w_in):
    depth, d, n_in = w_in.shape
    nkt = d // LANES
    view = w_in.reshape(depth, nkt, LANES, n_in).transpose(3, 1, 0, 2).reshape(n_in * nkt * depth, LANES)
    return pl.pallas_call(
        functools.partial(_w_in_prep_kernel, depth),
        grid=(depth, N_FUSED // LANES),
        in_specs=[pl.BlockSpec(view.shape, lambda l, j: (0, 0), pipeline_mode=pl.Buffered(1))],
        out_specs=pl.BlockSpec((1, d, LANES), lambda l, j: (l, 0, j)),
        out_shape=jax.ShapeDtypeStruct((depth, d, N_FUSED), BF16),
        compiler_params=_params(("arbitrary", "arbitrary")),
        name="w_in_prep",
    )(view)


def _in_kernel(dec, bb, lt, *refs):
    (x_ref, shift_ref, scale_ref, gpre_ref, w_ref, cw_ref, cst_ref, smb_ref, alog_ref,
     lng_ref, lnb_ref, ws_ref, bsb_ref) = refs[:13]
    if dec:
        (qkva_ref, gate_ref, small_ref, ob_ref, tail_ref, kc_ref, vc_ref, vb_ref, q_ref,
         ext_ref) = refs[13:]
    else:
        (_, _, qkva_ref, gate_ref, small_ref, ob_ref, tail_ref, pk_ref, pv_ref, qbf_ref, kbf_ref,
         vbf_ref, dcum_ref, smt_ref, ext_ref, dcar_ref) = refs[13:]
    l = pl.program_id(1)
    m = bb * lt

    x = x_ref[...]
    ms = jnp.mean(x * x, axis=-1, keepdims=True)
    h = x * lax.rsqrt(ms + EPS) * gpre_ref[...]
    h = h * (1.0 + scale_ref[...]) + shift_ref[...]
    h2 = h.reshape(m, D_MODEL).astype(BF16)

    wa = 3 * W_A
    za = _dot(h2, w_ref[:, QA_OFF:QA_OFF + wa]).reshape(bb, lt, wa)

    @pl.when(l == 0)
    def _():
        ext_ref[:, SUBLANES - (GDN_CONV - 1):SUBLANES, :] = cst_ref[...]

    @pl.when(l > 0)
    def _():
        ext_ref[:, 0:SUBLANES, :] = ext_ref[:, lt:lt + SUBLANES, :]

    ext_ref[:, SUBLANES:, :] = za
    y = za * cw_ref[GDN_CONV - 1:GDN_CONV, :]
    for i in range(GDN_CONV - 1):
        o = SUBLANES - (GDN_CONV - 1) + i
        y = y + ext_ref[:, o:o + lt, :] * cw_ref[i:i + 1, :]
    qkva_ref[...] = _silu(y)
    tail_ref[...] = ext_ref[:, lt:lt + SUBLANES, :]

    gate_ref[...] = _silu(_dot(h2, w_ref[:, GA_OFF:GA_OFF + W_A])).reshape(bb, lt, W_A)

    zs = _dot(h2, w_ref[:, SM_OFF:SM_OFF + LANES]) + smb_ref[...]
    lane = _iota2((1, LANES), 1)
    beta = jax.nn.sigmoid(zs)
    gval = -jnp.exp(alog_ref[...]) * _softplus(zs)
    lf = -_softplus(-zs)
    small = jnp.where(lane < SM_G, beta,
                      jnp.where(lane < SM_LF, gval, jnp.where(lane < SM_LF + H_C, lf, 0.0)))
    small_ref[...] = small.reshape(bb, lt, LANES)

    if not dec:
        tri = (_iota2((CHUNK, CHUNK), 0) >= _iota2((CHUNK, CHUNK), 1)).astype(F32)

        @pl.when(l == 0)
        def _():
            dcar_ref[...] = jnp.zeros_like(dcar_ref)

        run = dcar_ref[0:1, :]
        parts = []
        for c in range(lt // CHUNK):
            parts.append(_sel_dot(tri, small[c * CHUNK:(c + 1) * CHUNK]) + run)
            run = parts[-1][CHUNK - 1:CHUNK, :]
        dc = jnp.concatenate(parts, axis=0)
        dcum_ref[...] = dc.reshape(bb, lt, LANES)
        dcar_ref[0:1, :] = dc[lt - 1:lt, :]
        smt_ref[0] = small.T

    uv = jax.nn.gelu(_dot(h2, w_ref[:, UV_OFF:UV_OFF + 2 * W_B]))
    u = uv[:, :W_B]
    v = uv[:, W_B:]
    mu = jnp.mean(v, axis=-1, keepdims=True)
    vc0 = v - mu
    var = jnp.mean(vc0 * vc0, axis=-1, keepdims=True)
    vb = vc0 * lax.rsqrt(var + EPS) * lng_ref[...] + lnb_ref[...]
    if dec:
        vb3 = vb.reshape(bb, lt, W_B)
        vb_ref[...] = vb3
        trow = _iota2((lt, W_B), 0)
        mix = jnp.zeros((bb, lt, W_B), F32)
        for s in range(lt):
            coef = jnp.where(trow >= s, ws_ref[s], 0.0)
            mix = mix + coef[None] * vb3[:, s:s + 1, :]
        ob_ref[...] = u.reshape(bb, lt, W_B) * (mix + bsb_ref[...][None])
    else:
        r_t = _iota2((G_B * CHUNK, CHUNK), 0) % CHUNK
        r_s = _iota2((G_B * CHUNK, CHUNK), 1)
        wst = jnp.where(r_s <= r_t, ws_ref[...], 0.0).astype(BF16)
        grp = _iota2((1, W_B), 1) // HEAD_DIM
        for c in range(lt // CHUNK):
            rows = slice(c * CHUNK, (c + 1) * CHUNK)
            r = _dot(wst, vb[rows].astype(BF16))
            mix = jnp.zeros((CHUNK, W_B), F32)
            for g in range(G_B):
                mix = jnp.where(grp == g, r[g * CHUNK:(g + 1) * CHUNK], mix)
            ob_ref[0, rows, :] = u[rows] * (mix + bsb_ref[...])

    zc = _dot(h2, w_ref[:, QC_OFF:QC_OFF + 3 * W_C])
    qc = zc[:, 0:W_C]
    kc = zc[:, W_C:2 * W_C]
    vc = zc[:, 2 * W_C:3 * W_C]
    if dec:
        kc_ref[...] = kc.reshape(bb, lt, W_C)
        vc_ref[...] = vc.reshape(bb, lt, W_C)
        q_ref[...] = qc.reshape(bb, lt, W_C)
    else:
        for c in range(lt // PAGE):
            rows = slice(c * PAGE, (c + 1) * PAGE)
            pk_ref[0, c] = kc[rows].T
            pv_ref[0, c] = vc[rows].T
        hd = HEAD_DIM
        d2 = dc * LOG2E
        for h in range(H_C):
            pair = slice((h // 2) * LANES, (h // 2 + 1) * LANES)
            at0 = (lambda a: a) if h % 2 == 0 else (lambda a: pltpu.roll(a, hd, axis=1))
            dh = d2[:, SM_LF + h:SM_LF + h + 1]
            t1 = dh.astype(BF16).astype(F32)
            t2 = (dh - t1).astype(BF16).astype(F32)
            t3 = dh - t1 - t2
            qa = jnp.where(lane < hd, at0(qc[:, pair]) * (LOG2E * hd ** -0.5), jnp.where(lane < hd + 3, 1.0, 0.0))
            ka = jnp.where(lane < hd, at0(kc[:, pair]),
                           jnp.where(lane == hd, -t1, jnp.where(lane == hd + 1, -t2,
                                                                jnp.where(lane == hd + 2, -t3, 0.0))))
            va = jnp.where(lane < hd, at0(vc[:, pair]), jnp.where(lane == hd, 1.0, 0.0))
            grp = slice(h * LANES, (h + 1) * LANES)
            qbf_ref[0, :, grp] = qa.astype(BF16)
            kbf_ref[0, :, grp] = ka.astype(BF16)
            vbf_ref[0, :, grp] = va.astype(BF16)


def _in_proj(dec, layer, x, shift, scale, gpre, w_fused, conv_w, conv_state, smb, alog, lng, lnb, ws, bsb,
             pages=()):
    b, L, _ = x.shape
    if dec:
        bb, lt = min(32, b), L
    else:
        bb, lt = 1, min(512, L)
    grid = (b // bb, L // lt)
    tok = lambda w: pl.BlockSpec((bb, lt, w), lambda i, j: (i, j, 0))
    per_b = lambda r, w: pl.BlockSpec((bb, r, w), lambda i, j: (i, 0, 0))
    in_specs = [tok(D_MODEL), per_b(1, D_MODEL), per_b(1, D_MODEL), _const_spec((1, D_MODEL)),
                pl.BlockSpec((None, D_MODEL, N_FUSED), lambda i, j: (layer, 0, 0), pipeline_mode=pl.Buffered(1)),
                _const_spec((GDN_CONV, 3 * W_A)),
                per_b(GDN_CONV - 1, 3 * W_A), _const_spec((1, LANES)), _const_spec((1, LANES)),
                _const_spec((1, W_B)), _const_spec((1, W_B)), _const_spec(ws.shape), _const_spec(bsb.shape)]
    sds = lambda w, dt=F32: jax.ShapeDtypeStruct((b, L, w), dt)
    out_shape = [sds(3 * W_A), sds(W_A), sds(LANES), sds(W_B), jax.ShapeDtypeStruct((b, SUBLANES, 3 * W_A), F32)]
    out_specs = [tok(3 * W_A), tok(W_A), tok(LANES), tok(W_B), per_b(SUBLANES, 3 * W_A)]
    scratch = [pltpu.VMEM((bb, lt + SUBLANES, 3 * W_A), F32)]
    aliases = {}
    if dec:
        out_shape += [sds(W_C), sds(W_C), sds(W_B), sds(W_C)]
        out_specs += [tok(W_C), tok(W_C), tok(W_B), tok(W_C)]
    else:
        n_in = len(in_specs)
        in_specs += [pl.BlockSpec(memory_space=pl.ANY)] * 2
        aliases = {n_in: len(out_shape), n_in + 1: len(out_shape) + 1}
        page_spec = pl.BlockSpec((None, 1, lt // PAGE, W_C, PAGE), lambda i, j: (layer, i, j, 0, 0))
        head_major = pl.BlockSpec((1, LANES, lt), lambda i, j: (i, 0, j))
        out_shape += [jax.ShapeDtypeStruct(p.shape, p.dtype) for p in pages]
        out_specs += [page_spec, page_spec]
        wf = H_C * LANES
        out_shape += [sds(wf, BF16), sds(wf, BF16), sds(wf, BF16), sds(LANES),
                      jax.ShapeDtypeStruct((b, LANES, L), F32)]
        out_specs += [tok(wf), tok(wf), tok(wf), tok(LANES), head_major]
        scratch += [pltpu.VMEM((SUBLANES, LANES), F32)]
    return pl.pallas_call(
        functools.partial(_in_kernel, dec, bb, lt),
        grid=grid, in_specs=in_specs, out_specs=out_specs, out_shape=out_shape,
        scratch_shapes=scratch, input_output_aliases=aliases,
        compiler_params=_params(("arbitrary", "arbitrary")),
        name="in_proj_dec" if dec else "in_proj_seq",
    )(x, shift, scale, gpre, w_fused, conv_w, conv_state, smb, alog, lng, lnb, ws, bsb, *pages)


GDN_GROUP = 256
N_PAIR = H_A // 2


def _tri_inverse_m1(lms, blk):
    bf = lambda xs: [x.astype(BF16) for x in xs]
    mm_ = lambda xs, ys: [_dot(x, y) for x, y in zip(xs, ys)]
    dm = [jnp.where(blk, lm, 0.0) for lm in lms]
    nm = [lm - d for lm, d in zip(lms, dm)]
    dmb = bf(dm)
    d2 = mm_(dmb, dmb)
    d2b = bf(d2)
    d4 = mm_(d2b, d2b)
    d4b = bf(d4)
    d8 = mm_(d4b, d4b)
    a1 = [x2 - x1 - c for x2, x1, c in zip(d2, dm, mm_(dmb, d2b))]
    a2 = [x4 + x8 + c for x4, x8, c in zip(d4, d8, mm_(d4b, bf(d8)))]
    et = [x + y + c for x, y, c in zip(a1, a2, mm_(bf(a1), bf(a2)))]
    etb = bf(et)
    mm = [n + c for n, c in zip(nm, mm_(etb, bf(nm)))]
    mmb = bf(mm)
    m2 = mm_(mmb, mmb)
    a3 = [x2 - x1 - c for x2, x1, c in zip(m2, mm, mm_(mmb, bf(m2)))]
    return [x + e + c for x, e, c in zip(a3, et, mm_(bf(a3), etb))]


def _pair_cols(lo, a, c0, c1):
    return jnp.where(lo, a[:, c0:c0 + 1], a[:, c1:c1 + 1])


def _pair_rsqrt_norm(lo, x, scale):
    x2 = x * x
    s_lo = jnp.sum(jnp.where(lo, x2, 0.0), axis=-1, keepdims=True)
    s_hi = jnp.sum(jnp.where(lo, 0.0, x2), axis=-1, keepdims=True)
    return jnp.where(lo, lax.rsqrt(s_lo * scale + EPS), lax.rsqrt(s_hi * scale + EPS))


def _gdn_pre_kernel(T, qkv_ref, small_ref, qe_ref, o0_ref, egl_ref, m_ref, bm_ref):
    C = GDN_CHUNK
    row = _iota2((T, T), 0)
    col = _iota2((T, T), 1)
    same = (row // C) == (col // C)
    bd_tri = same & (row >= col)
    bd_strict = same & (row > col)
    blk = (row // 16) == (col // 16)
    lane = _iota2((1, LANES), 1)
    lo = lane < HEAD_DIM
    lo2 = (_iota2((1, 2 * LANES), 1) % LANES) < HEAD_DIM
    bd2 = (_iota2((LANES, 2 * LANES), 0) // HEAD_DIM) == ((_iota2((LANES, 2 * LANES), 1) % LANES) // HEAD_DIM)

    sm = small_ref[0]
    gc_all = _sel_dot(bd_tri, sm)
    gl_all = jnp.concatenate(
        [jnp.broadcast_to(gc_all[(c + 1) * C - 1:(c + 1) * C, :], (C, LANES)) for c in range(T // C)], axis=0)
    gc_t = gc_all.T
    rhs, rhs_b, lms, attns, qins, kouts = [], [], [], [], [], []
    for j in range(N_PAIR):
        sl = slice(j * LANES, (j + 1) * LANES)
        qp = qkv_ref[0, :, j * LANES:(j + 1) * LANES]
        kp = qkv_ref[0, :, W_A + j * LANES:W_A + (j + 1) * LANES]
        vp = qkv_ref[0, :, 2 * W_A + j * LANES:2 * W_A + (j + 1) * LANES]
        qn = qp * _pair_rsqrt_norm(lo, qp, 1.0) * (HEAD_DIM ** -0.5)
        kn = kp * _pair_rsqrt_norm(lo, kp, 1.0)
        h0, h1 = 2 * j, 2 * j + 1
        beta_p = _pair_cols(lo, sm, SM_BETA + h0, SM_BETA + h1)
        gc_p = _pair_cols(lo, gc_all, SM_G + h0, SM_G + h1)
        gl_p = _pair_cols(lo, gl_all, SM_G + h0, SM_G + h1)
        eg_p = jnp.exp(gc_p)
        kb = kn * beta_p
        kn_bf = kn.astype(BF16)
        rhs.append(jnp.concatenate([kb * eg_p, vp * beta_p], axis=1))
        rhs_b.append(rhs[j].astype(BF16))
        qins.append(qn * eg_p)
        kouts.append((kn * jnp.exp(gl_p - gc_p)).astype(BF16))
        egl_ref[0, :, sl] = jnp.exp(gl_p)
        zero = jnp.zeros_like(kb)
        stacked = jnp.concatenate([jnp.where(lo, kb, zero), jnp.where(lo, zero, kb),
                                   jnp.where(lo, qn, zero), jnp.where(lo, zero, qn)], axis=0).astype(BF16)
        gq = _dot_nt(stacked, kn_bf)
        for hh in range(2):
            h = 2 * j + hh
            gcol = gc_all[:, SM_G + h:SM_G + h + 1]
            grow = gc_t[SM_G + h:SM_G + h + 1, :]
            decay = jnp.where(bd_tri, jnp.exp(jnp.where(bd_tri, gcol - grow, 0.0)), 0.0)
            lms.append(jnp.where(bd_strict, gq[hh * T:(hh + 1) * T] * decay, 0.0))
            attns.append((gq[(2 + hh) * T:(3 + hh) * T] * decay).astype(BF16))

    tm1 = _tri_inverse_m1(lms, blk)
    tok_chunk = _iota2((T, LANES), 0) // C
    for j in range(N_PAIR):
        sl = slice(j * LANES, (j + 1) * LANES)
        both = _dot(jnp.concatenate([tm1[2 * j], tm1[2 * j + 1]], axis=0).astype(BF16), rhs_b[j])
        wu = rhs[j] + jnp.where(lo2, both[:T], both[T:])
        wu_b = wu.astype(BF16)
        both = _dot(jnp.concatenate([attns[2 * j], attns[2 * j + 1]], axis=0), wu_b)
        aw = jnp.where(lo2, both[:T], both[T:])
        qe_ref[0, :, sl] = (qins[j] - aw[:, :LANES]).astype(BF16)
        o0_ref[0, :, sl] = aw[:, LANES:]
        ko = kouts[j]
        spread = jnp.concatenate([jnp.where(tok_chunk == c, ko, jnp.zeros_like(ko)) for c in range(T // C)],
                                 axis=1)
        mb_all = _dot_tn(spread, wu_b)
        for c in range(T // C):
            mb = jnp.where(bd2, mb_all[c * LANES:(c + 1) * LANES], 0.0)
            m_ref[0, c, j] = mb[:, :LANES].astype(BF16)
            bm_ref[0, c, j] = mb[:, LANES:]


def _gdn_pre(qkva, small):
    b, L, _ = qkva.shape
    T = min(GDN_GROUP, L)
    nc = T // GDN_CHUNK
    tok = lambda w: pl.BlockSpec((1, T, w), lambda i, j: (i, j, 0))
    per_chunk = pl.BlockSpec((1, nc, N_PAIR, LANES, LANES), lambda i, j: (i, j, 0, 0, 0))
    sds = lambda dt: jax.ShapeDtypeStruct((b, L, W_A), dt)
    chunk_sds = lambda dt: jax.ShapeDtypeStruct((b, L // GDN_CHUNK, N_PAIR, LANES, LANES), dt)
    return pl.pallas_call(
        functools.partial(_gdn_pre_kernel, T),
        grid=(b, L // T),
        in_specs=[tok(3 * W_A), tok(LANES)],
        out_specs=[tok(W_A)] * 3 + [per_chunk] * 2,
        out_shape=[sds(BF16), sds(F32), sds(F32), chunk_sds(BF16), chunk_sds(F32)],
        compiler_params=_params(("arbitrary", "arbitrary")),
        name="gdn_pre",
    )(qkva, small)


def _gdn_scan_kernel(nb, lt, qe_ref, o0_ref, egl_ref, m_ref, bm_ref, gate_ref, ng_ref, s0_ref,
                     oa_ref, sout_ref, s_sc):
    l = pl.program_id(0)
    C = GDN_CHUNK

    @pl.when(l == 0)
    def _():
        s_sc[...] = s0_ref[...]

    lane = _iota2((1, LANES), 1)
    lo = lane < HEAD_DIM

    def body(c, carry):
        r0 = pl.multiple_of(c * C, C)
        rows = pl.ds(r0, C)
        for b in range(nb):
            for j in range(N_PAIR):
                sl = slice(j * LANES, (j + 1) * LANES)
                s_old = s_sc[b, j]
                lhs = jnp.concatenate([m_ref[b, c, j], qe_ref[b, rows, sl]], axis=0)
                r = _dot(lhs, s_old.astype(BF16))
                s_sc[b, j] = s_old * egl_ref[b, pl.ds(r0, 1), sl] + (bm_ref[b, c, j] - r[:LANES])
                o = r[LANES:] + o0_ref[b, rows, sl]
                on = o * _pair_rsqrt_norm(lo, o, 1.0 / HEAD_DIM)
                oa_ref[b, rows, sl] = on * ng_ref[:, sl] * gate_ref[b, rows, sl]
        return carry

    lax.fori_loop(0, lt // C, body, 0)
    sout_ref[...] = s_sc[...]


def _gdn_scan(qe, o0, egl, m, bm, gate, ng, s0_bd):
    b, L, _ = qe.shape
    lt = min(512, L)
    tok = pl.BlockSpec((b, lt, W_A), lambda i: (0, i, 0))
    per_chunk = pl.BlockSpec((b, lt // GDN_CHUNK, N_PAIR, LANES, LANES), lambda i: (0, i, 0, 0, 0))
    st = pl.BlockSpec((b, N_PAIR, LANES, LANES), lambda i: (0, 0, 0, 0))
    return pl.pallas_call(
        functools.partial(_gdn_scan_kernel, b, lt),
        grid=(L // lt,),
        in_specs=[tok, tok, tok, per_chunk, per_chunk, tok, _const_spec((1, W_A)), st],
        out_specs=[tok, st],
        out_shape=[jax.ShapeDtypeStruct((b, L, W_A), F32),
                   jax.ShapeDtypeStruct((b, N_PAIR, LANES, LANES), F32)],
        scratch_shapes=[pltpu.VMEM((b, N_PAIR, LANES, LANES), F32)],
        compiler_params=_params(("arbitrary",)),
        name="gdn_scan",
    )(qe, o0, egl, m, bm, gate, ng, s0_bd)


def _to_pair_blockdiag(s):
    b = s.shape[0]
    s = s.reshape(b, N_PAIR, 2, HEAD_DIM, HEAD_DIM)
    z = jnp.zeros_like(s[:, :, 0])
    top = jnp.concatenate([s[:, :, 0], z], axis=-1)
    bot = jnp.concatenate([z, s[:, :, 1]], axis=-1)
    return jnp.concatenate([top, bot], axis=-2)


def _from_pair_blockdiag(sbd):
    b = sbd.shape[0]
    s = jnp.stack([sbd[:, :, :HEAD_DIM, :HEAD_DIM], sbd[:, :, HEAD_DIM:, HEAD_DIM:]], axis=2)
    return s.reshape(b, H_A, HEAD_DIM, HEAD_DIM)


def _gdn_seq(qkva, small, gate, ng, s0):
    qe, o0, egl, m, bm = _gdn_pre(qkva, small)
    oa, sbd = _gdn_scan(qe, o0, egl, m, bm, gate, ng, _to_pair_blockdiag(s0))
    return oa, _from_pair_blockdiag(sbd)


def _gdn_dec_kernel(L, q_ref, k_ref, v_ref, smt_ref, gate_ref, ngb_ref, s0_ref, oa_ref, sout_ref, kq_sc):
    h = pl.program_id(0)
    nb = q_ref.shape[-1]
    for t in range(L):
        q_t = q_ref[t]
        k_t = k_ref[t]
        q_t = q_t * (lax.rsqrt(jnp.sum(q_t * q_t, axis=0, keepdims=True) + EPS) * (HEAD_DIM ** -0.5))
        k_t = k_t * lax.rsqrt(jnp.sum(k_t * k_t, axis=0, keepdims=True) + EPS)
        kq_sc[0] = k_t
        kq_sc[1] = q_t
        beta = smt_ref[t, pl.ds(SM_BETA + h, 1), :]
        a = jnp.exp(smt_ref[t, pl.ds(SM_G + h, 1), :])
        src = s0_ref if t == 0 else sout_ref

        def k_dot_s(kk, acc):
            return acc + src[0, kk] * kq_sc[0, pl.ds(kk, 1), :]

        ks = lax.fori_loop(0, HEAD_DIM, k_dot_s, jnp.zeros((HEAD_DIM, nb), F32), unroll=8)
        delta = beta * (v_ref[t] - a * ks)

        def update(kk, acc):
            s_new = a * src[0, kk] + kq_sc[0, pl.ds(kk, 1), :] * delta
            sout_ref[0, kk] = s_new
            return acc + s_new * kq_sc[1, pl.ds(kk, 1), :]

        o = lax.fori_loop(0, HEAD_DIM, update, jnp.zeros((HEAD_DIM, nb), F32), unroll=8)
        on = o * lax.rsqrt(jnp.mean(o * o, axis=0, keepdims=True) + EPS)
        oa_ref[t] = on * ngb_ref[...] * gate_ref[t]


def _gdn_dec(qkv_t, small_t, gate_t, ngb, s0_t):
    L, _, nb = qkv_t.shape
    nh = W_A // HEAD_DIM
    head = lambda off: pl.BlockSpec((L, HEAD_DIM, nb), lambda h: (0, off + h, 0))
    st = pl.BlockSpec((1, HEAD_DIM, HEAD_DIM, nb), lambda h: (h, 0, 0, 0))
    return pl.pallas_call(
        functools.partial(_gdn_dec_kernel, L),
        grid=(H_A,),
        in_specs=[head(0), head(nh), head(2 * nh),
                  pl.BlockSpec((L, LANES, nb), lambda h: (0, 0, 0), pipeline_mode=pl.Buffered(1)),
                  head(0), pl.BlockSpec((HEAD_DIM, nb), lambda h: (h, 0)), st],
        out_specs=[head(0), st],
        out_shape=[jax.ShapeDtypeStruct((L, W_A, nb), F32),
                   jax.ShapeDtypeStruct((H_A, HEAD_DIM, HEAD_DIM, nb), F32)],
        scratch_shapes=[pltpu.VMEM((2, HEAD_DIM, nb), F32)],
        compiler_params=_params(("arbitrary",)),
        name="gdn_dec",
    )(qkv_t, qkv_t, qkv_t, small_t, gate_t, ngb, s0_t)


FOX_STRIP = 64
FOX_TQ = 512
FOX_TK = 512


def _fox_seq_kernel(tq, tk, q_ref, k_ref, v_ref, dq_ref, o_ref, m_sc, acc_sc, dqb_sc, s_sc, p_sc, al_sc):
    qi = pl.program_id(1)
    ndiag = tq // tk
    m_sc[...] = jnp.full_like(m_sc, NEG_INF)
    acc_sc[...] = jnp.zeros_like(acc_sc)
    dq = dq_ref[0] * LOG2E
    for h in range(H_C):
        dqb_sc[h] = jnp.broadcast_to(dq[:, SM_LF + h:SM_LF + h + 1], (tq, LANES))

    def step(ki, diag):
        k0 = pl.multiple_of(ki * tk, tk)
        r_lo = 0 if diag is None else diag * tk
        live = slice(r_lo, tq)

        def scores(h):
            grp = slice(h * LANES, (h + 1) * LANES)
            s_sc[h % 2, live, :] = _dot_nt(q_ref[0, live, grp], k_ref[0, pl.ds(k0, tk), grp])

        def accumulate(h):
            grp = slice(h * LANES, (h + 1) * LANES)
            acc_sc[h, live, :] = (al_sc[h % 2, live, :] * acc_sc[h, live, :]
                                  + _dot(p_sc[h % 2, live, :], v_ref[0, pl.ds(k0, tk), grp]))

        scores(0)
        for h in range(H_C):
            buf = h % 2
            if h + 1 < H_C:
                scores(h + 1)
            for r in range(r_lo // FOX_STRIP, tq // FOX_STRIP):
                rows = slice(r * FOX_STRIP, (r + 1) * FOX_STRIP)
                nc = tk if diag is None else min(tk, -(-((r + 1) * FOX_STRIP - r_lo) // LANES) * LANES)
                s = s_sc[buf, rows, 0:nc]
                if diag is not None:
                    keep = _iota2((FOX_STRIP, nc), 1) <= _iota2((FOX_STRIP, nc), 0) + (r * FOX_STRIP - r_lo)
                    s = jnp.where(keep, s, NEG_INF)
                dqb = dqb_sc[h, rows, :]
                m_old = m_sc[h, rows, :]
                m_new = jnp.maximum(m_old, jnp.max(s, axis=-1, keepdims=True) + dqb)
                shift = m_new - dqb
                if nc > LANES:
                    shift = jnp.concatenate([shift] * (nc // LANES), axis=1)
                m_sc[h, rows, :] = m_new
                al_sc[buf, rows, :] = jnp.exp2(m_old - m_new)
                p_sc[buf, rows, 0:nc] = jnp.exp2(s - shift).astype(BF16)
                if nc < tk:
                    p_sc[buf, rows, nc:tk] = jnp.zeros((FOX_STRIP, tk - nc), BF16)
            if h > 0:
                accumulate(h - 1)
        accumulate(H_C - 1)

    def body(ki, carry):
        step(ki, None)
        return carry

    lax.fori_loop(0, ndiag * qi, body, 0)
    for d in range(ndiag):
        step(ndiag * qi + d, d)
    for j in range(H_C // 2):
        halves = []
        for h in (2 * j, 2 * j + 1):
            acc = acc_sc[h]
            halves.append(acc[:, :HEAD_DIM] / acc[:, HEAD_DIM:HEAD_DIM + 1])
        o_ref[0, :, j * LANES:(j + 1) * LANES] = jnp.concatenate(halves, axis=1)


def _fox_seq(q_aug, k_aug, v_aug, dcum):
    b, L, wf = q_aug.shape
    tk = min(FOX_TK, L)
    tq = min(FOX_TQ, L)
    full = pl.BlockSpec((1, L, wf), lambda i, j: (i, 0, 0), pipeline_mode=pl.Buffered(1))
    return pl.pallas_call(
        functools.partial(_fox_seq_kernel, tq, tk),
        grid=(b, L // tq),
        in_specs=[pl.BlockSpec((1, tq, wf), lambda i, j: (i, j, 0)), full, full,
                  pl.BlockSpec((1, tq, LANES), lambda i, j: (i, j, 0))],
        out_specs=pl.BlockSpec((1, tq, W_C), lambda i, j: (i, j, 0)),
        out_shape=jax.ShapeDtypeStruct((b, L, W_C), F32),
        scratch_shapes=[pltpu.VMEM((H_C, tq, LANES), F32), pltpu.VMEM((H_C, tq, LANES), F32),
                        pltpu.VMEM((H_C, tq, LANES), F32),
                        pltpu.VMEM((2, tq, tk), F32), pltpu.VMEM((2, tq, tk), BF16),
                        pltpu.VMEM((2, tq, LANES), F32)],
        compiler_params=_params(("arbitrary", "arbitrary")),
        name="fox_seq",
    )(q_aug, k_aug, v_aug, dcum)


def _fox_dec_kernel(L, n_pages, pt_ref, q_ref, kn_ref, vn_ref, sm_ref, lf_ref, *refs):
    k_refs = refs[0:n_pages]
    v_refs = refs[n_pages:2 * n_pages]
    o_ref = refs[2 * n_pages]
    x_sc = refs[2 * n_pages + 1]
    i = pl.program_id(0)
    R = H_C * L
    q = q_ref[0] * (HEAD_DIM ** -0.5)
    rowh = _iota2((R, W_C), 0) // L
    colh = _iota2((R, W_C), 1) // HEAD_DIM
    qbd = jnp.where(rowh == colh, jnp.concatenate([q] * H_C, axis=0), 0.0).astype(BF16)

    for p in range(n_pages):
        pg = pt_ref[i * n_pages + p]
        for h in range(H_C):
            r = h * n_pages + p
            x_sc[r:r + 1, :] = lf_ref[h, pl.ds(pg, 1), :]
    x = x_sc[...]
    n = H_C * n_pages
    later = (_iota2((PAGE, PAGE), 0) > _iota2((PAGE, PAGE), 1)).astype(F32)
    within = _dot_sel(x, later)
    tot = _dot_sel(x, jnp.ones((PAGE, PAGE), F32))
    ri = _iota2((n, n), 0)
    ci = _iota2((n, n), 1)
    later_pages = ((ci // n_pages == ri // n_pages) & (ci % n_pages > ri % n_pages)).astype(F32)
    rsum = within + _sel_dot(later_pages, tot)

    sm = sm_ref[0]
    tri = (_iota2((L, L), 0) >= _iota2((L, L), 1)).astype(F32)
    cq = _sel_dot(tri, sm)
    cq_t = cq.T
    cq_col = jnp.concatenate([cq[:, SM_LF + h:SM_LF + h + 1] for h in range(H_C)], axis=0)
    cq_row = jnp.concatenate([jnp.broadcast_to(cq_t[SM_LF + h:SM_LF + h + 1, :], (L, L))
                              for h in range(H_C)], axis=0)

    s_pages = []
    for p in range(n_pages):
        sp = _dot(qbd, k_refs[p][...].astype(BF16))
        bias = jnp.concatenate(
            [jnp.broadcast_to(rsum[h * n_pages + p:h * n_pages + p + 1, :], (L, PAGE)) for h in range(H_C)],
            axis=0)
        s_pages.append(sp + bias + cq_col)
    s_new = _dot_nt(qbd, kn_ref[0].astype(BF16)) + cq_col - cq_row
    qpos = _iota2((R, L), 0) % L
    s_new = jnp.where(_iota2((R, L), 1) <= qpos, s_new, NEG_INF)

    mx = jnp.max(s_new, axis=-1, keepdims=True)
    for sp in s_pages:
        mx = jnp.maximum(mx, jnp.max(sp, axis=-1, keepdims=True))
    p_new = jnp.exp(s_new - mx)
    den = jnp.sum(p_new, axis=-1, keepdims=True)
    acc = _dot(p_new.astype(BF16), vn_ref[0].astype(BF16))
    for p in range(n_pages):
        pp = jnp.exp(s_pages[p] - mx)
        den = den + jnp.sum(pp, axis=-1, keepdims=True)
        acc = acc + _dot_nt(pp.astype(BF16), v_refs[p][...].astype(BF16))
    acc = jnp.where(rowh == colh, acc / den, 0.0)
    out = acc[0:L]
    for h in range(1, H_C):
        out = out + acc[h * L:(h + 1) * L]
    o_ref[0] = out


def _fox_dec(layer, page_table, q, kn, vn, small, cache_kt, cache_vt, cache_lft):
    b, L, _ = q.shape
    n_pages = page_table.shape[1]
    n_pool = cache_kt.shape[1]
    tok = lambda w: pl.BlockSpec((1, L, w), lambda i, pt: (i, 0, 0))

    def page_spec(p):
        return pl.BlockSpec((None, None, W_C, PAGE), lambda i, pt: (layer, pt[i * n_pages + p], 0, 0))

    in_specs = [tok(W_C), tok(W_C), tok(W_C), tok(LANES),
                pl.BlockSpec((None, H_C, n_pool, PAGE), lambda i, pt: (layer, 0, 0, 0),
                             pipeline_mode=pl.Buffered(1))]
    in_specs += [page_spec(p) for p in range(n_pages)]
    in_specs += [page_spec(p) for p in range(n_pages)]
    grid_spec = pltpu.PrefetchScalarGridSpec(
        num_scalar_prefetch=1, grid=(b,), in_specs=in_specs, out_specs=tok(W_C),
        scratch_shapes=[pltpu.VMEM((H_C * n_pages, PAGE), F32)])
    return pl.pallas_call(
        functools.partial(_fox_dec_kernel, L, n_pages),
        grid_spec=grid_spec,
        out_shape=jax.ShapeDtypeStruct((b, L, W_C), F32),
        compiler_params=_params(("arbitrary",)),
        name="fox_dec",
    )(page_table.reshape(-1), q, kn, vn, small, cache_lft,
      *([cache_kt] * n_pages), *([cache_vt] * n_pages))


def _post_kernel(bb, lt, x_ref, oa_ref, ob_ref, oc_ref, gate1_ref, shift2_ref, scale2_ref, gate2_ref,
                 gpm_ref, gpf_ref, gqf_ref, wo_ref, wu_ref, cw_ref, cst_ref, wd_ref,
                 y_ref, tail_ref, ext_ref, car_ref):
    l = pl.program_id(1)
    m = bb * lt
    keep = FFN_CONV - 1

    def rms(v, g_ref):
        return v * lax.rsqrt(jnp.mean(v * v, axis=-1, keepdims=True) + EPS) * g_ref[...]

    oa = oa_ref[...].reshape(m, W_A).astype(BF16)
    ob = ob_ref[...].reshape(m, W_B).astype(BF16)
    oc = oc_ref[...].reshape(m, W_C).astype(BF16)
    o = (_dot(oa, wo_ref[0:W_A, :]) + _dot(ob, wo_ref[W_A:W_A + W_B, :])
         + _dot(oc, wo_ref[W_A + W_B:W_A + W_B + W_C, :]))
    x1 = x_ref[...] + gate1_ref[...] * rms(o, gpm_ref).reshape(bb, lt, D_MODEL)

    h = rms(x1, gpf_ref) * (1.0 + scale2_ref[...]) + shift2_ref[...]
    h2 = h.reshape(m, D_MODEL).astype(BF16)

    @pl.when(l == 0)
    def _():
        car_ref[...] = jnp.zeros_like(car_ref)
        car_ref[:, SUBLANES - keep:SUBLANES, :] = cst_ref[...]

    w = FF_CHUNK
    nchunk = D_FF // FF_CHUNK

    def project(slot, c0):
        up = _dot(h2, wu_ref[:, c0:c0 + w]).reshape(bb, lt, w)
        ext_ref[slot, :, 0:SUBLANES, :] = car_ref[:, :, c0:c0 + w]
        ext_ref[slot, :, SUBLANES:, :] = up
        car_ref[:, :, c0:c0 + w] = ext_ref[slot, :, lt:lt + SUBLANES, :]

    def conv(slot, c0):
        y = None
        for i in range(FFN_CONV):
            off = SUBLANES - keep + i
            term = ext_ref[slot, :, off:off + lt, :] * cw_ref[i:i + 1, c0:c0 + w]
            y = term if y is None else y + term
        return y

    project(0, 0)
    project(1, D_FF)
    y = jnp.zeros((m, D_MODEL), F32)
    for j in range(nchunk):
        if j + 1 < nchunk:
            project(2 * (j + 1), (j + 1) * w)
            project(2 * (j + 1) + 1, D_FF + (j + 1) * w)
        a = conv(2 * j, j * w)
        b = conv(2 * j + 1, D_FF + j * w)
        g = (jax.nn.gelu(a) * b).reshape(m, w).astype(BF16)
        y = y + _dot(g, wd_ref[j * w:(j + 1) * w, :])
    tail_ref[...] = car_ref[...]
    y_ref[...] = x1 + gate2_ref[...] * rms(y, gqf_ref).reshape(bb, lt, D_MODEL)


def _post(dec, layer, x, oa, ob, oc, gate1, shift2, scale2, gate2, gpm, gpf, gqf, wo, wu, cw, cst, wd):
    b, L, _ = x.shape
    if dec:
        bb, lt = min(16, b), L
    else:
        bb, lt = 1, min(512, L)
    tok = lambda w: pl.BlockSpec((bb, lt, w), lambda i, j: (i, j, 0))
    per_b = lambda r, w: pl.BlockSpec((bb, r, w), lambda i, j: (i, 0, 0))
    vec = _const_spec((1, D_MODEL))
    weight = lambda r, c: pl.BlockSpec((None, r, c), lambda i, j: (layer, 0, 0), pipeline_mode=pl.Buffered(1))
    in_specs = [tok(D_MODEL), tok(W_A), tok(W_B), tok(W_C), per_b(1, D_MODEL), per_b(1, D_MODEL),
                per_b(1, D_MODEL), per_b(1, D_MODEL), vec, vec, vec,
                weight(W_A + W_B + W_C, D_MODEL), weight(D_MODEL, 2 * D_FF),
                _const_spec((FFN_CONV, 2 * D_FF)), per_b(FFN_CONV - 1, 2 * D_FF),
                weight(D_FF, D_MODEL)]
    return pl.pallas_call(
        functools.partial(_post_kernel, bb, lt),
        grid=(b // bb, L // lt),
        in_specs=in_specs,
        out_specs=[tok(D_MODEL), per_b(SUBLANES, 2 * D_FF)],
        out_shape=[jax.ShapeDtypeStruct((b, L, D_MODEL), F32),
                   jax.ShapeDtypeStruct((b, SUBLANES, 2 * D_FF), F32)],
        scratch_shapes=[pltpu.VMEM((2 * D_FF // FF_CHUNK, bb, lt + SUBLANES, FF_CHUNK), F32),
                        pltpu.VMEM((bb, SUBLANES, 2 * D_FF), F32)],
        compiler_params=_params(("arbitrary", "arbitrary")),
        name="post_dec" if dec else "post_seq",
    )(x, oa, ob, oc, gate1, shift2, scale2, gate2, gpm, gpf, gqf, wo, wu, cw, cst, wd)


def _layer_params(l, gdn_A_log, gdn_dt_bias, fox_f_bias, chunk_w_s, chunk_b_s, dec_len):
    z = lambda n: jnp.zeros((n,), F32)
    smb = jnp.concatenate([z(H_A), gdn_dt_bias[l], fox_f_bias[l], z(LANES - 2 * H_A - H_C)]).reshape(1, LANES)
    alog = jnp.concatenate([z(H_A), gdn_A_log[l], z(LANES - 2 * H_A)]).reshape(1, LANES)
    ws = chunk_w_s[l]
    bs = chunk_b_s[l]
    ws_seq = ws.reshape(G_B * CHUNK, CHUNK)
    bsb_seq = jnp.repeat(bs.T, HEAD_DIM, axis=1)
    ws_dec = jnp.repeat(jnp.transpose(ws[:, :dec_len, :dec_len], (2, 1, 0)), HEAD_DIM, axis=2)
    bsb_dec = bsb_seq[:dec_len]
    return smb, alog, ws_seq, bsb_seq, ws_dec, bsb_dec


def kernel(x_prompt, x_sample, state_gdn_conv, state_gdn_S, cache_fox_k, cache_fox_v, cache_fox_logf,
           state_ffn_conv, page_table, c_prompt, c_sample, w_ada, b_ada, g_pre_mix, g_post_mix, g_pre_ffn,
           g_post_ffn, w_in, w_out, gdn_conv_w, gdn_A_log, gdn_dt_bias, gdn_norm_g, chunk_ln_g, chunk_ln_b,
           chunk_w_s, chunk_b_s, fox_f_bias, w_up, ffn_conv_w, w_down):
    depth = w_in.shape[0]
    bp, lp, _ = x_prompt.shape
    bs, ls, _ = x_sample.shape
    n_pool = cache_fox_k.shape[1]

    cache_kt = jnp.transpose(cache_fox_k, (0, 1, 3, 4, 2)).reshape(depth, n_pool, W_C, PAGE)
    cache_vt = jnp.transpose(cache_fox_v, (0, 1, 3, 4, 2)).reshape(depth, n_pool, W_C, PAGE)
    cache_lft = jnp.transpose(cache_fox_logf, (0, 3, 1, 2))

    c_all = jnp.concatenate([c_prompt, c_sample], axis=0)
    pad = (-c_all.shape[0]) % SUBLANES
    c_all = jnp.pad(c_all, ((0, pad), (0, 0)))

    w_fused = _w_in_prep(w_in)
    pages = tuple(jnp.zeros((depth, bp, lp // PAGE, W_C, PAGE), F32) for _ in range(2))

    vec = lambda a: a.reshape(1, -1)
    outs = {k: [] for k in ("p_conv", "p_S", "p_lf", "p_ffn",
                            "s_conv", "s_S", "s_k", "s_v", "s_lf", "s_cv", "s_ffn")}
    xp, xs = x_prompt, x_sample
    mod_all = _ada(c_all, w_ada, b_ada)
    wo = w_out.astype(BF16)
    wu = w_up.astype(BF16)
    wd = w_down.astype(BF16)
    for l in range(depth):
        mod_p = mod_all[l, :bp].reshape(bp, 6, 1, D_MODEL)
        mod_s = mod_all[l, bp:bp + bs].reshape(bs, 6, 1, D_MODEL)
        smb, alog, ws_seq, bsb_seq, ws_dec, bsb_dec = _layer_params(
            l, gdn_A_log, gdn_dt_bias, fox_f_bias, chunk_w_s, chunk_b_s, ls)
        ng = jnp.tile(gdn_norm_g[l], H_A).reshape(1, W_A)
        common_in = (vec(g_pre_mix[l]), w_fused, gdn_conv_w[l])
        common_b = (smb, alog, vec(chunk_ln_g[l]), vec(chunk_ln_b[l]))
        post_w = (vec(g_post_mix[l]), vec(g_pre_ffn[l]), vec(g_post_ffn[l]), wo, wu, ffn_conv_w[l])

        (qkva, gate, small, ob, tail, pk, pv, qbf, kbf, vbf, dcum, small_t) = _in_proj(
            False, l, xp, mod_p[:, 0], mod_p[:, 1], *common_in,
            jnp.zeros((bp, GDN_CONV - 1, 3 * W_A), F32), *common_b, ws_seq, bsb_seq, pages=pages)
        pages = (pk, pv)
        oa, s_new = _gdn_seq(qkva, small, gate, ng, jnp.zeros((bp, H_A, HEAD_DIM, HEAD_DIM), F32))
        oc = _fox_seq(qbf, kbf, vbf, dcum)
        xp, ftail = _post(False, l, xp, oa, ob, oc, mod_p[:, 2], mod_p[:, 3], mod_p[:, 4], mod_p[:, 5],
                          *post_w, jnp.zeros((bp, FFN_CONV - 1, 2 * D_FF), F32), wd)
        outs["p_conv"].append(tail[:, SUBLANES - (GDN_CONV - 1):])
        outs["p_S"].append(s_new)
        lf_hm = small_t[:, SM_LF:SM_LF + H_C, :].reshape(bp, H_C, lp // PAGE, PAGE)
        outs["p_lf"].append(jnp.transpose(lf_hm, (0, 2, 3, 1)))
        outs["p_ffn"].append(ftail[:, SUBLANES - (FFN_CONV - 1):])

        (qkva, gate, small, ob, tail, kc, vc, vb, qd) = _in_proj(
            True, l, xs, mod_s[:, 0], mod_s[:, 1], *common_in, state_gdn_conv[l], *common_b, ws_dec, bsb_dec)
        to_lanes = lambda a: jnp.transpose(a, (1, 2, 0))
        oa_t, s_new_t = _gdn_dec(to_lanes(qkva), to_lanes(small), to_lanes(gate),
                                 jnp.broadcast_to(ng.reshape(W_A, 1), (W_A, bs)),
                                 jnp.transpose(state_gdn_S[l], (1, 2, 3, 0)))
        oa = jnp.transpose(oa_t, (2, 0, 1))
        s_new = jnp.transpose(s_new_t, (3, 0, 1, 2))
        oc = _fox_dec(l, page_table, qd, kc, vc, small, cache_kt, cache_vt, cache_lft)
        xs, ftail = _post(True, l, xs, oa, ob, oc, mod_s[:, 2], mod_s[:, 3], mod_s[:, 4], mod_s[:, 5],
                          *post_w, state_ffn_conv[l], wd)
        outs["s_conv"].append(tail[:, SUBLANES - (GDN_CONV - 1):])
        outs["s_S"].append(s_new)
        outs["s_k"].append(kc.reshape(bs, ls, H_C, HEAD_DIM))
        outs["s_v"].append(vc.reshape(bs, ls, H_C, HEAD_DIM))
        outs["s_lf"].append(small[:, :, SM_LF:SM_LF + H_C])
        outs["s_cv"].append(vb)
        outs["s_ffn"].append(ftail[:, SUBLANES - (FFN_CONV - 1):])

    st = lambda k: jnp.stack(outs[k])
    from_pages = lambda p: jnp.transpose(
        p.reshape(depth, bp, lp // PAGE, H_C, HEAD_DIM, PAGE), (0, 1, 2, 5, 3, 4))
    return (xp, xs, st("p_conv"), st("p_S"), from_pages(pages[0]), from_pages(pages[1]), st("p_lf"), st("p_ffn"),
            st("s_conv"), st("s_S"), st("s_k"), st("s_v"), st("s_lf"), st("s_cv"), st("s_ffn"))
```

```python
import functools

import jax
import jax.numpy as jnp
from jax import lax
from jax.experimental import pallas as pl
from jax.experimental.pallas import tpu as pltpu

F32 = jnp.float32
BF16 = jnp.bfloat16

D_MODEL = 1024
HEAD_DIM = 64
H_A = 6
G_B = 4
H_C = 6
W_A = H_A * HEAD_DIM
W_B = G_B * HEAD_DIM
W_C = H_C * HEAD_DIM
GDN_CONV = 4
GDN_CHUNK = 64
CHUNK = 128
D_FF = 2816
FFN_CONV = 3
EPS = 1e-6
NEG_INF = -1e30
PAGE = 128
LOG2E = 1.4426950408889634

LANES = 128
SUBLANES = 8
VMEM_LIMIT = 56 * 1024 * 1024

QA_OFF = 0
GA_OFF = QA_OFF + 3 * W_A
UV_OFF = GA_OFF + W_A
QC_OFF = UV_OFF + 2 * W_B
SM_OFF = QC_OFF + 3 * W_C
N_FUSED = SM_OFF + LANES
SM_BETA = 0
SM_G = H_A
SM_LF = 2 * H_A

FF_CHUNK = D_FF // 2


def _silu(x):
    return x * jax.nn.sigmoid(x)


def _softplus(x):
    return jnp.maximum(x, 0.0) + jnp.log1p(jnp.exp(-jnp.abs(x)))


def _dot(a, b):
    return jnp.dot(a, b, preferred_element_type=F32)


def _split3(x):
    t1 = x.astype(BF16)
    r1 = x - t1.astype(F32)
    t2 = r1.astype(BF16)
    t3 = (r1 - t2.astype(F32)).astype(BF16)
    return t1, t2, t3


def _sel_dot(sel, x):
    s = sel.astype(BF16)
    t1, t2, t3 = _split3(x)
    return _dot(s, t1) + (_dot(s, t2) + _dot(s, t3))


def _dot_sel(x, sel):
    s = sel.astype(BF16)
    t1, t2, t3 = _split3(x)
    return _dot(t1, s) + (_dot(t2, s) + _dot(t3, s))


def _dot_nt(a, b):
    return lax.dot_general(a, b, (((1,), (1,)), ((), ())), preferred_element_type=F32)


def _dot_tn(a, b):
    return lax.dot_general(a, b, (((0,), (0,)), ((), ())), preferred_element_type=F32)


def _iota2(shape, dim):
    return lax.broadcasted_iota(jnp.int32, shape, dim)


def _params(sem):
    return pltpu.CompilerParams(dimension_semantics=sem, vmem_limit_bytes=VMEM_LIMIT)


def _const_spec(shape):
    nd = len(shape)
    return pl.BlockSpec(shape, lambda *_: (0,) * nd, pipeline_mode=pl.Buffered(1))


def _ada_kernel(c_ref, w_ref, b_ref, o_ref):
    c = _silu(c_ref[...]).astype(BF16)
    o_ref[...] = _dot(c, w_ref[...].astype(BF16)) + b_ref[...]


def _ada(c, w, b):
    m = c.shape[0]
    depth, _, n = w.shape
    tn = 1024
    return pl.pallas_call(
        _ada_kernel,
        grid=(depth, n // tn),
        in_specs=[pl.BlockSpec((m, D_MODEL), lambda l, j: (0, 0)),
                  pl.BlockSpec((None, D_MODEL, tn), lambda l, j: (l, 0, j)),
                  pl.BlockSpec((None, 1, tn), lambda l, j: (l, 0, j))],
        out_specs=pl.BlockSpec((None, m, tn), lambda l, j: (l, 0, j)),
        out_shape=jax.ShapeDtypeStruct((depth, m, n), F32),
        compiler_params=_params(("arbitrary", "arbitrary")),
        name="ada_mod",
    )(c, w, b.reshape(depth, 1, n))


def _w_in_prep_kernel(depth, a_ref, o_ref):
    l = pl.program_id(0)
    j = pl.program_id(1)
    nkt = D_MODEL // LANES
    rpn = nkt * depth
    n_regular = SM_OFF // LANES
    n_first = GA_OFF // LANES
    n_in = SM_OFF + 2 * H_A + H_C

    @pl.when(j < n_regular)
    def _():
        n0 = j * LANES + jnp.where(j >= n_first, 2 * H_A, 0)
        for kt in range(nkt):
            x = a_ref[pl.ds(n0 * rpn + kt * depth + l, LANES, stride=rpn), :]
            o_ref[0, kt * LANES:(kt + 1) * LANES, :] = x.T.astype(BF16)

    @pl.when(j == n_regular)
    def _():
        r1, r2 = 2 * SUBLANES, SUBLANES
        n2 = n_in - r2
        r = _iota2((r1 + r2, LANES), 0)
        m = _iota2((r1 + r2, LANES), 1)
        pick = ((m < 2 * H_A) & (r == m)) | (
            (m >= 2 * H_A) & (m < 2 * H_A + H_C) & (r == m - 2 * H_A + r1 + (r2 - H_C)))
        sel = pick.astype(BF16)
        for kt in range(nkt):
            x1 = a_ref[pl.ds(GA_OFF * rpn + kt * depth + l, r1, stride=rpn), :]
            x2 = a_ref[pl.ds(n2 * rpn + kt * depth + l, r2, stride=rpn), :]
            xc = jnp.concatenate([x1, x2], axis=0).astype(BF16)
            o_ref[0, kt * LANES:(kt + 1) * LANES, :] = _dot_tn(xc, sel).astype(BF16)


def _w_in_prep(w_in):
    depth, d, n_in = w_in.shape
    nkt = d // LANES
    view = w_in.reshape(depth, nkt, LANES, n_in).transpose(3, 1, 0, 2).reshape(n_in * nkt * depth, LANES)
    return pl.pallas_call(
        functools.partial(_w_in_prep_kernel, depth),
        grid=(depth, N_FUSED // LANES),
        in_specs=[pl.BlockSpec(view.shape, lambda l, j: (0, 0), pipeline_mode=pl.Buffered(1))],
        out_specs=pl.BlockSpec((1, d, LANES), lambda l, j: (l, 0, j)),
        out_shape=jax.ShapeDtypeStruct((depth, d, N_FUSED), BF16),
        compiler_params=_params(("arbitrary", "arbitrary")),
        name="w_in_prep",
    )(view)


def _in_kernel(dec, bb, lt, *refs):
    (x_ref, shift_ref, scale_ref, gpre_ref, w_ref, cw_ref, cst_ref, smb_ref, alog_ref,
     lng_ref, lnb_ref, ws_ref, bsb_ref) = refs[:13]
    if dec:
        (qkva_ref, gate_ref, small_ref, ob_ref, tail_ref, kc_ref, vc_ref, vb_ref, q_ref,
         ext_ref) = refs[13:]
    else:
        (_, _, qkva_ref, gate_ref, small_ref, ob_ref, tail_ref, pk_ref, pv_ref, qbf_ref, kbf_ref,
         vbf_ref, dcum_ref, smt_ref, ext_ref, dcar_ref) = refs[13:]
    l = pl.program_id(1)
    m = bb * lt

    x = x_ref[...]
    ms = jnp.mean(x * x, axis=-1, keepdims=True)
    h = x * lax.rsqrt(ms + EPS) * gpre_ref[...]
    h = h * (1.0 + scale_ref[...]) + shift_ref[...]
    h2 = h.reshape(m, D_MODEL).astype(BF16)

    wa = 3 * W_A
    za = _dot(h2, w_ref[:, QA_OFF:QA_OFF + wa]).reshape(bb, lt, wa)

    @pl.when(l == 0)
    def _():
        ext_ref[:, SUBLANES - (GDN_CONV - 1):SUBLANES, :] = cst_ref[...]

    @pl.when(l > 0)
    def _():
        ext_ref[:, 0:SUBLANES, :] = ext_ref[:, lt:lt + SUBLANES, :]

    ext_ref[:, SUBLANES:, :] = za
    y = za * cw_ref[GDN_CONV - 1:GDN_CONV, :]
    for i in range(GDN_CONV - 1):
        o = SUBLANES - (GDN_CONV - 1) + i
        y = y + ext_ref[:, o:o + lt, :] * cw_ref[i:i + 1, :]
    qkva_ref[...] = _silu(y)
    tail_ref[...] = ext_ref[:, lt:lt + SUBLANES, :]

    gate_ref[...] = _silu(_dot(h2, w_ref[:, GA_OFF:GA_OFF + W_A])).reshape(bb, lt, W_A)

    zs = _dot(h2, w_ref[:, SM_OFF:SM_OFF + LANES]) + smb_ref[...]
    lane = _iota2((1, LANES), 1)
    beta = jax.nn.sigmoid(zs)
    gval = -jnp.exp(alog_ref[...]) * _softplus(zs)
    lf = -_softplus(-zs)
    small = jnp.where(lane < SM_G, beta,
                      jnp.where(lane < SM_LF, gval, jnp.where(lane < SM_LF + H_C, lf, 0.0)))
    small_ref[...] = small.reshape(bb, lt, LANES)

    if not dec:
        tri = (_iota2((CHUNK, CHUNK), 0) >= _iota2((CHUNK, CHUNK), 1)).astype(F32)

        @pl.when(l == 0)
        def _():
            dcar_ref[...] = jnp.zeros_like(dcar_ref)

        run = dcar_ref[0:1, :]
        parts = []
        for c in range(lt // CHUNK):
            parts.append(_sel_dot(tri, small[c * CHUNK:(c + 1) * CHUNK]) + run)
            run = parts[-1][CHUNK - 1:CHUNK, :]
        dc = jnp.concatenate(parts, axis=0)
        dcum_ref[...] = dc.reshape(bb, lt, LANES)
        dcar_ref[0:1, :] = dc[lt - 1:lt, :]
        smt_ref[0] = small.T

    uv = jax.nn.gelu(_dot(h2, w_ref[:, UV_OFF:UV_OFF + 2 * W_B]))
    u = uv[:, :W_B]
    v = uv[:, W_B:]
    mu = jnp.mean(v, axis=-1, keepdims=True)
    vc0 = v - mu
    var = jnp.mean(vc0 * vc0, axis=-1, keepdims=True)
    vb = vc0 * lax.rsqrt(var + EPS) * lng_ref[...] + lnb_ref[...]
    if dec:
        vb3 = vb.reshape(bb, lt, W_B)
        vb_ref[...] = vb3
        trow = _iota2((lt, W_B), 0)
        mix = jnp.zeros((bb, lt, W_B), F32)
        for s in range(lt):
            coef = jnp.where(trow >= s, ws_ref[s], 0.0)
            mix = mix + coef[None] * vb3[:, s:s + 1, :]
        ob_ref[...] = u.reshape(bb, lt, W_B) * (mix + bsb_ref[...][None])
    else:
        r_t = _iota2((G_B * CHUNK, CHUNK), 0) % CHUNK
        r_s = _iota2((G_B * CHUNK, CHUNK), 1)
        wst = jnp.where(r_s <= r_t, ws_ref[...], 0.0).astype(BF16)
        grp = _iota2((1, W_B), 1) // HEAD_DIM
        for c in range(lt // CHUNK):
            rows = slice(c * CHUNK, (c + 1) * CHUNK)
            r = _dot(wst, vb[rows].astype(BF16))
            mix = jnp.zeros((CHUNK, W_B), F32)
            for g in range(G_B):
                mix = jnp.where(grp == g, r[g * CHUNK:(g + 1) * CHUNK], mix)
            ob_ref[0, rows, :] = u[rows] * (mix + bsb_ref[...])

    zc = _dot(h2, w_ref[:, QC_OFF:QC_OFF + 3 * W_C])
    qc = zc[:, 0:W_C]
    kc = zc[:, W_C:2 * W_C]
    vc = zc[:, 2 * W_C:3 * W_C]
    if dec:
        kc_ref[...] = kc.reshape(bb, lt, W_C)
        vc_ref[...] = vc.reshape(bb, lt, W_C)
        q_ref[...] = qc.reshape(bb, lt, W_C)
    else:
        for c in range(lt // PAGE):
            rows = slice(c * PAGE, (c + 1) * PAGE)
            pk_ref[0, c] = kc[rows].T
            pv_ref[0, c] = vc[rows].T
        hd = HEAD_DIM
        d2 = dc * LOG2E
        for h in range(H_C):
            pair = slice((h // 2) * LANES, (h // 2 + 1) * LANES)
            at0 = (lambda a: a) if h % 2 == 0 else (lambda a: pltpu.roll(a, hd, axis=1))
            dh = d2[:, SM_LF + h:SM_LF + h + 1]
            t1 = dh.astype(BF16).astype(F32)
            t2 = (dh - t1).astype(BF16).astype(F32)
            t3 = dh - t1 - t2
            qa = jnp.where(lane < hd, at0(qc[:, pair]) * (LOG2E * hd ** -0.5), jnp.where(lane < hd + 3, 1.0, 0.0))
            ka = jnp.where(lane < hd, at0(kc[:, pair]),
                           jnp.where(lane == hd, -t1, jnp.where(lane == hd + 1, -t2,
                                                                jnp.where(lane == hd + 2, -t3, 0.0))))
            va = jnp.where(lane < hd, at0(vc[:, pair]), jnp.where(lane == hd, 1.0, 0.0))
            grp = slice(h * LANES, (h + 1) * LANES)
            qbf_ref[0, :, grp] = qa.astype(BF16)
            kbf_ref[0, :, grp] = ka.astype(BF16)
            vbf_ref[0, :, grp] = va.astype(BF16)


def _in_proj(dec, layer, x, shift, scale, gpre, w_fused, conv_w, conv_state, smb, alog, lng, lnb, ws, bsb,
             pages=()):
    b, L, _ = x.shape
    if dec:
        bb, lt = min(32, b), L
    else:
        bb, lt = 1, min(512, L)
    grid = (b // bb, L // lt)
    tok = lambda w: pl.BlockSpec((bb, lt, w), lambda i, j: (i, j, 0))
    per_b = lambda r, w: pl.BlockSpec((bb, r, w), lambda i, j: (i, 0, 0))
    in_specs = [tok(D_MODEL), per_b(1, D_MODEL), per_b(1, D_MODEL), _const_spec((1, D_MODEL)),
                pl.BlockSpec((None, D_MODEL, N_FUSED), lambda i, j: (layer, 0, 0), pipeline_mode=pl.Buffered(1)),
                _const_spec((GDN_CONV, 3 * W_A)),
                per_b(GDN_CONV - 1, 3 * W_A), _const_spec((1, LANES)), _const_spec((1, LANES)),
                _const_spec((1, W_B)), _const_spec((1, W_B)), _const_spec(ws.shape), _const_spec(bsb.shape)]
    sds = lambda w, dt=F32: jax.ShapeDtypeStruct((b, L, w), dt)
    out_shape = [sds(3 * W_A), sds(W_A), sds(LANES), sds(W_B), jax.ShapeDtypeStruct((b, SUBLANES, 3 * W_A), F32)]
    out_specs = [tok(3 * W_A), tok(W_A), tok(LANES), tok(W_B), per_b(SUBLANES, 3 * W_A)]
    scratch = [pltpu.VMEM((bb, lt + SUBLANES, 3 * W_A), F32)]
    aliases = {}
    if dec:
        out_shape += [sds(W_C), sds(W_C), sds(W_B), sds(W_C)]
        out_specs += [tok(W_C), tok(W_C), tok(W_B), tok(W_C)]
    else:
        n_in = len(in_specs)
        in_specs += [pl.BlockSpec(memory_space=pl.ANY)] * 2
        aliases = {n_in: len(out_shape), n_in + 1: len(out_shape) + 1}
        page_spec = pl.BlockSpec((None, 1, lt // PAGE, W_C, PAGE), lambda i, j: (layer, i, j, 0, 0))
        head_major = pl.BlockSpec((1, LANES, lt), lambda i, j: (i, 0, j))
        out_shape += [jax.ShapeDtypeStruct(p.shape, p.dtype) for p in pages]
        out_specs += [page_spec, page_spec]
        wf = H_C * LANES
        out_shape += [sds(wf, BF16), sds(wf, BF16), sds(wf, BF16), sds(LANES),
                      jax.ShapeDtypeStruct((b, LANES, L), F32)]
        out_specs += [tok(wf), tok(wf), tok(wf), tok(LANES), head_major]
        scratch += [pltpu.VMEM((SUBLANES, LANES), F32)]
    return pl.pallas_call(
        functools.partial(_in_kernel, dec, bb, lt),
        grid=grid, in_specs=in_specs, out_specs=out_specs, out_shape=out_shape,
        scratch_shapes=scratch, input_output_aliases=aliases,
        compiler_params=_params(("arbitrary", "arbitrary")),
        name="in_proj_dec" if dec else "in_proj_seq",
    )(x, shift, scale, gpre, w_fused, conv_w, conv_state, smb, alog, lng, lnb, ws, bsb, *pages)


GDN_GROUP = 256
GDN_GROUPS_PER_STEP = 2
N_PAIR = H_A // 2


def _tri_inverse_m1(lms, blk):
    bf = lambda xs: [x.astype(BF16) for x in xs]
    mm_ = lambda xs, ys: [_dot(x, y) for x, y in zip(xs, ys)]
    dm = [jnp.where(blk, lm, 0.0) for lm in lms]
    nm = [lm - d for lm, d in zip(lms, dm)]
    dmb = bf(dm)
    d2 = mm_(dmb, dmb)
    d2b = bf(d2)
    d4 = mm_(d2b, d2b)
    d4b = bf(d4)
    d8 = mm_(d4b, d4b)
    a1 = [x2 - x1 - c for x2, x1, c in zip(d2, dm, mm_(dmb, d2b))]
    a2 = [x4 + x8 + c for x4, x8, c in zip(d4, d8, mm_(d4b, bf(d8)))]
    et = [x + y + c for x, y, c in zip(a1, a2, mm_(bf(a1), bf(a2)))]
    etb = bf(et)
    mm = [n + c for n, c in zip(nm, mm_(etb, bf(nm)))]
    mmb = bf(mm)
    m2 = mm_(mmb, mmb)
    a3 = [x2 - x1 - c for x2, x1, c in zip(m2, mm, mm_(mmb, bf(m2)))]
    return [x + e + c for x, e, c in zip(a3, et, mm_(bf(a3), etb))]


def _pair_cols(lo, a, c0, c1):
    return jnp.where(lo, a[:, c0:c0 + 1], a[:, c1:c1 + 1])


def _pair_rsqrt_norm(lo, x, scale):
    x2 = x * x
    s_lo = jnp.sum(jnp.where(lo, x2, 0.0), axis=-1, keepdims=True)
    s_hi = jnp.sum(jnp.where(lo, 0.0, x2), axis=-1, keepdims=True)
    return jnp.where(lo, lax.rsqrt(s_lo * scale + EPS), lax.rsqrt(s_hi * scale + EPS))


def _gdn_pre_kernel(T, G, qkv_ref, small_ref, qe_ref, o0_ref, egl_ref, m_ref, bm_ref):
    C = GDN_CHUNK
    row = _iota2((T, T), 0)
    col = _iota2((T, T), 1)
    same = (row // C) == (col // C)
    bd_tri = same & (row >= col)
    bd_strict = same & (row > col)
    blk = (row // 16) == (col // 16)
    lane = _iota2((1, LANES), 1)
    lo = lane < HEAD_DIM
    lo2 = (_iota2((1, 2 * LANES), 1) % LANES) < HEAD_DIM
    bd2 = (_iota2((LANES, 2 * LANES), 0) // HEAD_DIM) == ((_iota2((LANES, 2 * LANES), 1) % LANES) // HEAD_DIM)

    rhs, rhs_b, lms, attns, qins, kouts = [], [], [], [], [], []
    for g, j in [(g, j) for g in range(G) for j in range(N_PAIR)]:
        tok = slice(g * T, (g + 1) * T)
        if j == 0:
            sm = small_ref[0, tok, :]
            gc_all = _sel_dot(bd_tri, sm)
            gl_all = jnp.concatenate(
                [jnp.broadcast_to(gc_all[(c + 1) * C - 1:(c + 1) * C, :], (C, LANES)) for c in range(T // C)],
                axis=0)
            gc_t = gc_all.T
        sl = slice(j * LANES, (j + 1) * LANES)
        qp = qkv_ref[0, tok, j * LANES:(j + 1) * LANES]
        kp = qkv_ref[0, tok, W_A + j * LANES:W_A + (j + 1) * LANES]
        vp = qkv_ref[0, tok, 2 * W_A + j * LANES:2 * W_A + (j + 1) * LANES]
        qn = qp * _pair_rsqrt_norm(lo, qp, 1.0) * (HEAD_DIM ** -0.5)
        kn = kp * _pair_rsqrt_norm(lo, kp, 1.0)
        h0, h1 = 2 * j, 2 * j + 1
        beta_p = _pair_cols(lo, sm, SM_BETA + h0, SM_BETA + h1)
        gc_p = _pair_cols(lo, gc_all, SM_G + h0, SM_G + h1)
        gl_p = _pair_cols(lo, gl_all, SM_G + h0, SM_G + h1)
        eg_p = jnp.exp(gc_p)
        kb = kn * beta_p
        kn_bf = kn.astype(BF16)
        rhs.append(jnp.concatenate([kb * eg_p, vp * beta_p], axis=1))
        rhs_b.append(rhs[-1].astype(BF16))
        qins.append(qn * eg_p)
        kouts.append((kn * jnp.exp(gl_p - gc_p)).astype(BF16))
        egl_ref[0, tok, sl] = jnp.exp(gl_p)
        zero = jnp.zeros_like(kb)
        stacked = jnp.concatenate([jnp.where(lo, kb, zero), jnp.where(lo, zero, kb),
                                   jnp.where(lo, qn, zero), jnp.where(lo, zero, qn)], axis=0).astype(BF16)
        gq = _dot_nt(stacked, kn_bf)
        for hh in range(2):
            h = 2 * j + hh
            gcol = gc_all[:, SM_G + h:SM_G + h + 1]
            grow = gc_t[SM_G + h:SM_G + h + 1, :]
            decay = jnp.where(bd_tri, jnp.exp(jnp.where(bd_tri, gcol - grow, 0.0)), 0.0)
            lms.append(jnp.where(bd_strict, gq[hh * T:(hh + 1) * T] * decay, 0.0))
            attns.append((gq[(2 + hh) * T:(3 + hh) * T] * decay).astype(BF16))

    tm1 = _tri_inverse_m1(lms, blk)
    tok_chunk = _iota2((T, LANES), 0) // C
    for g, j in [(g, j) for g in range(G) for j in range(N_PAIR)]:
        tok = slice(g * T, (g + 1) * T)
        p = g * N_PAIR + j
        sl = slice(j * LANES, (j + 1) * LANES)
        both = _dot(jnp.concatenate([tm1[2 * p], tm1[2 * p + 1]], axis=0).astype(BF16), rhs_b[p])
        wu = rhs[p] + jnp.where(lo2, both[:T], both[T:])
        wu_b = wu.astype(BF16)
        both = _dot(jnp.concatenate([attns[2 * p], attns[2 * p + 1]], axis=0), wu_b)
        aw = jnp.where(lo2, both[:T], both[T:])
        qe_ref[0, tok, sl] = (qins[p] - aw[:, :LANES]).astype(BF16)
        o0_ref[0, tok, sl] = aw[:, LANES:]
        ko = kouts[p]
        spread = jnp.concatenate([jnp.where(tok_chunk == c, ko, jnp.zeros_like(ko)) for c in range(T // C)],
                                 axis=1)
        mb_all = _dot_tn(spread, wu_b)
        for c in range(T // C):
            mb = jnp.where(bd2, mb_all[c * LANES:(c + 1) * LANES], 0.0)
            m_ref[0, g * (T // C) + c, j] = mb[:, :LANES].astype(BF16)
            bm_ref[0, g * (T // C) + c, j] = mb[:, LANES:]


def _gdn_pre(qkva, small):
    b, L, _ = qkva.shape
    T = min(GDN_GROUP, L)
    G = max(1, min(GDN_GROUPS_PER_STEP, L // T))
    nc = G * T // GDN_CHUNK
    tok = lambda w: pl.BlockSpec((1, G * T, w), lambda i, j: (i, j, 0))
    per_chunk = pl.BlockSpec((1, nc, N_PAIR, LANES, LANES), lambda i, j: (i, j, 0, 0, 0))
    sds = lambda dt: jax.ShapeDtypeStruct((b, L, W_A), dt)
    chunk_sds = lambda dt: jax.ShapeDtypeStruct((b, L // GDN_CHUNK, N_PAIR, LANES, LANES), dt)
    return pl.pallas_call(
        functools.partial(_gdn_pre_kernel, T, G),
        grid=(b, L // (G * T)),
        in_specs=[tok(3 * W_A), tok(LANES)],
        out_specs=[tok(W_A)] * 3 + [per_chunk] * 2,
        out_shape=[sds(BF16), sds(F32), sds(F32), chunk_sds(BF16), chunk_sds(F32)],
        compiler_params=_params(("arbitrary", "arbitrary")),
        name="gdn_pre",
    )(qkva, small)


def _gdn_scan_kernel(nb, lt, qe_ref, o0_ref, egl_ref, m_ref, bm_ref, gate_ref, ng_ref, s0_ref,
                     oa_ref, sout_ref, s_sc):
    l = pl.program_id(0)
    C = GDN_CHUNK

    @pl.when(l == 0)
    def _():
        s_sc[...] = s0_ref[...]

    lane = _iota2((1, LANES), 1)
    lo = lane < HEAD_DIM

    def body(c, carry):
        r0 = pl.multiple_of(c * C, C)
        rows = pl.ds(r0, C)
        for b in range(nb):
            for j in range(N_PAIR):
                sl = slice(j * LANES, (j + 1) * LANES)
                s_old = s_sc[b, j]
                lhs = jnp.concatenate([m_ref[b, c, j], qe_ref[b, rows, sl]], axis=0)
                r = _dot(lhs, s_old.astype(BF16))
                s_sc[b, j] = s_old * egl_ref[b, pl.ds(r0, 1), sl] + (bm_ref[b, c, j] - r[:LANES])
                o = r[LANES:] + o0_ref[b, rows, sl]
                on = o * _pair_rsqrt_norm(lo, o, 1.0 / HEAD_DIM)
                oa_ref[b, rows, sl] = on * ng_ref[:, sl] * gate_ref[b, rows, sl]
        return carry

    lax.fori_loop(0, lt // C, body, 0)
    sout_ref[...] = s_sc[...]


def _gdn_scan(qe, o0, egl, m, bm, gate, ng, s0_bd):
    b, L, _ = qe.shape
    lt = min(512, L)
    tok = pl.BlockSpec((b, lt, W_A), lambda i: (0, i, 0))
    per_chunk = pl.BlockSpec((b, lt // GDN_CHUNK, N_PAIR, LANES, LANES), lambda i: (0, i, 0, 0, 0))
    st = pl.BlockSpec((b, N_PAIR, LANES, LANES), lambda i: (0, 0, 0, 0))
    return pl.pallas_call(
        functools.partial(_gdn_scan_kernel, b, lt),
        grid=(L // lt,),
        in_specs=[tok, tok, tok, per_chunk, per_chunk, tok, _const_spec((1, W_A)), st],
        out_specs=[tok, st],
        out_shape=[jax.ShapeDtypeStruct((b, L, W_A), F32),
                   jax.ShapeDtypeStruct((b, N_PAIR, LANES, LANES), F32)],
        scratch_shapes=[pltpu.VMEM((b, N_PAIR, LANES, LANES), F32)],
        compiler_params=_params(("arbitrary",)),
        name="gdn_scan",
    )(qe, o0, egl, m, bm, gate, ng, s0_bd)


def _to_pair_blockdiag(s):
    b = s.shape[0]
    s = s.reshape(b, N_PAIR, 2, HEAD_DIM, HEAD_DIM)
    z = jnp.zeros_like(s[:, :, 0])
    top = jnp.concatenate([s[:, :, 0], z], axis=-1)
    bot = jnp.concatenate([z, s[:, :, 1]], axis=-1)
    return jnp.concatenate([top, bot], axis=-2)


def _from_pair_blockdiag(sbd):
    b = sbd.shape[0]
    s = jnp.stack([sbd[:, :, :HEAD_DIM, :HEAD_DIM], sbd[:, :, HEAD_DIM:, HEAD_DIM:]], axis=2)
    return s.reshape(b, H_A, HEAD_DIM, HEAD_DIM)


def _gdn_seq(qkva, small, gate, ng, s0):
    qe, o0, egl, m, bm = _gdn_pre(qkva, small)
    oa, sbd = _gdn_scan(qe, o0, egl, m, bm, gate, ng, _to_pair_blockdiag(s0))
    return oa, _from_pair_blockdiag(sbd)


def _gdn_dec_kernel(L, q_ref, k_ref, v_ref, smt_ref, gate_ref, ngb_ref, s0_ref, oa_ref, sout_ref, kq_sc):
    h = pl.program_id(0)
    nb = q_ref.shape[-1]
    for t in range(L):
        q_t = q_ref[t]
        k_t = k_ref[t]
        q_t = q_t * (lax.rsqrt(jnp.sum(q_t * q_t, axis=0, keepdims=True) + EPS) * (HEAD_DIM ** -0.5))
        k_t = k_t * lax.rsqrt(jnp.sum(k_t * k_t, axis=0, keepdims=True) + EPS)
        kq_sc[0] = k_t
        kq_sc[1] = q_t
        beta = smt_ref[t, pl.ds(SM_BETA + h, 1), :]
        a = jnp.exp(smt_ref[t, pl.ds(SM_G + h, 1), :])
        src = s0_ref if t == 0 else sout_ref

        def k_dot_s(kk, acc):
            return acc + src[0, kk] * kq_sc[0, pl.ds(kk, 1), :]

        ks = lax.fori_loop(0, HEAD_DIM, k_dot_s, jnp.zeros((HEAD_DIM, nb), F32), unroll=8)
        delta = beta * (v_ref[t] - a * ks)

        def update(kk, acc):
            s_new = a * src[0, kk] + kq_sc[0, pl.ds(kk, 1), :] * delta
            sout_ref[0, kk] = s_new
            return acc + s_new * kq_sc[1, pl.ds(kk, 1), :]

        o = lax.fori_loop(0, HEAD_DIM, update, jnp.zeros((HEAD_DIM, nb), F32), unroll=8)
        on = o * lax.rsqrt(jnp.mean(o * o, axis=0, keepdims=True) + EPS)
        oa_ref[t] = on * ngb_ref[...] * gate_ref[t]


def _gdn_dec(qkv_t, small_t, gate_t, ngb, s0_t):
    L, _, nb = qkv_t.shape
    nh = W_A // HEAD_DIM
    head = lambda off: pl.BlockSpec((L, HEAD_DIM, nb), lambda h: (0, off + h, 0))
    st = pl.BlockSpec((1, HEAD_DIM, HEAD_DIM, nb), lambda h: (h, 0, 0, 0))
    return pl.pallas_call(
        functools.partial(_gdn_dec_kernel, L),
        grid=(H_A,),
        in_specs=[head(0), head(nh), head(2 * nh),
                  pl.BlockSpec((L, LANES, nb), lambda h: (0, 0, 0), pipeline_mode=pl.Buffered(1)),
                  head(0), pl.BlockSpec((HEAD_DIM, nb), lambda h: (h, 0)), st],
        out_specs=[head(0), st],
        out_shape=[jax.ShapeDtypeStruct((L, W_A, nb), F32),
                   jax.ShapeDtypeStruct((H_A, HEAD_DIM, HEAD_DIM, nb), F32)],
        scratch_shapes=[pltpu.VMEM((2, HEAD_DIM, nb), F32)],
        compiler_params=_params(("arbitrary",)),
        name="gdn_dec",
    )(qkv_t, qkv_t, qkv_t, small_t, gate_t, ngb, s0_t)


FOX_STRIP = 64
FOX_TQ = 512
FOX_TK = 512


def _fox_seq_kernel(tq, tk, q_ref, k_ref, v_ref, dq_ref, o_ref, m_sc, acc_sc, dqb_sc, s_sc, p_sc, al_sc):
    qi = pl.program_id(1)
    ndiag = tq // tk
    m_sc[...] = jnp.full_like(m_sc, NEG_INF)
    acc_sc[...] = jnp.zeros_like(acc_sc)
    dq = dq_ref[0] * LOG2E
    for h in range(H_C):
        dqb_sc[h] = jnp.broadcast_to(dq[:, SM_LF + h:SM_LF + h + 1], (tq, LANES))

    def step(ki, diag):
        k0 = pl.multiple_of(ki * tk, tk)
        r_lo = 0 if diag is None else diag * tk
        live = slice(r_lo, tq)

        def scores(h):
            grp = slice(h * LANES, (h + 1) * LANES)
            s_sc[h % 2, live, :] = _dot_nt(q_ref[0, live, grp], k_ref[0, pl.ds(k0, tk), grp])

        def accumulate(h):
            grp = slice(h * LANES, (h + 1) * LANES)
            acc_sc[h, live, :] = (al_sc[h % 2, live, :] * acc_sc[h, live, :]
                                  + _dot(p_sc[h % 2, live, :], v_ref[0, pl.ds(k0, tk), grp]))

        scores(0)
        for h in range(H_C):
            buf = h % 2
            if h + 1 < H_C:
                scores(h + 1)
            for r in range(r_lo // FOX_STRIP, tq // FOX_STRIP):
                rows = slice(r * FOX_STRIP, (r + 1) * FOX_STRIP)
                nc = tk if diag is None else min(tk, -(-((r + 1) * FOX_STRIP - r_lo) // LANES) * LANES)
                s = s_sc[buf, rows, 0:nc]
                if diag is not None:
                    keep = _iota2((FOX_STRIP, nc), 1) <= _iota2((FOX_STRIP, nc), 0) + (r * FOX_STRIP - r_lo)
                    s = jnp.where(keep, s, NEG_INF)
                dqb = dqb_sc[h, rows, :]
                m_old = m_sc[h, rows, :]
                m_new = jnp.maximum(m_old, jnp.max(s, axis=-1, keepdims=True) + dqb)
                shift = m_new - dqb
                if nc > LANES:
                    shift = jnp.concatenate([shift] * (nc // LANES), axis=1)
                m_sc[h, rows, :] = m_new
                al_sc[buf, rows, :] = jnp.exp2(m_old - m_new)
                p_sc[buf, rows, 0:nc] = jnp.exp2(s - shift).astype(BF16)
                if nc < tk:
                    p_sc[buf, rows, nc:tk] = jnp.zeros((FOX_STRIP, tk - nc), BF16)
            if h > 0:
                accumulate(h - 1)
        accumulate(H_C - 1)

    def body(ki, carry):
        step(ki, None)
        return carry

    lax.fori_loop(0, ndiag * qi, body, 0)
    for d in range(ndiag):
        step(ndiag * qi + d, d)
    for j in range(H_C // 2):
        halves = []
        for h in (2 * j, 2 * j + 1):
            acc = acc_sc[h]
            halves.append(acc[:, :HEAD_DIM] / acc[:, HEAD_DIM:HEAD_DIM + 1])
        o_ref[0, :, j * LANES:(j + 1) * LANES] = jnp.concatenate(halves, axis=1)


def _fox_seq(q_aug, k_aug, v_aug, dcum):
    b, L, wf = q_aug.shape
    tk = min(FOX_TK, L)
    tq = min(FOX_TQ, L)
    full = pl.BlockSpec((1, L, wf), lambda i, j: (i, 0, 0), pipeline_mode=pl.Buffered(1))
    return pl.pallas_call(
        functools.partial(_fox_seq_kernel, tq, tk),
        grid=(b, L // tq),
        in_specs=[pl.BlockSpec((1, tq, wf), lambda i, j: (i, j, 0)), full, full,
                  pl.BlockSpec((1, tq, LANES), lambda i, j: (i, j, 0))],
        out_specs=pl.BlockSpec((1, tq, W_C), lambda i, j: (i, j, 0)),
        out_shape=jax.ShapeDtypeStruct((b, L, W_C), F32),
        scratch_shapes=[pltpu.VMEM((H_C, tq, LANES), F32), pltpu.VMEM((H_C, tq, LANES), F32),
                        pltpu.VMEM((H_C, tq, LANES), F32),
                        pltpu.VMEM((2, tq, tk), F32), pltpu.VMEM((2, tq, tk), BF16),
                        pltpu.VMEM((2, tq, LANES), F32)],
        compiler_params=_params(("arbitrary", "arbitrary")),
        name="fox_seq",
    )(q_aug, k_aug, v_aug, dcum)


def _fox_dec_kernel(L, n_pages, pt_ref, q_ref, kn_ref, vn_ref, sm_ref, lf_ref, *refs):
    k_refs = refs[0:n_pages]
    v_refs = refs[n_pages:2 * n_pages]
    o_ref = refs[2 * n_pages]
    x_sc = refs[2 * n_pages + 1]
    i = pl.program_id(0)
    R = H_C * L
    q = q_ref[0] * (HEAD_DIM ** -0.5)
    rowh = _iota2((R, W_C), 0) // L
    colh = _iota2((R, W_C), 1) // HEAD_DIM
    qbd = jnp.where(rowh == colh, jnp.concatenate([q] * H_C, axis=0), 0.0).astype(BF16)

    for p in range(n_pages):
        pg = pt_ref[i * n_pages + p]
        for h in range(H_C):
            r = h * n_pages + p
            x_sc[r:r + 1, :] = lf_ref[h, pl.ds(pg, 1), :]
    x = x_sc[...]
    n = H_C * n_pages
    later = (_iota2((PAGE, PAGE), 0) > _iota2((PAGE, PAGE), 1)).astype(F32)
    within = _dot_sel(x, later)
    tot = _dot_sel(x, jnp.ones((PAGE, PAGE), F32))
    ri = _iota2((n, n), 0)
    ci = _iota2((n, n), 1)
    later_pages = ((ci // n_pages == ri // n_pages) & (ci % n_pages > ri % n_pages)).astype(F32)
    rsum = within + _sel_dot(later_pages, tot)

    sm = sm_ref[0]
    tri = (_iota2((L, L), 0) >= _iota2((L, L), 1)).astype(F32)
    cq = _sel_dot(tri, sm)
    cq_t = cq.T
    cq_col = jnp.concatenate([cq[:, SM_LF + h:SM_LF + h + 1] for h in range(H_C)], axis=0)
    cq_row = jnp.concatenate([jnp.broadcast_to(cq_t[SM_LF + h:SM_LF + h + 1, :], (L, L))
                              for h in range(H_C)], axis=0)

    s_pages = []
    for p in range(n_pages):
        sp = _dot(qbd, k_refs[p][...].astype(BF16))
        bias = jnp.concatenate(
            [jnp.broadcast_to(rsum[h * n_pages + p:h * n_pages + p + 1, :], (L, PAGE)) for h in range(H_C)],
            axis=0)
        s_pages.append(sp + bias + cq_col)
    s_new = _dot_nt(qbd, kn_ref[0].astype(BF16)) + cq_col - cq_row
    qpos = _iota2((R, L), 0) % L
    s_new = jnp.where(_iota2((R, L), 1) <= qpos, s_new, NEG_INF)

    mx = jnp.max(s_new, axis=-1, keepdims=True)
    for sp in s_pages:
        mx = jnp.maximum(mx, jnp.max(sp, axis=-1, keepdims=True))
    p_new = jnp.exp(s_new - mx)
    den = jnp.sum(p_new, axis=-1, keepdims=True)
    acc = _dot(p_new.astype(BF16), vn_ref[0].astype(BF16))
    for p in range(n_pages):
        pp = jnp.exp(s_pages[p] - mx)
        den = den + jnp.sum(pp, axis=-1, keepdims=True)
        acc = acc + _dot_nt(pp.astype(BF16), v_refs[p][...].astype(BF16))
    acc = jnp.where(rowh == colh, acc / den, 0.0)
    out = acc[0:L]
    for h in range(1, H_C):
        out = out + acc[h * L:(h + 1) * L]
    o_ref[0] = out


def _fox_dec(layer, page_table, q, kn, vn, small, cache_kt, cache_vt, cache_lft):
    b, L, _ = q.shape
    n_pages = page_table.shape[1]
    n_pool = cache_kt.shape[1]
    tok = lambda w: pl.BlockSpec((1, L, w), lambda i, pt: (i, 0, 0))

    def page_spec(p):
        return pl.BlockSpec((None, None, W_C, PAGE), lambda i, pt: (layer, pt[i * n_pages + p], 0, 0))

    in_specs = [tok(W_C), tok(W_C), tok(W_C), tok(LANES),
                pl.BlockSpec((None, H_C, n_pool, PAGE), lambda i, pt: (layer, 0, 0, 0),
                             pipeline_mode=pl.Buffered(1))]
    in_specs += [page_spec(p) for p in range(n_pages)]
    in_specs += [page_spec(p) for p in range(n_pages)]
    grid_spec = pltpu.PrefetchScalarGridSpec(
        num_scalar_prefetch=1, grid=(b,), in_specs=in_specs, out_specs=tok(W_C),
        scratch_shapes=[pltpu.VMEM((H_C * n_pages, PAGE), F32)])
    return pl.pallas_call(
        functools.partial(_fox_dec_kernel, L, n_pages),
        grid_spec=grid_spec,
        out_shape=jax.ShapeDtypeStruct((b, L, W_C), F32),
        compiler_params=_params(("arbitrary",)),
        name="fox_dec",
    )(page_table.reshape(-1), q, kn, vn, small, cache_lft,
      *([cache_kt] * n_pages), *([cache_vt] * n_pages))


def _post_kernel(bb, lt, x_ref, oa_ref, ob_ref, oc_ref, gate1_ref, shift2_ref, scale2_ref, gate2_ref,
                 gpm_ref, gpf_ref, gqf_ref, wo_ref, wu_ref, cw_ref, cst_ref, wd_ref,
                 y_ref, tail_ref, ext_ref, car_ref):
    l = pl.program_id(1)
    m = bb * lt
    keep = FFN_CONV - 1

    def rms(v, g_ref):
        return v * lax.rsqrt(jnp.mean(v * v, axis=-1, keepdims=True) + EPS) * g_ref[...]

    oa = oa_ref[...].reshape(m, W_A).astype(BF16)
    ob = ob_ref[...].reshape(m, W_B).astype(BF16)
    oc = oc_ref[...].reshape(m, W_C).astype(BF16)
    o = (_dot(oa, wo_ref[0:W_A, :]) + _dot(ob, wo_ref[W_A:W_A + W_B, :])
         + _dot(oc, wo_ref[W_A + W_B:W_A + W_B + W_C, :]))
    x1 = x_ref[...] + gate1_ref[...] * rms(o, gpm_ref).reshape(bb, lt, D_MODEL)

    h = rms(x1, gpf_ref) * (1.0 + scale2_ref[...]) + shift2_ref[...]
    h2 = h.reshape(m, D_MODEL).astype(BF16)

    @pl.when(l == 0)
    def _():
        car_ref[...] = jnp.zeros_like(car_ref)
        car_ref[:, SUBLANES - keep:SUBLANES, :] = cst_ref[...]

    w = FF_CHUNK
    nchunk = D_FF // FF_CHUNK

    def project(slot, c0):
        up = _dot(h2, wu_ref[:, c0:c0 + w]).reshape(bb, lt, w)
        ext_ref[slot, :, 0:SUBLANES, :] = car_ref[:, :, c0:c0 + w]
        ext_ref[slot, :, SUBLANES:, :] = up
        car_ref[:, :, c0:c0 + w] = ext_ref[slot, :, lt:lt + SUBLANES, :]

    def conv(slot, c0):
        y = None
        for i in range(FFN_CONV):
            off = SUBLANES - keep + i
            term = ext_ref[slot, :, off:off + lt, :] * cw_ref[i:i + 1, c0:c0 + w]
            y = term if y is None else y + term
        return y

    project(0, 0)
    project(1, D_FF)
    y = jnp.zeros((m, D_MODEL), F32)
    for j in range(nchunk):
        if j + 1 < nchunk:
            project(2 * (j + 1), (j + 1) * w)
            project(2 * (j + 1) + 1, D_FF + (j + 1) * w)
        a = conv(2 * j, j * w)
        b = conv(2 * j + 1, D_FF + j * w)
        g = (jax.nn.gelu(a) * b).reshape(m, w).astype(BF16)
        y = y + _dot(g, wd_ref[j * w:(j + 1) * w, :])
    tail_ref[...] = car_ref[...]
    y_ref[...] = x1 + gate2_ref[...] * rms(y, gqf_ref).reshape(bb, lt, D_MODEL)


def _post(dec, layer, x, oa, ob, oc, gate1, shift2, scale2, gate2, gpm, gpf, gqf, wo, wu, cw, cst, wd):
    b, L, _ = x.shape
    if dec:
        bb, lt = min(16, b), L
    else:
        bb, lt = 1, min(512, L)
    tok = lambda w: pl.BlockSpec((bb, lt, w), lambda i, j: (i, j, 0))
    per_b = lambda r, w: pl.BlockSpec((bb, r, w), lambda i, j: (i, 0, 0))
    vec = _const_spec((1, D_MODEL))
    weight = lambda r, c: pl.BlockSpec((None, r, c), lambda i, j: (layer, 0, 0), pipeline_mode=pl.Buffered(1))
    in_specs = [tok(D_MODEL), tok(W_A), tok(W_B), tok(W_C), per_b(1, D_MODEL), per_b(1, D_MODEL),
                per_b(1, D_MODEL), per_b(1, D_MODEL), vec, vec, vec,
                weight(W_A + W_B + W_C, D_MODEL), weight(D_MODEL, 2 * D_FF),
                _const_spec((FFN_CONV, 2 * D_FF)), per_b(FFN_CONV - 1, 2 * D_FF),
                weight(D_FF, D_MODEL)]
    return pl.pallas_call(
        functools.partial(_post_kernel, bb, lt),
        grid=(b // bb, L // lt),
        in_specs=in_specs,
        out_specs=[tok(D_MODEL), per_b(SUBLANES, 2 * D_FF)],
        out_shape=[jax.ShapeDtypeStruct((b, L, D_MODEL), F32),
                   jax.ShapeDtypeStruct((b, SUBLANES, 2 * D_FF), F32)],
        scratch_shapes=[pltpu.VMEM((2 * D_FF // FF_CHUNK, bb, lt + SUBLANES, FF_CHUNK), F32),
                        pltpu.VMEM((bb, SUBLANES, 2 * D_FF), F32)],
        compiler_params=_params(("arbitrary", "arbitrary")),
        name="post_dec" if dec else "post_seq",
    )(x, oa, ob, oc, gate1, shift2, scale2, gate2, gpm, gpf, gqf, wo, wu, cw, cst, wd)


def _layer_params(l, gdn_A_log, gdn_dt_bias, fox_f_bias, chunk_w_s, chunk_b_s, dec_len):
    z = lambda n: jnp.zeros((n,), F32)
    smb = jnp.concatenate([z(H_A), gdn_dt_bias[l], fox_f_bias[l], z(LANES - 2 * H_A - H_C)]).reshape(1, LANES)
    alog = jnp.concatenate([z(H_A), gdn_A_log[l], z(LANES - 2 * H_A)]).reshape(1, LANES)
    ws = chunk_w_s[l]
    bs = chunk_b_s[l]
    ws_seq = ws.reshape(G_B * CHUNK, CHUNK)
    bsb_seq = jnp.repeat(bs.T, HEAD_DIM, axis=1)
    ws_dec = jnp.repeat(jnp.transpose(ws[:, :dec_len, :dec_len], (2, 1, 0)), HEAD_DIM, axis=2)
    bsb_dec = bsb_seq[:dec_len]
    return smb, alog, ws_seq, bsb_seq, ws_dec, bsb_dec


def kernel(x_prompt, x_sample, state_gdn_conv, state_gdn_S, cache_fox_k, cache_fox_v, cache_fox_logf,
           state_ffn_conv, page_table, c_prompt, c_sample, w_ada, b_ada, g_pre_mix, g_post_mix, g_pre_ffn,
           g_post_ffn, w_in, w_out, gdn_conv_w, gdn_A_log, gdn_dt_bias, gdn_norm_g, chunk_ln_g, chunk_ln_b,
           chunk_w_s, chunk_b_s, fox_f_bias, w_up, ffn_conv_w, w_down):
    depth = w_in.shape[0]
    bp, lp, _ = x_prompt.shape
    bs, ls, _ = x_sample.shape
    n_pool = cache_fox_k.shape[1]

    cache_kt = jnp.transpose(cache_fox_k, (0, 1, 3, 4, 2)).reshape(depth, n_pool, W_C, PAGE)
    cache_vt = jnp.transpose(cache_fox_v, (0, 1, 3, 4, 2)).reshape(depth, n_pool, W_C, PAGE)
    cache_lft = jnp.transpose(cache_fox_logf, (0, 3, 1, 2))

    c_all = jnp.concatenate([c_prompt, c_sample], axis=0)
    pad = (-c_all.shape[0]) % SUBLANES
    c_all = jnp.pad(c_all, ((0, pad), (0, 0)))

    w_fused = _w_in_prep(w_in)
    pages = tuple(jnp.zeros((depth, bp, lp // PAGE, W_C, PAGE), F32) for _ in range(2))

    vec = lambda a: a.reshape(1, -1)
    outs = {k: [] for k in ("p_conv", "p_S", "p_lf", "p_ffn",
                            "s_conv", "s_S", "s_k", "s_v", "s_lf", "s_cv", "s_ffn")}
    xp, xs = x_prompt, x_sample
    mod_all = _ada(c_all, w_ada, b_ada)
    wo = w_out.astype(BF16)
    wu = w_up.astype(BF16)
    wd = w_down.astype(BF16)
    for l in range(depth):
        mod_p = mod_all[l, :bp].reshape(bp, 6, 1, D_MODEL)
        mod_s = mod_all[l, bp:bp + bs].reshape(bs, 6, 1, D_MODEL)
        smb, alog, ws_seq, bsb_seq, ws_dec, bsb_dec = _layer_params(
            l, gdn_A_log, gdn_dt_bias, fox_f_bias, chunk_w_s, chunk_b_s, ls)
        ng = jnp.tile(gdn_norm_g[l], H_A).reshape(1, W_A)
        common_in = (vec(g_pre_mix[l]), w_fused, gdn_conv_w[l])
        common_b = (smb, alog, vec(chunk_ln_g[l]), vec(chunk_ln_b[l]))
        post_w = (vec(g_post_mix[l]), vec(g_pre_ffn[l]), vec(g_post_ffn[l]), wo, wu, ffn_conv_w[l])

        (qkva, gate, small, ob, tail, pk, pv, qbf, kbf, vbf, dcum, small_t) = _in_proj(
            False, l, xp, mod_p[:, 0], mod_p[:, 1], *common_in,
            jnp.zeros((bp, GDN_CONV - 1, 3 * W_A), F32), *common_b, ws_seq, bsb_seq, pages=pages)
        pages = (pk, pv)
        oa, s_new = _gdn_seq(qkva, small, gate, ng, jnp.zeros((bp, H_A, HEAD_DIM, HEAD_DIM), F32))
        oc = _fox_seq(qbf, kbf, vbf, dcum)
        xp, ftail = _post(False, l, xp, oa, ob, oc, mod_p[:, 2], mod_p[:, 3], mod_p[:, 4], mod_p[:, 5],
                          *post_w, jnp.zeros((bp, FFN_CONV - 1, 2 * D_FF), F32), wd)
        outs["p_conv"].append(tail[:, SUBLANES - (GDN_CONV - 1):])
        outs["p_S"].append(s_new)
        lf_hm = small_t[:, SM_LF:SM_LF + H_C, :].reshape(bp, H_C, lp // PAGE, PAGE)
        outs["p_lf"].append(jnp.transpose(lf_hm, (0, 2, 3, 1)))
        outs["p_ffn"].append(ftail[:, SUBLANES - (FFN_CONV - 1):])

        (qkva, gate, small, ob, tail, kc, vc, vb, qd) = _in_proj(
            True, l, xs, mod_s[:, 0], mod_s[:, 1], *common_in, state_gdn_conv[l], *common_b, ws_dec, bsb_dec)
        to_lanes = lambda a: jnp.transpose(a, (1, 2, 0))
        oa_t, s_new_t = _gdn_dec(to_lanes(qkva), to_lanes(small), to_lanes(gate),
                                 jnp.broadcast_to(ng.reshape(W_A, 1), (W_A, bs)),
                                 jnp.transpose(state_gdn_S[l], (1, 2, 3, 0)))
        oa = jnp.transpose(oa_t, (2, 0, 1))
        s_new = jnp.transpose(s_new_t, (3, 0, 1, 2))
        oc = _fox_dec(l, page_table, qd, kc, vc, small, cache_kt, cache_vt, cache_lft)
        xs, ftail = _post(True, l, xs, oa, ob, oc, mod_s[:, 2], mod_s[:, 3], mod_s[:, 4], mod_s[:, 5],
                          *post_w, state_ffn_conv[l], wd)
        outs["s_conv"].append(tail[:, SUBLANES - (GDN_CONV - 1):])
        outs["s_S"].append(s_new)
        outs["s_k"].append(kc.reshape(bs, ls, H_C, HEAD_DIM))
        outs["s_v"].append(vc.reshape(bs, ls, H_C, HEAD_DIM))
        outs["s_lf"].append(small[:, :, SM_LF:SM_LF + H_C])
        outs["s_cv"].append(vb)
        outs["s_ffn"].append(ftail[:, SUBLANES - (FFN_CONV - 1):])

    st = lambda k: jnp.stack(outs[k])
    from_pages = lambda p: jnp.transpose(
        p.reshape(depth, bp, lp // PAGE, H_C, HEAD_DIM, PAGE), (0, 1, 2, 5, 3, 4))
    return (xp, xs, st("p_conv"), st("p_S"), from_pages(pages[0]), from_pages(pages[1]), st("p_lf"), st("p_ffn"),
            st("s_conv"), st("s_S"), st("s_k"), st("s_v"), st("s_lf"), st("s_cv"), st("s_ffn"))
```

```python
import functools

import jax
import jax.numpy as jnp
from jax import lax
from jax.experimental import pallas as pl
from jax.experimental.pallas import tpu as pltpu

F32 = jnp.float32
BF16 = jnp.bfloat16

D_MODEL = 1024
HEAD_DIM = 64
H_A = 6
G_B = 4
H_C = 6
W_A = H_A * HEAD_DIM
W_B = G_B * HEAD_DIM
W_C = H_C * HEAD_DIM
GDN_CONV = 4
GDN_CHUNK = 64
CHUNK = 128
D_FF = 2816
FFN_CONV = 3
EPS = 1e-6
NEG_INF = -1e30
PAGE = 128
LOG2E = 1.4426950408889634

LANES = 128
SUBLANES = 8
VMEM_LIMIT = 56 * 1024 * 1024

QA_OFF = 0
GA_OFF = QA_OFF + 3 * W_A
UV_OFF = GA_OFF + W_A
QC_OFF = UV_OFF + 2 * W_B
SM_OFF = QC_OFF + 3 * W_C
N_FUSED = SM_OFF + LANES
SM_BETA = 0
SM_G = H_A
SM_LF = 2 * H_A

FF_CHUNK = D_FF // 2


def _silu(x):
    return x * jax.nn.sigmoid(x)


def _softplus(x):
    return jnp.maximum(x, 0.0) + jnp.log1p(jnp.exp(-jnp.abs(x)))


def _dot(a, b):
    return jnp.dot(a, b, preferred_element_type=F32)


def _split3(x):
    t1 = x.astype(BF16)
    r1 = x - t1.astype(F32)
    t2 = r1.astype(BF16)
    t3 = (r1 - t2.astype(F32)).astype(BF16)
    return t1, t2, t3


def _sel_dot(sel, x):
    s = sel.astype(BF16)
    t1, t2, t3 = _split3(x)
    return _dot(s, t1) + (_dot(s, t2) + _dot(s, t3))


def _dot_sel(x, sel):
    s = sel.astype(BF16)
    t1, t2, t3 = _split3(x)
    return _dot(t1, s) + (_dot(t2, s) + _dot(t3, s))


def _dot_nt(a, b):
    return lax.dot_general(a, b, (((1,), (1,)), ((), ())), preferred_element_type=F32)


def _dot_tn(a, b):
    return lax.dot_general(a, b, (((0,), (0,)), ((), ())), preferred_element_type=F32)


def _iota2(shape, dim):
    return lax.broadcasted_iota(jnp.int32, shape, dim)


def _params(sem):
    return pltpu.CompilerParams(dimension_semantics=sem, vmem_limit_bytes=VMEM_LIMIT)


def _const_spec(shape):
    nd = len(shape)
    return pl.BlockSpec(shape, lambda *_: (0,) * nd, pipeline_mode=pl.Buffered(1))


def _ada_kernel(c_ref, w_ref, b_ref, o_ref):
    c = _silu(c_ref[...]).astype(BF16)
    o_ref[...] = _dot(c, w_ref[...].astype(BF16)) + b_ref[...]


def _ada(c, w, b):
    m = c.shape[0]
    depth, _, n = w.shape
    tn = 1024
    return pl.pallas_call(
        _ada_kernel,
        grid=(depth, n // tn),
        in_specs=[pl.BlockSpec((m, D_MODEL), lambda l, j: (0, 0)),
                  pl.BlockSpec((None, D_MODEL, tn), lambda l, j: (l, 0, j)),
                  pl.BlockSpec((None, 1, tn), lambda l, j: (l, 0, j))],
        out_specs=pl.BlockSpec((None, m, tn), lambda l, j: (l, 0, j)),
        out_shape=jax.ShapeDtypeStruct((depth, m, n), F32),
        compiler_params=_params(("arbitrary", "arbitrary")),
        name="ada_mod",
    )(c, w, b.reshape(depth, 1, n))


def _w_in_prep_kernel(depth, a_ref, o_ref):
    l = pl.program_id(0)
    j = pl.program_id(1)
    nkt = D_MODEL // LANES
    rpn = nkt * depth
    n_regular = SM_OFF // LANES
    n_first = GA_OFF // LANES
    n_in = SM_OFF + 2 * H_A + H_C

    @pl.when(j < n_regular)
    def _():
        n0 = j * LANES + jnp.where(j >= n_first, 2 * H_A, 0)
        for kt in range(nkt):
            x = a_ref[pl.ds(n0 * rpn + kt * depth + l, LANES, stride=rpn), :]
            o_ref[0, kt * LANES:(kt + 1) * LANES, :] = x.T.astype(BF16)

    @pl.when(j == n_regular)
    def _():
        r1, r2 = 2 * SUBLANES, SUBLANES
        n2 = n_in - r2
        r = _iota2((r1 + r2, LANES), 0)
        m = _iota2((r1 + r2, LANES), 1)
        pick = ((m < 2 * H_A) & (r == m)) | (
            (m >= 2 * H_A) & (m < 2 * H_A + H_C) & (r == m - 2 * H_A + r1 + (r2 - H_C)))
        sel = pick.astype(BF16)
        for kt in range(nkt):
            x1 = a_ref[pl.ds(GA_OFF * rpn + kt * depth + l, r1, stride=rpn), :]
            x2 = a_ref[pl.ds(n2 * rpn + kt * depth + l, r2, stride=rpn), :]
            xc = jnp.concatenate([x1, x2], axis=0).astype(BF16)
            o_ref[0, kt * LANES:(kt + 1) * LANES, :] = _dot_tn(xc, sel).astype(BF16)


def _w_in_prep(w_in):
    depth, d, n_in = w_in.shape
    nkt = d // LANES
    view = w_in.reshape(depth, nkt, LANES, n_in).transpose(3, 1, 0, 2).reshape(n_in * nkt * depth, LANES)
    return pl.pallas_call(
        functools.partial(_w_in_prep_kernel, depth),
        grid=(depth, N_FUSED // LANES),
        in_specs=[pl.BlockSpec(view.shape, lambda l, j: (0, 0), pipeline_mode=pl.Buffered(1))],
        out_specs=pl.BlockSpec((1, d, LANES), lambda l, j: (l, 0, j)),
        out_shape=jax.ShapeDtypeStruct((depth, d, N_FUSED), BF16),
        compiler_params=_params(("arbitrary", "arbitrary")),
        name="w_in_prep",
    )(view)


def _in_kernel(dec, bb, lt, *refs):
    (x_ref, shift_ref, scale_ref, gpre_ref, w_ref, cw_ref, cst_ref, smb_ref, alog_ref,
     lng_ref, lnb_ref, ws_ref, bsb_ref) = refs[:13]
    if dec:
        (qkva_ref, gate_ref, small_ref, ob_ref, tail_ref, kc_ref, vc_ref, vb_ref, q_ref,
         ext_ref) = refs[13:]
    else:
        (_, _, qkva_ref, gate_ref, small_ref, ob_ref, tail_ref, pk_ref, pv_ref, qbf_ref, kbf_ref,
         vbf_ref, dcum_ref, smt_ref, ext_ref, dcar_ref) = refs[13:]
    l = pl.program_id(1)
    m = bb * lt

    x = x_ref[...]
    ms = jnp.mean(x * x, axis=-1, keepdims=True)
    h = x * lax.rsqrt(ms + EPS) * gpre_ref[...]
    h = h * (1.0 + scale_ref[...]) + shift_ref[...]
    h2 = h.reshape(m, D_MODEL).astype(BF16)

    wa = 3 * W_A
    za = _dot(h2, w_ref[:, QA_OFF:QA_OFF + wa]).reshape(bb, lt, wa)

    @pl.when(l == 0)
    def _():
        ext_ref[:, SUBLANES - (GDN_CONV - 1):SUBLANES, :] = cst_ref[...]

    @pl.when(l > 0)
    def _():
        ext_ref[:, 0:SUBLANES, :] = ext_ref[:, lt:lt + SUBLANES, :]

    ext_ref[:, SUBLANES:, :] = za
    y = za * cw_ref[GDN_CONV - 1:GDN_CONV, :]
    for i in range(GDN_CONV - 1):
        o = SUBLANES - (GDN_CONV - 1) + i
        y = y + ext_ref[:, o:o + lt, :] * cw_ref[i:i + 1, :]
    qkva_ref[...] = _silu(y)
    tail_ref[...] = ext_ref[:, lt + SUBLANES - (GDN_CONV - 1):lt + SUBLANES, :]

    gate_ref[...] = _silu(_dot(h2, w_ref[:, GA_OFF:GA_OFF + W_A])).reshape(bb, lt, W_A)

    zs = _dot(h2, w_ref[:, SM_OFF:SM_OFF + LANES]) + smb_ref[...]
    lane = _iota2((1, LANES), 1)
    beta = jax.nn.sigmoid(zs)
    gval = -jnp.exp(alog_ref[...]) * _softplus(zs)
    lf = -_softplus(-zs)
    small = jnp.where(lane < SM_G, beta,
                      jnp.where(lane < SM_LF, gval, jnp.where(lane < SM_LF + H_C, lf, 0.0)))
    small_ref[...] = small.reshape(bb, lt, LANES)

    if not dec:
        tri = (_iota2((CHUNK, CHUNK), 0) >= _iota2((CHUNK, CHUNK), 1)).astype(F32)

        @pl.when(l == 0)
        def _():
            dcar_ref[...] = jnp.zeros_like(dcar_ref)

        run = dcar_ref[0:1, :]
        parts = []
        for c in range(lt // CHUNK):
            parts.append(_sel_dot(tri, small[c * CHUNK:(c + 1) * CHUNK]) + run)
            run = parts[-1][CHUNK - 1:CHUNK, :]
        dc = jnp.concatenate(parts, axis=0)
        dcum_ref[...] = dc.reshape(bb, lt, LANES)
        dcar_ref[0:1, :] = dc[lt - 1:lt, :]
        smt_ref[0] = small.T

    uv = jax.nn.gelu(_dot(h2, w_ref[:, UV_OFF:UV_OFF + 2 * W_B]))
    u = uv[:, :W_B]
    v = uv[:, W_B:]
    mu = jnp.mean(v, axis=-1, keepdims=True)
    vc0 = v - mu
    var = jnp.mean(vc0 * vc0, axis=-1, keepdims=True)
    vb = vc0 * lax.rsqrt(var + EPS) * lng_ref[...] + lnb_ref[...]
    if dec:
        vb3 = vb.reshape(bb, lt, W_B)
        vb_ref[...] = vb3
        trow = _iota2((lt, W_B), 0)
        mix = jnp.zeros((bb, lt, W_B), F32)
        for s in range(lt):
            coef = jnp.where(trow >= s, ws_ref[s], 0.0)
            mix = mix + coef[None] * vb3[:, s:s + 1, :]
        ob_ref[...] = u.reshape(bb, lt, W_B) * (mix + bsb_ref[...][None])
    else:
        r_t = _iota2((G_B * CHUNK, CHUNK), 0) % CHUNK
        r_s = _iota2((G_B * CHUNK, CHUNK), 1)
        wst = jnp.where(r_s <= r_t, ws_ref[...], 0.0).astype(BF16)
        grp = _iota2((1, W_B), 1) // HEAD_DIM
        for c in range(lt // CHUNK):
            rows = slice(c * CHUNK, (c + 1) * CHUNK)
            r = _dot(wst, vb[rows].astype(BF16))
            mix = jnp.zeros((CHUNK, W_B), F32)
            for g in range(G_B):
                mix = jnp.where(grp == g, r[g * CHUNK:(g + 1) * CHUNK], mix)
            ob_ref[0, rows, :] = u[rows] * (mix + bsb_ref[...])

    zc = _dot(h2, w_ref[:, QC_OFF:QC_OFF + 3 * W_C])
    qc = zc[:, 0:W_C]
    kc = zc[:, W_C:2 * W_C]
    vc = zc[:, 2 * W_C:3 * W_C]
    if dec:
        kc_ref[...] = kc.reshape(bb, lt, W_C)
        vc_ref[...] = vc.reshape(bb, lt, W_C)
        q_ref[...] = qc.reshape(bb, lt, W_C)
    else:
        for c in range(lt // PAGE):
            rows = slice(c * PAGE, (c + 1) * PAGE)
            pk_ref[0, c] = kc[rows].T
            pv_ref[0, c] = vc[rows].T
        hd = HEAD_DIM
        d2 = dc * LOG2E
        for h in range(H_C):
            pair = slice((h // 2) * LANES, (h // 2 + 1) * LANES)
            at0 = (lambda a: a) if h % 2 == 0 else (lambda a: pltpu.roll(a, hd, axis=1))
            dh = d2[:, SM_LF + h:SM_LF + h + 1]
            t1 = dh.astype(BF16).astype(F32)
            t2 = (dh - t1).astype(BF16).astype(F32)
            t3 = dh - t1 - t2
            qa = jnp.where(lane < hd, at0(qc[:, pair]) * (LOG2E * hd ** -0.5), jnp.where(lane < hd + 3, 1.0, 0.0))
            ka = jnp.where(lane < hd, at0(kc[:, pair]),
                           jnp.where(lane == hd, -t1, jnp.where(lane == hd + 1, -t2,
                                                                jnp.where(lane == hd + 2, -t3, 0.0))))
            va = jnp.where(lane < hd, at0(vc[:, pair]), jnp.where(lane == hd, 1.0, 0.0))
            grp = slice(h * LANES, (h + 1) * LANES)
            qbf_ref[0, :, grp] = qa.astype(BF16)
            kbf_ref[0, :, grp] = ka.astype(BF16)
            vbf_ref[0, :, grp] = va.astype(BF16)


def _in_proj(dec, layer, x, shift, scale, gpre, w_fused, conv_w, conv_state, smb, alog, lng, lnb, ws, bsb,
             pages=()):
    b, L, _ = x.shape
    if dec:
        bb, lt = min(32, b), L
    else:
        bb, lt = 1, min(512, L)
    grid = (b // bb, L // lt)
    tok = lambda w: pl.BlockSpec((bb, lt, w), lambda i, j: (i, j, 0))
    per_b = lambda r, w: pl.BlockSpec((bb, r, w), lambda i, j: (i, 0, 0))
    in_specs = [tok(D_MODEL), per_b(1, D_MODEL), per_b(1, D_MODEL), _const_spec((1, D_MODEL)),
                pl.BlockSpec((None, D_MODEL, N_FUSED), lambda i, j: (layer, 0, 0), pipeline_mode=pl.Buffered(1)),
                _const_spec((GDN_CONV, 3 * W_A)),
                per_b(GDN_CONV - 1, 3 * W_A), _const_spec((1, LANES)), _const_spec((1, LANES)),
                _const_spec((1, W_B)), _const_spec((1, W_B)), _const_spec(ws.shape), _const_spec(bsb.shape)]
    sds = lambda w, dt=F32: jax.ShapeDtypeStruct((b, L, w), dt)
    out_shape = [sds(3 * W_A), sds(W_A), sds(LANES), sds(W_B),
                 jax.ShapeDtypeStruct((b, GDN_CONV - 1, 3 * W_A), F32)]
    out_specs = [tok(3 * W_A), tok(W_A), tok(LANES), tok(W_B), per_b(GDN_CONV - 1, 3 * W_A)]
    scratch = [pltpu.VMEM((bb, lt + SUBLANES, 3 * W_A), F32)]
    aliases = {}
    if dec:
        out_shape += [sds(W_C), sds(W_C), sds(W_B), sds(W_C)]
        out_specs += [tok(W_C), tok(W_C), tok(W_B), tok(W_C)]
    else:
        n_in = len(in_specs)
        in_specs += [pl.BlockSpec(memory_space=pl.ANY)] * 2
        aliases = {n_in: len(out_shape), n_in + 1: len(out_shape) + 1}
        page_spec = pl.BlockSpec((None, 1, lt // PAGE, W_C, PAGE), lambda i, j: (layer, i, j, 0, 0))
        head_major = pl.BlockSpec((1, LANES, lt), lambda i, j: (i, 0, j))
        out_shape += [jax.ShapeDtypeStruct(p.shape, p.dtype) for p in pages]
        out_specs += [page_spec, page_spec]
        wf = H_C * LANES
        out_shape += [sds(wf, BF16), sds(wf, BF16), sds(wf, BF16), sds(LANES),
                      jax.ShapeDtypeStruct((b, LANES, L), F32)]
        out_specs += [tok(wf), tok(wf), tok(wf), tok(LANES), head_major]
        scratch += [pltpu.VMEM((SUBLANES, LANES), F32)]
    return pl.pallas_call(
        functools.partial(_in_kernel, dec, bb, lt),
        grid=grid, in_specs=in_specs, out_specs=out_specs, out_shape=out_shape,
        scratch_shapes=scratch, input_output_aliases=aliases,
        compiler_params=_params(("arbitrary", "arbitrary")),
        name="in_proj_dec" if dec else "in_proj_seq",
    )(x, shift, scale, gpre, w_fused, conv_w, conv_state, smb, alog, lng, lnb, ws, bsb, *pages)


GDN_GROUP = 256
GDN_GROUPS_PER_STEP = 2
N_PAIR = H_A // 2


def _tri_inverse_m1(lms, blk):
    bf = lambda xs: [x.astype(BF16) for x in xs]
    mm_ = lambda xs, ys: [_dot(x, y) for x, y in zip(xs, ys)]
    dm = [jnp.where(blk, lm, 0.0) for lm in lms]
    nm = [lm - d for lm, d in zip(lms, dm)]
    dmb = bf(dm)
    d2 = mm_(dmb, dmb)
    d2b = bf(d2)
    d4 = mm_(d2b, d2b)
    d4b = bf(d4)
    d8 = mm_(d4b, d4b)
    a1 = [x2 - x1 - c for x2, x1, c in zip(d2, dm, mm_(dmb, d2b))]
    a2 = [x4 + x8 + c for x4, x8, c in zip(d4, d8, mm_(d4b, bf(d8)))]
    et = [x + y + c for x, y, c in zip(a1, a2, mm_(bf(a1), bf(a2)))]
    etb = bf(et)
    mm = [n + c for n, c in zip(nm, mm_(etb, bf(nm)))]
    mmb = bf(mm)
    m2 = mm_(mmb, mmb)
    a3 = [x2 - x1 - c for x2, x1, c in zip(m2, mm, mm_(mmb, bf(m2)))]
    return [x + e + c for x, e, c in zip(a3, et, mm_(bf(a3), etb))]


def _pair_cols(lo, a, c0, c1):
    return jnp.where(lo, a[:, c0:c0 + 1], a[:, c1:c1 + 1])


def _pair_rsqrt_norm(lo, x, scale):
    x2 = x * x
    s_lo = jnp.sum(jnp.where(lo, x2, 0.0), axis=-1, keepdims=True)
    s_hi = jnp.sum(jnp.where(lo, 0.0, x2), axis=-1, keepdims=True)
    return jnp.where(lo, lax.rsqrt(s_lo * scale + EPS), lax.rsqrt(s_hi * scale + EPS))


def _gdn_pre_kernel(T, G, qkv_ref, small_ref, qe_ref, o0_ref, egl_ref, m_ref, bm_ref):
    C = GDN_CHUNK
    row = _iota2((T, T), 0)
    col = _iota2((T, T), 1)
    same = (row // C) == (col // C)
    bd_tri = same & (row >= col)
    bd_strict = same & (row > col)
    blk = (row // 16) == (col // 16)
    lane = _iota2((1, LANES), 1)
    lo = lane < HEAD_DIM
    lo2 = (_iota2((1, 2 * LANES), 1) % LANES) < HEAD_DIM
    bd2 = (_iota2((LANES, 2 * LANES), 0) // HEAD_DIM) == ((_iota2((LANES, 2 * LANES), 1) % LANES) // HEAD_DIM)

    rhs, rhs_b, lms, attns, qins, kouts = [], [], [], [], [], []
    for g, j in [(g, j) for g in range(G) for j in range(N_PAIR)]:
        tok = slice(g * T, (g + 1) * T)
        if j == 0:
            sm = small_ref[0, tok, :]
            gc_all = _sel_dot(bd_tri, sm)
            gl_all = jnp.concatenate(
                [jnp.broadcast_to(gc_all[(c + 1) * C - 1:(c + 1) * C, :], (C, LANES)) for c in range(T // C)],
                axis=0)
            gc_t = gc_all.T
        sl = slice(j * LANES, (j + 1) * LANES)
        qp = qkv_ref[0, tok, j * LANES:(j + 1) * LANES]
        kp = qkv_ref[0, tok, W_A + j * LANES:W_A + (j + 1) * LANES]
        vp = qkv_ref[0, tok, 2 * W_A + j * LANES:2 * W_A + (j + 1) * LANES]
        qn = qp * _pair_rsqrt_norm(lo, qp, 1.0) * (HEAD_DIM ** -0.5)
        kn = kp * _pair_rsqrt_norm(lo, kp, 1.0)
        h0, h1 = 2 * j, 2 * j + 1
        beta_p = _pair_cols(lo, sm, SM_BETA + h0, SM_BETA + h1)
        gc_p = _pair_cols(lo, gc_all, SM_G + h0, SM_G + h1)
        gl_p = _pair_cols(lo, gl_all, SM_G + h0, SM_G + h1)
        eg_p = jnp.exp(gc_p)
        kb = kn * beta_p
        kn_bf = kn.astype(BF16)
        rhs.append(jnp.concatenate([kb * eg_p, vp * beta_p], axis=1))
        rhs_b.append(rhs[-1].astype(BF16))
        qins.append(qn * eg_p)
        kouts.append((kn * jnp.exp(gl_p - gc_p)).astype(BF16))
        egl_ref[0, tok, sl] = jnp.exp(gl_p)
        zero = jnp.zeros_like(kb)
        stacked = jnp.concatenate([jnp.where(lo, kb, zero), jnp.where(lo, zero, kb),
                                   jnp.where(lo, qn, zero), jnp.where(lo, zero, qn)], axis=0).astype(BF16)
        gq = _dot_nt(stacked, kn_bf)
        for hh in range(2):
            h = 2 * j + hh
            gcol = gc_all[:, SM_G + h:SM_G + h + 1]
            grow = gc_t[SM_G + h:SM_G + h + 1, :]
            decay = jnp.where(bd_tri, jnp.exp(jnp.where(bd_tri, gcol - grow, 0.0)), 0.0)
            lms.append(jnp.where(bd_strict, gq[hh * T:(hh + 1) * T] * decay, 0.0))
            attns.append((gq[(2 + hh) * T:(3 + hh) * T] * decay).astype(BF16))

    tm1 = _tri_inverse_m1(lms, blk)
    tok_chunk = _iota2((T, LANES), 0) // C
    for g, j in [(g, j) for g in range(G) for j in range(N_PAIR)]:
        tok = slice(g * T, (g + 1) * T)
        p = g * N_PAIR + j
        sl = slice(j * LANES, (j + 1) * LANES)
        both = _dot(jnp.concatenate([tm1[2 * p], tm1[2 * p + 1]], axis=0).astype(BF16), rhs_b[p])
        wu = rhs[p] + jnp.where(lo2, both[:T], both[T:])
        wu_b = wu.astype(BF16)
        both = _dot(jnp.concatenate([attns[2 * p], attns[2 * p + 1]], axis=0), wu_b)
        aw = jnp.where(lo2, both[:T], both[T:])
        qe_ref[0, tok, sl] = (qins[p] - aw[:, :LANES]).astype(BF16)
        o0_ref[0, tok, sl] = aw[:, LANES:]
        ko = kouts[p]
        spread = jnp.concatenate([jnp.where(tok_chunk == c, ko, jnp.zeros_like(ko)) for c in range(T // C)],
                                 axis=1)
        mb_all = _dot_tn(spread, wu_b)
        for c in range(T // C):
            mb = jnp.where(bd2, mb_all[c * LANES:(c + 1) * LANES], 0.0)
            m_ref[0, g * (T // C) + c, j] = mb[:, :LANES].astype(BF16)
            bm_ref[0, g * (T // C) + c, j] = mb[:, LANES:]


def _gdn_pre(qkva, small):
    b, L, _ = qkva.shape
    T = min(GDN_GROUP, L)
    G = max(1, min(GDN_GROUPS_PER_STEP, L // T))
    nc = G * T // GDN_CHUNK
    tok = lambda w: pl.BlockSpec((1, G * T, w), lambda i, j: (i, j, 0))
    per_chunk = pl.BlockSpec((1, nc, N_PAIR, LANES, LANES), lambda i, j: (i, j, 0, 0, 0))
    sds = lambda dt: jax.ShapeDtypeStruct((b, L, W_A), dt)
    chunk_sds = lambda dt: jax.ShapeDtypeStruct((b, L // GDN_CHUNK, N_PAIR, LANES, LANES), dt)
    return pl.pallas_call(
        functools.partial(_gdn_pre_kernel, T, G),
        grid=(b, L // (G * T)),
        in_specs=[tok(3 * W_A), tok(LANES)],
        out_specs=[tok(W_A)] * 3 + [per_chunk] * 2,
        out_shape=[sds(BF16), sds(F32), sds(F32), chunk_sds(BF16), chunk_sds(F32)],
        compiler_params=_params(("arbitrary", "arbitrary")),
        name="gdn_pre",
    )(qkva, small)


def _gdn_scan_kernel(nb, lt, qe_ref, o0_ref, egl_ref, m_ref, bm_ref, gate_ref, ng_ref, s0_ref,
                     oa_ref, sout_ref, s_sc):
    l = pl.program_id(0)
    C = GDN_CHUNK

    @pl.when(l == 0)
    def _():
        s_sc[...] = s0_ref[...]

    lane = _iota2((1, LANES), 1)
    lo = lane < HEAD_DIM

    def body(c, carry):
        r0 = pl.multiple_of(c * C, C)
        rows = pl.ds(r0, C)
        for b in range(nb):
            for j in range(N_PAIR):
                sl = slice(j * LANES, (j + 1) * LANES)
                s_old = s_sc[b, j]
                lhs = jnp.concatenate([m_ref[b, c, j], qe_ref[b, rows, sl]], axis=0)
                r = _dot(lhs, s_old.astype(BF16))
                s_sc[b, j] = s_old * egl_ref[b, pl.ds(r0, 1), sl] + (bm_ref[b, c, j] - r[:LANES])
                o = r[LANES:] + o0_ref[b, rows, sl]
                on = o * _pair_rsqrt_norm(lo, o, 1.0 / HEAD_DIM)
                oa_ref[b, rows, sl] = on * ng_ref[:, sl] * gate_ref[b, rows, sl]
        return carry

    lax.fori_loop(0, lt // C, body, 0)
    sout_ref[...] = s_sc[...]


def _gdn_scan(qe, o0, egl, m, bm, gate, ng, s0_bd):
    b, L, _ = qe.shape
    lt = min(512, L)
    tok = pl.BlockSpec((b, lt, W_A), lambda i: (0, i, 0))
    per_chunk = pl.BlockSpec((b, lt // GDN_CHUNK, N_PAIR, LANES, LANES), lambda i: (0, i, 0, 0, 0))
    st = pl.BlockSpec((b, N_PAIR, LANES, LANES), lambda i: (0, 0, 0, 0))
    return pl.pallas_call(
        functools.partial(_gdn_scan_kernel, b, lt),
        grid=(L // lt,),
        in_specs=[tok, tok, tok, per_chunk, per_chunk, tok, _const_spec((1, W_A)), st],
        out_specs=[tok, st],
        out_shape=[jax.ShapeDtypeStruct((b, L, W_A), F32),
                   jax.ShapeDtypeStruct((b, N_PAIR, LANES, LANES), F32)],
        scratch_shapes=[pltpu.VMEM((b, N_PAIR, LANES, LANES), F32)],
        compiler_params=_params(("arbitrary",)),
        name="gdn_scan",
    )(qe, o0, egl, m, bm, gate, ng, s0_bd)


def _to_pair_blockdiag(s):
    b = s.shape[0]
    s = s.reshape(b, N_PAIR, 2, HEAD_DIM, HEAD_DIM)
    z = jnp.zeros_like(s[:, :, 0])
    top = jnp.concatenate([s[:, :, 0], z], axis=-1)
    bot = jnp.concatenate([z, s[:, :, 1]], axis=-1)
    return jnp.concatenate([top, bot], axis=-2)


def _from_pair_blockdiag(sbd):
    b = sbd.shape[0]
    s = jnp.stack([sbd[:, :, :HEAD_DIM, :HEAD_DIM], sbd[:, :, HEAD_DIM:, HEAD_DIM:]], axis=2)
    return s.reshape(b, H_A, HEAD_DIM, HEAD_DIM)


def _gdn_seq(qkva, small, gate, ng, s0):
    qe, o0, egl, m, bm = _gdn_pre(qkva, small)
    oa, sbd = _gdn_scan(qe, o0, egl, m, bm, gate, ng, _to_pair_blockdiag(s0))
    return oa, _from_pair_blockdiag(sbd)


def _gdn_dec_kernel(L, q_ref, k_ref, v_ref, smt_ref, gate_ref, ngb_ref, s0_ref, oa_ref, sout_ref, kq_sc):
    h = pl.program_id(0)
    nb = q_ref.shape[-1]
    for t in range(L):
        q_t = q_ref[t]
        k_t = k_ref[t]
        q_t = q_t * (lax.rsqrt(jnp.sum(q_t * q_t, axis=0, keepdims=True) + EPS) * (HEAD_DIM ** -0.5))
        k_t = k_t * lax.rsqrt(jnp.sum(k_t * k_t, axis=0, keepdims=True) + EPS)
        kq_sc[0] = k_t
        kq_sc[1] = q_t
        beta = smt_ref[t, pl.ds(SM_BETA + h, 1), :]
        a = jnp.exp(smt_ref[t, pl.ds(SM_G + h, 1), :])
        src = s0_ref if t == 0 else sout_ref

        def k_dot_s(kk, acc):
            return acc + src[0, kk] * kq_sc[0, pl.ds(kk, 1), :]

        ks = lax.fori_loop(0, HEAD_DIM, k_dot_s, jnp.zeros((HEAD_DIM, nb), F32), unroll=8)
        delta = beta * (v_ref[t] - a * ks)

        def update(kk, acc):
            s_new = a * src[0, kk] + kq_sc[0, pl.ds(kk, 1), :] * delta
            sout_ref[0, kk] = s_new
            return acc + s_new * kq_sc[1, pl.ds(kk, 1), :]

        o = lax.fori_loop(0, HEAD_DIM, update, jnp.zeros((HEAD_DIM, nb), F32), unroll=8)
        on = o * lax.rsqrt(jnp.mean(o * o, axis=0, keepdims=True) + EPS)
        oa_ref[t] = on * ngb_ref[...] * gate_ref[t]


def _gdn_dec(layer, qkv_t, small_t, gate_t, ngb, s0_t):
    L, _, nb = qkv_t.shape
    nh = W_A // HEAD_DIM
    head = lambda off: pl.BlockSpec((L, HEAD_DIM, nb), lambda h: (0, off + h, 0))
    st = pl.BlockSpec((1, HEAD_DIM, HEAD_DIM, nb), lambda h: (h, 0, 0, 0))
    st_in = pl.BlockSpec((None, 1, HEAD_DIM, HEAD_DIM, nb), lambda h: (layer, h, 0, 0, 0))
    return pl.pallas_call(
        functools.partial(_gdn_dec_kernel, L),
        grid=(H_A,),
        in_specs=[head(0), head(nh), head(2 * nh),
                  pl.BlockSpec((L, LANES, nb), lambda h: (0, 0, 0), pipeline_mode=pl.Buffered(1)),
                  head(0), pl.BlockSpec((HEAD_DIM, nb), lambda h: (h, 0)), st_in],
        out_specs=[head(0), st],
        out_shape=[jax.ShapeDtypeStruct((L, W_A, nb), F32),
                   jax.ShapeDtypeStruct((H_A, HEAD_DIM, HEAD_DIM, nb), F32)],
        scratch_shapes=[pltpu.VMEM((2, HEAD_DIM, nb), F32)],
        compiler_params=_params(("arbitrary",)),
        name="gdn_dec",
    )(qkv_t, qkv_t, qkv_t, small_t, gate_t, ngb, s0_t)


FOX_STRIP = 64
FOX_TQ = 512
FOX_TK = 512


def _fox_seq_kernel(tq, tk, q_ref, k_ref, v_ref, dq_ref, o_ref, m_sc, acc_sc, dqb_sc, s_sc, p_sc, al_sc):
    qi = pl.program_id(1)
    ndiag = tq // tk
    m_sc[...] = jnp.full_like(m_sc, NEG_INF)
    acc_sc[...] = jnp.zeros_like(acc_sc)
    dq = dq_ref[0] * LOG2E
    for h in range(H_C):
        dqb_sc[h] = jnp.broadcast_to(dq[:, SM_LF + h:SM_LF + h + 1], (tq, LANES))

    def step(ki, diag):
        k0 = pl.multiple_of(ki * tk, tk)
        r_lo = 0 if diag is None else diag * tk
        live = slice(r_lo, tq)

        def scores(h):
            grp = slice(h * LANES, (h + 1) * LANES)
            s_sc[h % 2, live, :] = _dot_nt(q_ref[0, live, grp], k_ref[0, pl.ds(k0, tk), grp])

        def accumulate(h):
            grp = slice(h * LANES, (h + 1) * LANES)
            acc_sc[h, live, :] = (al_sc[h % 2, live, :] * acc_sc[h, live, :]
                                  + _dot(p_sc[h % 2, live, :], v_ref[0, pl.ds(k0, tk), grp]))

        scores(0)
        for h in range(H_C):
            buf = h % 2
            if h + 1 < H_C:
                scores(h + 1)
            for r in range(r_lo // FOX_STRIP, tq // FOX_STRIP):
                rows = slice(r * FOX_STRIP, (r + 1) * FOX_STRIP)
                nc = tk if diag is None else min(tk, -(-((r + 1) * FOX_STRIP - r_lo) // LANES) * LANES)
                s = s_sc[buf, rows, 0:nc]
                if diag is not None:
                    keep = _iota2((FOX_STRIP, nc), 1) <= _iota2((FOX_STRIP, nc), 0) + (r * FOX_STRIP - r_lo)
                    s = jnp.where(keep, s, NEG_INF)
                dqb = dqb_sc[h, rows, :]
                m_old = m_sc[h, rows, :]
                m_new = jnp.maximum(m_old, jnp.max(s, axis=-1, keepdims=True) + dqb)
                shift = m_new - dqb
                if nc > LANES:
                    shift = jnp.concatenate([shift] * (nc // LANES), axis=1)
                m_sc[h, rows, :] = m_new
                al_sc[buf, rows, :] = jnp.exp2(m_old - m_new)
                p_sc[buf, rows, 0:nc] = jnp.exp2(s - shift).astype(BF16)
                if nc < tk:
                    p_sc[buf, rows, nc:tk] = jnp.zeros((FOX_STRIP, tk - nc), BF16)
            if h > 0:
                accumulate(h - 1)
        accumulate(H_C - 1)

    def body(ki, carry):
        step(ki, None)
        return carry

    lax.fori_loop(0, ndiag * qi, body, 0)
    for d in range(ndiag):
        step(ndiag * qi + d, d)
    for j in range(H_C // 2):
        halves = []
        for h in (2 * j, 2 * j + 1):
            acc = acc_sc[h]
            halves.append(acc[:, :HEAD_DIM] / acc[:, HEAD_DIM:HEAD_DIM + 1])
        o_ref[0, :, j * LANES:(j + 1) * LANES] = jnp.concatenate(halves, axis=1)


def _fox_seq(q_aug, k_aug, v_aug, dcum):
    b, L, wf = q_aug.shape
    tk = min(FOX_TK, L)
    tq = min(FOX_TQ, L)
    full = pl.BlockSpec((1, L, wf), lambda i, j: (i, 0, 0), pipeline_mode=pl.Buffered(1))
    return pl.pallas_call(
        functools.partial(_fox_seq_kernel, tq, tk),
        grid=(b, L // tq),
        in_specs=[pl.BlockSpec((1, tq, wf), lambda i, j: (i, j, 0)), full, full,
                  pl.BlockSpec((1, tq, LANES), lambda i, j: (i, j, 0))],
        out_specs=pl.BlockSpec((1, tq, W_C), lambda i, j: (i, j, 0)),
        out_shape=jax.ShapeDtypeStruct((b, L, W_C), F32),
        scratch_shapes=[pltpu.VMEM((H_C, tq, LANES), F32), pltpu.VMEM((H_C, tq, LANES), F32),
                        pltpu.VMEM((H_C, tq, LANES), F32),
                        pltpu.VMEM((2, tq, tk), F32), pltpu.VMEM((2, tq, tk), BF16),
                        pltpu.VMEM((2, tq, LANES), F32)],
        compiler_params=_params(("arbitrary", "arbitrary")),
        name="fox_seq",
    )(q_aug, k_aug, v_aug, dcum)


def _fox_dec_kernel(L, n_pages, pt_ref, q_ref, kn_ref, vn_ref, sm_ref, lf_ref, *refs):
    k_refs = refs[0:n_pages]
    v_refs = refs[n_pages:2 * n_pages]
    o_ref = refs[2 * n_pages]
    x_sc = refs[2 * n_pages + 1]
    i = pl.program_id(0)
    R = H_C * L
    q = q_ref[0] * (HEAD_DIM ** -0.5)
    rowh = _iota2((R, W_C), 0) // L
    colh = _iota2((R, W_C), 1) // HEAD_DIM
    qbd = jnp.where(rowh == colh, jnp.concatenate([q] * H_C, axis=0), 0.0).astype(BF16)

    for p in range(n_pages):
        pg = pt_ref[i * n_pages + p]
        for h in range(H_C):
            r = h * n_pages + p
            x_sc[r:r + 1, :] = lf_ref[h, pl.ds(pg, 1), :]
    x = x_sc[...]
    n = H_C * n_pages
    later = (_iota2((PAGE, PAGE), 0) > _iota2((PAGE, PAGE), 1)).astype(F32)
    within = _dot_sel(x, later)
    tot = _dot_sel(x, jnp.ones((PAGE, PAGE), F32))
    ri = _iota2((n, n), 0)
    ci = _iota2((n, n), 1)
    later_pages = ((ci // n_pages == ri // n_pages) & (ci % n_pages > ri % n_pages)).astype(F32)
    rsum = within + _sel_dot(later_pages, tot)

    sm = sm_ref[0]
    tri = (_iota2((L, L), 0) >= _iota2((L, L), 1)).astype(F32)
    cq = _sel_dot(tri, sm)
    cq_t = cq.T
    cq_col = jnp.concatenate([cq[:, SM_LF + h:SM_LF + h + 1] for h in range(H_C)], axis=0)
    cq_row = jnp.concatenate([jnp.broadcast_to(cq_t[SM_LF + h:SM_LF + h + 1, :], (L, L))
                              for h in range(H_C)], axis=0)

    s_pages = []
    for p in range(n_pages):
        sp = _dot(qbd, k_refs[p][...].astype(BF16))
        bias = jnp.concatenate(
            [jnp.broadcast_to(rsum[h * n_pages + p:h * n_pages + p + 1, :], (L, PAGE)) for h in range(H_C)],
            axis=0)
        s_pages.append(sp + bias + cq_col)
    s_new = _dot_nt(qbd, kn_ref[0].astype(BF16)) + cq_col - cq_row
    qpos = _iota2((R, L), 0) % L
    s_new = jnp.where(_iota2((R, L), 1) <= qpos, s_new, NEG_INF)

    mx = jnp.max(s_new, axis=-1, keepdims=True)
    for sp in s_pages:
        mx = jnp.maximum(mx, jnp.max(sp, axis=-1, keepdims=True))
    p_new = jnp.exp(s_new - mx)
    den = jnp.sum(p_new, axis=-1, keepdims=True)
    acc = _dot(p_new.astype(BF16), vn_ref[0].astype(BF16))
    for p in range(n_pages):
        pp = jnp.exp(s_pages[p] - mx)
        den = den + jnp.sum(pp, axis=-1, keepdims=True)
        acc = acc + _dot_nt(pp.astype(BF16), v_refs[p][...].astype(BF16))
    acc = jnp.where(rowh == colh, acc / den, 0.0)
    out = acc[0:L]
    for h in range(1, H_C):
        out = out + acc[h * L:(h + 1) * L]
    o_ref[0] = out


def _fox_dec(layer, page_table, q, kn, vn, small, cache_kt, cache_vt, cache_lft):
    b, L, _ = q.shape
    n_pages = page_table.shape[1]
    n_pool = cache_kt.shape[1]
    tok = lambda w: pl.BlockSpec((1, L, w), lambda i, pt: (i, 0, 0))

    def page_spec(p):
        return pl.BlockSpec((None, None, W_C, PAGE), lambda i, pt: (layer, pt[i * n_pages + p], 0, 0))

    in_specs = [tok(W_C), tok(W_C), tok(W_C), tok(LANES),
                pl.BlockSpec((None, H_C, n_pool, PAGE), lambda i, pt: (layer, 0, 0, 0),
                             pipeline_mode=pl.Buffered(1))]
    in_specs += [page_spec(p) for p in range(n_pages)]
    in_specs += [page_spec(p) for p in range(n_pages)]
    grid_spec = pltpu.PrefetchScalarGridSpec(
        num_scalar_prefetch=1, grid=(b,), in_specs=in_specs, out_specs=tok(W_C),
        scratch_shapes=[pltpu.VMEM((H_C * n_pages, PAGE), F32)])
    return pl.pallas_call(
        functools.partial(_fox_dec_kernel, L, n_pages),
        grid_spec=grid_spec,
        out_shape=jax.ShapeDtypeStruct((b, L, W_C), F32),
        compiler_params=_params(("arbitrary",)),
        name="fox_dec",
    )(page_table.reshape(-1), q, kn, vn, small, cache_lft,
      *([cache_kt] * n_pages), *([cache_vt] * n_pages))


def _post_kernel(bb, lt, x_ref, oa_ref, ob_ref, oc_ref, gate1_ref, shift2_ref, scale2_ref, gate2_ref,
                 gpm_ref, gpf_ref, gqf_ref, wo_ref, wu_ref, cw_ref, cst_ref, wd_ref,
                 y_ref, tail_ref, ext_ref, car_ref):
    l = pl.program_id(1)
    m = bb * lt
    keep = FFN_CONV - 1

    def rms(v, g_ref):
        return v * lax.rsqrt(jnp.mean(v * v, axis=-1, keepdims=True) + EPS) * g_ref[...]

    oa = oa_ref[...].reshape(m, W_A).astype(BF16)
    ob = ob_ref[...].reshape(m, W_B).astype(BF16)
    oc = oc_ref[...].reshape(m, W_C).astype(BF16)
    o = (_dot(oa, wo_ref[0:W_A, :]) + _dot(ob, wo_ref[W_A:W_A + W_B, :])
         + _dot(oc, wo_ref[W_A + W_B:W_A + W_B + W_C, :]))
    x1 = x_ref[...] + gate1_ref[...] * rms(o, gpm_ref).reshape(bb, lt, D_MODEL)

    h = rms(x1, gpf_ref) * (1.0 + scale2_ref[...]) + shift2_ref[...]
    h2 = h.reshape(m, D_MODEL).astype(BF16)

    @pl.when(l == 0)
    def _():
        car_ref[...] = jnp.zeros_like(car_ref)
        car_ref[:, SUBLANES - keep:SUBLANES, :] = cst_ref[...]

    w = FF_CHUNK
    nchunk = D_FF // FF_CHUNK

    def project(slot, c0):
        up = _dot(h2, wu_ref[:, c0:c0 + w]).reshape(bb, lt, w)
        ext_ref[slot, :, 0:SUBLANES, :] = car_ref[:, :, c0:c0 + w]
        ext_ref[slot, :, SUBLANES:, :] = up
        car_ref[:, :, c0:c0 + w] = ext_ref[slot, :, lt:lt + SUBLANES, :]

    def conv(slot, c0):
        y = None
        for i in range(FFN_CONV):
            off = SUBLANES - keep + i
            term = ext_ref[slot, :, off:off + lt, :] * cw_ref[i:i + 1, c0:c0 + w]
            y = term if y is None else y + term
        return y

    project(0, 0)
    project(1, D_FF)
    y = jnp.zeros((m, D_MODEL), F32)
    for j in range(nchunk):
        if j + 1 < nchunk:
            project(2 * (j + 1), (j + 1) * w)
            project(2 * (j + 1) + 1, D_FF + (j + 1) * w)
        a = conv(2 * j, j * w)
        b = conv(2 * j + 1, D_FF + j * w)
        g = (jax.nn.gelu(a) * b).reshape(m, w).astype(BF16)
        y = y + _dot(g, wd_ref[j * w:(j + 1) * w, :])
    tail_ref[...] = car_ref[:, SUBLANES - keep:SUBLANES, :]
    y_ref[...] = x1 + gate2_ref[...] * rms(y, gqf_ref).reshape(bb, lt, D_MODEL)


def _post(dec, layer, x, oa, ob, oc, gate1, shift2, scale2, gate2, gpm, gpf, gqf, wo, wu, cw, cst, wd):
    b, L, _ = x.shape
    if dec:
        bb, lt = min(16, b), L
    else:
        bb, lt = 1, min(512, L)
    tok = lambda w: pl.BlockSpec((bb, lt, w), lambda i, j: (i, j, 0))
    per_b = lambda r, w: pl.BlockSpec((bb, r, w), lambda i, j: (i, 0, 0))
    vec = _const_spec((1, D_MODEL))
    weight = lambda r, c: pl.BlockSpec((None, r, c), lambda i, j: (layer, 0, 0), pipeline_mode=pl.Buffered(1))
    in_specs = [tok(D_MODEL), tok(W_A), tok(W_B), tok(W_C), per_b(1, D_MODEL), per_b(1, D_MODEL),
                per_b(1, D_MODEL), per_b(1, D_MODEL), vec, vec, vec,
                weight(W_A + W_B + W_C, D_MODEL), weight(D_MODEL, 2 * D_FF),
                _const_spec((FFN_CONV, 2 * D_FF)), per_b(FFN_CONV - 1, 2 * D_FF),
                weight(D_FF, D_MODEL)]
    return pl.pallas_call(
        functools.partial(_post_kernel, bb, lt),
        grid=(b // bb, L // lt),
        in_specs=in_specs,
        out_specs=[tok(D_MODEL), per_b(FFN_CONV - 1, 2 * D_FF)],
        out_shape=[jax.ShapeDtypeStruct((b, L, D_MODEL), F32),
                   jax.ShapeDtypeStruct((b, FFN_CONV - 1, 2 * D_FF), F32)],
        scratch_shapes=[pltpu.VMEM((2 * D_FF // FF_CHUNK, bb, lt + SUBLANES, FF_CHUNK), F32),
                        pltpu.VMEM((bb, SUBLANES, 2 * D_FF), F32)],
        compiler_params=_params(("arbitrary", "arbitrary")),
        name="post_dec" if dec else "post_seq",
    )(x, oa, ob, oc, gate1, shift2, scale2, gate2, gpm, gpf, gqf, wo, wu, cw, cst, wd)


def _layer_params(l, gdn_A_log, gdn_dt_bias, fox_f_bias, chunk_w_s, chunk_b_s, dec_len):
    z = lambda n: jnp.zeros((n,), F32)
    smb = jnp.concatenate([z(H_A), gdn_dt_bias[l], fox_f_bias[l], z(LANES - 2 * H_A - H_C)]).reshape(1, LANES)
    alog = jnp.concatenate([z(H_A), gdn_A_log[l], z(LANES - 2 * H_A)]).reshape(1, LANES)
    ws = chunk_w_s[l]
    bs = chunk_b_s[l]
    ws_seq = ws.reshape(G_B * CHUNK, CHUNK)
    bsb_seq = jnp.repeat(bs.T, HEAD_DIM, axis=1)
    ws_dec = jnp.repeat(jnp.transpose(ws[:, :dec_len, :dec_len], (2, 1, 0)), HEAD_DIM, axis=2)
    bsb_dec = bsb_seq[:dec_len]
    return smb, alog, ws_seq, bsb_seq, ws_dec, bsb_dec


def kernel(x_prompt, x_sample, state_gdn_conv, state_gdn_S, cache_fox_k, cache_fox_v, cache_fox_logf,
           state_ffn_conv, page_table, c_prompt, c_sample, w_ada, b_ada, g_pre_mix, g_post_mix, g_pre_ffn,
           g_post_ffn, w_in, w_out, gdn_conv_w, gdn_A_log, gdn_dt_bias, gdn_norm_g, chunk_ln_g, chunk_ln_b,
           chunk_w_s, chunk_b_s, fox_f_bias, w_up, ffn_conv_w, w_down):
    depth = w_in.shape[0]
    bp, lp, _ = x_prompt.shape
    bs, ls, _ = x_sample.shape
    n_pool = cache_fox_k.shape[1]

    cache_kt = jnp.transpose(cache_fox_k, (0, 1, 3, 4, 2)).reshape(depth, n_pool, W_C, PAGE)
    cache_vt = jnp.transpose(cache_fox_v, (0, 1, 3, 4, 2)).reshape(depth, n_pool, W_C, PAGE)
    cache_lft = jnp.transpose(cache_fox_logf, (0, 3, 1, 2))
    state_s_lanes = jnp.transpose(state_gdn_S, (0, 2, 3, 4, 1))

    c_all = jnp.concatenate([c_prompt, c_sample], axis=0)
    pad = (-c_all.shape[0]) % SUBLANES
    c_all = jnp.pad(c_all, ((0, pad), (0, 0)))

    w_fused = _w_in_prep(w_in)
    pages = tuple(jnp.zeros((depth, bp, lp // PAGE, W_C, PAGE), F32) for _ in range(2))

    vec = lambda a: a.reshape(1, -1)
    outs = {k: [] for k in ("p_conv", "p_S", "p_lf", "p_ffn",
                            "s_conv", "s_S", "s_k", "s_v", "s_lf", "s_cv", "s_ffn")}
    xp, xs = x_prompt, x_sample
    mod_all = _ada(c_all, w_ada, b_ada)
    wo = w_out.astype(BF16)
    wu = w_up.astype(BF16)
    wd = w_down.astype(BF16)
    for l in range(depth):
        mod_p = mod_all[l, :bp].reshape(bp, 6, 1, D_MODEL)
        mod_s = mod_all[l, bp:bp + bs].reshape(bs, 6, 1, D_MODEL)
        smb, alog, ws_seq, bsb_seq, ws_dec, bsb_dec = _layer_params(
            l, gdn_A_log, gdn_dt_bias, fox_f_bias, chunk_w_s, chunk_b_s, ls)
        ng = jnp.tile(gdn_norm_g[l], H_A).reshape(1, W_A)
        common_in = (vec(g_pre_mix[l]), w_fused, gdn_conv_w[l])
        common_b = (smb, alog, vec(chunk_ln_g[l]), vec(chunk_ln_b[l]))
        post_w = (vec(g_post_mix[l]), vec(g_pre_ffn[l]), vec(g_post_ffn[l]), wo, wu, ffn_conv_w[l])

        (qkva, gate, small, ob, tail, pk, pv, qbf, kbf, vbf, dcum, small_t) = _in_proj(
            False, l, xp, mod_p[:, 0], mod_p[:, 1], *common_in,
            jnp.zeros((bp, GDN_CONV - 1, 3 * W_A), F32), *common_b, ws_seq, bsb_seq, pages=pages)
        pages = (pk, pv)
        oa, s_new = _gdn_seq(qkva, small, gate, ng, jnp.zeros((bp, H_A, HEAD_DIM, HEAD_DIM), F32))
        oc = _fox_seq(qbf, kbf, vbf, dcum)
        xp, ftail = _post(False, l, xp, oa, ob, oc, mod_p[:, 2], mod_p[:, 3], mod_p[:, 4], mod_p[:, 5],
                          *post_w, jnp.zeros((bp, FFN_CONV - 1, 2 * D_FF), F32), wd)
        outs["p_conv"].append(tail)
        outs["p_S"].append(s_new)
        lf_hm = small_t[:, SM_LF:SM_LF + H_C, :].reshape(bp, H_C, lp // PAGE, PAGE)
        outs["p_lf"].append(jnp.transpose(lf_hm, (0, 2, 3, 1)))
        outs["p_ffn"].append(ftail)

        (qkva, gate, small, ob, tail, kc, vc, vb, qd) = _in_proj(
            True, l, xs, mod_s[:, 0], mod_s[:, 1], *common_in, state_gdn_conv[l], *common_b, ws_dec, bsb_dec)
        to_lanes = lambda a: jnp.transpose(a, (1, 2, 0))
        oa_t, s_new_t = _gdn_dec(l, to_lanes(qkva), to_lanes(small), to_lanes(gate),
                                 jnp.broadcast_to(ng.reshape(W_A, 1), (W_A, bs)), state_s_lanes)
        oa = jnp.transpose(oa_t, (2, 0, 1))
        s_new = jnp.transpose(s_new_t, (3, 0, 1, 2))
        oc = _fox_dec(l, page_table, qd, kc, vc, small, cache_kt, cache_vt, cache_lft)
        xs, ftail = _post(True, l, xs, oa, ob, oc, mod_s[:, 2], mod_s[:, 3], mod_s[:, 4], mod_s[:, 5],
                          *post_w, state_ffn_conv[l], wd)
        outs["s_conv"].append(tail)
        outs["s_S"].append(s_new)
        outs["s_k"].append(kc.reshape(bs, ls, H_C, HEAD_DIM))
        outs["s_v"].append(vc.reshape(bs, ls, H_C, HEAD_DIM))
        outs["s_lf"].append(small[:, :, SM_LF:SM_LF + H_C])
        outs["s_cv"].append(vb)
        outs["s_ffn"].append(ftail)

    st = lambda k: jnp.stack(outs[k])
    from_pages = lambda p: jnp.transpose(
        p.reshape(depth, bp, lp // PAGE, H_C, HEAD_DIM, PAGE), (0, 1, 2, 5, 3, 4))
    return (xp, xs, st("p_conv"), st("p_S"), from_pages(pages[0]), from_pages(pages[1]), st("p_lf"), st("p_ffn"),
            st("s_conv"), st("s_S"), st("s_k"), st("s_v"), st("s_lf"), st("s_cv"), st("s_ffn"))
```

```python
import functools

import jax
import jax.numpy as jnp
from jax import lax
from jax.experimental import pallas as pl
from jax.experimental.pallas import tpu as pltpu

F32 = jnp.float32
BF16 = jnp.bfloat16

D_MODEL = 1024
HEAD_DIM = 64
H_A = 6
G_B = 4
H_C = 6
W_A = H_A * HEAD_DIM
W_B = G_B * HEAD_DIM
W_C = H_C * HEAD_DIM
GDN_CONV = 4
GDN_CHUNK = 64
CHUNK = 128
D_FF = 2816
FFN_CONV = 3
EPS = 1e-6
NEG_INF = -1e30
PAGE = 128
LOG2E = 1.4426950408889634

LANES = 128
SUBLANES = 8
VMEM_LIMIT = 56 * 1024 * 1024

QA_OFF = 0
GA_OFF = QA_OFF + 3 * W_A
UV_OFF = GA_OFF + W_A
QC_OFF = UV_OFF + 2 * W_B
SM_OFF = QC_OFF + 3 * W_C
N_FUSED = SM_OFF + LANES
SM_BETA = 0
SM_G = H_A
SM_LF = 2 * H_A

FF_CHUNK = D_FF

SEQ_TILE = 512
DEC_TILE_IN = 32
DEC_TILE_POST = 16
ADA_TILE = 1024
INV_BLOCK = 16


def _silu(x):
    return x * jax.nn.sigmoid(x)


def _softplus(x):
    return jnp.maximum(x, 0.0) + jnp.log1p(jnp.exp(-jnp.abs(x)))


def _dot(a, b):
    return jnp.dot(a, b, preferred_element_type=F32)


def _split3(x):
    t1 = x.astype(BF16)
    r1 = x - t1.astype(F32)
    t2 = r1.astype(BF16)
    t3 = (r1 - t2.astype(F32)).astype(BF16)
    return t1, t2, t3


def _sel_dot(sel, x):
    s = sel.astype(BF16)
    t1, t2, t3 = _split3(x)
    return _dot(s, t1) + (_dot(s, t2) + _dot(s, t3))


def _dot_sel(x, sel):
    s = sel.astype(BF16)
    t1, t2, t3 = _split3(x)
    return _dot(t1, s) + (_dot(t2, s) + _dot(t3, s))


def _dot_nt(a, b):
    return lax.dot_general(a, b, (((1,), (1,)), ((), ())), preferred_element_type=F32)


def _dot_tn(a, b):
    return lax.dot_general(a, b, (((0,), (0,)), ((), ())), preferred_element_type=F32)


def _iota2(shape, dim):
    return lax.broadcasted_iota(jnp.int32, shape, dim)


def _params(sem):
    return pltpu.CompilerParams(dimension_semantics=sem, vmem_limit_bytes=VMEM_LIMIT)


def _const_spec(shape):
    nd = len(shape)
    return pl.BlockSpec(shape, lambda *_: (0,) * nd, pipeline_mode=pl.Buffered(1))


def _ada_kernel(c_ref, w_ref, b_ref, o_ref):
    c = _silu(c_ref[...]).astype(BF16)
    o_ref[...] = _dot(c, w_ref[...].astype(BF16)) + b_ref[...]


def _ada(c, w, b):
    m = c.shape[0]
    depth, _, n = w.shape
    tn = ADA_TILE
    return pl.pallas_call(
        _ada_kernel,
        grid=(depth, n // tn),
        in_specs=[pl.BlockSpec((m, D_MODEL), lambda l, j: (0, 0)),
                  pl.BlockSpec((None, D_MODEL, tn), lambda l, j: (l, 0, j)),
                  pl.BlockSpec((None, 1, tn), lambda l, j: (l, 0, j))],
        out_specs=pl.BlockSpec((None, m, tn), lambda l, j: (l, 0, j)),
        out_shape=jax.ShapeDtypeStruct((depth, m, n), F32),
        compiler_params=_params(("arbitrary", "arbitrary")),
        name="ada_mod",
    )(c, w, b.reshape(depth, 1, n))


def _w_in_prep_kernel(depth, a_ref, o_ref):
    l = pl.program_id(0)
    j = pl.program_id(1)
    nkt = D_MODEL // LANES
    rpn = nkt * depth
    n_regular = SM_OFF // LANES
    n_first = GA_OFF // LANES
    n_in = SM_OFF + 2 * H_A + H_C

    @pl.when(j < n_regular)
    def _():
        n0 = j * LANES + jnp.where(j >= n_first, 2 * H_A, 0)
        for kt in range(nkt):
            x = a_ref[pl.ds(n0 * rpn + kt * depth + l, LANES, stride=rpn), :]
            o_ref[0, kt * LANES:(kt + 1) * LANES, :] = x.T.astype(BF16)

    @pl.when(j == n_regular)
    def _():
        r1, r2 = 2 * SUBLANES, SUBLANES
        n2 = n_in - r2
        r = _iota2((r1 + r2, LANES), 0)
        m = _iota2((r1 + r2, LANES), 1)
        pick = ((m < 2 * H_A) & (r == m)) | (
            (m >= 2 * H_A) & (m < 2 * H_A + H_C) & (r == m - 2 * H_A + r1 + (r2 - H_C)))
        sel = pick.astype(BF16)
        for kt in range(nkt):
            x1 = a_ref[pl.ds(GA_OFF * rpn + kt * depth + l, r1, stride=rpn), :]
            x2 = a_ref[pl.ds(n2 * rpn + kt * depth + l, r2, stride=rpn), :]
            xc = jnp.concatenate([x1, x2], axis=0).astype(BF16)
            o_ref[0, kt * LANES:(kt + 1) * LANES, :] = _dot_tn(xc, sel).astype(BF16)


def _w_in_prep(w_in):
    depth, d, n_in = w_in.shape
    nkt = d // LANES
    view = w_in.reshape(depth, nkt, LANES, n_in).transpose(3, 1, 0, 2).reshape(n_in * nkt * depth, LANES)
    return pl.pallas_call(
        functools.partial(_w_in_prep_kernel, depth),
        grid=(depth, N_FUSED // LANES),
        in_specs=[pl.BlockSpec(view.shape, lambda l, j: (0, 0), pipeline_mode=pl.Buffered(1))],
        out_specs=pl.BlockSpec((1, d, LANES), lambda l, j: (l, 0, j)),
        out_shape=jax.ShapeDtypeStruct((depth, d, N_FUSED), BF16),
        compiler_params=_params(("arbitrary", "arbitrary")),
        name="w_in_prep",
    )(view)


def _in_kernel(dec, bb, lt, *refs):
    (x_ref, shift_ref, scale_ref, gpre_ref, w_ref, cw_ref, cst_ref, smb_ref, alog_ref,
     lng_ref, lnb_ref, ws_ref, bsb_ref) = refs[:13]
    if dec:
        (qkva_ref, gate_ref, small_ref, ob_ref, tail_ref, kc_ref, vc_ref, vb_ref, q_ref,
         ext_ref) = refs[13:]
    else:
        (_, _, qkva_ref, gate_ref, small_ref, ob_ref, tail_ref, pk_ref, pv_ref, qbf_ref, kbf_ref,
         vbf_ref, dcum_ref, smt_ref, ext_ref, dcar_ref) = refs[13:]
    l = pl.program_id(1)
    m = bb * lt

    x = x_ref[...]
    ms = jnp.mean(x * x, axis=-1, keepdims=True)
    h = x * lax.rsqrt(ms + EPS) * gpre_ref[...]
    h = h * (1.0 + scale_ref[...]) + shift_ref[...]
    h2 = h.reshape(m, D_MODEL).astype(BF16)

    wa = 3 * W_A
    za = _dot(h2, w_ref[:, QA_OFF:QA_OFF + wa]).reshape(bb, lt, wa)

    @pl.when(l == 0)
    def _():
        ext_ref[:, SUBLANES - (GDN_CONV - 1):SUBLANES, :] = cst_ref[...]

    @pl.when(l > 0)
    def _():
        ext_ref[:, 0:SUBLANES, :] = ext_ref[:, lt:lt + SUBLANES, :]

    ext_ref[:, SUBLANES:, :] = za
    y = za * cw_ref[GDN_CONV - 1:GDN_CONV, :]
    for i in range(GDN_CONV - 1):
        o = SUBLANES - (GDN_CONV - 1) + i
        y = y + ext_ref[:, o:o + lt, :] * cw_ref[i:i + 1, :]
    qkva_ref[...] = _silu(y)
    tail_ref[...] = ext_ref[:, lt + SUBLANES - (GDN_CONV - 1):lt + SUBLANES, :]

    gate_ref[...] = _silu(_dot(h2, w_ref[:, GA_OFF:GA_OFF + W_A])).reshape(bb, lt, W_A)

    zs = _dot(h2, w_ref[:, SM_OFF:SM_OFF + LANES]) + smb_ref[...]
    lane = _iota2((1, LANES), 1)
    beta = jax.nn.sigmoid(zs)
    gval = -jnp.exp(alog_ref[...]) * _softplus(zs)
    lf = -_softplus(-zs)
    small = jnp.where(lane < SM_G, beta,
                      jnp.where(lane < SM_LF, gval, jnp.where(lane < SM_LF + H_C, lf, 0.0)))
    small_ref[...] = small.reshape(bb, lt, LANES)

    if not dec:
        tri = (_iota2((CHUNK, CHUNK), 0) >= _iota2((CHUNK, CHUNK), 1)).astype(F32)

        @pl.when(l == 0)
        def _():
            dcar_ref[...] = jnp.zeros_like(dcar_ref)

        run = dcar_ref[0:1, :]
        parts = []
        for c in range(lt // CHUNK):
            parts.append(_sel_dot(tri, small[c * CHUNK:(c + 1) * CHUNK]) + run)
            run = parts[-1][CHUNK - 1:CHUNK, :]
        dc = jnp.concatenate(parts, axis=0)
        dcum_ref[...] = dc.reshape(bb, lt, LANES)
        dcar_ref[0:1, :] = dc[lt - 1:lt, :]
        smt_ref[0] = small.T

    uv = jax.nn.gelu(_dot(h2, w_ref[:, UV_OFF:UV_OFF + 2 * W_B]))
    u = uv[:, :W_B]
    v = uv[:, W_B:]
    mu = jnp.mean(v, axis=-1, keepdims=True)
    vc0 = v - mu
    var = jnp.mean(vc0 * vc0, axis=-1, keepdims=True)
    vb = vc0 * lax.rsqrt(var + EPS) * lng_ref[...] + lnb_ref[...]
    if dec:
        vb3 = vb.reshape(bb, lt, W_B)
        vb_ref[...] = vb3
        trow = _iota2((lt, W_B), 0)
        mix = jnp.zeros((bb, lt, W_B), F32)
        for s in range(lt):
            coef = jnp.where(trow >= s, ws_ref[s], 0.0)
            mix = mix + coef[None] * vb3[:, s:s + 1, :]
        ob_ref[...] = u.reshape(bb, lt, W_B) * (mix + bsb_ref[...][None])
    else:
        r_t = _iota2((G_B * CHUNK, CHUNK), 0) % CHUNK
        r_s = _iota2((G_B * CHUNK, CHUNK), 1)
        wst = jnp.where(r_s <= r_t, ws_ref[...], 0.0).astype(BF16)
        grp = _iota2((1, W_B), 1) // HEAD_DIM
        for c in range(lt // CHUNK):
            rows = slice(c * CHUNK, (c + 1) * CHUNK)
            r = _dot(wst, vb[rows].astype(BF16))
            mix = jnp.zeros((CHUNK, W_B), F32)
            for g in range(G_B):
                mix = jnp.where(grp == g, r[g * CHUNK:(g + 1) * CHUNK], mix)
            ob_ref[0, rows, :] = u[rows] * (mix + bsb_ref[...])

    zc = _dot(h2, w_ref[:, QC_OFF:QC_OFF + 3 * W_C])
    qc = zc[:, 0:W_C]
    kc = zc[:, W_C:2 * W_C]
    vc = zc[:, 2 * W_C:3 * W_C]
    if dec:
        kc_ref[...] = kc.reshape(bb, lt, W_C)
        vc_ref[...] = vc.reshape(bb, lt, W_C)
        q_ref[...] = qc.reshape(bb, lt, W_C)
    else:
        for c in range(lt // PAGE):
            rows = slice(c * PAGE, (c + 1) * PAGE)
            pk_ref[0, c] = kc[rows].T
            pv_ref[0, c] = vc[rows].T
        hd = HEAD_DIM
        d2 = dc * LOG2E
        for h in range(H_C):
            pair = slice((h // 2) * LANES, (h // 2 + 1) * LANES)
            at0 = (lambda a: a) if h % 2 == 0 else (lambda a: pltpu.roll(a, hd, axis=1))
            dh = d2[:, SM_LF + h:SM_LF + h + 1]
            t1 = dh.astype(BF16).astype(F32)
            t2 = (dh - t1).astype(BF16).astype(F32)
            t3 = dh - t1 - t2
            qa = jnp.where(lane < hd, at0(qc[:, pair]) * (LOG2E * hd ** -0.5), jnp.where(lane < hd + 3, 1.0, 0.0))
            ka = jnp.where(lane < hd, at0(kc[:, pair]),
                           jnp.where(lane == hd, -t1, jnp.where(lane == hd + 1, -t2,
                                                                jnp.where(lane == hd + 2, -t3, 0.0))))
            va = jnp.where(lane < hd, at0(vc[:, pair]), jnp.where(lane == hd, 1.0, 0.0))
            grp = slice(h * LANES, (h + 1) * LANES)
            qbf_ref[0, :, grp] = qa.astype(BF16)
            kbf_ref[0, :, grp] = ka.astype(BF16)
            vbf_ref[0, :, grp] = va.astype(BF16)


def _in_proj(dec, layer, x, shift, scale, gpre, w_fused, conv_w, conv_state, smb, alog, lng, lnb, ws, bsb,
             pages=()):
    b, L, _ = x.shape
    if dec:
        bb, lt = min(DEC_TILE_IN, b), L
    else:
        bb, lt = 1, min(SEQ_TILE, L)
    grid = (b // bb, L // lt)
    tok = lambda w: pl.BlockSpec((bb, lt, w), lambda i, j: (i, j, 0))
    per_b = lambda r, w: pl.BlockSpec((bb, r, w), lambda i, j: (i, 0, 0))
    in_specs = [tok(D_MODEL), per_b(1, D_MODEL), per_b(1, D_MODEL), _const_spec((1, D_MODEL)),
                pl.BlockSpec((None, D_MODEL, N_FUSED), lambda i, j: (layer, 0, 0), pipeline_mode=pl.Buffered(1)),
                _const_spec((GDN_CONV, 3 * W_A)),
                per_b(GDN_CONV - 1, 3 * W_A), _const_spec((1, LANES)), _const_spec((1, LANES)),
                _const_spec((1, W_B)), _const_spec((1, W_B)), _const_spec(ws.shape), _const_spec(bsb.shape)]
    sds = lambda w, dt=F32: jax.ShapeDtypeStruct((b, L, w), dt)
    out_shape = [sds(3 * W_A), sds(W_A), sds(LANES), sds(W_B),
                 jax.ShapeDtypeStruct((b, GDN_CONV - 1, 3 * W_A), F32)]
    out_specs = [tok(3 * W_A), tok(W_A), tok(LANES), tok(W_B), per_b(GDN_CONV - 1, 3 * W_A)]
    scratch = [pltpu.VMEM((bb, lt + SUBLANES, 3 * W_A), F32)]
    aliases = {}
    if dec:
        out_shape += [sds(W_C), sds(W_C), sds(W_B), sds(W_C)]
        out_specs += [tok(W_C), tok(W_C), tok(W_B), tok(W_C)]
    else:
        n_in = len(in_specs)
        in_specs += [pl.BlockSpec(memory_space=pl.ANY)] * 2
        aliases = {n_in: len(out_shape), n_in + 1: len(out_shape) + 1}
        page_spec = pl.BlockSpec((None, 1, lt // PAGE, W_C, PAGE), lambda i, j: (layer, i, j, 0, 0))
        head_major = pl.BlockSpec((1, LANES, lt), lambda i, j: (i, 0, j))
        out_shape += [jax.ShapeDtypeStruct(p.shape, p.dtype) for p in pages]
        out_specs += [page_spec, page_spec]
        wf = H_C * LANES
        out_shape += [sds(wf, BF16), sds(wf, BF16), sds(wf, BF16), sds(LANES),
                      jax.ShapeDtypeStruct((b, LANES, L), F32)]
        out_specs += [tok(wf), tok(wf), tok(wf), tok(LANES), head_major]
        scratch += [pltpu.VMEM((SUBLANES, LANES), F32)]
    return pl.pallas_call(
        functools.partial(_in_kernel, dec, bb, lt),
        grid=grid, in_specs=in_specs, out_specs=out_specs, out_shape=out_shape,
        scratch_shapes=scratch, input_output_aliases=aliases,
        compiler_params=_params(("arbitrary", "arbitrary")),
        name="in_proj_dec" if dec else "in_proj_seq",
    )(x, shift, scale, gpre, w_fused, conv_w, conv_state, smb, alog, lng, lnb, ws, bsb, *pages)


GDN_GROUP = 256
GDN_GROUPS_PER_STEP = 2
N_PAIR = H_A // 2


def _tri_inverse_m1(lms, blk):
    bf = lambda xs: [x.astype(BF16) for x in xs]
    mm_ = lambda xs, ys: [_dot(x, y) for x, y in zip(xs, ys)]
    dm = [jnp.where(blk, lm, 0.0) for lm in lms]
    nm = [lm - d for lm, d in zip(lms, dm)]
    dmb = bf(dm)
    d2 = mm_(dmb, dmb)
    d2b = bf(d2)
    d4 = mm_(d2b, d2b)
    d4b = bf(d4)
    d8 = mm_(d4b, d4b)
    a1 = [x2 - x1 - c for x2, x1, c in zip(d2, dm, mm_(dmb, d2b))]
    a2 = [x4 + x8 + c for x4, x8, c in zip(d4, d8, mm_(d4b, bf(d8)))]
    et = [x + y + c for x, y, c in zip(a1, a2, mm_(bf(a1), bf(a2)))]
    etb = bf(et)
    mm = [n + c for n, c in zip(nm, mm_(etb, bf(nm)))]
    mmb = bf(mm)
    m2 = mm_(mmb, mmb)
    a3 = [x2 - x1 - c for x2, x1, c in zip(m2, mm, mm_(mmb, bf(m2)))]
    return [x + e + c for x, e, c in zip(a3, et, mm_(bf(a3), etb))]


def _pair_cols(lo, a, c0, c1):
    return jnp.where(lo, a[:, c0:c0 + 1], a[:, c1:c1 + 1])


def _pair_rsqrt_norm(lo, x, scale):
    x2 = x * x
    s_lo = jnp.sum(jnp.where(lo, x2, 0.0), axis=-1, keepdims=True)
    s_hi = jnp.sum(jnp.where(lo, 0.0, x2), axis=-1, keepdims=True)
    return jnp.where(lo, lax.rsqrt(s_lo * scale + EPS), lax.rsqrt(s_hi * scale + EPS))


def _gdn_pre_kernel(T, G, qkv_ref, small_ref, qe_ref, o0_ref, egl_ref, m_ref, bm_ref):
    C = GDN_CHUNK
    row = _iota2((T, T), 0)
    col = _iota2((T, T), 1)
    same = (row // C) == (col // C)
    bd_tri = same & (row >= col)
    bd_strict = same & (row > col)
    blk = (row // INV_BLOCK) == (col // INV_BLOCK)
    lane = _iota2((1, LANES), 1)
    lo = lane < HEAD_DIM
    lo2 = (_iota2((1, 2 * LANES), 1) % LANES) < HEAD_DIM
    bd2 = (_iota2((LANES, 2 * LANES), 0) // HEAD_DIM) == ((_iota2((LANES, 2 * LANES), 1) % LANES) // HEAD_DIM)

    rhs, rhs_b, lms, attns, qins, kouts = [], [], [], [], [], []
    for g, j in [(g, j) for g in range(G) for j in range(N_PAIR)]:
        tok = slice(g * T, (g + 1) * T)
        if j == 0:
            sm = small_ref[0, tok, :]
            gc_all = _sel_dot(bd_tri, sm)
            gl_all = jnp.concatenate(
                [jnp.broadcast_to(gc_all[(c + 1) * C - 1:(c + 1) * C, :], (C, LANES)) for c in range(T // C)],
                axis=0)
            gc_t = gc_all.T
        sl = slice(j * LANES, (j + 1) * LANES)
        qp = qkv_ref[0, tok, j * LANES:(j + 1) * LANES]
        kp = qkv_ref[0, tok, W_A + j * LANES:W_A + (j + 1) * LANES]
        vp = qkv_ref[0, tok, 2 * W_A + j * LANES:2 * W_A + (j + 1) * LANES]
        qn = qp * _pair_rsqrt_norm(lo, qp, 1.0) * (HEAD_DIM ** -0.5)
        kn = kp * _pair_rsqrt_norm(lo, kp, 1.0)
        h0, h1 = 2 * j, 2 * j + 1
        beta_p = _pair_cols(lo, sm, SM_BETA + h0, SM_BETA + h1)
        gc_p = _pair_cols(lo, gc_all, SM_G + h0, SM_G + h1)
        gl_p = _pair_cols(lo, gl_all, SM_G + h0, SM_G + h1)
        eg_p = jnp.exp(gc_p)
        kb = kn * beta_p
        kn_bf = kn.astype(BF16)
        rhs.append(jnp.concatenate([kb * eg_p, vp * beta_p], axis=1))
        rhs_b.append(rhs[-1].astype(BF16))
        qins.append(qn * eg_p)
        kouts.append((kn * jnp.exp(gl_p - gc_p)).astype(BF16))
        egl_ref[0, tok, sl] = jnp.exp(gl_p)
        zero = jnp.zeros_like(kb)
        stacked = jnp.concatenate([jnp.where(lo, kb, zero), jnp.where(lo, zero, kb),
                                   jnp.where(lo, qn, zero), jnp.where(lo, zero, qn)], axis=0).astype(BF16)
        gq = _dot_nt(stacked, kn_bf)
        for hh in range(2):
            h = 2 * j + hh
            gcol = gc_all[:, SM_G + h:SM_G + h + 1]
            grow = gc_t[SM_G + h:SM_G + h + 1, :]
            decay = jnp.where(bd_tri, jnp.exp(jnp.where(bd_tri, gcol - grow, 0.0)), 0.0)
            lms.append(jnp.where(bd_strict, gq[hh * T:(hh + 1) * T] * decay, 0.0))
            attns.append((gq[(2 + hh) * T:(3 + hh) * T] * decay).astype(BF16))

    tm1 = _tri_inverse_m1(lms, blk)
    tok_chunk = _iota2((T, LANES), 0) // C
    for g, j in [(g, j) for g in range(G) for j in range(N_PAIR)]:
        tok = slice(g * T, (g + 1) * T)
        p = g * N_PAIR + j
        sl = slice(j * LANES, (j + 1) * LANES)
        both = _dot(jnp.concatenate([tm1[2 * p], tm1[2 * p + 1]], axis=0).astype(BF16), rhs_b[p])
        wu = rhs[p] + jnp.where(lo2, both[:T], both[T:])
        wu_b = wu.astype(BF16)
        both = _dot(jnp.concatenate([attns[2 * p], attns[2 * p + 1]], axis=0), wu_b)
        aw = jnp.where(lo2, both[:T], both[T:])
        qe_ref[0, tok, sl] = (qins[p] - aw[:, :LANES]).astype(BF16)
        o0_ref[0, tok, sl] = aw[:, LANES:]
        ko = kouts[p]
        spread = jnp.concatenate([jnp.where(tok_chunk == c, ko, jnp.zeros_like(ko)) for c in range(T // C)],
                                 axis=1)
        mb_all = _dot_tn(spread, wu_b)
        for c in range(T // C):
            mb = jnp.where(bd2, mb_all[c * LANES:(c + 1) * LANES], 0.0)
            m_ref[0, g * (T // C) + c, j] = mb[:, :LANES].astype(BF16)
            bm_ref[0, g * (T // C) + c, j] = mb[:, LANES:]


def _gdn_pre(qkva, small):
    b, L, _ = qkva.shape
    T = min(GDN_GROUP, L)
    G = max(1, min(GDN_GROUPS_PER_STEP, L // T))
    nc = G * T // GDN_CHUNK
    tok = lambda w: pl.BlockSpec((1, G * T, w), lambda i, j: (i, j, 0))
    per_chunk = pl.BlockSpec((1, nc, N_PAIR, LANES, LANES), lambda i, j: (i, j, 0, 0, 0))
    sds = lambda dt: jax.ShapeDtypeStruct((b, L, W_A), dt)
    chunk_sds = lambda dt: jax.ShapeDtypeStruct((b, L // GDN_CHUNK, N_PAIR, LANES, LANES), dt)
    return pl.pallas_call(
        functools.partial(_gdn_pre_kernel, T, G),
        grid=(b, L // (G * T)),
        in_specs=[tok(3 * W_A), tok(LANES)],
        out_specs=[tok(W_A)] * 3 + [per_chunk] * 2,
        out_shape=[sds(BF16), sds(F32), sds(F32), chunk_sds(BF16), chunk_sds(F32)],
        compiler_params=_params(("arbitrary", "arbitrary")),
        name="gdn_pre",
    )(qkva, small)


def _gdn_scan_kernel(nb, lt, qe_ref, o0_ref, egl_ref, m_ref, bm_ref, gate_ref, ng_ref, s0_ref,
                     oa_ref, sout_ref, s_sc):
    l = pl.program_id(0)
    C = GDN_CHUNK

    @pl.when(l == 0)
    def _():
        s_sc[...] = s0_ref[...]

    lane = _iota2((1, LANES), 1)
    lo = lane < HEAD_DIM

    def body(c, carry):
        r0 = pl.multiple_of(c * C, C)
        rows = pl.ds(r0, C)
        for b in range(nb):
            for j in range(N_PAIR):
                sl = slice(j * LANES, (j + 1) * LANES)
                s_old = s_sc[b, j]
                lhs = jnp.concatenate([m_ref[b, c, j], qe_ref[b, rows, sl]], axis=0)
                r = _dot(lhs, s_old.astype(BF16))
                s_sc[b, j] = s_old * egl_ref[b, pl.ds(r0, 1), sl] + (bm_ref[b, c, j] - r[:LANES])
                o = r[LANES:] + o0_ref[b, rows, sl]
                on = o * _pair_rsqrt_norm(lo, o, 1.0 / HEAD_DIM)
                oa_ref[b, rows, sl] = on * ng_ref[:, sl] * gate_ref[b, rows, sl]
        return carry

    lax.fori_loop(0, lt // C, body, 0)
    sout_ref[...] = s_sc[...]


def _gdn_scan(qe, o0, egl, m, bm, gate, ng, s0_bd):
    b, L, _ = qe.shape
    lt = min(SEQ_TILE, L)
    tok = pl.BlockSpec((b, lt, W_A), lambda i: (0, i, 0))
    per_chunk = pl.BlockSpec((b, lt // GDN_CHUNK, N_PAIR, LANES, LANES), lambda i: (0, i, 0, 0, 0))
    st = pl.BlockSpec((b, N_PAIR, LANES, LANES), lambda i: (0, 0, 0, 0))
    return pl.pallas_call(
        functools.partial(_gdn_scan_kernel, b, lt),
        grid=(L // lt,),
        in_specs=[tok, tok, tok, per_chunk, per_chunk, tok, _const_spec((1, W_A)), st],
        out_specs=[tok, st],
        out_shape=[jax.ShapeDtypeStruct((b, L, W_A), F32),
                   jax.ShapeDtypeStruct((b, N_PAIR, LANES, LANES), F32)],
        scratch_shapes=[pltpu.VMEM((b, N_PAIR, LANES, LANES), F32)],
        compiler_params=_params(("arbitrary",)),
        name="gdn_scan",
    )(qe, o0, egl, m, bm, gate, ng, s0_bd)


def _to_pair_blockdiag(s):
    b = s.shape[0]
    s = s.reshape(b, N_PAIR, 2, HEAD_DIM, HEAD_DIM)
    z = jnp.zeros_like(s[:, :, 0])
    top = jnp.concatenate([s[:, :, 0], z], axis=-1)
    bot = jnp.concatenate([z, s[:, :, 1]], axis=-1)
    return jnp.concatenate([top, bot], axis=-2)


def _from_pair_blockdiag(sbd):
    b = sbd.shape[0]
    s = jnp.stack([sbd[:, :, :HEAD_DIM, :HEAD_DIM], sbd[:, :, HEAD_DIM:, HEAD_DIM:]], axis=2)
    return s.reshape(b, H_A, HEAD_DIM, HEAD_DIM)


def _gdn_seq(qkva, small, gate, ng, s0):
    qe, o0, egl, m, bm = _gdn_pre(qkva, small)
    oa, sbd = _gdn_scan(qe, o0, egl, m, bm, gate, ng, _to_pair_blockdiag(s0))
    return oa, _from_pair_blockdiag(sbd)


def _gdn_dec_kernel(L, q_ref, k_ref, v_ref, smt_ref, gate_ref, ngb_ref, s0_ref, oa_ref, sout_ref, kq_sc):
    h = pl.program_id(0)
    nb = q_ref.shape[-1]
    for t in range(L):
        q_t = q_ref[t]
        k_t = k_ref[t]
        q_t = q_t * (lax.rsqrt(jnp.sum(q_t * q_t, axis=0, keepdims=True) + EPS) * (HEAD_DIM ** -0.5))
        k_t = k_t * lax.rsqrt(jnp.sum(k_t * k_t, axis=0, keepdims=True) + EPS)
        kq_sc[0] = k_t
        kq_sc[1] = q_t
        beta = smt_ref[t, pl.ds(SM_BETA + h, 1), :]
        a = jnp.exp(smt_ref[t, pl.ds(SM_G + h, 1), :])
        src = s0_ref if t == 0 else sout_ref

        def k_dot_s(kk, acc):
            return acc + src[0, kk] * kq_sc[0, pl.ds(kk, 1), :]

        ks = lax.fori_loop(0, HEAD_DIM, k_dot_s, jnp.zeros((HEAD_DIM, nb), F32), unroll=8)
        delta = beta * (v_ref[t] - a * ks)

        def update(kk, acc):
            s_new = a * src[0, kk] + kq_sc[0, pl.ds(kk, 1), :] * delta
            sout_ref[0, kk] = s_new
            return acc + s_new * kq_sc[1, pl.ds(kk, 1), :]

        o = lax.fori_loop(0, HEAD_DIM, update, jnp.zeros((HEAD_DIM, nb), F32), unroll=8)
        on = o * lax.rsqrt(jnp.mean(o * o, axis=0, keepdims=True) + EPS)
        oa_ref[t] = on * ngb_ref[...] * gate_ref[t]


def _gdn_dec(layer, qkv_t, small_t, gate_t, ngb, s0_t):
    L, _, nb = qkv_t.shape
    nh = W_A // HEAD_DIM
    head = lambda off: pl.BlockSpec((L, HEAD_DIM, nb), lambda h: (0, off + h, 0))
    st = pl.BlockSpec((1, HEAD_DIM, HEAD_DIM, nb), lambda h: (h, 0, 0, 0))
    st_in = pl.BlockSpec((None, 1, HEAD_DIM, HEAD_DIM, nb), lambda h: (layer, h, 0, 0, 0))
    return pl.pallas_call(
        functools.partial(_gdn_dec_kernel, L),
        grid=(H_A,),
        in_specs=[head(0), head(nh), head(2 * nh),
                  pl.BlockSpec((L, LANES, nb), lambda h: (0, 0, 0), pipeline_mode=pl.Buffered(1)),
                  head(0), pl.BlockSpec((HEAD_DIM, nb), lambda h: (h, 0)), st_in],
        out_specs=[head(0), st],
        out_shape=[jax.ShapeDtypeStruct((L, W_A, nb), F32),
                   jax.ShapeDtypeStruct((H_A, HEAD_DIM, HEAD_DIM, nb), F32)],
        scratch_shapes=[pltpu.VMEM((2, HEAD_DIM, nb), F32)],
        compiler_params=_params(("arbitrary",)),
        name="gdn_dec",
    )(qkv_t, qkv_t, qkv_t, small_t, gate_t, ngb, s0_t)


FOX_STRIP = 64
FOX_TQ = 512
FOX_TK = 512


def _fox_seq_kernel(tq, tk, q_ref, k_ref, v_ref, dq_ref, o_ref, m_sc, acc_sc, dqb_sc, s_sc, p_sc, al_sc):
    qi = pl.program_id(1)
    ndiag = tq // tk
    m_sc[...] = jnp.full_like(m_sc, NEG_INF)
    acc_sc[...] = jnp.zeros_like(acc_sc)
    dq = dq_ref[0] * LOG2E
    for h in range(H_C):
        dqb_sc[h] = jnp.broadcast_to(dq[:, SM_LF + h:SM_LF + h + 1], (tq, LANES))

    def step(ki, diag):
        k0 = pl.multiple_of(ki * tk, tk)
        r_lo = 0 if diag is None else diag * tk
        live = slice(r_lo, tq)

        def scores(h):
            grp = slice(h * LANES, (h + 1) * LANES)
            s_sc[h % 2, live, :] = _dot_nt(q_ref[0, live, grp], k_ref[0, pl.ds(k0, tk), grp])

        def accumulate(h):
            grp = slice(h * LANES, (h + 1) * LANES)
            acc_sc[h, live, :] = (al_sc[h % 2, live, :] * acc_sc[h, live, :]
                                  + _dot(p_sc[h % 2, live, :], v_ref[0, pl.ds(k0, tk), grp]))

        scores(0)
        for h in range(H_C):
            buf = h % 2
            if h + 1 < H_C:
                scores(h + 1)
            for r in range(r_lo // FOX_STRIP, tq // FOX_STRIP):
                rows = slice(r * FOX_STRIP, (r + 1) * FOX_STRIP)
                nc = tk if diag is None else min(tk, -(-((r + 1) * FOX_STRIP - r_lo) // LANES) * LANES)
                s = s_sc[buf, rows, 0:nc]
                if diag is not None:
                    keep = _iota2((FOX_STRIP, nc), 1) <= _iota2((FOX_STRIP, nc), 0) + (r * FOX_STRIP - r_lo)
                    s = jnp.where(keep, s, NEG_INF)
                dqb = dqb_sc[h, rows, :]
                m_old = m_sc[h, rows, :]
                m_new = jnp.maximum(m_old, jnp.max(s, axis=-1, keepdims=True) + dqb)
                shift = m_new - dqb
                if nc > LANES:
                    shift = jnp.concatenate([shift] * (nc // LANES), axis=1)
                m_sc[h, rows, :] = m_new
                al_sc[buf, rows, :] = jnp.exp2(m_old - m_new)
                p_sc[buf, rows, 0:nc] = jnp.exp2(s - shift).astype(BF16)
                if nc < tk:
                    p_sc[buf, rows, nc:tk] = jnp.zeros((FOX_STRIP, tk - nc), BF16)
            if h > 0:
                accumulate(h - 1)
        accumulate(H_C - 1)

    def body(ki, carry):
        step(ki, None)
        return carry

    lax.fori_loop(0, ndiag * qi, body, 0)
    for d in range(ndiag):
        step(ndiag * qi + d, d)
    for j in range(H_C // 2):
        halves = []
        for h in (2 * j, 2 * j + 1):
            acc = acc_sc[h]
            halves.append(acc[:, :HEAD_DIM] / acc[:, HEAD_DIM:HEAD_DIM + 1])
        o_ref[0, :, j * LANES:(j + 1) * LANES] = jnp.concatenate(halves, axis=1)


def _fox_seq(q_aug, k_aug, v_aug, dcum):
    b, L, wf = q_aug.shape
    tk = min(FOX_TK, L)
    tq = min(FOX_TQ, L)
    full = pl.BlockSpec((1, L, wf), lambda i, j: (i, 0, 0), pipeline_mode=pl.Buffered(1))
    return pl.pallas_call(
        functools.partial(_fox_seq_kernel, tq, tk),
        grid=(b, L // tq),
        in_specs=[pl.BlockSpec((1, tq, wf), lambda i, j: (i, j, 0)), full, full,
                  pl.BlockSpec((1, tq, LANES), lambda i, j: (i, j, 0))],
        out_specs=pl.BlockSpec((1, tq, W_C), lambda i, j: (i, j, 0)),
        out_shape=jax.ShapeDtypeStruct((b, L, W_C), F32),
        scratch_shapes=[pltpu.VMEM((H_C, tq, LANES), F32), pltpu.VMEM((H_C, tq, LANES), F32),
                        pltpu.VMEM((H_C, tq, LANES), F32),
                        pltpu.VMEM((2, tq, tk), F32), pltpu.VMEM((2, tq, tk), BF16),
                        pltpu.VMEM((2, tq, LANES), F32)],
        compiler_params=_params(("arbitrary", "arbitrary")),
        name="fox_seq",
    )(q_aug, k_aug, v_aug, dcum)


def _fox_dec_kernel(L, n_pages, pt_ref, q_ref, kn_ref, vn_ref, sm_ref, lf_ref, *refs):
    k_refs = refs[0:n_pages]
    v_refs = refs[n_pages:2 * n_pages]
    o_ref = refs[2 * n_pages]
    x_sc = refs[2 * n_pages + 1]
    i = pl.program_id(0)
    R = H_C * L
    q = q_ref[0] * (HEAD_DIM ** -0.5)
    rowh = _iota2((R, W_C), 0) // L
    colh = _iota2((R, W_C), 1) // HEAD_DIM
    qbd = jnp.where(rowh == colh, jnp.concatenate([q] * H_C, axis=0), 0.0).astype(BF16)

    for p in range(n_pages):
        pg = pt_ref[i * n_pages + p]
        for h in range(H_C):
            r = h * n_pages + p
            x_sc[r:r + 1, :] = lf_ref[h, pl.ds(pg, 1), :]
    x = x_sc[...]
    n = H_C * n_pages
    later = (_iota2((PAGE, PAGE), 0) > _iota2((PAGE, PAGE), 1)).astype(F32)
    within = _dot_sel(x, later)
    tot = _dot_sel(x, jnp.ones((PAGE, PAGE), F32))
    ri = _iota2((n, n), 0)
    ci = _iota2((n, n), 1)
    later_pages = ((ci // n_pages == ri // n_pages) & (ci % n_pages > ri % n_pages)).astype(F32)
    rsum = within + _sel_dot(later_pages, tot)

    sm = sm_ref[0]
    tri = (_iota2((L, L), 0) >= _iota2((L, L), 1)).astype(F32)
    cq = _sel_dot(tri, sm)
    cq_t = cq.T
    cq_col = jnp.concatenate([cq[:, SM_LF + h:SM_LF + h + 1] for h in range(H_C)], axis=0)
    cq_row = jnp.concatenate([jnp.broadcast_to(cq_t[SM_LF + h:SM_LF + h + 1, :], (L, L))
                              for h in range(H_C)], axis=0)

    s_pages = []
    for p in range(n_pages):
        sp = _dot(qbd, k_refs[p][...].astype(BF16))
        bias = jnp.concatenate(
            [jnp.broadcast_to(rsum[h * n_pages + p:h * n_pages + p + 1, :], (L, PAGE)) for h in range(H_C)],
            axis=0)
        s_pages.append(sp + bias + cq_col)
    s_new = _dot_nt(qbd, kn_ref[0].astype(BF16)) + cq_col - cq_row
    qpos = _iota2((R, L), 0) % L
    s_new = jnp.where(_iota2((R, L), 1) <= qpos, s_new, NEG_INF)

    mx = jnp.max(s_new, axis=-1, keepdims=True)
    for sp in s_pages:
        mx = jnp.maximum(mx, jnp.max(sp, axis=-1, keepdims=True))
    p_new = jnp.exp(s_new - mx)
    den = jnp.sum(p_new, axis=-1, keepdims=True)
    acc = _dot(p_new.astype(BF16), vn_ref[0].astype(BF16))
    for p in range(n_pages):
        pp = jnp.exp(s_pages[p] - mx)
        den = den + jnp.sum(pp, axis=-1, keepdims=True)
        acc = acc + _dot_nt(pp.astype(BF16), v_refs[p][...].astype(BF16))
    acc = jnp.where(rowh == colh, acc / den, 0.0)
    out = acc[0:L]
    for h in range(1, H_C):
        out = out + acc[h * L:(h + 1) * L]
    o_ref[0] = out


def _fox_dec(layer, page_table, q, kn, vn, small, cache_kt, cache_vt, cache_lft):
    b, L, _ = q.shape
    n_pages = page_table.shape[1]
    n_pool = cache_kt.shape[1]
    tok = lambda w: pl.BlockSpec((1, L, w), lambda i, pt: (i, 0, 0))

    def page_spec(p):
        return pl.BlockSpec((None, None, W_C, PAGE), lambda i, pt: (layer, pt[i * n_pages + p], 0, 0))

    in_specs = [tok(W_C), tok(W_C), tok(W_C), tok(LANES),
                pl.BlockSpec((None, H_C, n_pool, PAGE), lambda i, pt: (layer, 0, 0, 0),
                             pipeline_mode=pl.Buffered(1))]
    in_specs += [page_spec(p) for p in range(n_pages)]
    in_specs += [page_spec(p) for p in range(n_pages)]
    grid_spec = pltpu.PrefetchScalarGridSpec(
        num_scalar_prefetch=1, grid=(b,), in_specs=in_specs, out_specs=tok(W_C),
        scratch_shapes=[pltpu.VMEM((H_C * n_pages, PAGE), F32)])
    return pl.pallas_call(
        functools.partial(_fox_dec_kernel, L, n_pages),
        grid_spec=grid_spec,
        out_shape=jax.ShapeDtypeStruct((b, L, W_C), F32),
        compiler_params=_params(("arbitrary",)),
        name="fox_dec",
    )(page_table.reshape(-1), q, kn, vn, small, cache_lft,
      *([cache_kt] * n_pages), *([cache_vt] * n_pages))


def _post_kernel(bb, lt, x_ref, oa_ref, ob_ref, oc_ref, gate1_ref, shift2_ref, scale2_ref, gate2_ref,
                 gpm_ref, gpf_ref, gqf_ref, wo_ref, wu_ref, cw_ref, cst_ref, wd_ref,
                 y_ref, tail_ref, ext_ref, car_ref):
    l = pl.program_id(1)
    m = bb * lt
    keep = FFN_CONV - 1

    def rms(v, g_ref):
        return v * lax.rsqrt(jnp.mean(v * v, axis=-1, keepdims=True) + EPS) * g_ref[...]

    oa = oa_ref[...].reshape(m, W_A).astype(BF16)
    ob = ob_ref[...].reshape(m, W_B).astype(BF16)
    oc = oc_ref[...].reshape(m, W_C).astype(BF16)
    o = (_dot(oa, wo_ref[0:W_A, :]) + _dot(ob, wo_ref[W_A:W_A + W_B, :])
         + _dot(oc, wo_ref[W_A + W_B:W_A + W_B + W_C, :]))
    x1 = x_ref[...] + gate1_ref[...] * rms(o, gpm_ref).reshape(bb, lt, D_MODEL)

    h = rms(x1, gpf_ref) * (1.0 + scale2_ref[...]) + shift2_ref[...]
    h2 = h.reshape(m, D_MODEL).astype(BF16)

    @pl.when(l == 0)
    def _():
        car_ref[...] = jnp.zeros_like(car_ref)
        car_ref[:, SUBLANES - keep:SUBLANES, :] = cst_ref[...]

    w = FF_CHUNK
    nchunk = D_FF // FF_CHUNK

    def project(slot, c0):
        up = _dot(h2, wu_ref[:, c0:c0 + w]).reshape(bb, lt, w)
        ext_ref[slot, :, 0:SUBLANES, :] = car_ref[:, :, c0:c0 + w]
        ext_ref[slot, :, SUBLANES:, :] = up
        car_ref[:, :, c0:c0 + w] = ext_ref[slot, :, lt:lt + SUBLANES, :]

    def conv(slot, c0):
        y = None
        for i in range(FFN_CONV):
            off = SUBLANES - keep + i
            term = ext_ref[slot, :, off:off + lt, :] * cw_ref[i:i + 1, c0:c0 + w]
            y = term if y is None else y + term
        return y

    project(0, 0)
    project(1, D_FF)
    y = jnp.zeros((m, D_MODEL), F32)
    for j in range(nchunk):
        if j + 1 < nchunk:
            project(2 * (j + 1), (j + 1) * w)
            project(2 * (j + 1) + 1, D_FF + (j + 1) * w)
        a = conv(2 * j, j * w)
        b = conv(2 * j + 1, D_FF + j * w)
        g = (jax.nn.gelu(a) * b).reshape(m, w).astype(BF16)
        y = y + _dot(g, wd_ref[j * w:(j + 1) * w, :])
    tail_ref[...] = car_ref[:, SUBLANES - keep:SUBLANES, :]
    y_ref[...] = x1 + gate2_ref[...] * rms(y, gqf_ref).reshape(bb, lt, D_MODEL)


def _post(dec, layer, x, oa, ob, oc, gate1, shift2, scale2, gate2, gpm, gpf, gqf, wo, wu, cw, cst, wd):
    b, L, _ = x.shape
    if dec:
        bb, lt = min(DEC_TILE_POST, b), L
    else:
        bb, lt = 1, min(SEQ_TILE, L)
    tok = lambda w: pl.BlockSpec((bb, lt, w), lambda i, j: (i, j, 0))
    per_b = lambda r, w: pl.BlockSpec((bb, r, w), lambda i, j: (i, 0, 0))
    vec = _const_spec((1, D_MODEL))
    weight = lambda r, c: pl.BlockSpec((None, r, c), lambda i, j: (layer, 0, 0), pipeline_mode=pl.Buffered(1))
    in_specs = [tok(D_MODEL), tok(W_A), tok(W_B), tok(W_C), per_b(1, D_MODEL), per_b(1, D_MODEL),
                per_b(1, D_MODEL), per_b(1, D_MODEL), vec, vec, vec,
                weight(W_A + W_B + W_C, D_MODEL), weight(D_MODEL, 2 * D_FF),
                _const_spec((FFN_CONV, 2 * D_FF)), per_b(FFN_CONV - 1, 2 * D_FF),
                weight(D_FF, D_MODEL)]
    return pl.pallas_call(
        functools.partial(_post_kernel, bb, lt),
        grid=(b // bb, L // lt),
        in_specs=in_specs,
        out_specs=[tok(D_MODEL), per_b(FFN_CONV - 1, 2 * D_FF)],
        out_shape=[jax.ShapeDtypeStruct((b, L, D_MODEL), F32),
                   jax.ShapeDtypeStruct((b, FFN_CONV - 1, 2 * D_FF), F32)],
        scratch_shapes=[pltpu.VMEM((2 * D_FF // FF_CHUNK, bb, lt + SUBLANES, FF_CHUNK), F32),
                        pltpu.VMEM((bb, SUBLANES, 2 * D_FF), F32)],
        compiler_params=_params(("arbitrary", "arbitrary")),
        name="post_dec" if dec else "post_seq",
    )(x, oa, ob, oc, gate1, shift2, scale2, gate2, gpm, gpf, gqf, wo, wu, cw, cst, wd)


def _layer_params(l, gdn_A_log, gdn_dt_bias, fox_f_bias, chunk_w_s, chunk_b_s, dec_len):
    z = lambda n: jnp.zeros((n,), F32)
    smb = jnp.concatenate([z(H_A), gdn_dt_bias[l], fox_f_bias[l], z(LANES - 2 * H_A - H_C)]).reshape(1, LANES)
    alog = jnp.concatenate([z(H_A), gdn_A_log[l], z(LANES - 2 * H_A)]).reshape(1, LANES)
    ws = chunk_w_s[l]
    bs = chunk_b_s[l]
    ws_seq = ws.reshape(G_B * CHUNK, CHUNK)
    bsb_seq = jnp.repeat(bs.T, HEAD_DIM, axis=1)
    ws_dec = jnp.repeat(jnp.transpose(ws[:, :dec_len, :dec_len], (2, 1, 0)), HEAD_DIM, axis=2)
    bsb_dec = bsb_seq[:dec_len]
    return smb, alog, ws_seq, bsb_seq, ws_dec, bsb_dec


def kernel(x_prompt, x_sample, state_gdn_conv, state_gdn_S, cache_fox_k, cache_fox_v, cache_fox_logf,
           state_ffn_conv, page_table, c_prompt, c_sample, w_ada, b_ada, g_pre_mix, g_post_mix, g_pre_ffn,
           g_post_ffn, w_in, w_out, gdn_conv_w, gdn_A_log, gdn_dt_bias, gdn_norm_g, chunk_ln_g, chunk_ln_b,
           chunk_w_s, chunk_b_s, fox_f_bias, w_up, ffn_conv_w, w_down):
    depth = w_in.shape[0]
    bp, lp, _ = x_prompt.shape
    bs, ls, _ = x_sample.shape
    n_pool = cache_fox_k.shape[1]

    cache_kt = jnp.transpose(cache_fox_k, (0, 1, 3, 4, 2)).reshape(depth, n_pool, W_C, PAGE)
    cache_vt = jnp.transpose(cache_fox_v, (0, 1, 3, 4, 2)).reshape(depth, n_pool, W_C, PAGE)
    cache_lft = jnp.transpose(cache_fox_logf, (0, 3, 1, 2))
    state_s_lanes = jnp.transpose(state_gdn_S, (0, 2, 3, 4, 1))

    c_all = jnp.concatenate([c_prompt, c_sample], axis=0)
    pad = (-c_all.shape[0]) % SUBLANES
    c_all = jnp.pad(c_all, ((0, pad), (0, 0)))

    w_fused = _w_in_prep(w_in)
    pages = tuple(jnp.zeros((depth, bp, lp // PAGE, W_C, PAGE), F32) for _ in range(2))

    vec = lambda a: a.reshape(1, -1)
    outs = {k: [] for k in ("p_conv", "p_S", "p_lf", "p_ffn",
                            "s_conv", "s_S", "s_k", "s_v", "s_lf", "s_cv", "s_ffn")}
    xp, xs = x_prompt, x_sample
    mod_all = _ada(c_all, w_ada, b_ada)
    wo = w_out.astype(BF16)
    wu = w_up.astype(BF16)
    wd = w_down.astype(BF16)
    for l in range(depth):
        mod_p = mod_all[l, :bp].reshape(bp, 6, 1, D_MODEL)
        mod_s = mod_all[l, bp:bp + bs].reshape(bs, 6, 1, D_MODEL)
        smb, alog, ws_seq, bsb_seq, ws_dec, bsb_dec = _layer_params(
            l, gdn_A_log, gdn_dt_bias, fox_f_bias, chunk_w_s, chunk_b_s, ls)
        ng = jnp.tile(gdn_norm_g[l], H_A).reshape(1, W_A)
        common_in = (vec(g_pre_mix[l]), w_fused, gdn_conv_w[l])
        common_b = (smb, alog, vec(chunk_ln_g[l]), vec(chunk_ln_b[l]))
        post_w = (vec(g_post_mix[l]), vec(g_pre_ffn[l]), vec(g_post_ffn[l]), wo, wu, ffn_conv_w[l])

        (qkva, gate, small, ob, tail, pk, pv, qbf, kbf, vbf, dcum, small_t) = _in_proj(
            False, l, xp, mod_p[:, 0], mod_p[:, 1], *common_in,
            jnp.zeros((bp, GDN_CONV - 1, 3 * W_A), F32), *common_b, ws_seq, bsb_seq, pages=pages)
        pages = (pk, pv)
        oa, s_new = _gdn_seq(qkva, small, gate, ng, jnp.zeros((bp, H_A, HEAD_DIM, HEAD_DIM), F32))
        oc = _fox_seq(qbf, kbf, vbf, dcum)
        xp, ftail = _post(False, l, xp, oa, ob, oc, mod_p[:, 2], mod_p[:, 3], mod_p[:, 4], mod_p[:, 5],
                          *post_w, jnp.zeros((bp, FFN_CONV - 1, 2 * D_FF), F32), wd)
        outs["p_conv"].append(tail)
        outs["p_S"].append(s_new)
        lf_hm = small_t[:, SM_LF:SM_LF + H_C, :].reshape(bp, H_C, lp // PAGE, PAGE)
        outs["p_lf"].append(jnp.transpose(lf_hm, (0, 2, 3, 1)))
        outs["p_ffn"].append(ftail)

        (qkva, gate, small, ob, tail, kc, vc, vb, qd) = _in_proj(
            True, l, xs, mod_s[:, 0], mod_s[:, 1], *common_in, state_gdn_conv[l], *common_b, ws_dec, bsb_dec)
        to_lanes = lambda a: jnp.transpose(a, (1, 2, 0))
        oa_t, s_new_t = _gdn_dec(l, to_lanes(qkva), to_lanes(small), to_lanes(gate),
                                 jnp.broadcast_to(ng.reshape(W_A, 1), (W_A, bs)), state_s_lanes)
        oa = jnp.transpose(oa_t, (2, 0, 1))
        s_new = jnp.transpose(s_new_t, (3, 0, 1, 2))
        oc = _fox_dec(l, page_table, qd, kc, vc, small, cache_kt, cache_vt, cache_lft)
        xs, ftail = _post(True, l, xs, oa, ob, oc, mod_s[:, 2], mod_s[:, 3], mod_s[:, 4], mod_s[:, 5],
                          *post_w, state_ffn_conv[l], wd)
        outs["s_conv"].append(tail)
        outs["s_S"].append(s_new)
        outs["s_k"].append(kc.reshape(bs, ls, H_C, HEAD_DIM))
        outs["s_v"].append(vc.reshape(bs, ls, H_C, HEAD_DIM))
        outs["s_lf"].append(small[:, :, SM_LF:SM_LF + H_C])
        outs["s_cv"].append(vb)
        outs["s_ffn"].append(ftail)

    st = lambda k: jnp.stack(outs[k])
    from_pages = lambda p: jnp.transpose(
        p.reshape(depth, bp, lp // PAGE, H_C, HEAD_DIM, PAGE), (0, 1, 2, 5, 3, 4))
    return (xp, xs, st("p_conv"), st("p_S"), from_pages(pages[0]), from_pages(pages[1]), st("p_lf"), st("p_ffn"),
            st("s_conv"), st("s_S"), st("s_k"), st("s_v"), st("s_lf"), st("s_cv"), st("s_ffn"))
```

```python
import functools

import jax
import jax.numpy as jnp
from jax import lax
from jax.experimental import pallas as pl
from jax.experimental.pallas import tpu as pltpu

F32 = jnp.float32
BF16 = jnp.bfloat16

D_MODEL = 1024
HEAD_DIM = 64
H_A = 6
G_B = 4
H_C = 6
W_A = H_A * HEAD_DIM
W_B = G_B * HEAD_DIM
W_C = H_C * HEAD_DIM
GDN_CONV = 4
GDN_CHUNK = 64
CHUNK = 128
D_FF = 2816
FFN_CONV = 3
EPS = 1e-6
NEG_INF = -1e30
PAGE = 128
LOG2E = 1.4426950408889634

LANES = 128
SUBLANES = 8
VMEM_LIMIT = 56 * 1024 * 1024

QA_OFF = 0
GA_OFF = QA_OFF + 3 * W_A
UV_OFF = GA_OFF + W_A
QC_OFF = UV_OFF + 2 * W_B
SM_OFF = QC_OFF + 3 * W_C
N_FUSED = SM_OFF + LANES
SM_BETA = 0
SM_G = H_A
SM_LF = 2 * H_A

FF_CHUNK = D_FF

SEQ_TILE = 512
DEC_TILE_IN = 32
DEC_TILE_POST = 16
ADA_TILE = 1024
INV_BLOCK = 16


def _silu(x):
    return x * jax.nn.sigmoid(x)


def _softplus(x):
    return jnp.maximum(x, 0.0) + jnp.log1p(jnp.exp(-jnp.abs(x)))


def _dot(a, b):
    return jnp.dot(a, b, preferred_element_type=F32)


def _split3(x):
    t1 = x.astype(BF16)
    r1 = x - t1.astype(F32)
    t2 = r1.astype(BF16)
    t3 = (r1 - t2.astype(F32)).astype(BF16)
    return t1, t2, t3


def _sel_dot(sel, x):
    s = sel.astype(BF16)
    t1, t2, t3 = _split3(x)
    return _dot(s, t1) + (_dot(s, t2) + _dot(s, t3))


def _dot_sel(x, sel):
    s = sel.astype(BF16)
    t1, t2, t3 = _split3(x)
    return _dot(t1, s) + (_dot(t2, s) + _dot(t3, s))


def _dot_nt(a, b):
    return lax.dot_general(a, b, (((1,), (1,)), ((), ())), preferred_element_type=F32)


def _dot_tn(a, b):
    return lax.dot_general(a, b, (((0,), (0,)), ((), ())), preferred_element_type=F32)


def _iota2(shape, dim):
    return lax.broadcasted_iota(jnp.int32, shape, dim)


def _params(sem):
    return pltpu.CompilerParams(dimension_semantics=sem, vmem_limit_bytes=VMEM_LIMIT)


def _const_spec(shape):
    nd = len(shape)
    return pl.BlockSpec(shape, lambda *_: (0,) * nd, pipeline_mode=pl.Buffered(1))


def _ada_kernel(c_ref, w_ref, b_ref, o_ref):
    c = _silu(c_ref[...]).astype(BF16)
    o_ref[...] = _dot(c, w_ref[...].astype(BF16)) + b_ref[...]


def _ada(c, w, b):
    m = c.shape[0]
    depth, _, n = w.shape
    tn = ADA_TILE
    return pl.pallas_call(
        _ada_kernel,
        grid=(depth, n // tn),
        in_specs=[pl.BlockSpec((m, D_MODEL), lambda l, j: (0, 0)),
                  pl.BlockSpec((None, D_MODEL, tn), lambda l, j: (l, 0, j)),
                  pl.BlockSpec((None, 1, tn), lambda l, j: (l, 0, j))],
        out_specs=pl.BlockSpec((None, m, tn), lambda l, j: (l, 0, j)),
        out_shape=jax.ShapeDtypeStruct((depth, m, n), F32),
        compiler_params=_params(("arbitrary", "arbitrary")),
        name="ada_mod",
    )(c, w, b.reshape(depth, 1, n))


def _w_in_prep_kernel(depth, a_ref, o_ref):
    l = pl.program_id(0)
    j = pl.program_id(1)
    nkt = D_MODEL // LANES
    rpn = nkt * depth
    n_regular = SM_OFF // LANES
    n_first = GA_OFF // LANES
    n_in = SM_OFF + 2 * H_A + H_C

    @pl.when(j < n_regular)
    def _():
        n0 = j * LANES + jnp.where(j >= n_first, 2 * H_A, 0)
        for kt in range(nkt):
            x = a_ref[pl.ds(n0 * rpn + kt * depth + l, LANES, stride=rpn), :]
            o_ref[0, kt * LANES:(kt + 1) * LANES, :] = x.T.astype(BF16)

    @pl.when(j == n_regular)
    def _():
        r1, r2 = 2 * SUBLANES, SUBLANES
        n2 = n_in - r2
        r = _iota2((r1 + r2, LANES), 0)
        m = _iota2((r1 + r2, LANES), 1)
        pick = ((m < 2 * H_A) & (r == m)) | (
            (m >= 2 * H_A) & (m < 2 * H_A + H_C) & (r == m - 2 * H_A + r1 + (r2 - H_C)))
        sel = pick.astype(BF16)
        for kt in range(nkt):
            x1 = a_ref[pl.ds(GA_OFF * rpn + kt * depth + l, r1, stride=rpn), :]
            x2 = a_ref[pl.ds(n2 * rpn + kt * depth + l, r2, stride=rpn), :]
            xc = jnp.concatenate([x1, x2], axis=0).astype(BF16)
            o_ref[0, kt * LANES:(kt + 1) * LANES, :] = _dot_tn(xc, sel).astype(BF16)


def _w_in_prep(w_in):
    depth, d, n_in = w_in.shape
    nkt = d // LANES
    view = w_in.reshape(depth, nkt, LANES, n_in).transpose(3, 1, 0, 2).reshape(n_in * nkt * depth, LANES)
    return pl.pallas_call(
        functools.partial(_w_in_prep_kernel, depth),
        grid=(depth, N_FUSED // LANES),
        in_specs=[pl.BlockSpec(view.shape, lambda l, j: (0, 0), pipeline_mode=pl.Buffered(1))],
        out_specs=pl.BlockSpec((1, d, LANES), lambda l, j: (l, 0, j)),
        out_shape=jax.ShapeDtypeStruct((depth, d, N_FUSED), BF16),
        compiler_params=_params(("arbitrary", "arbitrary")),
        name="w_in_prep",
    )(view)


def _in_kernel(dec, bb, lt, *refs):
    (x_ref, shift_ref, scale_ref, gpre_ref, w_ref, cw_ref, cst_ref, smb_ref, alog_ref,
     lng_ref, lnb_ref, ws_ref, bsb_ref) = refs[:13]
    if dec:
        (qkva_ref, gate_ref, small_ref, ob_ref, tail_ref, kc_ref, vc_ref, vb_ref, q_ref,
         ext_ref) = refs[13:]
    else:
        (_, _, qkva_ref, gate_ref, small_ref, ob_ref, tail_ref, pk_ref, pv_ref, qbf_ref, kbf_ref,
         vbf_ref, dcum_ref, smt_ref, ext_ref, dcar_ref) = refs[13:]
    l = pl.program_id(1)
    m = bb * lt

    x = x_ref[...]
    ms = jnp.mean(x * x, axis=-1, keepdims=True)
    h = x * lax.rsqrt(ms + EPS) * gpre_ref[...]
    h = h * (1.0 + scale_ref[...]) + shift_ref[...]
    h2 = h.reshape(m, D_MODEL).astype(BF16)

    wa = 3 * W_A
    za = _dot(h2, w_ref[:, QA_OFF:QA_OFF + wa]).reshape(bb, lt, wa)

    @pl.when(l == 0)
    def _():
        ext_ref[:, SUBLANES - (GDN_CONV - 1):SUBLANES, :] = cst_ref[...]

    @pl.when(l > 0)
    def _():
        ext_ref[:, 0:SUBLANES, :] = ext_ref[:, lt:lt + SUBLANES, :]

    ext_ref[:, SUBLANES:, :] = za
    y = za * cw_ref[GDN_CONV - 1:GDN_CONV, :]
    for i in range(GDN_CONV - 1):
        o = SUBLANES - (GDN_CONV - 1) + i
        y = y + ext_ref[:, o:o + lt, :] * cw_ref[i:i + 1, :]
    qkva_ref[...] = _silu(y)
    tail_ref[...] = ext_ref[:, lt + SUBLANES - (GDN_CONV - 1):lt + SUBLANES, :]

    gate_ref[...] = _silu(_dot(h2, w_ref[:, GA_OFF:GA_OFF + W_A])).reshape(bb, lt, W_A)

    zs = _dot(h2, w_ref[:, SM_OFF:SM_OFF + LANES]) + smb_ref[...]
    lane = _iota2((1, LANES), 1)
    beta = jax.nn.sigmoid(zs)
    gval = -jnp.exp(alog_ref[...]) * _softplus(zs)
    lf = -_softplus(-zs)
    small = jnp.where(lane < SM_G, beta,
                      jnp.where(lane < SM_LF, gval, jnp.where(lane < SM_LF + H_C, lf, 0.0)))
    small_ref[...] = small.reshape(bb, lt, LANES)

    if not dec:
        tri = (_iota2((CHUNK, CHUNK), 0) >= _iota2((CHUNK, CHUNK), 1)).astype(F32)

        @pl.when(l == 0)
        def _():
            dcar_ref[...] = jnp.zeros_like(dcar_ref)

        run = dcar_ref[0:1, :]
        parts = []
        for c in range(lt // CHUNK):
            parts.append(_sel_dot(tri, small[c * CHUNK:(c + 1) * CHUNK]) + run)
            run = parts[-1][CHUNK - 1:CHUNK, :]
        dc = jnp.concatenate(parts, axis=0)
        dcum_ref[...] = dc.reshape(bb, lt, LANES)
        dcar_ref[0:1, :] = dc[lt - 1:lt, :]
        smt_ref[0] = small.T

    uv = jax.nn.gelu(_dot(h2, w_ref[:, UV_OFF:UV_OFF + 2 * W_B]))
    u = uv[:, :W_B]
    v = uv[:, W_B:]
    mu = jnp.mean(v, axis=-1, keepdims=True)
    vc0 = v - mu
    var = jnp.mean(vc0 * vc0, axis=-1, keepdims=True)
    vb = vc0 * lax.rsqrt(var + EPS) * lng_ref[...] + lnb_ref[...]
    if dec:
        vb3 = vb.reshape(bb, lt, W_B)
        vb_ref[...] = vb3
        trow = _iota2((lt, W_B), 0)
        mix = jnp.zeros((bb, lt, W_B), F32)
        for s in range(lt):
            coef = jnp.where(trow >= s, ws_ref[s], 0.0)
            mix = mix + coef[None] * vb3[:, s:s + 1, :]
        ob_ref[...] = u.reshape(bb, lt, W_B) * (mix + bsb_ref[...][None])
    else:
        r_t = _iota2((G_B * CHUNK, CHUNK), 0) % CHUNK
        r_s = _iota2((G_B * CHUNK, CHUNK), 1)
        wst = jnp.where(r_s <= r_t, ws_ref[...], 0.0).astype(BF16)
        grp = _iota2((1, W_B), 1) // HEAD_DIM
        for c in range(lt // CHUNK):
            rows = slice(c * CHUNK, (c + 1) * CHUNK)
            r = _dot(wst, vb[rows].astype(BF16))
            mix = jnp.zeros((CHUNK, W_B), F32)
            for g in range(G_B):
                mix = jnp.where(grp == g, r[g * CHUNK:(g + 1) * CHUNK], mix)
            ob_ref[0, rows, :] = u[rows] * (mix + bsb_ref[...])

    zc = _dot(h2, w_ref[:, QC_OFF:QC_OFF + 3 * W_C])
    qc = zc[:, 0:W_C]
    kc = zc[:, W_C:2 * W_C]
    vc = zc[:, 2 * W_C:3 * W_C]
    if dec:
        kc_ref[...] = kc.reshape(bb, lt, W_C)
        vc_ref[...] = vc.reshape(bb, lt, W_C)
        q_ref[...] = qc.reshape(bb, lt, W_C)
    else:
        for c in range(lt // PAGE):
            rows = slice(c * PAGE, (c + 1) * PAGE)
            pk_ref[0, c] = kc[rows].T
            pv_ref[0, c] = vc[rows].T
        hd = HEAD_DIM
        d2 = dc * LOG2E
        for h in range(H_C):
            pair = slice((h // 2) * LANES, (h // 2 + 1) * LANES)
            at0 = (lambda a: a) if h % 2 == 0 else (lambda a: pltpu.roll(a, hd, axis=1))
            dh = d2[:, SM_LF + h:SM_LF + h + 1]
            t1 = dh.astype(BF16).astype(F32)
            t2 = (dh - t1).astype(BF16).astype(F32)
            t3 = dh - t1 - t2
            qa = jnp.where(lane < hd, at0(qc[:, pair]) * (LOG2E * hd ** -0.5), jnp.where(lane < hd + 3, 1.0, 0.0))
            ka = jnp.where(lane < hd, at0(kc[:, pair]),
                           jnp.where(lane == hd, -t1, jnp.where(lane == hd + 1, -t2,
                                                                jnp.where(lane == hd + 2, -t3, 0.0))))
            va = jnp.where(lane < hd, at0(vc[:, pair]), jnp.where(lane == hd, 1.0, 0.0))
            grp = slice(h * LANES, (h + 1) * LANES)
            qbf_ref[0, :, grp] = qa.astype(BF16)
            kbf_ref[0, :, grp] = ka.astype(BF16)
            vbf_ref[0, :, grp] = va.astype(BF16)


def _in_proj(dec, layer, x, shift, scale, gpre, w_fused, conv_w, conv_state, smb, alog, lng, lnb, ws, bsb,
             pages=()):
    b, L, _ = x.shape
    if dec:
        bb, lt = min(DEC_TILE_IN, b), L
    else:
        bb, lt = 1, min(SEQ_TILE, L)
    grid = (b // bb, L // lt)
    tok = lambda w: pl.BlockSpec((bb, lt, w), lambda i, j: (i, j, 0))
    per_b = lambda r, w: pl.BlockSpec((bb, r, w), lambda i, j: (i, 0, 0))
    in_specs = [tok(D_MODEL), per_b(1, D_MODEL), per_b(1, D_MODEL), _const_spec((1, D_MODEL)),
                pl.BlockSpec((None, D_MODEL, N_FUSED), lambda i, j: (layer, 0, 0), pipeline_mode=pl.Buffered(1)),
                _const_spec((GDN_CONV, 3 * W_A)),
                per_b(GDN_CONV - 1, 3 * W_A), _const_spec((1, LANES)), _const_spec((1, LANES)),
                _const_spec((1, W_B)), _const_spec((1, W_B)), _const_spec(ws.shape), _const_spec(bsb.shape)]
    sds = lambda w, dt=F32: jax.ShapeDtypeStruct((b, L, w), dt)
    out_shape = [sds(3 * W_A), sds(W_A), sds(LANES), sds(W_B),
                 jax.ShapeDtypeStruct((b, GDN_CONV - 1, 3 * W_A), F32)]
    out_specs = [tok(3 * W_A), tok(W_A), tok(LANES), tok(W_B), per_b(GDN_CONV - 1, 3 * W_A)]
    scratch = [pltpu.VMEM((bb, lt + SUBLANES, 3 * W_A), F32)]
    aliases = {}
    if dec:
        out_shape += [sds(W_C), sds(W_C), sds(W_B), sds(W_C)]
        out_specs += [tok(W_C), tok(W_C), tok(W_B), tok(W_C)]
    else:
        n_in = len(in_specs)
        in_specs += [pl.BlockSpec(memory_space=pl.ANY)] * 2
        aliases = {n_in: len(out_shape), n_in + 1: len(out_shape) + 1}
        page_spec = pl.BlockSpec((None, 1, lt // PAGE, W_C, PAGE), lambda i, j: (layer, i, j, 0, 0))
        head_major = pl.BlockSpec((1, LANES, lt), lambda i, j: (i, 0, j))
        out_shape += [jax.ShapeDtypeStruct(p.shape, p.dtype) for p in pages]
        out_specs += [page_spec, page_spec]
        wf = H_C * LANES
        out_shape += [sds(wf, BF16), sds(wf, BF16), sds(wf, BF16), sds(LANES),
                      jax.ShapeDtypeStruct((b, LANES, L), F32)]
        out_specs += [tok(wf), tok(wf), tok(wf), tok(LANES), head_major]
        scratch += [pltpu.VMEM((SUBLANES, LANES), F32)]
    return pl.pallas_call(
        functools.partial(_in_kernel, dec, bb, lt),
        grid=grid, in_specs=in_specs, out_specs=out_specs, out_shape=out_shape,
        scratch_shapes=scratch, input_output_aliases=aliases,
        compiler_params=_params(("arbitrary", "arbitrary")),
        name="in_proj_dec" if dec else "in_proj_seq",
    )(x, shift, scale, gpre, w_fused, conv_w, conv_state, smb, alog, lng, lnb, ws, bsb, *pages)


GDN_GROUP = 256
GDN_GROUPS_PER_STEP = 2
N_PAIR = H_A // 2


def _tri_inverse_m1(lms, blk):
    bf = lambda xs: [x.astype(BF16) for x in xs]
    mm_ = lambda xs, ys: [_dot(x, y) for x, y in zip(xs, ys)]
    dm = [jnp.where(blk, lm, 0.0) for lm in lms]
    nm = [lm - d for lm, d in zip(lms, dm)]
    dmb = bf(dm)
    d2 = mm_(dmb, dmb)
    d2b = bf(d2)
    d4 = mm_(d2b, d2b)
    d4b = bf(d4)
    d8 = mm_(d4b, d4b)
    a1 = [x2 - x1 - c for x2, x1, c in zip(d2, dm, mm_(dmb, d2b))]
    a2 = [x4 + x8 + c for x4, x8, c in zip(d4, d8, mm_(d4b, bf(d8)))]
    et = [x + y + c for x, y, c in zip(a1, a2, mm_(bf(a1), bf(a2)))]
    etb = bf(et)
    mm = [n + c for n, c in zip(nm, mm_(etb, bf(nm)))]
    mmb = bf(mm)
    m2 = mm_(mmb, mmb)
    a3 = [x2 - x1 - c for x2, x1, c in zip(m2, mm, mm_(mmb, bf(m2)))]
    return [x + e + c for x, e, c in zip(a3, et, mm_(bf(a3), etb))]


def _pair_cols(lo, a, c0, c1):
    return jnp.where(lo, a[:, c0:c0 + 1], a[:, c1:c1 + 1])


def _pair_rsqrt_norm(lo, x, scale):
    x2 = x * x
    s_lo = jnp.sum(jnp.where(lo, x2, 0.0), axis=-1, keepdims=True)
    s_hi = jnp.sum(jnp.where(lo, 0.0, x2), axis=-1, keepdims=True)
    return jnp.where(lo, lax.rsqrt(s_lo * scale + EPS), lax.rsqrt(s_hi * scale + EPS))


def _gdn_pre_kernel(T, G, qkv_ref, small_ref, qe_ref, o0_ref, egl_ref, m_ref, bm_ref):
    C = GDN_CHUNK
    row = _iota2((T, T), 0)
    col = _iota2((T, T), 1)
    same = (row // C) == (col // C)
    bd_tri = same & (row >= col)
    bd_strict = same & (row > col)
    blk = (row // INV_BLOCK) == (col // INV_BLOCK)
    lane = _iota2((1, LANES), 1)
    lo = lane < HEAD_DIM
    lo2 = (_iota2((1, 2 * LANES), 1) % LANES) < HEAD_DIM
    bd2 = (_iota2((LANES, 2 * LANES), 0) // HEAD_DIM) == ((_iota2((LANES, 2 * LANES), 1) % LANES) // HEAD_DIM)

    rhs, rhs_b, lms, attns, qins, kouts = [], [], [], [], [], []
    for g, j in [(g, j) for g in range(G) for j in range(N_PAIR)]:
        tok = slice(g * T, (g + 1) * T)
        if j == 0:
            sm = small_ref[0, tok, :]
            gc_all = _sel_dot(bd_tri, sm)
            gl_all = jnp.concatenate(
                [jnp.broadcast_to(gc_all[(c + 1) * C - 1:(c + 1) * C, :], (C, LANES)) for c in range(T // C)],
                axis=0)
            gc_t = gc_all.T
        sl = slice(j * LANES, (j + 1) * LANES)
        qp = qkv_ref[0, tok, j * LANES:(j + 1) * LANES]
        kp = qkv_ref[0, tok, W_A + j * LANES:W_A + (j + 1) * LANES]
        vp = qkv_ref[0, tok, 2 * W_A + j * LANES:2 * W_A + (j + 1) * LANES]
        qn = qp * _pair_rsqrt_norm(lo, qp, 1.0) * (HEAD_DIM ** -0.5)
        kn = kp * _pair_rsqrt_norm(lo, kp, 1.0)
        h0, h1 = 2 * j, 2 * j + 1
        beta_p = _pair_cols(lo, sm, SM_BETA + h0, SM_BETA + h1)
        gc_p = _pair_cols(lo, gc_all, SM_G + h0, SM_G + h1)
        gl_p = _pair_cols(lo, gl_all, SM_G + h0, SM_G + h1)
        eg_p = jnp.exp(gc_p)
        kb = kn * beta_p
        kn_bf = kn.astype(BF16)
        rhs.append(jnp.concatenate([kb * eg_p, vp * beta_p], axis=1))
        rhs_b.append(rhs[-1].astype(BF16))
        qins.append(qn * eg_p)
        kouts.append((kn * jnp.exp(gl_p - gc_p)).astype(BF16))
        egl_ref[0, tok, sl] = jnp.exp(gl_p)
        zero = jnp.zeros_like(kb)
        stacked = jnp.concatenate([jnp.where(lo, kb, zero), jnp.where(lo, zero, kb),
                                   jnp.where(lo, qn, zero), jnp.where(lo, zero, qn)], axis=0).astype(BF16)
        gq = _dot_nt(stacked, kn_bf)
        for hh in range(2):
            h = 2 * j + hh
            gcol = gc_all[:, SM_G + h:SM_G + h + 1]
            grow = gc_t[SM_G + h:SM_G + h + 1, :]
            decay = jnp.where(bd_tri, jnp.exp(jnp.where(bd_tri, gcol - grow, 0.0)), 0.0)
            lms.append(jnp.where(bd_strict, gq[hh * T:(hh + 1) * T] * decay, 0.0))
            attns.append((gq[(2 + hh) * T:(3 + hh) * T] * decay).astype(BF16))

    tm1 = _tri_inverse_m1(lms, blk)
    tok_chunk = _iota2((T, LANES), 0) // C
    for g, j in [(g, j) for g in range(G) for j in range(N_PAIR)]:
        tok = slice(g * T, (g + 1) * T)
        p = g * N_PAIR + j
        sl = slice(j * LANES, (j + 1) * LANES)
        both = _dot(jnp.concatenate([tm1[2 * p], tm1[2 * p + 1]], axis=0).astype(BF16), rhs_b[p])
        wu = rhs[p] + jnp.where(lo2, both[:T], both[T:])
        wu_b = wu.astype(BF16)
        both = _dot(jnp.concatenate([attns[2 * p], attns[2 * p + 1]], axis=0), wu_b)
        aw = jnp.where(lo2, both[:T], both[T:])
        qe_ref[0, tok, sl] = (qins[p] - aw[:, :LANES]).astype(BF16)
        o0_ref[0, tok, sl] = aw[:, LANES:]
        ko = kouts[p]
        spread = jnp.concatenate([jnp.where(tok_chunk == c, ko, jnp.zeros_like(ko)) for c in range(T // C)],
                                 axis=1)
        mb_all = _dot_tn(spread, wu_b)
        for c in range(T // C):
            mb = jnp.where(bd2, mb_all[c * LANES:(c + 1) * LANES], 0.0)
            m_ref[0, g * (T // C) + c, j] = mb[:, :LANES].astype(BF16)
            bm_ref[0, g * (T // C) + c, j] = mb[:, LANES:]


def _gdn_pre(qkva, small):
    b, L, _ = qkva.shape
    T = min(GDN_GROUP, L)
    G = max(1, min(GDN_GROUPS_PER_STEP, L // T))
    nc = G * T // GDN_CHUNK
    tok = lambda w: pl.BlockSpec((1, G * T, w), lambda i, j: (i, j, 0))
    per_chunk = pl.BlockSpec((1, nc, N_PAIR, LANES, LANES), lambda i, j: (i, j, 0, 0, 0))
    sds = lambda dt: jax.ShapeDtypeStruct((b, L, W_A), dt)
    chunk_sds = lambda dt: jax.ShapeDtypeStruct((b, L // GDN_CHUNK, N_PAIR, LANES, LANES), dt)
    return pl.pallas_call(
        functools.partial(_gdn_pre_kernel, T, G),
        grid=(b, L // (G * T)),
        in_specs=[tok(3 * W_A), tok(LANES)],
        out_specs=[tok(W_A)] * 3 + [per_chunk] * 2,
        out_shape=[sds(BF16), sds(F32), sds(F32), chunk_sds(BF16), chunk_sds(F32)],
        compiler_params=_params(("arbitrary", "arbitrary")),
        name="gdn_pre",
    )(qkva, small)


def _gdn_scan_kernel(nb, lt, qe_ref, o0_ref, egl_ref, m_ref, bm_ref, gate_ref, ng_ref, s0_ref,
                     oa_ref, sout_ref, s_sc):
    l = pl.program_id(0)
    C = GDN_CHUNK

    @pl.when(l == 0)
    def _():
        s_sc[...] = s0_ref[...]

    lane = _iota2((1, LANES), 1)
    lo = lane < HEAD_DIM

    def body(c, carry):
        r0 = pl.multiple_of(c * C, C)
        rows = pl.ds(r0, C)
        for b in range(nb):
            for j in range(N_PAIR):
                sl = slice(j * LANES, (j + 1) * LANES)
                s_old = s_sc[b, j]
                lhs = jnp.concatenate([m_ref[b, c, j], qe_ref[b, rows, sl]], axis=0)
                r = _dot(lhs, s_old.astype(BF16))
                s_sc[b, j] = s_old * egl_ref[b, pl.ds(r0, 1), sl] + (bm_ref[b, c, j] - r[:LANES])
                o = r[LANES:] + o0_ref[b, rows, sl]
                on = o * _pair_rsqrt_norm(lo, o, 1.0 / HEAD_DIM)
                oa_ref[b, rows, sl] = on * ng_ref[:, sl] * gate_ref[b, rows, sl]
        return carry

    lax.fori_loop(0, lt // C, body, 0)
    sout_ref[...] = s_sc[...]


def _gdn_scan(qe, o0, egl, m, bm, gate, ng, s0_bd):
    b, L, _ = qe.shape
    lt = min(SEQ_TILE, L)
    tok = pl.BlockSpec((b, lt, W_A), lambda i: (0, i, 0))
    per_chunk = pl.BlockSpec((b, lt // GDN_CHUNK, N_PAIR, LANES, LANES), lambda i: (0, i, 0, 0, 0))
    st = pl.BlockSpec((b, N_PAIR, LANES, LANES), lambda i: (0, 0, 0, 0))
    return pl.pallas_call(
        functools.partial(_gdn_scan_kernel, b, lt),
        grid=(L // lt,),
        in_specs=[tok, tok, tok, per_chunk, per_chunk, tok, _const_spec((1, W_A)), st],
        out_specs=[tok, st],
        out_shape=[jax.ShapeDtypeStruct((b, L, W_A), F32),
                   jax.ShapeDtypeStruct((b, N_PAIR, LANES, LANES), F32)],
        scratch_shapes=[pltpu.VMEM((b, N_PAIR, LANES, LANES), F32)],
        compiler_params=_params(("arbitrary",)),
        name="gdn_scan",
    )(qe, o0, egl, m, bm, gate, ng, s0_bd)


def _to_pair_blockdiag(s):
    b = s.shape[0]
    s = s.reshape(b, N_PAIR, 2, HEAD_DIM, HEAD_DIM)
    z = jnp.zeros_like(s[:, :, 0])
    top = jnp.concatenate([s[:, :, 0], z], axis=-1)
    bot = jnp.concatenate([z, s[:, :, 1]], axis=-1)
    return jnp.concatenate([top, bot], axis=-2)


def _from_pair_blockdiag(sbd):
    b = sbd.shape[0]
    s = jnp.stack([sbd[:, :, :HEAD_DIM, :HEAD_DIM], sbd[:, :, HEAD_DIM:, HEAD_DIM:]], axis=2)
    return s.reshape(b, H_A, HEAD_DIM, HEAD_DIM)


def _gdn_seq(qkva, small, gate, ng, s0):
    qe, o0, egl, m, bm = _gdn_pre(qkva, small)
    oa, sbd = _gdn_scan(qe, o0, egl, m, bm, gate, ng, _to_pair_blockdiag(s0))
    return oa, _from_pair_blockdiag(sbd)


def _gdn_dec_kernel(L, q_ref, k_ref, v_ref, smt_ref, gate_ref, ngb_ref, s0_ref, oa_ref, sout_ref, kq_sc):
    h = pl.program_id(0)
    nb = q_ref.shape[-1]
    for t in range(L):
        q_t = q_ref[t]
        k_t = k_ref[t]
        q_t = q_t * (lax.rsqrt(jnp.sum(q_t * q_t, axis=0, keepdims=True) + EPS) * (HEAD_DIM ** -0.5))
        k_t = k_t * lax.rsqrt(jnp.sum(k_t * k_t, axis=0, keepdims=True) + EPS)
        kq_sc[0] = k_t
        kq_sc[1] = q_t
        beta = smt_ref[t, pl.ds(SM_BETA + h, 1), :]
        a = jnp.exp(smt_ref[t, pl.ds(SM_G + h, 1), :])
        src = s0_ref if t == 0 else sout_ref

        def k_dot_s(kk, acc):
            return acc + src[0, kk] * kq_sc[0, pl.ds(kk, 1), :]

        ks = lax.fori_loop(0, HEAD_DIM, k_dot_s, jnp.zeros((HEAD_DIM, nb), F32), unroll=8)
        delta = beta * (v_ref[t] - a * ks)

        def update(kk, acc):
            s_new = a * src[0, kk] + kq_sc[0, pl.ds(kk, 1), :] * delta
            sout_ref[0, kk] = s_new
            return acc + s_new * kq_sc[1, pl.ds(kk, 1), :]

        o = lax.fori_loop(0, HEAD_DIM, update, jnp.zeros((HEAD_DIM, nb), F32), unroll=8)
        on = o * lax.rsqrt(jnp.mean(o * o, axis=0, keepdims=True) + EPS)
        oa_ref[t] = on * ngb_ref[...] * gate_ref[t]


def _gdn_dec(layer, qkv_t, small_t, gate_t, ngb, s0_t):
    L, _, nb = qkv_t.shape
    nh = W_A // HEAD_DIM
    head = lambda off: pl.BlockSpec((L, HEAD_DIM, nb), lambda h: (0, off + h, 0))
    st = pl.BlockSpec((1, HEAD_DIM, HEAD_DIM, nb), lambda h: (h, 0, 0, 0))
    st_in = pl.BlockSpec((None, 1, HEAD_DIM, HEAD_DIM, nb), lambda h: (layer, h, 0, 0, 0))
    return pl.pallas_call(
        functools.partial(_gdn_dec_kernel, L),
        grid=(H_A,),
        in_specs=[head(0), head(nh), head(2 * nh),
                  pl.BlockSpec((L, LANES, nb), lambda h: (0, 0, 0), pipeline_mode=pl.Buffered(1)),
                  head(0), pl.BlockSpec((HEAD_DIM, nb), lambda h: (h, 0)), st_in],
        out_specs=[head(0), st],
        out_shape=[jax.ShapeDtypeStruct((L, W_A, nb), F32),
                   jax.ShapeDtypeStruct((H_A, HEAD_DIM, HEAD_DIM, nb), F32)],
        scratch_shapes=[pltpu.VMEM((2, HEAD_DIM, nb), F32)],
        compiler_params=_params(("arbitrary",)),
        name="gdn_dec",
    )(qkv_t, qkv_t, qkv_t, small_t, gate_t, ngb, s0_t)


FOX_STRIP = 64
FOX_TQ = 512
FOX_TK = 512


def _fox_seq_kernel(tq, tk, q_ref, k_ref, v_ref, dq_ref, o_ref, m_sc, acc_sc, dqb_sc, s_sc, p_sc, al_sc):
    qi = pl.program_id(1)
    ndiag = tq // tk
    m_sc[...] = jnp.full_like(m_sc, NEG_INF)
    acc_sc[...] = jnp.zeros_like(acc_sc)
    dq = dq_ref[0] * LOG2E
    for h in range(H_C):
        dqb_sc[h] = jnp.broadcast_to(dq[:, SM_LF + h:SM_LF + h + 1], (tq, LANES))

    def step(ki, diag):
        k0 = pl.multiple_of(ki * tk, tk)
        r_lo = 0 if diag is None else diag * tk
        live = slice(r_lo, tq)

        def scores(h):
            grp = slice(h * LANES, (h + 1) * LANES)
            s_sc[h % 2, live, :] = _dot_nt(q_ref[0, live, grp], k_ref[0, pl.ds(k0, tk), grp])

        def accumulate(h):
            grp = slice(h * LANES, (h + 1) * LANES)
            acc_sc[h, live, :] = (al_sc[h % 2, live, :] * acc_sc[h, live, :]
                                  + _dot(p_sc[h % 2, live, :], v_ref[0, pl.ds(k0, tk), grp]))

        scores(0)
        for h in range(H_C):
            buf = h % 2
            if h + 1 < H_C:
                scores(h + 1)
            for r in range(r_lo // FOX_STRIP, tq // FOX_STRIP):
                rows = slice(r * FOX_STRIP, (r + 1) * FOX_STRIP)
                nc = tk if diag is None else min(tk, -(-((r + 1) * FOX_STRIP - r_lo) // LANES) * LANES)
                s = s_sc[buf, rows, 0:nc]
                if diag is not None:
                    keep = _iota2((FOX_STRIP, nc), 1) <= _iota2((FOX_STRIP, nc), 0) + (r * FOX_STRIP - r_lo)
                    s = jnp.where(keep, s, NEG_INF)
                dqb = dqb_sc[h, rows, :]
                m_old = m_sc[h, rows, :]
                m_new = jnp.maximum(m_old, jnp.max(s, axis=-1, keepdims=True) + dqb)
                shift = m_new - dqb
                if nc > LANES:
                    shift = jnp.concatenate([shift] * (nc // LANES), axis=1)
                m_sc[h, rows, :] = m_new
                al_sc[buf, rows, :] = jnp.exp2(m_old - m_new)
                p_sc[buf, rows, 0:nc] = jnp.exp2(s - shift).astype(BF16)
                if nc < tk:
                    p_sc[buf, rows, nc:tk] = jnp.zeros((FOX_STRIP, tk - nc), BF16)
            if h > 0:
                accumulate(h - 1)
        accumulate(H_C - 1)

    def body(ki, carry):
        step(ki, None)
        return carry

    lax.fori_loop(0, ndiag * qi, body, 0)
    for d in range(ndiag):
        step(ndiag * qi + d, d)
    for j in range(H_C // 2):
        halves = []
        for h in (2 * j, 2 * j + 1):
            acc = acc_sc[h]
            halves.append(acc[:, :HEAD_DIM] / acc[:, HEAD_DIM:HEAD_DIM + 1])
        o_ref[0, :, j * LANES:(j + 1) * LANES] = jnp.concatenate(halves, axis=1)


def _fox_seq(q_aug, k_aug, v_aug, dcum):
    b, L, wf = q_aug.shape
    tk = min(FOX_TK, L)
    tq = min(FOX_TQ, L)
    full = pl.BlockSpec((1, L, wf), lambda i, j: (i, 0, 0), pipeline_mode=pl.Buffered(1))
    return pl.pallas_call(
        functools.partial(_fox_seq_kernel, tq, tk),
        grid=(b, L // tq),
        in_specs=[pl.BlockSpec((1, tq, wf), lambda i, j: (i, j, 0)), full, full,
                  pl.BlockSpec((1, tq, LANES), lambda i, j: (i, j, 0))],
        out_specs=pl.BlockSpec((1, tq, W_C), lambda i, j: (i, j, 0)),
        out_shape=jax.ShapeDtypeStruct((b, L, W_C), F32),
        scratch_shapes=[pltpu.VMEM((H_C, tq, LANES), F32), pltpu.VMEM((H_C, tq, LANES), F32),
                        pltpu.VMEM((H_C, tq, LANES), F32),
                        pltpu.VMEM((2, tq, tk), F32), pltpu.VMEM((2, tq, tk), BF16),
                        pltpu.VMEM((2, tq, LANES), F32)],
        compiler_params=_params(("arbitrary", "arbitrary")),
        name="fox_seq",
    )(q_aug, k_aug, v_aug, dcum)


def _fox_dec_kernel(L, n_pages, pt_ref, q_ref, kn_ref, vn_ref, sm_ref, lf_ref, *refs):
    k_refs = refs[0:n_pages]
    v_refs = refs[n_pages:2 * n_pages]
    o_ref = refs[2 * n_pages]
    x_sc = refs[2 * n_pages + 1]
    i = pl.program_id(0)
    R = H_C * L
    q = q_ref[0] * (HEAD_DIM ** -0.5)
    rowh = _iota2((R, W_C), 0) // L
    colh = _iota2((R, W_C), 1) // HEAD_DIM
    qbd = jnp.where(rowh == colh, jnp.concatenate([q] * H_C, axis=0), 0.0).astype(BF16)

    for p in range(n_pages):
        pg = pt_ref[i * n_pages + p]
        for h in range(H_C):
            r = h * n_pages + p
            x_sc[r:r + 1, :] = lf_ref[h, pl.ds(pg, 1), :]
    x = x_sc[...]
    n = H_C * n_pages
    later = (_iota2((PAGE, PAGE), 0) > _iota2((PAGE, PAGE), 1)).astype(F32)
    within = _dot_sel(x, later)
    tot = _dot_sel(x, jnp.ones((PAGE, PAGE), F32))
    ri = _iota2((n, n), 0)
    ci = _iota2((n, n), 1)
    later_pages = ((ci // n_pages == ri // n_pages) & (ci % n_pages > ri % n_pages)).astype(F32)
    rsum = within + _sel_dot(later_pages, tot)

    sm = sm_ref[0]
    tri = (_iota2((L, L), 0) >= _iota2((L, L), 1)).astype(F32)
    cq = _sel_dot(tri, sm)
    cq_t = cq.T
    cq_col = jnp.concatenate([cq[:, SM_LF + h:SM_LF + h + 1] for h in range(H_C)], axis=0)
    cq_row = jnp.concatenate([jnp.broadcast_to(cq_t[SM_LF + h:SM_LF + h + 1, :], (L, L))
                              for h in range(H_C)], axis=0)

    s_pages = []
    for p in range(n_pages):
        sp = _dot(qbd, k_refs[p][...].astype(BF16))
        bias = jnp.concatenate(
            [jnp.broadcast_to(rsum[h * n_pages + p:h * n_pages + p + 1, :], (L, PAGE)) for h in range(H_C)],
            axis=0)
        s_pages.append(sp + bias + cq_col)
    s_new = _dot_nt(qbd, kn_ref[0].astype(BF16)) + cq_col - cq_row
    qpos = _iota2((R, L), 0) % L
    s_new = jnp.where(_iota2((R, L), 1) <= qpos, s_new, NEG_INF)

    mx = jnp.max(s_new, axis=-1, keepdims=True)
    for sp in s_pages:
        mx = jnp.maximum(mx, jnp.max(sp, axis=-1, keepdims=True))
    p_new = jnp.exp(s_new - mx)
    den = jnp.sum(p_new, axis=-1, keepdims=True)
    acc = _dot(p_new.astype(BF16), vn_ref[0].astype(BF16))
    for p in range(n_pages):
        pp = jnp.exp(s_pages[p] - mx)
        den = den + jnp.sum(pp, axis=-1, keepdims=True)
        acc = acc + _dot_nt(pp.astype(BF16), v_refs[p][...].astype(BF16))
    acc = jnp.where(rowh == colh, acc / den, 0.0)
    out = acc[0:L]
    for h in range(1, H_C):
        out = out + acc[h * L:(h + 1) * L]
    o_ref[0] = out


def _fox_dec(layer, page_table, q, kn, vn, small, cache_kt, cache_vt, cache_lft):
    b, L, _ = q.shape
    n_pages = page_table.shape[1]
    n_pool = cache_kt.shape[1]
    tok = lambda w: pl.BlockSpec((1, L, w), lambda i, pt: (i, 0, 0))

    def page_spec(p):
        return pl.BlockSpec((None, None, W_C, PAGE), lambda i, pt: (layer, pt[i * n_pages + p], 0, 0))

    in_specs = [tok(W_C), tok(W_C), tok(W_C), tok(LANES),
                pl.BlockSpec((None, H_C, n_pool, PAGE), lambda i, pt: (layer, 0, 0, 0),
                             pipeline_mode=pl.Buffered(1))]
    in_specs += [page_spec(p) for p in range(n_pages)]
    in_specs += [page_spec(p) for p in range(n_pages)]
    grid_spec = pltpu.PrefetchScalarGridSpec(
        num_scalar_prefetch=1, grid=(b,), in_specs=in_specs, out_specs=tok(W_C),
        scratch_shapes=[pltpu.VMEM((H_C * n_pages, PAGE), F32)])
    return pl.pallas_call(
        functools.partial(_fox_dec_kernel, L, n_pages),
        grid_spec=grid_spec,
        out_shape=jax.ShapeDtypeStruct((b, L, W_C), F32),
        compiler_params=_params(("arbitrary",)),
        name="fox_dec",
    )(page_table.reshape(-1), q, kn, vn, small, cache_lft,
      *([cache_kt] * n_pages), *([cache_vt] * n_pages))


def _post_kernel(bb, lt, x_ref, oa_ref, ob_ref, oc_ref, gate1_ref, shift2_ref, scale2_ref, gate2_ref,
                 gpm_ref, gpf_ref, gqf_ref, wo_ref, wu_ref, cw_ref, cst_ref, wd_ref,
                 y_ref, tail_ref, ext_ref, car_ref):
    l = pl.program_id(1)
    m = bb * lt
    keep = FFN_CONV - 1

    def rms(v, g_ref):
        return v * lax.rsqrt(jnp.mean(v * v, axis=-1, keepdims=True) + EPS) * g_ref[...]

    mixed = jnp.concatenate([oa_ref[...].reshape(m, W_A), ob_ref[...].reshape(m, W_B),
                             oc_ref[...].reshape(m, W_C)], axis=1).astype(BF16)
    o = _dot(mixed, wo_ref[...])
    x1 = x_ref[...] + gate1_ref[...] * rms(o, gpm_ref).reshape(bb, lt, D_MODEL)

    h = rms(x1, gpf_ref) * (1.0 + scale2_ref[...]) + shift2_ref[...]
    h2 = h.reshape(m, D_MODEL).astype(BF16)

    @pl.when(l == 0)
    def _():
        car_ref[...] = jnp.zeros_like(car_ref)
        car_ref[:, SUBLANES - keep:SUBLANES, :] = cst_ref[...]

    w = FF_CHUNK
    nchunk = D_FF // FF_CHUNK

    def project(slot, c0):
        up = _dot(h2, wu_ref[:, c0:c0 + w]).reshape(bb, lt, w)
        ext_ref[slot, :, 0:SUBLANES, :] = car_ref[:, :, c0:c0 + w]
        ext_ref[slot, :, SUBLANES:, :] = up
        car_ref[:, :, c0:c0 + w] = ext_ref[slot, :, lt:lt + SUBLANES, :]

    def conv(slot, c0):
        y = None
        for i in range(FFN_CONV):
            off = SUBLANES - keep + i
            term = ext_ref[slot, :, off:off + lt, :] * cw_ref[i:i + 1, c0:c0 + w]
            y = term if y is None else y + term
        return y

    project(0, 0)
    project(1, D_FF)
    y = jnp.zeros((m, D_MODEL), F32)
    for j in range(nchunk):
        if j + 1 < nchunk:
            project(2 * (j + 1), (j + 1) * w)
            project(2 * (j + 1) + 1, D_FF + (j + 1) * w)
        a = conv(2 * j, j * w)
        b = conv(2 * j + 1, D_FF + j * w)
        g = (jax.nn.gelu(a) * b).reshape(m, w).astype(BF16)
        y = y + _dot(g, wd_ref[j * w:(j + 1) * w, :])
    tail_ref[...] = car_ref[:, SUBLANES - keep:SUBLANES, :]
    y_ref[...] = x1 + gate2_ref[...] * rms(y, gqf_ref).reshape(bb, lt, D_MODEL)


def _post(dec, layer, x, oa, ob, oc, gate1, shift2, scale2, gate2, gpm, gpf, gqf, wo, wu, cw, cst, wd):
    b, L, _ = x.shape
    if dec:
        bb, lt = min(DEC_TILE_POST, b), L
    else:
        bb, lt = 1, min(SEQ_TILE, L)
    tok = lambda w: pl.BlockSpec((bb, lt, w), lambda i, j: (i, j, 0))
    per_b = lambda r, w: pl.BlockSpec((bb, r, w), lambda i, j: (i, 0, 0))
    vec = _const_spec((1, D_MODEL))
    weight = lambda r, c: pl.BlockSpec((None, r, c), lambda i, j: (layer, 0, 0), pipeline_mode=pl.Buffered(1))
    in_specs = [tok(D_MODEL), tok(W_A), tok(W_B), tok(W_C), per_b(1, D_MODEL), per_b(1, D_MODEL),
                per_b(1, D_MODEL), per_b(1, D_MODEL), vec, vec, vec,
                weight(W_A + W_B + W_C, D_MODEL), weight(D_MODEL, 2 * D_FF),
                _const_spec((FFN_CONV, 2 * D_FF)), per_b(FFN_CONV - 1, 2 * D_FF),
                weight(D_FF, D_MODEL)]
    return pl.pallas_call(
        functools.partial(_post_kernel, bb, lt),
        grid=(b // bb, L // lt),
        in_specs=in_specs,
        out_specs=[tok(D_MODEL), per_b(FFN_CONV - 1, 2 * D_FF)],
        out_shape=[jax.ShapeDtypeStruct((b, L, D_MODEL), F32),
                   jax.ShapeDtypeStruct((b, FFN_CONV - 1, 2 * D_FF), F32)],
        scratch_shapes=[pltpu.VMEM((2 * D_FF // FF_CHUNK, bb, lt + SUBLANES, FF_CHUNK), F32),
                        pltpu.VMEM((bb, SUBLANES, 2 * D_FF), F32)],
        compiler_params=_params(("arbitrary", "arbitrary")),
        name="post_dec" if dec else "post_seq",
    )(x, oa, ob, oc, gate1, shift2, scale2, gate2, gpm, gpf, gqf, wo, wu, cw, cst, wd)


def _layer_params(l, gdn_A_log, gdn_dt_bias, fox_f_bias, chunk_w_s, chunk_b_s, dec_len):
    z = lambda n: jnp.zeros((n,), F32)
    smb = jnp.concatenate([z(H_A), gdn_dt_bias[l], fox_f_bias[l], z(LANES - 2 * H_A - H_C)]).reshape(1, LANES)
    alog = jnp.concatenate([z(H_A), gdn_A_log[l], z(LANES - 2 * H_A)]).reshape(1, LANES)
    ws = chunk_w_s[l]
    bs = chunk_b_s[l]
    ws_seq = ws.reshape(G_B * CHUNK, CHUNK)
    bsb_seq = jnp.repeat(bs.T, HEAD_DIM, axis=1)
    ws_dec = jnp.repeat(jnp.transpose(ws[:, :dec_len, :dec_len], (2, 1, 0)), HEAD_DIM, axis=2)
    bsb_dec = bsb_seq[:dec_len]
    return smb, alog, ws_seq, bsb_seq, ws_dec, bsb_dec


def kernel(x_prompt, x_sample, state_gdn_conv, state_gdn_S, cache_fox_k, cache_fox_v, cache_fox_logf,
           state_ffn_conv, page_table, c_prompt, c_sample, w_ada, b_ada, g_pre_mix, g_post_mix, g_pre_ffn,
           g_post_ffn, w_in, w_out, gdn_conv_w, gdn_A_log, gdn_dt_bias, gdn_norm_g, chunk_ln_g, chunk_ln_b,
           chunk_w_s, chunk_b_s, fox_f_bias, w_up, ffn_conv_w, w_down):
    depth = w_in.shape[0]
    bp, lp, _ = x_prompt.shape
    bs, ls, _ = x_sample.shape
    n_pool = cache_fox_k.shape[1]

    cache_kt = jnp.transpose(cache_fox_k, (0, 1, 3, 4, 2)).reshape(depth, n_pool, W_C, PAGE)
    cache_vt = jnp.transpose(cache_fox_v, (0, 1, 3, 4, 2)).reshape(depth, n_pool, W_C, PAGE)
    cache_lft = jnp.transpose(cache_fox_logf, (0, 3, 1, 2))
    state_s_lanes = jnp.transpose(state_gdn_S, (0, 2, 3, 4, 1))

    c_all = jnp.concatenate([c_prompt, c_sample], axis=0)
    pad = (-c_all.shape[0]) % SUBLANES
    c_all = jnp.pad(c_all, ((0, pad), (0, 0)))

    w_fused = _w_in_prep(w_in)
    pages = tuple(jnp.zeros((depth, bp, lp // PAGE, W_C, PAGE), F32) for _ in range(2))

    vec = lambda a: a.reshape(1, -1)
    outs = {k: [] for k in ("p_conv", "p_S", "p_lf", "p_ffn",
                            "s_conv", "s_S", "s_k", "s_v", "s_lf", "s_cv", "s_ffn")}
    xp, xs = x_prompt, x_sample
    mod_all = _ada(c_all, w_ada, b_ada)
    wo = w_out.astype(BF16)
    wu = w_up.astype(BF16)
    wd = w_down.astype(BF16)
    for l in range(depth):
        mod_p = mod_all[l, :bp].reshape(bp, 6, 1, D_MODEL)
        mod_s = mod_all[l, bp:bp + bs].reshape(bs, 6, 1, D_MODEL)
        smb, alog, ws_seq, bsb_seq, ws_dec, bsb_dec = _layer_params(
            l, gdn_A_log, gdn_dt_bias, fox_f_bias, chunk_w_s, chunk_b_s, ls)
        ng = jnp.tile(gdn_norm_g[l], H_A).reshape(1, W_A)
        common_in = (vec(g_pre_mix[l]), w_fused, gdn_conv_w[l])
        common_b = (smb, alog, vec(chunk_ln_g[l]), vec(chunk_ln_b[l]))
        post_w = (vec(g_post_mix[l]), vec(g_pre_ffn[l]), vec(g_post_ffn[l]), wo, wu, ffn_conv_w[l])

        (qkva, gate, small, ob, tail, pk, pv, qbf, kbf, vbf, dcum, small_t) = _in_proj(
            False, l, xp, mod_p[:, 0], mod_p[:, 1], *common_in,
            jnp.zeros((bp, GDN_CONV - 1, 3 * W_A), F32), *common_b, ws_seq, bsb_seq, pages=pages)
        pages = (pk, pv)
        oa, s_new = _gdn_seq(qkva, small, gate, ng, jnp.zeros((bp, H_A, HEAD_DIM, HEAD_DIM), F32))
        oc = _fox_seq(qbf, kbf, vbf, dcum)
        xp, ftail = _post(False, l, xp, oa, ob, oc, mod_p[:, 2], mod_p[:, 3], mod_p[:, 4], mod_p[:, 5],
                          *post_w, jnp.zeros((bp, FFN_CONV - 1, 2 * D_FF), F32), wd)
        outs["p_conv"].append(tail)
        outs["p_S"].append(s_new)
        lf_hm = small_t[:, SM_LF:SM_LF + H_C, :].reshape(bp, H_C, lp // PAGE, PAGE)
        outs["p_lf"].append(jnp.transpose(lf_hm, (0, 2, 3, 1)))
        outs["p_ffn"].append(ftail)

        (qkva, gate, small, ob, tail, kc, vc, vb, qd) = _in_proj(
            True, l, xs, mod_s[:, 0], mod_s[:, 1], *common_in, state_gdn_conv[l], *common_b, ws_dec, bsb_dec)
        to_lanes = lambda a: jnp.transpose(a, (1, 2, 0))
        oa_t, s_new_t = _gdn_dec(l, to_lanes(qkva), to_lanes(small), to_lanes(gate),
                                 jnp.broadcast_to(ng.reshape(W_A, 1), (W_A, bs)), state_s_lanes)
        oa = jnp.transpose(oa_t, (2, 0, 1))
        s_new = jnp.transpose(s_new_t, (3, 0, 1, 2))
        oc = _fox_dec(l, page_table, qd, kc, vc, small, cache_kt, cache_vt, cache_lft)
        xs, ftail = _post(True, l, xs, oa, ob, oc, mod_s[:, 2], mod_s[:, 3], mod_s[:, 4], mod_s[:, 5],
                          *post_w, state_ffn_conv[l], wd)
        outs["s_conv"].append(tail)
        outs["s_S"].append(s_new)
        outs["s_k"].append(kc.reshape(bs, ls, H_C, HEAD_DIM))
        outs["s_v"].append(vc.reshape(bs, ls, H_C, HEAD_DIM))
        outs["s_lf"].append(small[:, :, SM_LF:SM_LF + H_C])
        outs["s_cv"].append(vb)
        outs["s_ffn"].append(ftail)

    st = lambda k: jnp.stack(outs[k])
    from_pages = lambda p: jnp.transpose(
        p.reshape(depth, bp, lp // PAGE, H_C, HEAD_DIM, PAGE), (0, 1, 2, 5, 3, 4))
    return (xp, xs, st("p_conv"), st("p_S"), from_pages(pages[0]), from_pages(pages[1]), st("p_lf"), st("p_ffn"),
            st("s_conv"), st("s_S"), st("s_k"), st("s_v"), st("s_lf"), st("s_cv"), st("s_ffn"))
```

```python
import functools

import jax
import jax.numpy as jnp
from jax import lax
from jax.experimental import pallas as pl
from jax.experimental.pallas import tpu as pltpu

F32 = jnp.float32
BF16 = jnp.bfloat16

D_MODEL = 1024
HEAD_DIM = 64
H_A = 6
G_B = 4
H_C = 6
W_A = H_A * HEAD_DIM
W_B = G_B * HEAD_DIM
W_C = H_C * HEAD_DIM
GDN_CONV = 4
GDN_CHUNK = 64
CHUNK = 128
D_FF = 2816
FFN_CONV = 3
EPS = 1e-6
NEG_INF = -1e30
PAGE = 128
LOG2E = 1.4426950408889634

LANES = 128
SUBLANES = 8
VMEM_LIMIT = 56 * 1024 * 1024

QA_OFF = 0
GA_OFF = QA_OFF + 3 * W_A
UV_OFF = GA_OFF + W_A
QC_OFF = UV_OFF + 2 * W_B
SM_OFF = QC_OFF + 3 * W_C
N_FUSED = SM_OFF + LANES
SM_BETA = 0
SM_G = H_A
SM_LF = 2 * H_A

FF_CHUNK = D_FF

SEQ_TILE = 512
DEC_TILE_IN = 32
DEC_TILE_POST = 16
ADA_TILE = 1024
INV_BLOCK = 16


def _silu(x):
    return x * jax.nn.sigmoid(x)


def _softplus(x):
    return jnp.maximum(x, 0.0) + jnp.log1p(jnp.exp(-jnp.abs(x)))


def _dot(a, b):
    return jnp.dot(a, b, preferred_element_type=F32)


def _split3(x):
    t1 = x.astype(BF16)
    r1 = x - t1.astype(F32)
    t2 = r1.astype(BF16)
    t3 = (r1 - t2.astype(F32)).astype(BF16)
    return t1, t2, t3


def _sel_dot(sel, x):
    s = sel.astype(BF16)
    t1, t2, t3 = _split3(x)
    return _dot(s, t1) + (_dot(s, t2) + _dot(s, t3))


def _dot_sel(x, sel):
    s = sel.astype(BF16)
    t1, t2, t3 = _split3(x)
    return _dot(t1, s) + (_dot(t2, s) + _dot(t3, s))


def _dot_nt(a, b):
    return lax.dot_general(a, b, (((1,), (1,)), ((), ())), preferred_element_type=F32)


def _dot_tn(a, b):
    return lax.dot_general(a, b, (((0,), (0,)), ((), ())), preferred_element_type=F32)


def _iota2(shape, dim):
    return lax.broadcasted_iota(jnp.int32, shape, dim)


def _params(sem):
    return pltpu.CompilerParams(dimension_semantics=sem, vmem_limit_bytes=VMEM_LIMIT)


def _const_spec(shape):
    nd = len(shape)
    return pl.BlockSpec(shape, lambda *_: (0,) * nd, pipeline_mode=pl.Buffered(1))


def _ada_kernel(c_ref, w_ref, b_ref, o_ref):
    c = _silu(c_ref[...]).astype(BF16)
    o_ref[...] = _dot(c, w_ref[...].astype(BF16)) + b_ref[...]


def _ada(c, w, b):
    m = c.shape[0]
    depth, _, n = w.shape
    tn = ADA_TILE
    return pl.pallas_call(
        _ada_kernel,
        grid=(depth, n // tn),
        in_specs=[pl.BlockSpec((m, D_MODEL), lambda l, j: (0, 0)),
                  pl.BlockSpec((None, D_MODEL, tn), lambda l, j: (l, 0, j)),
                  pl.BlockSpec((None, 1, tn), lambda l, j: (l, 0, j))],
        out_specs=pl.BlockSpec((None, m, tn), lambda l, j: (l, 0, j)),
        out_shape=jax.ShapeDtypeStruct((depth, m, n), F32),
        compiler_params=_params(("arbitrary", "arbitrary")),
        name="ada_mod",
    )(c, w, b.reshape(depth, 1, n))


def _w_in_prep_kernel(depth, a_ref, o_ref):
    l = pl.program_id(0)
    j = pl.program_id(1)
    nkt = D_MODEL // LANES
    rpn = nkt * depth
    n_regular = SM_OFF // LANES
    n_first = GA_OFF // LANES
    n_in = SM_OFF + 2 * H_A + H_C

    @pl.when(j < n_regular)
    def _():
        n0 = j * LANES + jnp.where(j >= n_first, 2 * H_A, 0)
        for kt in range(nkt):
            x = a_ref[pl.ds(n0 * rpn + kt * depth + l, LANES, stride=rpn), :]
            o_ref[0, kt * LANES:(kt + 1) * LANES, :] = x.T.astype(BF16)

    @pl.when(j == n_regular)
    def _():
        r1, r2 = 2 * SUBLANES, SUBLANES
        n2 = n_in - r2
        r = _iota2((r1 + r2, LANES), 0)
        m = _iota2((r1 + r2, LANES), 1)
        pick = ((m < 2 * H_A) & (r == m)) | (
            (m >= 2 * H_A) & (m < 2 * H_A + H_C) & (r == m - 2 * H_A + r1 + (r2 - H_C)))
        sel = pick.astype(BF16)
        for kt in range(nkt):
            x1 = a_ref[pl.ds(GA_OFF * rpn + kt * depth + l, r1, stride=rpn), :]
            x2 = a_ref[pl.ds(n2 * rpn + kt * depth + l, r2, stride=rpn), :]
            xc = jnp.concatenate([x1, x2], axis=0).astype(BF16)
            o_ref[0, kt * LANES:(kt + 1) * LANES, :] = _dot_tn(xc, sel).astype(BF16)


def _w_in_prep(w_in):
    depth, d, n_in = w_in.shape
    nkt = d // LANES
    view = w_in.reshape(depth, nkt, LANES, n_in).transpose(3, 1, 0, 2).reshape(n_in * nkt * depth, LANES)
    return pl.pallas_call(
        functools.partial(_w_in_prep_kernel, depth),
        grid=(depth, N_FUSED // LANES),
        in_specs=[pl.BlockSpec(view.shape, lambda l, j: (0, 0), pipeline_mode=pl.Buffered(1))],
        out_specs=pl.BlockSpec((1, d, LANES), lambda l, j: (l, 0, j)),
        out_shape=jax.ShapeDtypeStruct((depth, d, N_FUSED), BF16),
        compiler_params=_params(("arbitrary", "arbitrary")),
        name="w_in_prep",
    )(view)


def _in_kernel(dec, bb, lt, *refs):
    (x_ref, shift_ref, scale_ref, gpre_ref, w_ref, cw_ref, cst_ref, smb_ref, alog_ref,
     lng_ref, lnb_ref, ws_ref, bsb_ref) = refs[:13]
    if dec:
        (qkva_ref, gate_ref, small_ref, ob_ref, tail_ref, kc_ref, vc_ref, vb_ref, q_ref,
         ext_ref) = refs[13:]
    else:
        (_, _, qkva_ref, gate_ref, small_ref, ob_ref, tail_ref, pk_ref, pv_ref, qbf_ref, kbf_ref,
         vbf_ref, dcum_ref, smt_ref, ext_ref, dcar_ref) = refs[13:]
    l = pl.program_id(1)
    m = bb * lt

    x = x_ref[...]
    ms = jnp.mean(x * x, axis=-1, keepdims=True)
    h = x * lax.rsqrt(ms + EPS) * gpre_ref[...]
    h = h * (1.0 + scale_ref[...]) + shift_ref[...]
    h2 = h.reshape(m, D_MODEL).astype(BF16)

    wa = 3 * W_A
    za = _dot(h2, w_ref[:, QA_OFF:QA_OFF + wa]).reshape(bb, lt, wa)

    @pl.when(l == 0)
    def _():
        ext_ref[:, SUBLANES - (GDN_CONV - 1):SUBLANES, :] = cst_ref[...]

    @pl.when(l > 0)
    def _():
        ext_ref[:, 0:SUBLANES, :] = ext_ref[:, lt:lt + SUBLANES, :]

    ext_ref[:, SUBLANES:, :] = za
    y = za * cw_ref[GDN_CONV - 1:GDN_CONV, :]
    for i in range(GDN_CONV - 1):
        o = SUBLANES - (GDN_CONV - 1) + i
        y = y + ext_ref[:, o:o + lt, :] * cw_ref[i:i + 1, :]
    qkva_ref[...] = _silu(y)
    tail_ref[...] = ext_ref[:, lt + SUBLANES - (GDN_CONV - 1):lt + SUBLANES, :]

    gate_ref[...] = _silu(_dot(h2, w_ref[:, GA_OFF:GA_OFF + W_A])).reshape(bb, lt, W_A)

    zs = _dot(h2, w_ref[:, SM_OFF:SM_OFF + LANES]) + smb_ref[...]
    lane = _iota2((1, LANES), 1)
    beta = jax.nn.sigmoid(zs)
    gval = -jnp.exp(alog_ref[...]) * _softplus(zs)
    lf = -_softplus(-zs)
    small = jnp.where(lane < SM_G, beta,
                      jnp.where(lane < SM_LF, gval, jnp.where(lane < SM_LF + H_C, lf, 0.0)))
    small_ref[...] = small.reshape(bb, lt, LANES)

    if not dec:
        tri = (_iota2((CHUNK, CHUNK), 0) >= _iota2((CHUNK, CHUNK), 1)).astype(F32)

        @pl.when(l == 0)
        def _():
            dcar_ref[...] = jnp.zeros_like(dcar_ref)

        run = dcar_ref[0:1, :]
        parts = []
        for c in range(lt // CHUNK):
            parts.append(_sel_dot(tri, small[c * CHUNK:(c + 1) * CHUNK]) + run)
            run = parts[-1][CHUNK - 1:CHUNK, :]
        dc = jnp.concatenate(parts, axis=0)
        dcum_ref[...] = dc.reshape(bb, lt, LANES)
        dcar_ref[0:1, :] = dc[lt - 1:lt, :]
        smt_ref[0] = small.T

    uv = jax.nn.gelu(_dot(h2, w_ref[:, UV_OFF:UV_OFF + 2 * W_B]))
    u = uv[:, :W_B]
    v = uv[:, W_B:]
    mu = jnp.mean(v, axis=-1, keepdims=True)
    vc0 = v - mu
    var = jnp.mean(vc0 * vc0, axis=-1, keepdims=True)
    vb = vc0 * lax.rsqrt(var + EPS) * lng_ref[...] + lnb_ref[...]
    if dec:
        vb3 = vb.reshape(bb, lt, W_B)
        vb_ref[...] = vb3
        trow = _iota2((lt, W_B), 0)
        mix = jnp.zeros((bb, lt, W_B), F32)
        for s in range(lt):
            coef = jnp.where(trow >= s, ws_ref[s], 0.0)
            mix = mix + coef[None] * vb3[:, s:s + 1, :]
        ob_ref[...] = u.reshape(bb, lt, W_B) * (mix + bsb_ref[...][None])
    else:
        r_t = _iota2((G_B * CHUNK, CHUNK), 0) % CHUNK
        r_s = _iota2((G_B * CHUNK, CHUNK), 1)
        wst = jnp.where(r_s <= r_t, ws_ref[...], 0.0).astype(BF16)
        grp = _iota2((1, W_B), 1) // HEAD_DIM
        for c in range(lt // CHUNK):
            rows = slice(c * CHUNK, (c + 1) * CHUNK)
            r = _dot(wst, vb[rows].astype(BF16))
            mix = jnp.zeros((CHUNK, W_B), F32)
            for g in range(G_B):
                mix = jnp.where(grp == g, r[g * CHUNK:(g + 1) * CHUNK], mix)
            ob_ref[0, rows, :] = u[rows] * (mix + bsb_ref[...])

    zc = _dot(h2, w_ref[:, QC_OFF:QC_OFF + 3 * W_C])
    qc = zc[:, 0:W_C]
    kc = zc[:, W_C:2 * W_C]
    vc = zc[:, 2 * W_C:3 * W_C]
    if dec:
        kc_ref[...] = kc.reshape(bb, lt, W_C)
        vc_ref[...] = vc.reshape(bb, lt, W_C)
        q_ref[...] = qc.reshape(bb, lt, W_C)
    else:
        for c in range(lt // PAGE):
            rows = slice(c * PAGE, (c + 1) * PAGE)
            pk_ref[0, c] = kc[rows].T
            pv_ref[0, c] = vc[rows].T
        hd = HEAD_DIM
        d2 = dc * LOG2E
        for h in range(H_C):
            pair = slice((h // 2) * LANES, (h // 2 + 1) * LANES)
            at0 = (lambda a: a) if h % 2 == 0 else (lambda a: pltpu.roll(a, hd, axis=1))
            dh = d2[:, SM_LF + h:SM_LF + h + 1]
            t1 = dh.astype(BF16).astype(F32)
            t2 = (dh - t1).astype(BF16).astype(F32)
            t3 = dh - t1 - t2
            qa = jnp.where(lane < hd, at0(qc[:, pair]) * (LOG2E * hd ** -0.5), jnp.where(lane < hd + 3, 1.0, 0.0))
            ka = jnp.where(lane < hd, at0(kc[:, pair]),
                           jnp.where(lane == hd, -t1, jnp.where(lane == hd + 1, -t2,
                                                                jnp.where(lane == hd + 2, -t3, 0.0))))
            va = jnp.where(lane < hd, at0(vc[:, pair]), jnp.where(lane == hd, 1.0, 0.0))
            grp = slice(h * LANES, (h + 1) * LANES)
            qbf_ref[0, :, grp] = qa.astype(BF16)
            kbf_ref[0, :, grp] = ka.astype(BF16)
            vbf_ref[0, :, grp] = va.astype(BF16)


def _in_proj(dec, layer, x, shift, scale, gpre, w_fused, conv_w, conv_state, smb, alog, lng, lnb, ws, bsb,
             pages=()):
    b, L, _ = x.shape
    if dec:
        bb, lt = min(DEC_TILE_IN, b), L
    else:
        bb, lt = 1, min(SEQ_TILE, L)
    grid = (b // bb, L // lt)
    tok = lambda w: pl.BlockSpec((bb, lt, w), lambda i, j: (i, j, 0))
    per_b = lambda r, w: pl.BlockSpec((bb, r, w), lambda i, j: (i, 0, 0))
    in_specs = [tok(D_MODEL), per_b(1, D_MODEL), per_b(1, D_MODEL), _const_spec((1, D_MODEL)),
                pl.BlockSpec((None, D_MODEL, N_FUSED), lambda i, j: (layer, 0, 0), pipeline_mode=pl.Buffered(1)),
                _const_spec((GDN_CONV, 3 * W_A)),
                per_b(GDN_CONV - 1, 3 * W_A), _const_spec((1, LANES)), _const_spec((1, LANES)),
                _const_spec((1, W_B)), _const_spec((1, W_B)), _const_spec(ws.shape), _const_spec(bsb.shape)]
    sds = lambda w, dt=F32: jax.ShapeDtypeStruct((b, L, w), dt)
    out_shape = [sds(3 * W_A), sds(W_A), sds(LANES), sds(W_B),
                 jax.ShapeDtypeStruct((b, GDN_CONV - 1, 3 * W_A), F32)]
    out_specs = [tok(3 * W_A), tok(W_A), tok(LANES), tok(W_B), per_b(GDN_CONV - 1, 3 * W_A)]
    scratch = [pltpu.VMEM((bb, lt + SUBLANES, 3 * W_A), F32)]
    aliases = {}
    if dec:
        out_shape += [sds(W_C), sds(W_C), sds(W_B), sds(W_C)]
        out_specs += [tok(W_C), tok(W_C), tok(W_B), tok(W_C)]
    else:
        n_in = len(in_specs)
        in_specs += [pl.BlockSpec(memory_space=pl.ANY)] * 2
        aliases = {n_in: len(out_shape), n_in + 1: len(out_shape) + 1}
        page_spec = pl.BlockSpec((None, 1, lt // PAGE, W_C, PAGE), lambda i, j: (layer, i, j, 0, 0))
        head_major = pl.BlockSpec((1, LANES, lt), lambda i, j: (i, 0, j))
        out_shape += [jax.ShapeDtypeStruct(p.shape, p.dtype) for p in pages]
        out_specs += [page_spec, page_spec]
        wf = H_C * LANES
        out_shape += [sds(wf, BF16), sds(wf, BF16), sds(wf, BF16), sds(LANES),
                      jax.ShapeDtypeStruct((b, LANES, L), F32)]
        out_specs += [tok(wf), tok(wf), tok(wf), tok(LANES), head_major]
        scratch += [pltpu.VMEM((SUBLANES, LANES), F32)]
    return pl.pallas_call(
        functools.partial(_in_kernel, dec, bb, lt),
        grid=grid, in_specs=in_specs, out_specs=out_specs, out_shape=out_shape,
        scratch_shapes=scratch, input_output_aliases=aliases,
        compiler_params=_params(("arbitrary", "arbitrary")),
        name="in_proj_dec" if dec else "in_proj_seq",
    )(x, shift, scale, gpre, w_fused, conv_w, conv_state, smb, alog, lng, lnb, ws, bsb, *pages)


GDN_GROUP = 128
GDN_GROUPS_PER_STEP = 4
N_PAIR = H_A // 2


def _tri_inverse_m1(lms, blk):
    bf = lambda xs: [x.astype(BF16) for x in xs]
    mm_ = lambda xs, ys: [_dot(x, y) for x, y in zip(xs, ys)]
    dm = [jnp.where(blk, lm, 0.0) for lm in lms]
    nm = [lm - d for lm, d in zip(lms, dm)]
    dmb = bf(dm)
    d2 = mm_(dmb, dmb)
    d2b = bf(d2)
    d4 = mm_(d2b, d2b)
    d4b = bf(d4)
    d8 = mm_(d4b, d4b)
    a1 = [x2 - x1 - c for x2, x1, c in zip(d2, dm, mm_(dmb, d2b))]
    a2 = [x4 + x8 + c for x4, x8, c in zip(d4, d8, mm_(d4b, bf(d8)))]
    et = [x + y + c for x, y, c in zip(a1, a2, mm_(bf(a1), bf(a2)))]
    etb = bf(et)
    mm = [n + c for n, c in zip(nm, mm_(etb, bf(nm)))]
    mmb = bf(mm)
    m2 = mm_(mmb, mmb)
    a3 = [x2 - x1 - c for x2, x1, c in zip(m2, mm, mm_(mmb, bf(m2)))]
    return [x + e + c for x, e, c in zip(a3, et, mm_(bf(a3), etb))]


def _pair_cols(lo, a, c0, c1):
    return jnp.where(lo, a[:, c0:c0 + 1], a[:, c1:c1 + 1])


def _pair_rsqrt_norm(lo, x, scale):
    x2 = x * x
    s_lo = jnp.sum(jnp.where(lo, x2, 0.0), axis=-1, keepdims=True)
    s_hi = jnp.sum(jnp.where(lo, 0.0, x2), axis=-1, keepdims=True)
    return jnp.where(lo, lax.rsqrt(s_lo * scale + EPS), lax.rsqrt(s_hi * scale + EPS))


def _gdn_pre_kernel(T, G, qkv_ref, small_ref, qe_ref, o0_ref, egl_ref, m_ref, bm_ref):
    C = GDN_CHUNK
    row = _iota2((T, T), 0)
    col = _iota2((T, T), 1)
    same = (row // C) == (col // C)
    bd_tri = same & (row >= col)
    bd_strict = same & (row > col)
    blk = (row // INV_BLOCK) == (col // INV_BLOCK)
    lane = _iota2((1, LANES), 1)
    lo = lane < HEAD_DIM
    lo2 = (_iota2((1, 2 * LANES), 1) % LANES) < HEAD_DIM
    bd2 = (_iota2((LANES, 2 * LANES), 0) // HEAD_DIM) == ((_iota2((LANES, 2 * LANES), 1) % LANES) // HEAD_DIM)

    rhs, rhs_b, lms, attns, qins, kouts = [], [], [], [], [], []
    for g, j in [(g, j) for g in range(G) for j in range(N_PAIR)]:
        tok = slice(g * T, (g + 1) * T)
        if j == 0:
            sm = small_ref[0, tok, :]
            gc_all = _sel_dot(bd_tri, sm)
            gl_all = jnp.concatenate(
                [jnp.broadcast_to(gc_all[(c + 1) * C - 1:(c + 1) * C, :], (C, LANES)) for c in range(T // C)],
                axis=0)
            gc_t = gc_all.T
        sl = slice(j * LANES, (j + 1) * LANES)
        qp = qkv_ref[0, tok, j * LANES:(j + 1) * LANES]
        kp = qkv_ref[0, tok, W_A + j * LANES:W_A + (j + 1) * LANES]
        vp = qkv_ref[0, tok, 2 * W_A + j * LANES:2 * W_A + (j + 1) * LANES]
        qn = qp * _pair_rsqrt_norm(lo, qp, 1.0) * (HEAD_DIM ** -0.5)
        kn = kp * _pair_rsqrt_norm(lo, kp, 1.0)
        h0, h1 = 2 * j, 2 * j + 1
        beta_p = _pair_cols(lo, sm, SM_BETA + h0, SM_BETA + h1)
        gc_p = _pair_cols(lo, gc_all, SM_G + h0, SM_G + h1)
        gl_p = _pair_cols(lo, gl_all, SM_G + h0, SM_G + h1)
        eg_p = jnp.exp(gc_p)
        kb = kn * beta_p
        kn_bf = kn.astype(BF16)
        rhs.append(jnp.concatenate([kb * eg_p, vp * beta_p], axis=1))
        rhs_b.append(rhs[-1].astype(BF16))
        qins.append(qn * eg_p)
        kouts.append((kn * jnp.exp(gl_p - gc_p)).astype(BF16))
        egl_ref[0, tok, sl] = jnp.exp(gl_p)
        zero = jnp.zeros_like(kb)
        stacked = jnp.concatenate([jnp.where(lo, kb, zero), jnp.where(lo, zero, kb),
                                   jnp.where(lo, qn, zero), jnp.where(lo, zero, qn)], axis=0).astype(BF16)
        gq = _dot_nt(stacked, kn_bf)
        for hh in range(2):
            h = 2 * j + hh
            gcol = gc_all[:, SM_G + h:SM_G + h + 1]
            grow = gc_t[SM_G + h:SM_G + h + 1, :]
            decay = jnp.where(bd_tri, jnp.exp(jnp.where(bd_tri, gcol - grow, 0.0)), 0.0)
            lms.append(jnp.where(bd_strict, gq[hh * T:(hh + 1) * T] * decay, 0.0))
            attns.append((gq[(2 + hh) * T:(3 + hh) * T] * decay).astype(BF16))

    tm1 = _tri_inverse_m1(lms, blk)
    tok_chunk = _iota2((T, LANES), 0) // C
    for g, j in [(g, j) for g in range(G) for j in range(N_PAIR)]:
        tok = slice(g * T, (g + 1) * T)
        p = g * N_PAIR + j
        sl = slice(j * LANES, (j + 1) * LANES)
        both = _dot(jnp.concatenate([tm1[2 * p], tm1[2 * p + 1]], axis=0).astype(BF16), rhs_b[p])
        wu = rhs[p] + jnp.where(lo2, both[:T], both[T:])
        wu_b = wu.astype(BF16)
        both = _dot(jnp.concatenate([attns[2 * p], attns[2 * p + 1]], axis=0), wu_b)
        aw = jnp.where(lo2, both[:T], both[T:])
        qe_ref[0, tok, sl] = (qins[p] - aw[:, :LANES]).astype(BF16)
        o0_ref[0, tok, sl] = aw[:, LANES:]
        ko = kouts[p]
        spread = jnp.concatenate([jnp.where(tok_chunk == c, ko, jnp.zeros_like(ko)) for c in range(T // C)],
                                 axis=1)
        mb_all = _dot_tn(spread, wu_b)
        for c in range(T // C):
            mb = jnp.where(bd2, mb_all[c * LANES:(c + 1) * LANES], 0.0)
            m_ref[0, g * (T // C) + c, j] = mb[:, :LANES].astype(BF16)
            bm_ref[0, g * (T // C) + c, j] = mb[:, LANES:]


def _gdn_pre(qkva, small):
    b, L, _ = qkva.shape
    T = min(GDN_GROUP, L)
    G = max(1, min(GDN_GROUPS_PER_STEP, L // T))
    nc = G * T // GDN_CHUNK
    tok = lambda w: pl.BlockSpec((1, G * T, w), lambda i, j: (i, j, 0))
    per_chunk = pl.BlockSpec((1, nc, N_PAIR, LANES, LANES), lambda i, j: (i, j, 0, 0, 0))
    sds = lambda dt: jax.ShapeDtypeStruct((b, L, W_A), dt)
    chunk_sds = lambda dt: jax.ShapeDtypeStruct((b, L // GDN_CHUNK, N_PAIR, LANES, LANES), dt)
    return pl.pallas_call(
        functools.partial(_gdn_pre_kernel, T, G),
        grid=(b, L // (G * T)),
        in_specs=[tok(3 * W_A), tok(LANES)],
        out_specs=[tok(W_A)] * 3 + [per_chunk] * 2,
        out_shape=[sds(BF16), sds(F32), sds(F32), chunk_sds(BF16), chunk_sds(F32)],
        compiler_params=_params(("arbitrary", "arbitrary")),
        name="gdn_pre",
    )(qkva, small)


def _gdn_scan_kernel(nb, lt, qe_ref, o0_ref, egl_ref, m_ref, bm_ref, gate_ref, ng_ref, s0_ref,
                     oa_ref, sout_ref, s_sc):
    l = pl.program_id(0)
    C = GDN_CHUNK

    @pl.when(l == 0)
    def _():
        s_sc[...] = s0_ref[...]

    lane = _iota2((1, LANES), 1)
    lo = lane < HEAD_DIM

    def body(c, carry):
        r0 = pl.multiple_of(c * C, C)
        rows = pl.ds(r0, C)
        for b in range(nb):
            for j in range(N_PAIR):
                sl = slice(j * LANES, (j + 1) * LANES)
                s_old = s_sc[b, j]
                lhs = jnp.concatenate([m_ref[b, c, j], qe_ref[b, rows, sl]], axis=0)
                r = _dot(lhs, s_old.astype(BF16))
                s_sc[b, j] = s_old * egl_ref[b, pl.ds(r0, 1), sl] + (bm_ref[b, c, j] - r[:LANES])
                o = r[LANES:] + o0_ref[b, rows, sl]
                on = o * _pair_rsqrt_norm(lo, o, 1.0 / HEAD_DIM)
                oa_ref[b, rows, sl] = on * ng_ref[:, sl] * gate_ref[b, rows, sl]
        return carry

    lax.fori_loop(0, lt // C, body, 0)
    sout_ref[...] = s_sc[...]


def _gdn_scan(qe, o0, egl, m, bm, gate, ng, s0_bd):
    b, L, _ = qe.shape
    lt = min(SEQ_TILE, L)
    tok = pl.BlockSpec((b, lt, W_A), lambda i: (0, i, 0))
    per_chunk = pl.BlockSpec((b, lt // GDN_CHUNK, N_PAIR, LANES, LANES), lambda i: (0, i, 0, 0, 0))
    st = pl.BlockSpec((b, N_PAIR, LANES, LANES), lambda i: (0, 0, 0, 0))
    return pl.pallas_call(
        functools.partial(_gdn_scan_kernel, b, lt),
        grid=(L // lt,),
        in_specs=[tok, tok, tok, per_chunk, per_chunk, tok, _const_spec((1, W_A)), st],
        out_specs=[tok, st],
        out_shape=[jax.ShapeDtypeStruct((b, L, W_A), F32),
                   jax.ShapeDtypeStruct((b, N_PAIR, LANES, LANES), F32)],
        scratch_shapes=[pltpu.VMEM((b, N_PAIR, LANES, LANES), F32)],
        compiler_params=_params(("arbitrary",)),
        name="gdn_scan",
    )(qe, o0, egl, m, bm, gate, ng, s0_bd)


def _to_pair_blockdiag(s):
    b = s.shape[0]
    s = s.reshape(b, N_PAIR, 2, HEAD_DIM, HEAD_DIM)
    z = jnp.zeros_like(s[:, :, 0])
    top = jnp.concatenate([s[:, :, 0], z], axis=-1)
    bot = jnp.concatenate([z, s[:, :, 1]], axis=-1)
    return jnp.concatenate([top, bot], axis=-2)


def _from_pair_blockdiag(sbd):
    b = sbd.shape[0]
    s = jnp.stack([sbd[:, :, :HEAD_DIM, :HEAD_DIM], sbd[:, :, HEAD_DIM:, HEAD_DIM:]], axis=2)
    return s.reshape(b, H_A, HEAD_DIM, HEAD_DIM)


def _gdn_seq(qkva, small, gate, ng, s0):
    qe, o0, egl, m, bm = _gdn_pre(qkva, small)
    oa, sbd = _gdn_scan(qe, o0, egl, m, bm, gate, ng, _to_pair_blockdiag(s0))
    return oa, _from_pair_blockdiag(sbd)


def _gdn_dec_kernel(L, q_ref, k_ref, v_ref, smt_ref, gate_ref, ngb_ref, s0_ref, oa_ref, sout_ref, kq_sc):
    h = pl.program_id(0)
    nb = q_ref.shape[-1]
    for t in range(L):
        q_t = q_ref[t]
        k_t = k_ref[t]
        q_t = q_t * (lax.rsqrt(jnp.sum(q_t * q_t, axis=0, keepdims=True) + EPS) * (HEAD_DIM ** -0.5))
        k_t = k_t * lax.rsqrt(jnp.sum(k_t * k_t, axis=0, keepdims=True) + EPS)
        kq_sc[0] = k_t
        kq_sc[1] = q_t
        beta = smt_ref[t, pl.ds(SM_BETA + h, 1), :]
        a = jnp.exp(smt_ref[t, pl.ds(SM_G + h, 1), :])
        src = s0_ref if t == 0 else sout_ref

        def k_dot_s(kk, acc):
            return acc + src[0, kk] * kq_sc[0, pl.ds(kk, 1), :]

        ks = lax.fori_loop(0, HEAD_DIM, k_dot_s, jnp.zeros((HEAD_DIM, nb), F32), unroll=8)
        delta = beta * (v_ref[t] - a * ks)

        def update(kk, acc):
            s_new = a * src[0, kk] + kq_sc[0, pl.ds(kk, 1), :] * delta
            sout_ref[0, kk] = s_new
            return acc + s_new * kq_sc[1, pl.ds(kk, 1), :]

        o = lax.fori_loop(0, HEAD_DIM, update, jnp.zeros((HEAD_DIM, nb), F32), unroll=8)
        on = o * lax.rsqrt(jnp.mean(o * o, axis=0, keepdims=True) + EPS)
        oa_ref[t] = on * ngb_ref[...] * gate_ref[t]


def _gdn_dec(layer, qkv_t, small_t, gate_t, ngb, s0_t):
    L, _, nb = qkv_t.shape
    nh = W_A // HEAD_DIM
    head = lambda off: pl.BlockSpec((L, HEAD_DIM, nb), lambda h: (0, off + h, 0))
    st = pl.BlockSpec((1, HEAD_DIM, HEAD_DIM, nb), lambda h: (h, 0, 0, 0))
    st_in = pl.BlockSpec((None, 1, HEAD_DIM, HEAD_DIM, nb), lambda h: (layer, h, 0, 0, 0))
    return pl.pallas_call(
        functools.partial(_gdn_dec_kernel, L),
        grid=(H_A,),
        in_specs=[head(0), head(nh), head(2 * nh),
                  pl.BlockSpec((L, LANES, nb), lambda h: (0, 0, 0), pipeline_mode=pl.Buffered(1)),
                  head(0), pl.BlockSpec((HEAD_DIM, nb), lambda h: (h, 0)), st_in],
        out_specs=[head(0), st],
        out_shape=[jax.ShapeDtypeStruct((L, W_A, nb), F32),
                   jax.ShapeDtypeStruct((H_A, HEAD_DIM, HEAD_DIM, nb), F32)],
        scratch_shapes=[pltpu.VMEM((2, HEAD_DIM, nb), F32)],
        compiler_params=_params(("arbitrary",)),
        name="gdn_dec",
    )(qkv_t, qkv_t, qkv_t, small_t, gate_t, ngb, s0_t)


FOX_STRIP = 64
FOX_TQ = 512
FOX_TK = 512


def _fox_seq_kernel(tq, tk, q_ref, k_ref, v_ref, dq_ref, o_ref, m_sc, acc_sc, dqb_sc, s_sc, p_sc, al_sc):
    qi = pl.program_id(1)
    ndiag = tq // tk
    m_sc[...] = jnp.full_like(m_sc, NEG_INF)
    acc_sc[...] = jnp.zeros_like(acc_sc)
    dq = dq_ref[0] * LOG2E
    for h in range(H_C):
        dqb_sc[h] = jnp.broadcast_to(dq[:, SM_LF + h:SM_LF + h + 1], (tq, LANES))

    def step(ki, diag):
        k0 = pl.multiple_of(ki * tk, tk)
        r_lo = 0 if diag is None else diag * tk
        live = slice(r_lo, tq)

        def scores(h):
            grp = slice(h * LANES, (h + 1) * LANES)
            s_sc[h % 2, live, :] = _dot_nt(q_ref[0, live, grp], k_ref[0, pl.ds(k0, tk), grp])

        def accumulate(h):
            grp = slice(h * LANES, (h + 1) * LANES)
            acc_sc[h, live, :] = (al_sc[h % 2, live, :] * acc_sc[h, live, :]
                                  + _dot(p_sc[h % 2, live, :], v_ref[0, pl.ds(k0, tk), grp]))

        scores(0)
        for h in range(H_C):
            buf = h % 2
            if h + 1 < H_C:
                scores(h + 1)
            for r in range(r_lo // FOX_STRIP, tq // FOX_STRIP):
                rows = slice(r * FOX_STRIP, (r + 1) * FOX_STRIP)
                nc = tk if diag is None else min(tk, -(-((r + 1) * FOX_STRIP - r_lo) // LANES) * LANES)
                s = s_sc[buf, rows, 0:nc]
                if diag is not None:
                    keep = _iota2((FOX_STRIP, nc), 1) <= _iota2((FOX_STRIP, nc), 0) + (r * FOX_STRIP - r_lo)
                    s = jnp.where(keep, s, NEG_INF)
                dqb = dqb_sc[h, rows, :]
                m_old = m_sc[h, rows, :]
                m_new = jnp.maximum(m_old, jnp.max(s, axis=-1, keepdims=True) + dqb)
                shift = m_new - dqb
                if nc > LANES:
                    shift = jnp.concatenate([shift] * (nc // LANES), axis=1)
                m_sc[h, rows, :] = m_new
                al_sc[buf, rows, :] = jnp.exp2(m_old - m_new)
                p_sc[buf, rows, 0:nc] = jnp.exp2(s - shift).astype(BF16)
                if nc < tk:
                    p_sc[buf, rows, nc:tk] = jnp.zeros((FOX_STRIP, tk - nc), BF16)
            if h > 0:
                accumulate(h - 1)
        accumulate(H_C - 1)

    def body(ki, carry):
        step(ki, None)
        return carry

    lax.fori_loop(0, ndiag * qi, body, 0)
    for d in range(ndiag):
        step(ndiag * qi + d, d)
    for j in range(H_C // 2):
        halves = []
        for h in (2 * j, 2 * j + 1):
            acc = acc_sc[h]
            halves.append(acc[:, :HEAD_DIM] / acc[:, HEAD_DIM:HEAD_DIM + 1])
        o_ref[0, :, j * LANES:(j + 1) * LANES] = jnp.concatenate(halves, axis=1)


def _fox_seq(q_aug, k_aug, v_aug, dcum):
    b, L, wf = q_aug.shape
    tk = min(FOX_TK, L)
    tq = min(FOX_TQ, L)
    full = pl.BlockSpec((1, L, wf), lambda i, j: (i, 0, 0), pipeline_mode=pl.Buffered(1))
    return pl.pallas_call(
        functools.partial(_fox_seq_kernel, tq, tk),
        grid=(b, L // tq),
        in_specs=[pl.BlockSpec((1, tq, wf), lambda i, j: (i, j, 0)), full, full,
                  pl.BlockSpec((1, tq, LANES), lambda i, j: (i, j, 0))],
        out_specs=pl.BlockSpec((1, tq, W_C), lambda i, j: (i, j, 0)),
        out_shape=jax.ShapeDtypeStruct((b, L, W_C), F32),
        scratch_shapes=[pltpu.VMEM((H_C, tq, LANES), F32), pltpu.VMEM((H_C, tq, LANES), F32),
                        pltpu.VMEM((H_C, tq, LANES), F32),
                        pltpu.VMEM((2, tq, tk), F32), pltpu.VMEM((2, tq, tk), BF16),
                        pltpu.VMEM((2, tq, LANES), F32)],
        compiler_params=_params(("arbitrary", "arbitrary")),
        name="fox_seq",
    )(q_aug, k_aug, v_aug, dcum)


def _fox_dec_kernel(L, n_pages, pt_ref, q_ref, kn_ref, vn_ref, sm_ref, lf_ref, *refs):
    k_refs = refs[0:n_pages]
    v_refs = refs[n_pages:2 * n_pages]
    o_ref = refs[2 * n_pages]
    x_sc = refs[2 * n_pages + 1]
    i = pl.program_id(0)
    R = H_C * L
    q = q_ref[0] * (HEAD_DIM ** -0.5)
    rowh = _iota2((R, W_C), 0) // L
    colh = _iota2((R, W_C), 1) // HEAD_DIM
    qbd = jnp.where(rowh == colh, jnp.concatenate([q] * H_C, axis=0), 0.0).astype(BF16)

    for p in range(n_pages):
        pg = pt_ref[i * n_pages + p]
        for h in range(H_C):
            r = h * n_pages + p
            x_sc[r:r + 1, :] = lf_ref[h, pl.ds(pg, 1), :]
    x = x_sc[...]
    n = H_C * n_pages
    later = (_iota2((PAGE, PAGE), 0) > _iota2((PAGE, PAGE), 1)).astype(F32)
    within = _dot_sel(x, later)
    tot = _dot_sel(x, jnp.ones((PAGE, PAGE), F32))
    ri = _iota2((n, n), 0)
    ci = _iota2((n, n), 1)
    later_pages = ((ci // n_pages == ri // n_pages) & (ci % n_pages > ri % n_pages)).astype(F32)
    rsum = within + _sel_dot(later_pages, tot)

    sm = sm_ref[0]
    tri = (_iota2((L, L), 0) >= _iota2((L, L), 1)).astype(F32)
    cq = _sel_dot(tri, sm)
    cq_t = cq.T
    cq_col = jnp.concatenate([cq[:, SM_LF + h:SM_LF + h + 1] for h in range(H_C)], axis=0)
    cq_row = jnp.concatenate([jnp.broadcast_to(cq_t[SM_LF + h:SM_LF + h + 1, :], (L, L))
                              for h in range(H_C)], axis=0)

    s_pages = []
    for p in range(n_pages):
        sp = _dot(qbd, k_refs[p][...].astype(BF16))
        bias = jnp.concatenate(
            [jnp.broadcast_to(rsum[h * n_pages + p:h * n_pages + p + 1, :], (L, PAGE)) for h in range(H_C)],
            axis=0)
        s_pages.append(sp + bias + cq_col)
    s_new = _dot_nt(qbd, kn_ref[0].astype(BF16)) + cq_col - cq_row
    qpos = _iota2((R, L), 0) % L
    s_new = jnp.where(_iota2((R, L), 1) <= qpos, s_new, NEG_INF)

    mx = jnp.max(s_new, axis=-1, keepdims=True)
    for sp in s_pages:
        mx = jnp.maximum(mx, jnp.max(sp, axis=-1, keepdims=True))
    p_new = jnp.exp(s_new - mx)
    den = jnp.sum(p_new, axis=-1, keepdims=True)
    acc = _dot(p_new.astype(BF16), vn_ref[0].astype(BF16))
    for p in range(n_pages):
        pp = jnp.exp(s_pages[p] - mx)
        den = den + jnp.sum(pp, axis=-1, keepdims=True)
        acc = acc + _dot_nt(pp.astype(BF16), v_refs[p][...].astype(BF16))
    acc = jnp.where(rowh == colh, acc / den, 0.0)
    out = acc[0:L]
    for h in range(1, H_C):
        out = out + acc[h * L:(h + 1) * L]
    o_ref[0] = out


def _fox_dec(layer, page_table, q, kn, vn, small, cache_kt, cache_vt, cache_lft):
    b, L, _ = q.shape
    n_pages = page_table.shape[1]
    n_pool = cache_kt.shape[1]
    tok = lambda w: pl.BlockSpec((1, L, w), lambda i, pt: (i, 0, 0))

    def page_spec(p):
        return pl.BlockSpec((None, None, W_C, PAGE), lambda i, pt: (layer, pt[i * n_pages + p], 0, 0))

    in_specs = [tok(W_C), tok(W_C), tok(W_C), tok(LANES),
                pl.BlockSpec((None, H_C, n_pool, PAGE), lambda i, pt: (layer, 0, 0, 0),
                             pipeline_mode=pl.Buffered(1))]
    in_specs += [page_spec(p) for p in range(n_pages)]
    in_specs += [page_spec(p) for p in range(n_pages)]
    grid_spec = pltpu.PrefetchScalarGridSpec(
        num_scalar_prefetch=1, grid=(b,), in_specs=in_specs, out_specs=tok(W_C),
        scratch_shapes=[pltpu.VMEM((H_C * n_pages, PAGE), F32)])
    return pl.pallas_call(
        functools.partial(_fox_dec_kernel, L, n_pages),
        grid_spec=grid_spec,
        out_shape=jax.ShapeDtypeStruct((b, L, W_C), F32),
        compiler_params=_params(("arbitrary",)),
        name="fox_dec",
    )(page_table.reshape(-1), q, kn, vn, small, cache_lft,
      *([cache_kt] * n_pages), *([cache_vt] * n_pages))


def _post_kernel(bb, lt, x_ref, oa_ref, ob_ref, oc_ref, gate1_ref, shift2_ref, scale2_ref, gate2_ref,
                 gpm_ref, gpf_ref, gqf_ref, wo_ref, wu_ref, cw_ref, cst_ref, wd_ref,
                 y_ref, tail_ref, ext_ref, car_ref):
    l = pl.program_id(1)
    m = bb * lt
    keep = FFN_CONV - 1

    def rms(v, g_ref):
        return v * lax.rsqrt(jnp.mean(v * v, axis=-1, keepdims=True) + EPS) * g_ref[...]

    mixed = jnp.concatenate([oa_ref[...].reshape(m, W_A), ob_ref[...].reshape(m, W_B),
                             oc_ref[...].reshape(m, W_C)], axis=1).astype(BF16)
    o = _dot(mixed, wo_ref[...])
    x1 = x_ref[...] + gate1_ref[...] * rms(o, gpm_ref).reshape(bb, lt, D_MODEL)

    h = rms(x1, gpf_ref) * (1.0 + scale2_ref[...]) + shift2_ref[...]
    h2 = h.reshape(m, D_MODEL).astype(BF16)

    @pl.when(l == 0)
    def _():
        car_ref[...] = jnp.zeros_like(car_ref)
        car_ref[:, SUBLANES - keep:SUBLANES, :] = cst_ref[...]

    w = FF_CHUNK
    nchunk = D_FF // FF_CHUNK

    def project(slot, c0):
        up = _dot(h2, wu_ref[:, c0:c0 + w]).reshape(bb, lt, w)
        ext_ref[slot, :, 0:SUBLANES, :] = car_ref[:, :, c0:c0 + w]
        ext_ref[slot, :, SUBLANES:, :] = up
        car_ref[:, :, c0:c0 + w] = ext_ref[slot, :, lt:lt + SUBLANES, :]

    def conv(slot, c0):
        y = None
        for i in range(FFN_CONV):
            off = SUBLANES - keep + i
            term = ext_ref[slot, :, off:off + lt, :] * cw_ref[i:i + 1, c0:c0 + w]
            y = term if y is None else y + term
        return y

    project(0, 0)
    project(1, D_FF)
    y = jnp.zeros((m, D_MODEL), F32)
    for j in range(nchunk):
        if j + 1 < nchunk:
            project(2 * (j + 1), (j + 1) * w)
            project(2 * (j + 1) + 1, D_FF + (j + 1) * w)
        a = conv(2 * j, j * w)
        b = conv(2 * j + 1, D_FF + j * w)
        g = (jax.nn.gelu(a) * b).reshape(m, w).astype(BF16)
        y = y + _dot(g, wd_ref[j * w:(j + 1) * w, :])
    tail_ref[...] = car_ref[:, SUBLANES - keep:SUBLANES, :]
    y_ref[...] = x1 + gate2_ref[...] * rms(y, gqf_ref).reshape(bb, lt, D_MODEL)


def _post(dec, layer, x, oa, ob, oc, gate1, shift2, scale2, gate2, gpm, gpf, gqf, wo, wu, cw, cst, wd):
    b, L, _ = x.shape
    if dec:
        bb, lt = min(DEC_TILE_POST, b), L
    else:
        bb, lt = 1, min(SEQ_TILE, L)
    tok = lambda w: pl.BlockSpec((bb, lt, w), lambda i, j: (i, j, 0))
    per_b = lambda r, w: pl.BlockSpec((bb, r, w), lambda i, j: (i, 0, 0))
    vec = _const_spec((1, D_MODEL))
    weight = lambda r, c: pl.BlockSpec((None, r, c), lambda i, j: (layer, 0, 0), pipeline_mode=pl.Buffered(1))
    in_specs = [tok(D_MODEL), tok(W_A), tok(W_B), tok(W_C), per_b(1, D_MODEL), per_b(1, D_MODEL),
                per_b(1, D_MODEL), per_b(1, D_MODEL), vec, vec, vec,
                weight(W_A + W_B + W_C, D_MODEL), weight(D_MODEL, 2 * D_FF),
                _const_spec((FFN_CONV, 2 * D_FF)), per_b(FFN_CONV - 1, 2 * D_FF),
                weight(D_FF, D_MODEL)]
    return pl.pallas_call(
        functools.partial(_post_kernel, bb, lt),
        grid=(b // bb, L // lt),
        in_specs=in_specs,
        out_specs=[tok(D_MODEL), per_b(FFN_CONV - 1, 2 * D_FF)],
        out_shape=[jax.ShapeDtypeStruct((b, L, D_MODEL), F32),
                   jax.ShapeDtypeStruct((b, FFN_CONV - 1, 2 * D_FF), F32)],
        scratch_shapes=[pltpu.VMEM((2 * D_FF // FF_CHUNK, bb, lt + SUBLANES, FF_CHUNK), F32),
                        pltpu.VMEM((bb, SUBLANES, 2 * D_FF), F32)],
        compiler_params=_params(("arbitrary", "arbitrary")),
        name="post_dec" if dec else "post_seq",
    )(x, oa, ob, oc, gate1, shift2, scale2, gate2, gpm, gpf, gqf, wo, wu, cw, cst, wd)


def _layer_params(l, gdn_A_log, gdn_dt_bias, fox_f_bias, chunk_w_s, chunk_b_s, dec_len):
    z = lambda n: jnp.zeros((n,), F32)
    smb = jnp.concatenate([z(H_A), gdn_dt_bias[l], fox_f_bias[l], z(LANES - 2 * H_A - H_C)]).reshape(1, LANES)
    alog = jnp.concatenate([z(H_A), gdn_A_log[l], z(LANES - 2 * H_A)]).reshape(1, LANES)
    ws = chunk_w_s[l]
    bs = chunk_b_s[l]
    ws_seq = ws.reshape(G_B * CHUNK, CHUNK)
    bsb_seq = jnp.repeat(bs.T, HEAD_DIM, axis=1)
    ws_dec = jnp.repeat(jnp.transpose(ws[:, :dec_len, :dec_len], (2, 1, 0)), HEAD_DIM, axis=2)
    bsb_dec = bsb_seq[:dec_len]
    return smb, alog, ws_seq, bsb_seq, ws_dec, bsb_dec


def kernel(x_prompt, x_sample, state_gdn_conv, state_gdn_S, cache_fox_k, cache_fox_v, cache_fox_logf,
           state_ffn_conv, page_table, c_prompt, c_sample, w_ada, b_ada, g_pre_mix, g_post_mix, g_pre_ffn,
           g_post_ffn, w_in, w_out, gdn_conv_w, gdn_A_log, gdn_dt_bias, gdn_norm_g, chunk_ln_g, chunk_ln_b,
           chunk_w_s, chunk_b_s, fox_f_bias, w_up, ffn_conv_w, w_down):
    depth = w_in.shape[0]
    bp, lp, _ = x_prompt.shape
    bs, ls, _ = x_sample.shape
    n_pool = cache_fox_k.shape[1]

    cache_kt = jnp.transpose(cache_fox_k, (0, 1, 3, 4, 2)).reshape(depth, n_pool, W_C, PAGE)
    cache_vt = jnp.transpose(cache_fox_v, (0, 1, 3, 4, 2)).reshape(depth, n_pool, W_C, PAGE)
    cache_lft = jnp.transpose(cache_fox_logf, (0, 3, 1, 2))
    state_s_lanes = jnp.transpose(state_gdn_S, (0, 2, 3, 4, 1))

    c_all = jnp.concatenate([c_prompt, c_sample], axis=0)
    pad = (-c_all.shape[0]) % SUBLANES
    c_all = jnp.pad(c_all, ((0, pad), (0, 0)))

    w_fused = _w_in_prep(w_in)
    pages = tuple(jnp.zeros((depth, bp, lp // PAGE, W_C, PAGE), F32) for _ in range(2))

    vec = lambda a: a.reshape(1, -1)
    outs = {k: [] for k in ("p_conv", "p_S", "p_lf", "p_ffn",
                            "s_conv", "s_S", "s_k", "s_v", "s_lf", "s_cv", "s_ffn")}
    xp, xs = x_prompt, x_sample
    mod_all = _ada(c_all, w_ada, b_ada)
    wo = w_out.astype(BF16)
    wu = w_up.astype(BF16)
    wd = w_down.astype(BF16)
    for l in range(depth):
        mod_p = mod_all[l, :bp].reshape(bp, 6, 1, D_MODEL)
        mod_s = mod_all[l, bp:bp + bs].reshape(bs, 6, 1, D_MODEL)
        smb, alog, ws_seq, bsb_seq, ws_dec, bsb_dec = _layer_params(
            l, gdn_A_log, gdn_dt_bias, fox_f_bias, chunk_w_s, chunk_b_s, ls)
        ng = jnp.tile(gdn_norm_g[l], H_A).reshape(1, W_A)
        common_in = (vec(g_pre_mix[l]), w_fused, gdn_conv_w[l])
        common_b = (smb, alog, vec(chunk_ln_g[l]), vec(chunk_ln_b[l]))
        post_w = (vec(g_post_mix[l]), vec(g_pre_ffn[l]), vec(g_post_ffn[l]), wo, wu, ffn_conv_w[l])

        (qkva, gate, small, ob, tail, pk, pv, qbf, kbf, vbf, dcum, small_t) = _in_proj(
            False, l, xp, mod_p[:, 0], mod_p[:, 1], *common_in,
            jnp.zeros((bp, GDN_CONV - 1, 3 * W_A), F32), *common_b, ws_seq, bsb_seq, pages=pages)
        pages = (pk, pv)
        oa, s_new = _gdn_seq(qkva, small, gate, ng, jnp.zeros((bp, H_A, HEAD_DIM, HEAD_DIM), F32))
        oc = _fox_seq(qbf, kbf, vbf, dcum)
        xp, ftail = _post(False, l, xp, oa, ob, oc, mod_p[:, 2], mod_p[:, 3], mod_p[:, 4], mod_p[:, 5],
                          *post_w, jnp.zeros((bp, FFN_CONV - 1, 2 * D_FF), F32), wd)
        outs["p_conv"].append(tail)
        outs["p_S"].append(s_new)
        lf_hm = small_t[:, SM_LF:SM_LF + H_C, :].reshape(bp, H_C, lp // PAGE, PAGE)
        outs["p_lf"].append(jnp.transpose(lf_hm, (0, 2, 3, 1)))
        outs["p_ffn"].append(ftail)

        (qkva, gate, small, ob, tail, kc, vc, vb, qd) = _in_proj(
            True, l, xs, mod_s[:, 0], mod_s[:, 1], *common_in, state_gdn_conv[l], *common_b, ws_dec, bsb_dec)
        to_lanes = lambda a: jnp.transpose(a, (1, 2, 0))
        oa_t, s_new_t = _gdn_dec(l, to_lanes(qkva), to_lanes(small), to_lanes(gate),
                                 jnp.broadcast_to(ng.reshape(W_A, 1), (W_A, bs)), state_s_lanes)
        oa = jnp.transpose(oa_t, (2, 0, 1))
        s_new = jnp.transpose(s_new_t, (3, 0, 1, 2))
        oc = _fox_dec(l, page_table, qd, kc, vc, small, cache_kt, cache_vt, cache_lft)
        xs, ftail = _post(True, l, xs, oa, ob, oc, mod_s[:, 2], mod_s[:, 3], mod_s[:, 4], mod_s[:, 5],
                          *post_w, state_ffn_conv[l], wd)
        outs["s_conv"].append(tail)
        outs["s_S"].append(s_new)
        outs["s_k"].append(kc.reshape(bs, ls, H_C, HEAD_DIM))
        outs["s_v"].append(vc.reshape(bs, ls, H_C, HEAD_DIM))
        outs["s_lf"].append(small[:, :, SM_LF:SM_LF + H_C])
        outs["s_cv"].append(vb)
        outs["s_ffn"].append(ftail)

    st = lambda k: jnp.stack(outs[k])
    from_pages = lambda p: jnp.transpose(
        p.reshape(depth, bp, lp // PAGE, H_C, HEAD_DIM, PAGE), (0, 1, 2, 5, 3, 4))
    return (xp, xs, st("p_conv"), st("p_S"), from_pages(pages[0]), from_pages(pages[1]), st("p_lf"), st("p_ffn"),
            st("s_conv"), st("s_S"), st("s_k"), st("s_v"), st("s_lf"), st("s_cv"), st("s_ffn"))
```

```python
import functools

import jax
import jax.numpy as jnp
from jax import lax
from jax.experimental import pallas as pl
from jax.experimental.pallas import tpu as pltpu

F32 = jnp.float32
BF16 = jnp.bfloat16

D_MODEL = 1024
HEAD_DIM = 64
H_A = 6
G_B = 4
H_C = 6
W_A = H_A * HEAD_DIM
W_B = G_B * HEAD_DIM
W_C = H_C * HEAD_DIM
GDN_CONV = 4
GDN_CHUNK = 64
CHUNK = 128
D_FF = 2816
FFN_CONV = 3
EPS = 1e-6
NEG_INF = -1e30
PAGE = 128
LOG2E = 1.4426950408889634

LANES = 128
SUBLANES = 8
VMEM_LIMIT = 56 * 1024 * 1024

QA_OFF = 0
GA_OFF = QA_OFF + 3 * W_A
UV_OFF = GA_OFF + W_A
QC_OFF = UV_OFF + 2 * W_B
SM_OFF = QC_OFF + 3 * W_C
N_FUSED = SM_OFF + LANES
SM_BETA = 0
SM_G = H_A
SM_LF = 2 * H_A

FF_CHUNK = D_FF

SEQ_TILE = 512
DEC_TILE_IN = 32
DEC_TILE_POST = 16
ADA_TILE = 1024
INV_BLOCK = 16


def _silu(x):
    return x * jax.nn.sigmoid(x)


def _softplus(x):
    return jnp.maximum(x, 0.0) + jnp.log1p(jnp.exp(-jnp.abs(x)))


def _dot(a, b):
    return jnp.dot(a, b, preferred_element_type=F32)


def _split3(x):
    t1 = x.astype(BF16)
    r1 = x - t1.astype(F32)
    t2 = r1.astype(BF16)
    t3 = (r1 - t2.astype(F32)).astype(BF16)
    return t1, t2, t3


def _sel_dot(sel, x):
    s = sel.astype(BF16)
    t1, t2, t3 = _split3(x)
    return _dot(s, t1) + (_dot(s, t2) + _dot(s, t3))


def _dot_sel(x, sel):
    s = sel.astype(BF16)
    t1, t2, t3 = _split3(x)
    return _dot(t1, s) + (_dot(t2, s) + _dot(t3, s))


def _dot_nt(a, b):
    return lax.dot_general(a, b, (((1,), (1,)), ((), ())), preferred_element_type=F32)


def _dot_tn(a, b):
    return lax.dot_general(a, b, (((0,), (0,)), ((), ())), preferred_element_type=F32)


def _iota2(shape, dim):
    return lax.broadcasted_iota(jnp.int32, shape, dim)


def _params(sem):
    return pltpu.CompilerParams(dimension_semantics=sem, vmem_limit_bytes=VMEM_LIMIT)


def _const_spec(shape):
    nd = len(shape)
    return pl.BlockSpec(shape, lambda *_: (0,) * nd, pipeline_mode=pl.Buffered(1))


def _ada_kernel(c_ref, w_ref, b_ref, o_ref):
    c = _silu(c_ref[...]).astype(BF16)
    o_ref[...] = _dot(c, w_ref[...].astype(BF16)) + b_ref[...]


def _ada(c, w, b):
    m = c.shape[0]
    depth, _, n = w.shape
    tn = ADA_TILE
    return pl.pallas_call(
        _ada_kernel,
        grid=(depth, n // tn),
        in_specs=[pl.BlockSpec((m, D_MODEL), lambda l, j: (0, 0)),
                  pl.BlockSpec((None, D_MODEL, tn), lambda l, j: (l, 0, j)),
                  pl.BlockSpec((None, 1, tn), lambda l, j: (l, 0, j))],
        out_specs=pl.BlockSpec((None, m, tn), lambda l, j: (l, 0, j)),
        out_shape=jax.ShapeDtypeStruct((depth, m, n), F32),
        compiler_params=_params(("arbitrary", "arbitrary")),
        name="ada_mod",
    )(c, w, b.reshape(depth, 1, n))


def _w_in_prep_kernel(depth, a_ref, o_ref):
    l = pl.program_id(0)
    j = pl.program_id(1)
    nkt = D_MODEL // LANES
    rpn = nkt * depth
    n_regular = SM_OFF // LANES
    n_first = GA_OFF // LANES
    n_in = SM_OFF + 2 * H_A + H_C

    @pl.when(j < n_regular)
    def _():
        n0 = j * LANES + jnp.where(j >= n_first, 2 * H_A, 0)
        for kt in range(nkt):
            x = a_ref[pl.ds(n0 * rpn + kt * depth + l, LANES, stride=rpn), :]
            o_ref[0, kt * LANES:(kt + 1) * LANES, :] = x.T.astype(BF16)

    @pl.when(j == n_regular)
    def _():
        r1, r2 = 2 * SUBLANES, SUBLANES
        n2 = n_in - r2
        r = _iota2((r1 + r2, LANES), 0)
        m = _iota2((r1 + r2, LANES), 1)
        pick = ((m < 2 * H_A) & (r == m)) | (
            (m >= 2 * H_A) & (m < 2 * H_A + H_C) & (r == m - 2 * H_A + r1 + (r2 - H_C)))
        sel = pick.astype(BF16)
        for kt in range(nkt):
            x1 = a_ref[pl.ds(GA_OFF * rpn + kt * depth + l, r1, stride=rpn), :]
            x2 = a_ref[pl.ds(n2 * rpn + kt * depth + l, r2, stride=rpn), :]
            xc = jnp.concatenate([x1, x2], axis=0).astype(BF16)
            o_ref[0, kt * LANES:(kt + 1) * LANES, :] = _dot_tn(xc, sel).astype(BF16)


def _w_in_prep(w_in):
    depth, d, n_in = w_in.shape
    nkt = d // LANES
    view = w_in.reshape(depth, nkt, LANES, n_in).transpose(3, 1, 0, 2).reshape(n_in * nkt * depth, LANES)
    return pl.pallas_call(
        functools.partial(_w_in_prep_kernel, depth),
        grid=(depth, N_FUSED // LANES),
        in_specs=[pl.BlockSpec(view.shape, lambda l, j: (0, 0), pipeline_mode=pl.Buffered(1))],
        out_specs=pl.BlockSpec((1, d, LANES), lambda l, j: (l, 0, j)),
        out_shape=jax.ShapeDtypeStruct((depth, d, N_FUSED), BF16),
        compiler_params=_params(("arbitrary", "arbitrary")),
        name="w_in_prep",
    )(view)


def _in_kernel(dec, bb, lt, *refs):
    (x_ref, shift_ref, scale_ref, gpre_ref, w_ref, cw_ref, cst_ref, smb_ref, alog_ref,
     lng_ref, lnb_ref, ws_ref, bsb_ref) = refs[:13]
    if dec:
        (qkva_ref, gate_ref, small_ref, ob_ref, tail_ref, kc_ref, vc_ref, vb_ref, q_ref,
         ext_ref) = refs[13:]
    else:
        (_, _, qkva_ref, gate_ref, small_ref, ob_ref, tail_ref, pk_ref, pv_ref, qbf_ref, kbf_ref,
         vbf_ref, dcum_ref, smt_ref, ext_ref, dcar_ref) = refs[13:]
    l = pl.program_id(1)
    m = bb * lt

    x = x_ref[...]
    ms = jnp.mean(x * x, axis=-1, keepdims=True)
    h = x * lax.rsqrt(ms + EPS) * gpre_ref[...]
    h = h * (1.0 + scale_ref[...]) + shift_ref[...]
    h2 = h.reshape(m, D_MODEL).astype(BF16)

    wa = 3 * W_A
    za = _dot(h2, w_ref[:, QA_OFF:QA_OFF + wa]).reshape(bb, lt, wa)

    @pl.when(l == 0)
    def _():
        ext_ref[:, SUBLANES - (GDN_CONV - 1):SUBLANES, :] = cst_ref[...]

    @pl.when(l > 0)
    def _():
        ext_ref[:, 0:SUBLANES, :] = ext_ref[:, lt:lt + SUBLANES, :]

    ext_ref[:, SUBLANES:, :] = za
    y = za * cw_ref[GDN_CONV - 1:GDN_CONV, :]
    for i in range(GDN_CONV - 1):
        o = SUBLANES - (GDN_CONV - 1) + i
        y = y + ext_ref[:, o:o + lt, :] * cw_ref[i:i + 1, :]
    qkva_ref[...] = _silu(y)
    tail_ref[...] = ext_ref[:, lt + SUBLANES - (GDN_CONV - 1):lt + SUBLANES, :]

    gate_ref[...] = _silu(_dot(h2, w_ref[:, GA_OFF:GA_OFF + W_A])).reshape(bb, lt, W_A)

    zs = _dot(h2, w_ref[:, SM_OFF:SM_OFF + LANES]) + smb_ref[...]
    lane = _iota2((1, LANES), 1)
    beta = jax.nn.sigmoid(zs)
    gval = -jnp.exp(alog_ref[...]) * _softplus(zs)
    lf = -_softplus(-zs)
    small = jnp.where(lane < SM_G, beta,
                      jnp.where(lane < SM_LF, gval, jnp.where(lane < SM_LF + H_C, lf, 0.0)))
    small_ref[...] = small.reshape(bb, lt, LANES)

    if not dec:
        tri = (_iota2((CHUNK, CHUNK), 0) >= _iota2((CHUNK, CHUNK), 1)).astype(F32)

        @pl.when(l == 0)
        def _():
            dcar_ref[...] = jnp.zeros_like(dcar_ref)

        run = dcar_ref[0:1, :]
        parts = []
        for c in range(lt // CHUNK):
            parts.append(_sel_dot(tri, small[c * CHUNK:(c + 1) * CHUNK]) + run)
            run = parts[-1][CHUNK - 1:CHUNK, :]
        dc = jnp.concatenate(parts, axis=0)
        dcum_ref[...] = dc.reshape(bb, lt, LANES)
        dcar_ref[0:1, :] = dc[lt - 1:lt, :]
        smt_ref[0] = small.T

    uv = jax.nn.gelu(_dot(h2, w_ref[:, UV_OFF:UV_OFF + 2 * W_B]))
    u = uv[:, :W_B]
    v = uv[:, W_B:]
    mu = jnp.mean(v, axis=-1, keepdims=True)
    vc0 = v - mu
    var = jnp.mean(vc0 * vc0, axis=-1, keepdims=True)
    vb = vc0 * lax.rsqrt(var + EPS) * lng_ref[...] + lnb_ref[...]
    if dec:
        vb3 = vb.reshape(bb, lt, W_B)
        vb_ref[...] = vb3
        trow = _iota2((lt, W_B), 0)
        mix = jnp.zeros((bb, lt, W_B), F32)
        for s in range(lt):
            coef = jnp.where(trow >= s, ws_ref[s], 0.0)
            mix = mix + coef[None] * vb3[:, s:s + 1, :]
        ob_ref[...] = u.reshape(bb, lt, W_B) * (mix + bsb_ref[...][None])
    else:
        r_t = _iota2((G_B * CHUNK, CHUNK), 0) % CHUNK
        r_s = _iota2((G_B * CHUNK, CHUNK), 1)
        wst = jnp.where(r_s <= r_t, ws_ref[...], 0.0).astype(BF16)
        grp = _iota2((1, W_B), 1) // HEAD_DIM
        for c in range(lt // CHUNK):
            rows = slice(c * CHUNK, (c + 1) * CHUNK)
            r = _dot(wst, vb[rows].astype(BF16))
            mix = jnp.zeros((CHUNK, W_B), F32)
            for g in range(G_B):
                mix = jnp.where(grp == g, r[g * CHUNK:(g + 1) * CHUNK], mix)
            ob_ref[0, rows, :] = u[rows] * (mix + bsb_ref[...])

    zc = _dot(h2, w_ref[:, QC_OFF:QC_OFF + 3 * W_C])
    qc = zc[:, 0:W_C]
    kc = zc[:, W_C:2 * W_C]
    vc = zc[:, 2 * W_C:3 * W_C]
    if dec:
        kc_ref[...] = kc.reshape(bb, lt, W_C)
        vc_ref[...] = vc.reshape(bb, lt, W_C)
        q_ref[...] = qc.reshape(bb, lt, W_C)
    else:
        for c in range(lt // PAGE):
            rows = slice(c * PAGE, (c + 1) * PAGE)
            pk_ref[0, c] = kc[rows].T
            pv_ref[0, c] = vc[rows].T
        hd = HEAD_DIM
        d2 = dc * LOG2E
        for h in range(H_C):
            pair = slice((h // 2) * LANES, (h // 2 + 1) * LANES)
            at0 = (lambda a: a) if h % 2 == 0 else (lambda a: pltpu.roll(a, hd, axis=1))
            dh = d2[:, SM_LF + h:SM_LF + h + 1]
            t1 = dh.astype(BF16).astype(F32)
            t2 = (dh - t1).astype(BF16).astype(F32)
            t3 = dh - t1 - t2
            qa = jnp.where(lane < hd, at0(qc[:, pair]) * (LOG2E * hd ** -0.5), jnp.where(lane < hd + 3, 1.0, 0.0))
            ka = jnp.where(lane < hd, at0(kc[:, pair]),
                           jnp.where(lane == hd, -t1, jnp.where(lane == hd + 1, -t2,
                                                                jnp.where(lane == hd + 2, -t3, 0.0))))
            va = jnp.where(lane < hd, at0(vc[:, pair]), jnp.where(lane == hd, 1.0, 0.0))
            grp = slice(h * LANES, (h + 1) * LANES)
            qbf_ref[0, :, grp] = qa.astype(BF16)
            kbf_ref[0, :, grp] = ka.astype(BF16)
            vbf_ref[0, :, grp] = va.astype(BF16)


def _in_proj(dec, layer, x, shift, scale, gpre, w_fused, conv_w, conv_state, smb, alog, lng, lnb, ws, bsb,
             pages=()):
    b, L, _ = x.shape
    if dec:
        bb, lt = min(DEC_TILE_IN, b), L
    else:
        bb, lt = 1, min(SEQ_TILE, L)
    grid = (b // bb, L // lt)
    tok = lambda w: pl.BlockSpec((bb, lt, w), lambda i, j: (i, j, 0))
    per_b = lambda r, w: pl.BlockSpec((bb, r, w), lambda i, j: (i, 0, 0))
    in_specs = [tok(D_MODEL), per_b(1, D_MODEL), per_b(1, D_MODEL), _const_spec((1, D_MODEL)),
                pl.BlockSpec((None, D_MODEL, N_FUSED), lambda i, j: (layer, 0, 0), pipeline_mode=pl.Buffered(1)),
                _const_spec((GDN_CONV, 3 * W_A)),
                per_b(GDN_CONV - 1, 3 * W_A), _const_spec((1, LANES)), _const_spec((1, LANES)),
                _const_spec((1, W_B)), _const_spec((1, W_B)), _const_spec(ws.shape), _const_spec(bsb.shape)]
    sds = lambda w, dt=F32: jax.ShapeDtypeStruct((b, L, w), dt)
    out_shape = [sds(3 * W_A), sds(W_A), sds(LANES), sds(W_B),
                 jax.ShapeDtypeStruct((b, GDN_CONV - 1, 3 * W_A), F32)]
    out_specs = [tok(3 * W_A), tok(W_A), tok(LANES), tok(W_B), per_b(GDN_CONV - 1, 3 * W_A)]
    scratch = [pltpu.VMEM((bb, lt + SUBLANES, 3 * W_A), F32)]
    aliases = {}
    if dec:
        out_shape += [sds(W_C), sds(W_C), sds(W_B), sds(W_C)]
        out_specs += [tok(W_C), tok(W_C), tok(W_B), tok(W_C)]
    else:
        n_in = len(in_specs)
        in_specs += [pl.BlockSpec(memory_space=pl.ANY)] * 2
        aliases = {n_in: len(out_shape), n_in + 1: len(out_shape) + 1}
        page_spec = pl.BlockSpec((None, 1, lt // PAGE, W_C, PAGE), lambda i, j: (layer, i, j, 0, 0))
        head_major = pl.BlockSpec((1, LANES, lt), lambda i, j: (i, 0, j))
        out_shape += [jax.ShapeDtypeStruct(p.shape, p.dtype) for p in pages]
        out_specs += [page_spec, page_spec]
        wf = H_C * LANES
        out_shape += [sds(wf, BF16), sds(wf, BF16), sds(wf, BF16), sds(LANES),
                      jax.ShapeDtypeStruct((b, LANES, L), F32)]
        out_specs += [tok(wf), tok(wf), tok(wf), tok(LANES), head_major]
        scratch += [pltpu.VMEM((SUBLANES, LANES), F32)]
    return pl.pallas_call(
        functools.partial(_in_kernel, dec, bb, lt),
        grid=grid, in_specs=in_specs, out_specs=out_specs, out_shape=out_shape,
        scratch_shapes=scratch, input_output_aliases=aliases,
        compiler_params=_params(("arbitrary", "arbitrary")),
        name="in_proj_dec" if dec else "in_proj_seq",
    )(x, shift, scale, gpre, w_fused, conv_w, conv_state, smb, alog, lng, lnb, ws, bsb, *pages)


GDN_GROUP = 128
GDN_GROUPS_PER_STEP = 8
N_PAIR = H_A // 2


def _tri_inverse_m1(lms, blk):
    bf = lambda xs: [x.astype(BF16) for x in xs]
    mm_ = lambda xs, ys: [_dot(x, y) for x, y in zip(xs, ys)]
    dm = [jnp.where(blk, lm, 0.0) for lm in lms]
    nm = [lm - d for lm, d in zip(lms, dm)]
    dmb = bf(dm)
    d2 = mm_(dmb, dmb)
    d2b = bf(d2)
    d4 = mm_(d2b, d2b)
    d4b = bf(d4)
    d8 = mm_(d4b, d4b)
    a1 = [x2 - x1 - c for x2, x1, c in zip(d2, dm, mm_(dmb, d2b))]
    a2 = [x4 + x8 + c for x4, x8, c in zip(d4, d8, mm_(d4b, bf(d8)))]
    et = [x + y + c for x, y, c in zip(a1, a2, mm_(bf(a1), bf(a2)))]
    etb = bf(et)
    mm = [n + c for n, c in zip(nm, mm_(etb, bf(nm)))]
    mmb = bf(mm)
    m2 = mm_(mmb, mmb)
    a3 = [x2 - x1 - c for x2, x1, c in zip(m2, mm, mm_(mmb, bf(m2)))]
    return [x + e + c for x, e, c in zip(a3, et, mm_(bf(a3), etb))]


def _pair_cols(lo, a, c0, c1):
    return jnp.where(lo, a[:, c0:c0 + 1], a[:, c1:c1 + 1])


def _pair_rsqrt_norm(lo, x, scale):
    x2 = x * x
    s_lo = jnp.sum(jnp.where(lo, x2, 0.0), axis=-1, keepdims=True)
    s_hi = jnp.sum(jnp.where(lo, 0.0, x2), axis=-1, keepdims=True)
    return jnp.where(lo, lax.rsqrt(s_lo * scale + EPS), lax.rsqrt(s_hi * scale + EPS))


def _gdn_pre_kernel(T, G, qkv_ref, small_ref, qe_ref, o0_ref, egl_ref, m_ref, bm_ref):
    C = GDN_CHUNK
    row = _iota2((T, T), 0)
    col = _iota2((T, T), 1)
    same = (row // C) == (col // C)
    bd_tri = same & (row >= col)
    bd_strict = same & (row > col)
    blk = (row // INV_BLOCK) == (col // INV_BLOCK)
    lane = _iota2((1, LANES), 1)
    lo = lane < HEAD_DIM
    lo2 = (_iota2((1, 2 * LANES), 1) % LANES) < HEAD_DIM
    bd2 = (_iota2((LANES, 2 * LANES), 0) // HEAD_DIM) == ((_iota2((LANES, 2 * LANES), 1) % LANES) // HEAD_DIM)

    rhs, rhs_b, lms, attns, qins, kouts = [], [], [], [], [], []
    for g, j in [(g, j) for g in range(G) for j in range(N_PAIR)]:
        tok = slice(g * T, (g + 1) * T)
        if j == 0:
            sm = small_ref[0, tok, :]
            gc_all = _sel_dot(bd_tri, sm)
            gl_all = jnp.concatenate(
                [jnp.broadcast_to(gc_all[(c + 1) * C - 1:(c + 1) * C, :], (C, LANES)) for c in range(T // C)],
                axis=0)
            gc_t = gc_all.T
        sl = slice(j * LANES, (j + 1) * LANES)
        qp = qkv_ref[0, tok, j * LANES:(j + 1) * LANES]
        kp = qkv_ref[0, tok, W_A + j * LANES:W_A + (j + 1) * LANES]
        vp = qkv_ref[0, tok, 2 * W_A + j * LANES:2 * W_A + (j + 1) * LANES]
        qn = qp * _pair_rsqrt_norm(lo, qp, 1.0) * (HEAD_DIM ** -0.5)
        kn = kp * _pair_rsqrt_norm(lo, kp, 1.0)
        h0, h1 = 2 * j, 2 * j + 1
        beta_p = _pair_cols(lo, sm, SM_BETA + h0, SM_BETA + h1)
        gc_p = _pair_cols(lo, gc_all, SM_G + h0, SM_G + h1)
        gl_p = _pair_cols(lo, gl_all, SM_G + h0, SM_G + h1)
        eg_p = jnp.exp(gc_p)
        kb = kn * beta_p
        kn_bf = kn.astype(BF16)
        rhs.append(jnp.concatenate([kb * eg_p, vp * beta_p], axis=1))
        rhs_b.append(rhs[-1].astype(BF16))
        qins.append(qn * eg_p)
        kouts.append((kn * jnp.exp(gl_p - gc_p)).astype(BF16))
        egl_ref[0, tok, sl] = jnp.exp(gl_p)
        zero = jnp.zeros_like(kb)
        stacked = jnp.concatenate([jnp.where(lo, kb, zero), jnp.where(lo, zero, kb),
                                   jnp.where(lo, qn, zero), jnp.where(lo, zero, qn)], axis=0).astype(BF16)
        gq = _dot_nt(stacked, kn_bf)
        for hh in range(2):
            h = 2 * j + hh
            gcol = gc_all[:, SM_G + h:SM_G + h + 1]
            grow = gc_t[SM_G + h:SM_G + h + 1, :]
            decay = jnp.where(bd_tri, jnp.exp(jnp.where(bd_tri, gcol - grow, 0.0)), 0.0)
            lms.append(jnp.where(bd_strict, gq[hh * T:(hh + 1) * T] * decay, 0.0))
            attns.append((gq[(2 + hh) * T:(3 + hh) * T] * decay).astype(BF16))

    tm1 = _tri_inverse_m1(lms, blk)
    tok_chunk = _iota2((T, LANES), 0) // C
    for g, j in [(g, j) for g in range(G) for j in range(N_PAIR)]:
        tok = slice(g * T, (g + 1) * T)
        p = g * N_PAIR + j
        sl = slice(j * LANES, (j + 1) * LANES)
        both = _dot(jnp.concatenate([tm1[2 * p], tm1[2 * p + 1]], axis=0).astype(BF16), rhs_b[p])
        wu = rhs[p] + jnp.where(lo2, both[:T], both[T:])
        wu_b = wu.astype(BF16)
        both = _dot(jnp.concatenate([attns[2 * p], attns[2 * p + 1]], axis=0), wu_b)
        aw = jnp.where(lo2, both[:T], both[T:])
        qe_ref[0, tok, sl] = (qins[p] - aw[:, :LANES]).astype(BF16)
        o0_ref[0, tok, sl] = aw[:, LANES:]
        ko = kouts[p]
        spread = jnp.concatenate([jnp.where(tok_chunk == c, ko, jnp.zeros_like(ko)) for c in range(T // C)],
                                 axis=1)
        mb_all = _dot_tn(spread, wu_b)
        for c in range(T // C):
            mb = jnp.where(bd2, mb_all[c * LANES:(c + 1) * LANES], 0.0)
            m_ref[0, g * (T // C) + c, j] = mb[:, :LANES].astype(BF16)
            bm_ref[0, g * (T // C) + c, j] = mb[:, LANES:]


def _gdn_pre(qkva, small):
    b, L, _ = qkva.shape
    T = min(GDN_GROUP, L)
    G = max(1, min(GDN_GROUPS_PER_STEP, L // T))
    nc = G * T // GDN_CHUNK
    tok = lambda w: pl.BlockSpec((1, G * T, w), lambda i, j: (i, j, 0))
    per_chunk = pl.BlockSpec((1, nc, N_PAIR, LANES, LANES), lambda i, j: (i, j, 0, 0, 0))
    sds = lambda dt: jax.ShapeDtypeStruct((b, L, W_A), dt)
    chunk_sds = lambda dt: jax.ShapeDtypeStruct((b, L // GDN_CHUNK, N_PAIR, LANES, LANES), dt)
    return pl.pallas_call(
        functools.partial(_gdn_pre_kernel, T, G),
        grid=(b, L // (G * T)),
        in_specs=[tok(3 * W_A), tok(LANES)],
        out_specs=[tok(W_A)] * 3 + [per_chunk] * 2,
        out_shape=[sds(BF16), sds(F32), sds(F32), chunk_sds(BF16), chunk_sds(F32)],
        compiler_params=_params(("arbitrary", "arbitrary")),
        name="gdn_pre",
    )(qkva, small)


def _gdn_scan_kernel(nb, lt, qe_ref, o0_ref, egl_ref, m_ref, bm_ref, gate_ref, ng_ref, s0_ref,
                     oa_ref, sout_ref, s_sc):
    l = pl.program_id(0)
    C = GDN_CHUNK

    @pl.when(l == 0)
    def _():
        s_sc[...] = s0_ref[...]

    lane = _iota2((1, LANES), 1)
    lo = lane < HEAD_DIM

    def body(c, carry):
        r0 = pl.multiple_of(c * C, C)
        rows = pl.ds(r0, C)
        for b in range(nb):
            for j in range(N_PAIR):
                sl = slice(j * LANES, (j + 1) * LANES)
                s_old = s_sc[b, j]
                lhs = jnp.concatenate([m_ref[b, c, j], qe_ref[b, rows, sl]], axis=0)
                r = _dot(lhs, s_old.astype(BF16))
                s_sc[b, j] = s_old * egl_ref[b, pl.ds(r0, 1), sl] + (bm_ref[b, c, j] - r[:LANES])
                o = r[LANES:] + o0_ref[b, rows, sl]
                on = o * _pair_rsqrt_norm(lo, o, 1.0 / HEAD_DIM)
                oa_ref[b, rows, sl] = on * ng_ref[:, sl] * gate_ref[b, rows, sl]
        return carry

    lax.fori_loop(0, lt // C, body, 0)
    sout_ref[...] = s_sc[...]


def _gdn_scan(qe, o0, egl, m, bm, gate, ng, s0_bd):
    b, L, _ = qe.shape
    lt = min(SEQ_TILE, L)
    tok = pl.BlockSpec((b, lt, W_A), lambda i: (0, i, 0))
    per_chunk = pl.BlockSpec((b, lt // GDN_CHUNK, N_PAIR, LANES, LANES), lambda i: (0, i, 0, 0, 0))
    st = pl.BlockSpec((b, N_PAIR, LANES, LANES), lambda i: (0, 0, 0, 0))
    return pl.pallas_call(
        functools.partial(_gdn_scan_kernel, b, lt),
        grid=(L // lt,),
        in_specs=[tok, tok, tok, per_chunk, per_chunk, tok, _const_spec((1, W_A)), st],
        out_specs=[tok, st],
        out_shape=[jax.ShapeDtypeStruct((b, L, W_A), F32),
                   jax.ShapeDtypeStruct((b, N_PAIR, LANES, LANES), F32)],
        scratch_shapes=[pltpu.VMEM((b, N_PAIR, LANES, LANES), F32)],
        compiler_params=_params(("arbitrary",)),
        name="gdn_scan",
    )(qe, o0, egl, m, bm, gate, ng, s0_bd)


def _to_pair_blockdiag(s):
    b = s.shape[0]
    s = s.reshape(b, N_PAIR, 2, HEAD_DIM, HEAD_DIM)
    z = jnp.zeros_like(s[:, :, 0])
    top = jnp.concatenate([s[:, :, 0], z], axis=-1)
    bot = jnp.concatenate([z, s[:, :, 1]], axis=-1)
    return jnp.concatenate([top, bot], axis=-2)


def _from_pair_blockdiag(sbd):
    b = sbd.shape[0]
    s = jnp.stack([sbd[:, :, :HEAD_DIM, :HEAD_DIM], sbd[:, :, HEAD_DIM:, HEAD_DIM:]], axis=2)
    return s.reshape(b, H_A, HEAD_DIM, HEAD_DIM)


def _gdn_seq(qkva, small, gate, ng, s0):
    qe, o0, egl, m, bm = _gdn_pre(qkva, small)
    oa, sbd = _gdn_scan(qe, o0, egl, m, bm, gate, ng, _to_pair_blockdiag(s0))
    return oa, _from_pair_blockdiag(sbd)


def _gdn_dec_kernel(L, q_ref, k_ref, v_ref, smt_ref, gate_ref, ngb_ref, s0_ref, oa_ref, sout_ref, kq_sc):
    h = pl.program_id(0)
    nb = q_ref.shape[-1]
    for t in range(L):
        q_t = q_ref[t]
        k_t = k_ref[t]
        q_t = q_t * (lax.rsqrt(jnp.sum(q_t * q_t, axis=0, keepdims=True) + EPS) * (HEAD_DIM ** -0.5))
        k_t = k_t * lax.rsqrt(jnp.sum(k_t * k_t, axis=0, keepdims=True) + EPS)
        kq_sc[0] = k_t
        kq_sc[1] = q_t
        beta = smt_ref[t, pl.ds(SM_BETA + h, 1), :]
        a = jnp.exp(smt_ref[t, pl.ds(SM_G + h, 1), :])
        src = s0_ref if t == 0 else sout_ref

        def k_dot_s(kk, acc):
            return acc + src[0, kk] * kq_sc[0, pl.ds(kk, 1), :]

        ks = lax.fori_loop(0, HEAD_DIM, k_dot_s, jnp.zeros((HEAD_DIM, nb), F32), unroll=8)
        delta = beta * (v_ref[t] - a * ks)

        def update(kk, acc):
            s_new = a * src[0, kk] + kq_sc[0, pl.ds(kk, 1), :] * delta
            sout_ref[0, kk] = s_new
            return acc + s_new * kq_sc[1, pl.ds(kk, 1), :]

        o = lax.fori_loop(0, HEAD_DIM, update, jnp.zeros((HEAD_DIM, nb), F32), unroll=8)
        on = o * lax.rsqrt(jnp.mean(o * o, axis=0, keepdims=True) + EPS)
        oa_ref[t] = on * ngb_ref[...] * gate_ref[t]


def _gdn_dec(layer, qkv_t, small_t, gate_t, ngb, s0_t):
    L, _, nb = qkv_t.shape
    nh = W_A // HEAD_DIM
    head = lambda off: pl.BlockSpec((L, HEAD_DIM, nb), lambda h: (0, off + h, 0))
    st = pl.BlockSpec((1, HEAD_DIM, HEAD_DIM, nb), lambda h: (h, 0, 0, 0))
    st_in = pl.BlockSpec((None, 1, HEAD_DIM, HEAD_DIM, nb), lambda h: (layer, h, 0, 0, 0))
    return pl.pallas_call(
        functools.partial(_gdn_dec_kernel, L),
        grid=(H_A,),
        in_specs=[head(0), head(nh), head(2 * nh),
                  pl.BlockSpec((L, LANES, nb), lambda h: (0, 0, 0), pipeline_mode=pl.Buffered(1)),
                  head(0), pl.BlockSpec((HEAD_DIM, nb), lambda h: (h, 0)), st_in],
        out_specs=[head(0), st],
        out_shape=[jax.ShapeDtypeStruct((L, W_A, nb), F32),
                   jax.ShapeDtypeStruct((H_A, HEAD_DIM, HEAD_DIM, nb), F32)],
        scratch_shapes=[pltpu.VMEM((2, HEAD_DIM, nb), F32)],
        compiler_params=_params(("arbitrary",)),
        name="gdn_dec",
    )(qkv_t, qkv_t, qkv_t, small_t, gate_t, ngb, s0_t)


FOX_STRIP = 64
FOX_TQ = 512
FOX_TK = 512


def _fox_seq_kernel(tq, tk, q_ref, k_ref, v_ref, dq_ref, o_ref, m_sc, acc_sc, dqb_sc, s_sc, p_sc, al_sc):
    qi = pl.program_id(1)
    ndiag = tq // tk
    m_sc[...] = jnp.full_like(m_sc, NEG_INF)
    acc_sc[...] = jnp.zeros_like(acc_sc)
    dq = dq_ref[0] * LOG2E
    for h in range(H_C):
        dqb_sc[h] = jnp.broadcast_to(dq[:, SM_LF + h:SM_LF + h + 1], (tq, LANES))

    def step(ki, diag):
        k0 = pl.multiple_of(ki * tk, tk)
        r_lo = 0 if diag is None else diag * tk
        live = slice(r_lo, tq)

        def scores(h):
            grp = slice(h * LANES, (h + 1) * LANES)
            s_sc[h % 2, live, :] = _dot_nt(q_ref[0, live, grp], k_ref[0, pl.ds(k0, tk), grp])

        def accumulate(h):
            grp = slice(h * LANES, (h + 1) * LANES)
            acc_sc[h, live, :] = (al_sc[h % 2, live, :] * acc_sc[h, live, :]
                                  + _dot(p_sc[h % 2, live, :], v_ref[0, pl.ds(k0, tk), grp]))

        scores(0)
        for h in range(H_C):
            buf = h % 2
            if h + 1 < H_C:
                scores(h + 1)
            for r in range(r_lo // FOX_STRIP, tq // FOX_STRIP):
                rows = slice(r * FOX_STRIP, (r + 1) * FOX_STRIP)
                nc = tk if diag is None else min(tk, -(-((r + 1) * FOX_STRIP - r_lo) // LANES) * LANES)
                s = s_sc[buf, rows, 0:nc]
                if diag is not None:
                    keep = _iota2((FOX_STRIP, nc), 1) <= _iota2((FOX_STRIP, nc), 0) + (r * FOX_STRIP - r_lo)
                    s = jnp.where(keep, s, NEG_INF)
                dqb = dqb_sc[h, rows, :]
                m_old = m_sc[h, rows, :]
                m_new = jnp.maximum(m_old, jnp.max(s, axis=-1, keepdims=True) + dqb)
                shift = m_new - dqb
                if nc > LANES:
                    shift = jnp.concatenate([shift] * (nc // LANES), axis=1)
                m_sc[h, rows, :] = m_new
                al_sc[buf, rows, :] = jnp.exp2(m_old - m_new)
                p_sc[buf, rows, 0:nc] = jnp.exp2(s - shift).astype(BF16)
                if nc < tk:
                    p_sc[buf, rows, nc:tk] = jnp.zeros((FOX_STRIP, tk - nc), BF16)
            if h > 0:
                accumulate(h - 1)
        accumulate(H_C - 1)

    def body(ki, carry):
        step(ki, None)
        return carry

    lax.fori_loop(0, ndiag * qi, body, 0)
    for d in range(ndiag):
        step(ndiag * qi + d, d)
    for j in range(H_C // 2):
        halves = []
        for h in (2 * j, 2 * j + 1):
            acc = acc_sc[h]
            halves.append(acc[:, :HEAD_DIM] / acc[:, HEAD_DIM:HEAD_DIM + 1])
        o_ref[0, :, j * LANES:(j + 1) * LANES] = jnp.concatenate(halves, axis=1)


def _fox_seq(q_aug, k_aug, v_aug, dcum):
    b, L, wf = q_aug.shape
    tk = min(FOX_TK, L)
    tq = min(FOX_TQ, L)
    full = pl.BlockSpec((1, L, wf), lambda i, j: (i, 0, 0), pipeline_mode=pl.Buffered(1))
    return pl.pallas_call(
        functools.partial(_fox_seq_kernel, tq, tk),
        grid=(b, L // tq),
        in_specs=[pl.BlockSpec((1, tq, wf), lambda i, j: (i, j, 0)), full, full,
                  pl.BlockSpec((1, tq, LANES), lambda i, j: (i, j, 0))],
        out_specs=pl.BlockSpec((1, tq, W_C), lambda i, j: (i, j, 0)),
        out_shape=jax.ShapeDtypeStruct((b, L, W_C), F32),
        scratch_shapes=[pltpu.VMEM((H_C, tq, LANES), F32), pltpu.VMEM((H_C, tq, LANES), F32),
                        pltpu.VMEM((H_C, tq, LANES), F32),
                        pltpu.VMEM((2, tq, tk), F32), pltpu.VMEM((2, tq, tk), BF16),
                        pltpu.VMEM((2, tq, LANES), F32)],
        compiler_params=_params(("arbitrary", "arbitrary")),
        name="fox_seq",
    )(q_aug, k_aug, v_aug, dcum)


def _fox_dec_kernel(L, n_pages, pt_ref, q_ref, kn_ref, vn_ref, sm_ref, lf_ref, *refs):
    k_refs = refs[0:n_pages]
    v_refs = refs[n_pages:2 * n_pages]
    o_ref = refs[2 * n_pages]
    x_sc = refs[2 * n_pages + 1]
    i = pl.program_id(0)
    R = H_C * L
    q = q_ref[0] * (HEAD_DIM ** -0.5)
    rowh = _iota2((R, W_C), 0) // L
    colh = _iota2((R, W_C), 1) // HEAD_DIM
    qbd = jnp.where(rowh == colh, jnp.concatenate([q] * H_C, axis=0), 0.0).astype(BF16)

    for p in range(n_pages):
        pg = pt_ref[i * n_pages + p]
        for h in range(H_C):
            r = h * n_pages + p
            x_sc[r:r + 1, :] = lf_ref[h, pl.ds(pg, 1), :]
    x = x_sc[...]
    n = H_C * n_pages
    later = (_iota2((PAGE, PAGE), 0) > _iota2((PAGE, PAGE), 1)).astype(F32)
    within = _dot_sel(x, later)
    tot = _dot_sel(x, jnp.ones((PAGE, PAGE), F32))
    ri = _iota2((n, n), 0)
    ci = _iota2((n, n), 1)
    later_pages = ((ci // n_pages == ri // n_pages) & (ci % n_pages > ri % n_pages)).astype(F32)
    rsum = within + _sel_dot(later_pages, tot)

    sm = sm_ref[0]
    tri = (_iota2((L, L), 0) >= _iota2((L, L), 1)).astype(F32)
    cq = _sel_dot(tri, sm)
    cq_t = cq.T
    cq_col = jnp.concatenate([cq[:, SM_LF + h:SM_LF + h + 1] for h in range(H_C)], axis=0)
    cq_row = jnp.concatenate([jnp.broadcast_to(cq_t[SM_LF + h:SM_LF + h + 1, :], (L, L))
                              for h in range(H_C)], axis=0)

    s_pages = []
    for p in range(n_pages):
        sp = _dot(qbd, k_refs[p][...].astype(BF16))
        bias = jnp.concatenate(
            [jnp.broadcast_to(rsum[h * n_pages + p:h * n_pages + p + 1, :], (L, PAGE)) for h in range(H_C)],
            axis=0)
        s_pages.append(sp + bias + cq_col)
    s_new = _dot_nt(qbd, kn_ref[0].astype(BF16)) + cq_col - cq_row
    qpos = _iota2((R, L), 0) % L
    s_new = jnp.where(_iota2((R, L), 1) <= qpos, s_new, NEG_INF)

    mx = jnp.max(s_new, axis=-1, keepdims=True)
    for sp in s_pages:
        mx = jnp.maximum(mx, jnp.max(sp, axis=-1, keepdims=True))
    p_new = jnp.exp(s_new - mx)
    den = jnp.sum(p_new, axis=-1, keepdims=True)
    acc = _dot(p_new.astype(BF16), vn_ref[0].astype(BF16))
    for p in range(n_pages):
        pp = jnp.exp(s_pages[p] - mx)
        den = den + jnp.sum(pp, axis=-1, keepdims=True)
        acc = acc + _dot_nt(pp.astype(BF16), v_refs[p][...].astype(BF16))
    acc = jnp.where(rowh == colh, acc / den, 0.0)
    out = acc[0:L]
    for h in range(1, H_C):
        out = out + acc[h * L:(h + 1) * L]
    o_ref[0] = out


def _fox_dec(layer, page_table, q, kn, vn, small, cache_kt, cache_vt, cache_lft):
    b, L, _ = q.shape
    n_pages = page_table.shape[1]
    n_pool = cache_kt.shape[1]
    tok = lambda w: pl.BlockSpec((1, L, w), lambda i, pt: (i, 0, 0))

    def page_spec(p):
        return pl.BlockSpec((None, None, W_C, PAGE), lambda i, pt: (layer, pt[i * n_pages + p], 0, 0))

    in_specs = [tok(W_C), tok(W_C), tok(W_C), tok(LANES),
                pl.BlockSpec((None, H_C, n_pool, PAGE), lambda i, pt: (layer, 0, 0, 0),
                             pipeline_mode=pl.Buffered(1))]
    in_specs += [page_spec(p) for p in range(n_pages)]
    in_specs += [page_spec(p) for p in range(n_pages)]
    grid_spec = pltpu.PrefetchScalarGridSpec(
        num_scalar_prefetch=1, grid=(b,), in_specs=in_specs, out_specs=tok(W_C),
        scratch_shapes=[pltpu.VMEM((H_C * n_pages, PAGE), F32)])
    return pl.pallas_call(
        functools.partial(_fox_dec_kernel, L, n_pages),
        grid_spec=grid_spec,
        out_shape=jax.ShapeDtypeStruct((b, L, W_C), F32),
        compiler_params=_params(("arbitrary",)),
        name="fox_dec",
    )(page_table.reshape(-1), q, kn, vn, small, cache_lft,
      *([cache_kt] * n_pages), *([cache_vt] * n_pages))


def _post_kernel(bb, lt, x_ref, oa_ref, ob_ref, oc_ref, gate1_ref, shift2_ref, scale2_ref, gate2_ref,
                 gpm_ref, gpf_ref, gqf_ref, wo_ref, wu_ref, cw_ref, cst_ref, wd_ref,
                 y_ref, tail_ref, ext_ref, car_ref):
    l = pl.program_id(1)
    m = bb * lt
    keep = FFN_CONV - 1

    def rms(v, g_ref):
        return v * lax.rsqrt(jnp.mean(v * v, axis=-1, keepdims=True) + EPS) * g_ref[...]

    mixed = jnp.concatenate([oa_ref[...].reshape(m, W_A), ob_ref[...].reshape(m, W_B),
                             oc_ref[...].reshape(m, W_C)], axis=1).astype(BF16)
    o = _dot(mixed, wo_ref[...])
    x1 = x_ref[...] + gate1_ref[...] * rms(o, gpm_ref).reshape(bb, lt, D_MODEL)

    h = rms(x1, gpf_ref) * (1.0 + scale2_ref[...]) + shift2_ref[...]
    h2 = h.reshape(m, D_MODEL).astype(BF16)

    @pl.when(l == 0)
    def _():
        car_ref[...] = jnp.zeros_like(car_ref)
        car_ref[:, SUBLANES - keep:SUBLANES, :] = cst_ref[...]

    w = FF_CHUNK
    nchunk = D_FF // FF_CHUNK

    def project(slot, c0):
        up = _dot(h2, wu_ref[:, c0:c0 + w]).reshape(bb, lt, w)
        ext_ref[slot, :, 0:SUBLANES, :] = car_ref[:, :, c0:c0 + w]
        ext_ref[slot, :, SUBLANES:, :] = up
        car_ref[:, :, c0:c0 + w] = ext_ref[slot, :, lt:lt + SUBLANES, :]

    def conv(slot, c0):
        y = None
        for i in range(FFN_CONV):
            off = SUBLANES - keep + i
            term = ext_ref[slot, :, off:off + lt, :] * cw_ref[i:i + 1, c0:c0 + w]
            y = term if y is None else y + term
        return y

    project(0, 0)
    project(1, D_FF)
    y = jnp.zeros((m, D_MODEL), F32)
    for j in range(nchunk):
        if j + 1 < nchunk:
            project(2 * (j + 1), (j + 1) * w)
            project(2 * (j + 1) + 1, D_FF + (j + 1) * w)
        a = conv(2 * j, j * w)
        b = conv(2 * j + 1, D_FF + j * w)
        g = (jax.nn.gelu(a) * b).reshape(m, w).astype(BF16)
        y = y + _dot(g, wd_ref[j * w:(j + 1) * w, :])
    tail_ref[...] = car_ref[:, SUBLANES - keep:SUBLANES, :]
    y_ref[...] = x1 + gate2_ref[...] * rms(y, gqf_ref).reshape(bb, lt, D_MODEL)


def _post(dec, layer, x, oa, ob, oc, gate1, shift2, scale2, gate2, gpm, gpf, gqf, wo, wu, cw, cst, wd):
    b, L, _ = x.shape
    if dec:
        bb, lt = min(DEC_TILE_POST, b), L
    else:
        bb, lt = 1, min(SEQ_TILE, L)
    tok = lambda w: pl.BlockSpec((bb, lt, w), lambda i, j: (i, j, 0))
    per_b = lambda r, w: pl.BlockSpec((bb, r, w), lambda i, j: (i, 0, 0))
    vec = _const_spec((1, D_MODEL))
    weight = lambda r, c: pl.BlockSpec((None, r, c), lambda i, j: (layer, 0, 0), pipeline_mode=pl.Buffered(1))
    in_specs = [tok(D_MODEL), tok(W_A), tok(W_B), tok(W_C), per_b(1, D_MODEL), per_b(1, D_MODEL),
                per_b(1, D_MODEL), per_b(1, D_MODEL), vec, vec, vec,
                weight(W_A + W_B + W_C, D_MODEL), weight(D_MODEL, 2 * D_FF),
                _const_spec((FFN_CONV, 2 * D_FF)), per_b(FFN_CONV - 1, 2 * D_FF),
                weight(D_FF, D_MODEL)]
    return pl.pallas_call(
        functools.partial(_post_kernel, bb, lt),
        grid=(b // bb, L // lt),
        in_specs=in_specs,
        out_specs=[tok(D_MODEL), per_b(FFN_CONV - 1, 2 * D_FF)],
        out_shape=[jax.ShapeDtypeStruct((b, L, D_MODEL), F32),
                   jax.ShapeDtypeStruct((b, FFN_CONV - 1, 2 * D_FF), F32)],
        scratch_shapes=[pltpu.VMEM((2 * D_FF // FF_CHUNK, bb, lt + SUBLANES, FF_CHUNK), F32),
                        pltpu.VMEM((bb, SUBLANES, 2 * D_FF), F32)],
        compiler_params=_params(("arbitrary", "arbitrary")),
        name="post_dec" if dec else "post_seq",
    )(x, oa, ob, oc, gate1, shift2, scale2, gate2, gpm, gpf, gqf, wo, wu, cw, cst, wd)


def _layer_params(l, gdn_A_log, gdn_dt_bias, fox_f_bias, chunk_w_s, chunk_b_s, dec_len):
    z = lambda n: jnp.zeros((n,), F32)
    smb = jnp.concatenate([z(H_A), gdn_dt_bias[l], fox_f_bias[l], z(LANES - 2 * H_A - H_C)]).reshape(1, LANES)
    alog = jnp.concatenate([z(H_A), gdn_A_log[l], z(LANES - 2 * H_A)]).reshape(1, LANES)
    ws = chunk_w_s[l]
    bs = chunk_b_s[l]
    ws_seq = ws.reshape(G_B * CHUNK, CHUNK)
    bsb_seq = jnp.repeat(bs.T, HEAD_DIM, axis=1)
    ws_dec = jnp.repeat(jnp.transpose(ws[:, :dec_len, :dec_len], (2, 1, 0)), HEAD_DIM, axis=2)
    bsb_dec = bsb_seq[:dec_len]
    return smb, alog, ws_seq, bsb_seq, ws_dec, bsb_dec


def kernel(x_prompt, x_sample, state_gdn_conv, state_gdn_S, cache_fox_k, cache_fox_v, cache_fox_logf,
           state_ffn_conv, page_table, c_prompt, c_sample, w_ada, b_ada, g_pre_mix, g_post_mix, g_pre_ffn,
           g_post_ffn, w_in, w_out, gdn_conv_w, gdn_A_log, gdn_dt_bias, gdn_norm_g, chunk_ln_g, chunk_ln_b,
           chunk_w_s, chunk_b_s, fox_f_bias, w_up, ffn_conv_w, w_down):
    depth = w_in.shape[0]
    bp, lp, _ = x_prompt.shape
    bs, ls, _ = x_sample.shape
    n_pool = cache_fox_k.shape[1]

    cache_kt = jnp.transpose(cache_fox_k, (0, 1, 3, 4, 2)).reshape(depth, n_pool, W_C, PAGE)
    cache_vt = jnp.transpose(cache_fox_v, (0, 1, 3, 4, 2)).reshape(depth, n_pool, W_C, PAGE)
    cache_lft = jnp.transpose(cache_fox_logf, (0, 3, 1, 2))
    state_s_lanes = jnp.transpose(state_gdn_S, (0, 2, 3, 4, 1))

    c_all = jnp.concatenate([c_prompt, c_sample], axis=0)
    pad = (-c_all.shape[0]) % SUBLANES
    c_all = jnp.pad(c_all, ((0, pad), (0, 0)))

    w_fused = _w_in_prep(w_in)
    pages = tuple(jnp.zeros((depth, bp, lp // PAGE, W_C, PAGE), F32) for _ in range(2))

    vec = lambda a: a.reshape(1, -1)
    outs = {k: [] for k in ("p_conv", "p_S", "p_lf", "p_ffn",
                            "s_conv", "s_S", "s_k", "s_v", "s_lf", "s_cv", "s_ffn")}
    xp, xs = x_prompt, x_sample
    mod_all = _ada(c_all, w_ada, b_ada)
    wo = w_out.astype(BF16)
    wu = w_up.astype(BF16)
    wd = w_down.astype(BF16)
    for l in range(depth):
        mod_p = mod_all[l, :bp].reshape(bp, 6, 1, D_MODEL)
        mod_s = mod_all[l, bp:bp + bs].reshape(bs, 6, 1, D_MODEL)
        smb, alog, ws_seq, bsb_seq, ws_dec, bsb_dec = _layer_params(
            l, gdn_A_log, gdn_dt_bias, fox_f_bias, chunk_w_s, chunk_b_s, ls)
        ng = jnp.tile(gdn_norm_g[l], H_A).reshape(1, W_A)
        common_in = (vec(g_pre_mix[l]), w_fused, gdn_conv_w[l])
        common_b = (smb, alog, vec(chunk_ln_g[l]), vec(chunk_ln_b[l]))
        post_w = (vec(g_post_mix[l]), vec(g_pre_ffn[l]), vec(g_post_ffn[l]), wo, wu, ffn_conv_w[l])

        (qkva, gate, small, ob, tail, pk, pv, qbf, kbf, vbf, dcum, small_t) = _in_proj(
            False, l, xp, mod_p[:, 0], mod_p[:, 1], *common_in,
            jnp.zeros((bp, GDN_CONV - 1, 3 * W_A), F32), *common_b, ws_seq, bsb_seq, pages=pages)
        pages = (pk, pv)
        oa, s_new = _gdn_seq(qkva, small, gate, ng, jnp.zeros((bp, H_A, HEAD_DIM, HEAD_DIM), F32))
        oc = _fox_seq(qbf, kbf, vbf, dcum)
        xp, ftail = _post(False, l, xp, oa, ob, oc, mod_p[:, 2], mod_p[:, 3], mod_p[:, 4], mod_p[:, 5],
                          *post_w, jnp.zeros((bp, FFN_CONV - 1, 2 * D_FF), F32), wd)
        outs["p_conv"].append(tail)
        outs["p_S"].append(s_new)
        lf_hm = small_t[:, SM_LF:SM_LF + H_C, :].reshape(bp, H_C, lp // PAGE, PAGE)
        outs["p_lf"].append(jnp.transpose(lf_hm, (0, 2, 3, 1)))
        outs["p_ffn"].append(ftail)

        (qkva, gate, small, ob, tail, kc, vc, vb, qd) = _in_proj(
            True, l, xs, mod_s[:, 0], mod_s[:, 1], *common_in, state_gdn_conv[l], *common_b, ws_dec, bsb_dec)
        to_lanes = lambda a: jnp.transpose(a, (1, 2, 0))
        oa_t, s_new_t = _gdn_dec(l, to_lanes(qkva), to_lanes(small), to_lanes(gate),
                                 jnp.broadcast_to(ng.reshape(W_A, 1), (W_A, bs)), state_s_lanes)
        oa = jnp.transpose(oa_t, (2, 0, 1))
        s_new = jnp.transpose(s_new_t, (3, 0, 1, 2))
        oc = _fox_dec(l, page_table, qd, kc, vc, small, cache_kt, cache_vt, cache_lft)
        xs, ftail = _post(True, l, xs, oa, ob, oc, mod_s[:, 2], mod_s[:, 3], mod_s[:, 4], mod_s[:, 5],
                          *post_w, state_ffn_conv[l], wd)
        outs["s_conv"].append(tail)
        outs["s_S"].append(s_new)
        outs["s_k"].append(kc.reshape(bs, ls, H_C, HEAD_DIM))
        outs["s_v"].append(vc.reshape(bs, ls, H_C, HEAD_DIM))
        outs["s_lf"].append(small[:, :, SM_LF:SM_LF + H_C])
        outs["s_cv"].append(vb)
        outs["s_ffn"].append(ftail)

    st = lambda k: jnp.stack(outs[k])
    from_pages = lambda p: jnp.transpose(
        p.reshape(depth, bp, lp // PAGE, H_C, HEAD_DIM, PAGE), (0, 1, 2, 5, 3, 4))
    return (xp, xs, st("p_conv"), st("p_S"), from_pages(pages[0]), from_pages(pages[1]), st("p_lf"), st("p_ffn"),
            st("s_conv"), st("s_S"), st("s_k"), st("s_v"), st("s_lf"), st("s_cv"), st("s_ffn"))
```
